```python
import jax, jax.numpy as jnp
from jax import lax
import numpy as np

D_MODEL = 2048
BATCH = 8
SEQ = 8192
DEPTH = 2

HEAD_DIM = 128
POOL_WINDOWS = (2, 4, 8, 16)
POOL_WIDTH = D_MODEL // 4
POOL_GROUP = POOL_WIDTH // len(POOL_WINDOWS)
SGU_WIDTH = (D_MODEL - POOL_WIDTH) // 2
SGU_HEADS = SGU_WIDTH // HEAD_DIM
CHUNK = 128
CONV_WIDTH = D_MODEL - POOL_WIDTH - SGU_WIDTH
CONV_GROUPS = CONV_WIDTH // HEAD_DIM
CONV_KERNEL = 31
IN_WIDTH = POOL_WIDTH + 2 * SGU_WIDTH + 2 * CONV_WIDTH
D_FF = 4 * D_MODEL
DEEPNORM_ALPHA = (2 * DEPTH) ** 0.25
DEEPNORM_BETA = (8 * DEPTH) ** -0.25
LN_EPS = 1e-5

kernel_name = "hybrid_pool_sgu_conv_deepnorm"


def layer_norm(x, g, b):
    xf = x.astype(jnp.float32)
    mu = jnp.mean(xf, axis=-1, keepdims=True)
    xc = xf - mu
    var = jnp.mean(jnp.square(xc), axis=-1, keepdims=True)
    y = xc * lax.rsqrt(var + LN_EPS)
    return (y * g.astype(jnp.float32) + b.astype(jnp.float32)).astype(x.dtype)


def pool_mixer(a, w_pool, pool_scale):
    bsz, s, _ = a.shape
    cs = jnp.cumsum(a.astype(jnp.float32), axis=1)
    count = jnp.arange(1, s + 1, dtype=jnp.float32)[None, :, None]
    means = []
    for g, win in enumerate(POOL_WINDOWS):
        c = cs[..., g * POOL_GROUP:(g + 1) * POOL_GROUP]
        prev = jnp.pad(c[:, :-win], ((0, 0), (win, 0), (0, 0)))
        means.append((c - prev) / jnp.minimum(count, float(win)))
    pooled = jnp.concatenate(means, axis=-1).astype(a.dtype) - a
    pooled = pooled.reshape(bsz, s, len(POOL_WINDOWS), POOL_GROUP)
    y = jnp.einsum('bsgc,gcd->bsgd', pooled, w_pool).reshape(bsz, s, POOL_WIDTH)
    return y * pool_scale


def sgu_mixer(uv, ln_g, ln_b, w_s, b_s):
    bsz, s, _ = uv.shape
    uv = jax.nn.gelu(uv)
    u, v = jnp.split(uv, 2, axis=-1)
    v = layer_norm(v, ln_g, ln_b)
    vc = v.reshape(bsz, s // CHUNK, CHUNK, SGU_HEADS, HEAD_DIM)
    mask = jnp.tril(jnp.ones((CHUNK, CHUNK), dtype=w_s.dtype))
    mixed = jnp.einsum('hts,bnshc->bnthc', w_s * mask, vc) + b_s.T[None, None, :, :, None]
    return u * mixed.reshape(bsz, s, SGU_WIDTH)


def conv_module(ag, conv_w, conv_b, ln_g, ln_b):
    a, g = jnp.split(ag, 2, axis=-1)
    h = a * jax.nn.sigmoid(g)
    h = lax.conv_general_dilated(
        h, conv_w[:, None, :], window_strides=(1,), padding=[(CONV_KERNEL - 1, 0)],
        dimension_numbers=('NWC', 'WIO', 'NWC'), feature_group_count=CONV_WIDTH) + conv_b
    h = layer_norm(h, ln_g, ln_b)
    return jax.nn.silu(h)


def _fwd_setup_inputs(seed: int = 0) -> dict:
    key = jax.random.key(seed)
    ks = jax.random.split(key, 24)

    def nrm(k, shape, scale):
        return jax.random.normal(k, shape, dtype=jnp.float32) * scale

    L = DEPTH
    return {
        "x": nrm(ks[0], (BATCH, SEQ, D_MODEL), 1.0),
        "w_in": nrm(ks[1], (L, D_MODEL, IN_WIDTH), D_MODEL ** -0.5),
        "b_in": nrm(ks[2], (L, IN_WIDTH), 0.02),
        "w_pool": nrm(ks[3], (L, len(POOL_WINDOWS), POOL_GROUP, POOL_GROUP), POOL_GROUP ** -0.5),
        "pool_scale": 1.0 + nrm(ks[4], (L, POOL_WIDTH), 0.1),
        "sgu_ln_g": 1.0 + nrm(ks[5], (L, SGU_WIDTH), 0.02),
        "sgu_ln_b": nrm(ks[6], (L, SGU_WIDTH), 0.02),
        "sgu_w": nrm(ks[7], (L, SGU_HEADS, CHUNK, CHUNK), CHUNK ** -0.5),
        "sgu_b": 1.0 + nrm(ks[8], (L, SGU_HEADS, CHUNK), 0.02),
        "conv_w": nrm(ks[9], (L, CONV_KERNEL, CONV_WIDTH), CONV_KERNEL ** -0.5),
        "conv_b": nrm(ks[10], (L, CONV_WIDTH), 0.02),
        "conv_ln_g": 1.0 + nrm(ks[11], (L, CONV_WIDTH), 0.02),
        "conv_ln_b": nrm(ks[12], (L, CONV_WIDTH), 0.02),
        "w_out": nrm(ks[13], (L, D_MODEL, D_MODEL), DEEPNORM_BETA * D_MODEL ** -0.5),
        "b_out": nrm(ks[14], (L, D_MODEL), 0.02),
        "ln1_g": 1.0 + nrm(ks[15], (L, D_MODEL), 0.02),
        "ln1_b": nrm(ks[16], (L, D_MODEL), 0.02),
        "w_ff1": nrm(ks[17], (L, D_MODEL, D_FF), D_MODEL ** -0.5),
        "b_ff1": nrm(ks[18], (L, D_FF), 0.02),
        "w_ff2": nrm(ks[19], (L, D_FF, D_MODEL), DEEPNORM_BETA * D_FF ** -0.5),
        "b_ff2": nrm(ks[20], (L, D_MODEL), 0.02),
        "ln2_g": 1.0 + nrm(ks[21], (L, D_MODEL), 0.02),
        "ln2_b": nrm(ks[22], (L, D_MODEL), 0.02),
    }


def _fwd_reference(x, w_in, b_in, w_pool, pool_scale, sgu_ln_g, sgu_ln_b, sgu_w, sgu_b,
              conv_w, conv_b, conv_ln_g, conv_ln_b, w_out, b_out, ln1_g, ln1_b,
              w_ff1, b_ff1, w_ff2, b_ff2, ln2_g, ln2_b):
    for l in range(DEPTH):
        proj = jnp.einsum('bsd,de->bse', x, w_in[l]) + b_in[l]
        p_a = proj[..., :POOL_WIDTH]
        p_b = proj[..., POOL_WIDTH:POOL_WIDTH + 2 * SGU_WIDTH]
        p_c = proj[..., POOL_WIDTH + 2 * SGU_WIDTH:]
        y_a = pool_mixer(p_a, w_pool[l], pool_scale[l])
        y_b = sgu_mixer(p_b, sgu_ln_g[l], sgu_ln_b[l], sgu_w[l], sgu_b[l])
        y_c = conv_module(p_c, conv_w[l], conv_b[l], conv_ln_g[l], conv_ln_b[l])
        mixed = jnp.concatenate([y_a, y_b, y_c], axis=-1)
        mix_out = jnp.einsum('bsd,de->bse', mixed, w_out[l]) + b_out[l]
        x = layer_norm(DEEPNORM_ALPHA * x + mix_out, ln1_g[l], ln1_b[l])
        h = jnp.square(jax.nn.relu(jnp.einsum('bsd,df->bsf', x, w_ff1[l]) + b_ff1[l]))
        ff_out = jnp.einsum('bsf,fd->bsd', h, w_ff2[l]) + b_ff2[l]
        x = layer_norm(DEEPNORM_ALPHA * x + ff_out, ln2_g[l], ln2_b[l])
    return x


import jax as _jax
import jax.numpy as _jnp

TWIN_FORMAT = 'train_step'
FWD_PARAMS = ['x', 'w_in', 'b_in', 'w_pool', 'pool_scale', 'sgu_ln_g', 'sgu_ln_b', 'sgu_w', 'sgu_b', 'conv_w', 'conv_b', 'conv_ln_g', 'conv_ln_b', 'w_out', 'b_out', 'ln1_g', 'ln1_b', 'w_ff1', 'b_ff1', 'w_ff2', 'b_ff2', 'ln2_g', 'ln2_b']
TWIN_WEIGHTS = ['w_in', 'b_in', 'w_pool', 'pool_scale', 'sgu_ln_g', 'sgu_ln_b', 'sgu_w', 'sgu_b', 'conv_w', 'conv_b', 'conv_ln_g', 'conv_ln_b', 'w_out', 'b_out', 'ln1_g', 'ln1_b', 'w_ff1', 'b_ff1', 'w_ff2', 'b_ff2', 'ln2_g', 'ln2_b']
TWIN_DIFF_INPUT = 'x'
TWIN_INPUTS = ['x', 'w_in', 'b_in', 'w_pool', 'pool_scale', 'sgu_ln_g', 'sgu_ln_b', 'sgu_w', 'sgu_b', 'conv_w', 'conv_b', 'conv_ln_g', 'conv_ln_b', 'w_out', 'b_out', 'ln1_g', 'ln1_b', 'w_ff1', 'b_ff1', 'w_ff2', 'b_ff2', 'ln2_g', 'ln2_b', 'loss_target', 'm_w_in', 'm_b_in', 'm_w_pool', 'm_pool_scale', 'm_sgu_ln_g', 'm_sgu_ln_b', 'm_sgu_w', 'm_sgu_b', 'm_conv_w', 'm_conv_b', 'm_conv_ln_g', 'm_conv_ln_b', 'm_w_out', 'm_b_out', 'm_ln1_g', 'm_ln1_b', 'm_w_ff1', 'm_b_ff1', 'm_w_ff2', 'm_b_ff2', 'm_ln2_g', 'm_ln2_b', 'v_w_in', 'v_b_in', 'v_w_pool', 'v_pool_scale', 'v_sgu_ln_g', 'v_sgu_ln_b', 'v_sgu_w', 'v_sgu_b', 'v_conv_w', 'v_conv_b', 'v_conv_ln_g', 'v_conv_ln_b', 'v_w_out', 'v_b_out', 'v_ln1_g', 'v_ln1_b', 'v_w_ff1', 'v_b_ff1', 'v_w_ff2', 'v_b_ff2', 'v_ln2_g', 'v_ln2_b']
TWIN_OUTPUTS = ['loss', 'grad_x', 'grad_w_in', 'grad_b_in', 'grad_w_pool', 'grad_pool_scale', 'grad_sgu_ln_g', 'grad_sgu_ln_b', 'grad_sgu_w', 'grad_sgu_b', 'grad_conv_w', 'grad_conv_b', 'grad_conv_ln_g', 'grad_conv_ln_b', 'grad_w_out', 'grad_b_out', 'grad_ln1_g', 'grad_ln1_b', 'grad_w_ff1', 'grad_b_ff1', 'grad_w_ff2', 'grad_b_ff2', 'grad_ln2_g', 'grad_ln2_b', 'delta_w_in', 'delta_b_in', 'delta_w_pool', 'delta_pool_scale', 'delta_sgu_ln_g', 'delta_sgu_ln_b', 'delta_sgu_w', 'delta_sgu_b', 'delta_conv_w', 'delta_conv_b', 'delta_conv_ln_g', 'delta_conv_ln_b', 'delta_w_out', 'delta_b_out', 'delta_ln1_g', 'delta_ln1_b', 'delta_w_ff1', 'delta_b_ff1', 'delta_w_ff2', 'delta_b_ff2', 'delta_ln2_g', 'delta_ln2_b', 'new_m_w_in', 'new_m_b_in', 'new_m_w_pool', 'new_m_pool_scale', 'new_m_sgu_ln_g', 'new_m_sgu_ln_b', 'new_m_sgu_w', 'new_m_sgu_b', 'new_m_conv_w', 'new_m_conv_b', 'new_m_conv_ln_g', 'new_m_conv_ln_b', 'new_m_w_out', 'new_m_b_out', 'new_m_ln1_g', 'new_m_ln1_b', 'new_m_w_ff1', 'new_m_b_ff1', 'new_m_w_ff2', 'new_m_b_ff2', 'new_m_ln2_g', 'new_m_ln2_b', 'new_v_w_in', 'new_v_b_in', 'new_v_w_pool', 'new_v_pool_scale', 'new_v_sgu_ln_g', 'new_v_sgu_ln_b', 'new_v_sgu_w', 'new_v_sgu_b', 'new_v_conv_w', 'new_v_conv_b', 'new_v_conv_ln_g', 'new_v_conv_ln_b', 'new_v_w_out', 'new_v_b_out', 'new_v_ln1_g', 'new_v_ln1_b', 'new_v_w_ff1', 'new_v_b_ff1', 'new_v_w_ff2', 'new_v_b_ff2', 'new_v_ln2_g', 'new_v_ln2_b']
TWIN_LEAF_KINDS = {'loss': 'loss', 'grad_x': 'grad_x', 'grad_w_in': 'grad_w', 'grad_b_in': 'grad_w', 'grad_w_pool': 'grad_w', 'grad_pool_scale': 'grad_w', 'grad_sgu_ln_g': 'grad_w', 'grad_sgu_ln_b': 'grad_w', 'grad_sgu_w': 'grad_w', 'grad_sgu_b': 'grad_w', 'grad_conv_w': 'grad_w', 'grad_conv_b': 'grad_w', 'grad_conv_ln_g': 'grad_w', 'grad_conv_ln_b': 'grad_w', 'grad_w_out': 'grad_w', 'grad_b_out': 'grad_w', 'grad_ln1_g': 'grad_w', 'grad_ln1_b': 'grad_w', 'grad_w_ff1': 'grad_w', 'grad_b_ff1': 'grad_w', 'grad_w_ff2': 'grad_w', 'grad_b_ff2': 'grad_w', 'grad_ln2_g': 'grad_w', 'grad_ln2_b': 'grad_w', 'delta_w_in': 'delta_w', 'delta_b_in': 'delta_w', 'delta_w_pool': 'delta_w', 'delta_pool_scale': 'delta_w', 'delta_sgu_ln_g': 'delta_w', 'delta_sgu_ln_b': 'delta_w', 'delta_sgu_w': 'delta_w', 'delta_sgu_b': 'delta_w', 'delta_conv_w': 'delta_w', 'delta_conv_b': 'delta_w', 'delta_conv_ln_g': 'delta_w', 'delta_conv_ln_b': 'delta_w', 'delta_w_out': 'delta_w', 'delta_b_out': 'delta_w', 'delta_ln1_g': 'delta_w', 'delta_ln1_b': 'delta_w', 'delta_w_ff1': 'delta_w', 'delta_b_ff1': 'delta_w', 'delta_w_ff2': 'delta_w', 'delta_b_ff2': 'delta_w', 'delta_ln2_g': 'delta_w', 'delta_ln2_b': 'delta_w', 'new_m_w_in': 'new_m', 'new_m_b_in': 'new_m', 'new_m_w_pool': 'new_m', 'new_m_pool_scale': 'new_m', 'new_m_sgu_ln_g': 'new_m', 'new_m_sgu_ln_b': 'new_m', 'new_m_sgu_w': 'new_m', 'new_m_sgu_b': 'new_m', 'new_m_conv_w': 'new_m', 'new_m_conv_b': 'new_m', 'new_m_conv_ln_g': 'new_m', 'new_m_conv_ln_b': 'new_m', 'new_m_w_out': 'new_m', 'new_m_b_out': 'new_m', 'new_m_ln1_g': 'new_m', 'new_m_ln1_b': 'new_m', 'new_m_w_ff1': 'new_m', 'new_m_b_ff1': 'new_m', 'new_m_w_ff2': 'new_m', 'new_m_b_ff2': 'new_m', 'new_m_ln2_g': 'new_m', 'new_m_ln2_b': 'new_m', 'new_v_w_in': 'new_v', 'new_v_b_in': 'new_v', 'new_v_w_pool': 'new_v', 'new_v_pool_scale': 'new_v', 'new_v_sgu_ln_g': 'new_v', 'new_v_sgu_ln_b': 'new_v', 'new_v_sgu_w': 'new_v', 'new_v_sgu_b': 'new_v', 'new_v_conv_w': 'new_v', 'new_v_conv_b': 'new_v', 'new_v_conv_ln_g': 'new_v', 'new_v_conv_ln_b': 'new_v', 'new_v_w_out': 'new_v', 'new_v_b_out': 'new_v', 'new_v_ln1_g': 'new_v', 'new_v_ln1_b': 'new_v', 'new_v_w_ff1': 'new_v', 'new_v_b_ff1': 'new_v', 'new_v_w_ff2': 'new_v', 'new_v_b_ff2': 'new_v', 'new_v_ln2_g': 'new_v', 'new_v_ln2_b': 'new_v'}


def _forward(args):
    return _fwd_reference(*[args[k] for k in FWD_PARAMS])


def _output_shape():
    def fwd():
        inp = _fwd_setup_inputs(0)
        return _fwd_reference(*[inp[k] for k in FWD_PARAMS])
    out = _jax.eval_shape(fwd)
    return out.shape, out.dtype

N_MICROBATCH = 1
ADAM_LR = 0.001
ADAM_B1 = 0.9
ADAM_B2 = 0.999
ADAM_EPS = 1e-08
ADAM_WD = 0.01
ADAM_STEP = 10
PER_EXAMPLE_BATCH_AXIS = {'x': 0, 'loss_target': 0}
SHARED_INPUTS = []
_WEIGHT_DTYPES = {'w_in': _jnp.float32, 'b_in': _jnp.float32, 'w_pool': _jnp.float32, 'pool_scale': _jnp.float32, 'sgu_ln_g': _jnp.float32, 'sgu_ln_b': _jnp.float32, 'sgu_w': _jnp.float32, 'sgu_b': _jnp.float32, 'conv_w': _jnp.float32, 'conv_b': _jnp.float32, 'conv_ln_g': _jnp.float32, 'conv_ln_b': _jnp.float32, 'w_out': _jnp.float32, 'b_out': _jnp.float32, 'ln1_g': _jnp.float32, 'ln1_b': _jnp.float32, 'w_ff1': _jnp.float32, 'b_ff1': _jnp.float32, 'w_ff2': _jnp.float32, 'b_ff2': _jnp.float32, 'ln2_g': _jnp.float32, 'ln2_b': _jnp.float32}
MOMENT_SCALE = {'w_in': 2.915403e-02, 'b_in': 5.037102e-02, 'w_pool': 3.905182e-02, 'pool_scale': 4.040600e-02, 'sgu_ln_g': 2.096268e-02, 'sgu_ln_b': 2.179081e-02, 'sgu_w': 2.073406e-02, 'sgu_b': 2.963329e-02, 'conv_w': 2.951940e-02, 'conv_b': 1.355345e-01, 'conv_ln_g': 5.693436e-02, 'conv_ln_b': 8.068928e-02, 'w_out': 8.730508e-02, 'b_out': 3.508267e-01, 'ln1_g': 8.344329e-01, 'ln1_b': 5.134768e-01, 'w_ff1': 3.023065e-02, 'b_ff1': 6.585925e-02, 'w_ff2': 1.711031e-01, 'b_ff2': 3.563376e-01, 'ln2_g': 2.272579e+01, 'ln2_b': 5.518353e+00}


def _to_microbatches(a, axis):
    t = _jnp.moveaxis(a, axis, 0)
    t = t.reshape((N_MICROBATCH, t.shape[0] // N_MICROBATCH) + t.shape[1:])
    return _jnp.moveaxis(t, 1, axis + 1)


def setup_inputs(seed: int = 0) -> dict:
    inp = _fwd_setup_inputs(seed)
    key = _jax.random.fold_in(_jax.random.key(seed), 7919)
    shape, _ = _output_shape()
    out = dict(inp)
    out["loss_target"] = _jax.random.normal(_jax.random.fold_in(key, 0), shape, _jnp.float32)
    for i, name in enumerate(TWIN_WEIGHTS):
        w = inp[name].astype(_jnp.float32)
        if MOMENT_SCALE is None:
            s = _jnp.sqrt(_jnp.mean(_jnp.square(w)) + 1e-30)
        else:
            s = MOMENT_SCALE[name]
        km, kv = _jax.random.split(_jax.random.fold_in(key, i + 1))
        out[name] = w
        out["m_" + name] = s * _jax.random.normal(km, w.shape, _jnp.float32)
        out["v_" + name] = (s * s) * _jax.random.uniform(kv, w.shape, _jnp.float32, 0.5, 1.5)
    if N_MICROBATCH > 1:
        for name, axis in PER_EXAMPLE_BATCH_AXIS.items():
            out[name] = _to_microbatches(out[name], axis)
    return {'x': out['x'], 'w_in': out['w_in'], 'b_in': out['b_in'], 'w_pool': out['w_pool'], 'pool_scale': out['pool_scale'], 'sgu_ln_g': out['sgu_ln_g'], 'sgu_ln_b': out['sgu_ln_b'], 'sgu_w': out['sgu_w'], 'sgu_b': out['sgu_b'], 'conv_w': out['conv_w'], 'conv_b': out['conv_b'], 'conv_ln_g': out['conv_ln_g'], 'conv_ln_b': out['conv_ln_b'], 'w_out': out['w_out'], 'b_out': out['b_out'], 'ln1_g': out['ln1_g'], 'ln1_b': out['ln1_b'], 'w_ff1': out['w_ff1'], 'b_ff1': out['b_ff1'], 'w_ff2': out['w_ff2'], 'b_ff2': out['b_ff2'], 'ln2_g': out['ln2_g'], 'ln2_b': out['ln2_b'], 'loss_target': out['loss_target'], 'm_w_in': out['m_w_in'], 'm_b_in': out['m_b_in'], 'm_w_pool': out['m_w_pool'], 'm_pool_scale': out['m_pool_scale'], 'm_sgu_ln_g': out['m_sgu_ln_g'], 'm_sgu_ln_b': out['m_sgu_ln_b'], 'm_sgu_w': out['m_sgu_w'], 'm_sgu_b': out['m_sgu_b'], 'm_conv_w': out['m_conv_w'], 'm_conv_b': out['m_conv_b'], 'm_conv_ln_g': out['m_conv_ln_g'], 'm_conv_ln_b': out['m_conv_ln_b'], 'm_w_out': out['m_w_out'], 'm_b_out': out['m_b_out'], 'm_ln1_g': out['m_ln1_g'], 'm_ln1_b': out['m_ln1_b'], 'm_w_ff1': out['m_w_ff1'], 'm_b_ff1': out['m_b_ff1'], 'm_w_ff2': out['m_w_ff2'], 'm_b_ff2': out['m_b_ff2'], 'm_ln2_g': out['m_ln2_g'], 'm_ln2_b': out['m_ln2_b'], 'v_w_in': out['v_w_in'], 'v_b_in': out['v_b_in'], 'v_w_pool': out['v_w_pool'], 'v_pool_scale': out['v_pool_scale'], 'v_sgu_ln_g': out['v_sgu_ln_g'], 'v_sgu_ln_b': out['v_sgu_ln_b'], 'v_sgu_w': out['v_sgu_w'], 'v_sgu_b': out['v_sgu_b'], 'v_conv_w': out['v_conv_w'], 'v_conv_b': out['v_conv_b'], 'v_conv_ln_g': out['v_conv_ln_g'], 'v_conv_ln_b': out['v_conv_ln_b'], 'v_w_out': out['v_w_out'], 'v_b_out': out['v_b_out'], 'v_ln1_g': out['v_ln1_g'], 'v_ln1_b': out['v_ln1_b'], 'v_w_ff1': out['v_w_ff1'], 'v_b_ff1': out['v_b_ff1'], 'v_w_ff2': out['v_w_ff2'], 'v_b_ff2': out['v_b_ff2'], 'v_ln2_g': out['v_ln2_g'], 'v_ln2_b': out['v_ln2_b']}


def _loss(weights, diff, rest, loss_target):
    with _jax.named_scope("forward"):
        args = {**rest, TWIN_DIFF_INPUT: diff, **{k: w.astype(_WEIGHT_DTYPES[k]) for k, w in weights.items()}}
        y = _forward(args)
    with _jax.named_scope("loss_head"):
        err = _jnp.square(y.astype(_jnp.float32) - loss_target)
        return 0.5 * _jnp.sum(_jnp.mean(err, axis=-1)) if err.ndim else 0.5 * err


def _adamw(w, g, m, v):
    m = ADAM_B1 * m + (1.0 - ADAM_B1) * g
    v = ADAM_B2 * v + (1.0 - ADAM_B2) * _jnp.square(g)
    m_hat = m / (1.0 - ADAM_B1 ** ADAM_STEP)
    v_hat = v / (1.0 - ADAM_B2 ** ADAM_STEP)
    delta = -ADAM_LR * (m_hat / (_jnp.sqrt(v_hat) + ADAM_EPS) + ADAM_WD * w)
    return delta, m, v


def reference(x, w_in, b_in, w_pool, pool_scale, sgu_ln_g, sgu_ln_b, sgu_w, sgu_b, conv_w, conv_b, conv_ln_g, conv_ln_b, w_out, b_out, ln1_g, ln1_b, w_ff1, b_ff1, w_ff2, b_ff2, ln2_g, ln2_b, loss_target, m_w_in, m_b_in, m_w_pool, m_pool_scale, m_sgu_ln_g, m_sgu_ln_b, m_sgu_w, m_sgu_b, m_conv_w, m_conv_b, m_conv_ln_g, m_conv_ln_b, m_w_out, m_b_out, m_ln1_g, m_ln1_b, m_w_ff1, m_b_ff1, m_w_ff2, m_b_ff2, m_ln2_g, m_ln2_b, v_w_in, v_b_in, v_w_pool, v_pool_scale, v_sgu_ln_g, v_sgu_ln_b, v_sgu_w, v_sgu_b, v_conv_w, v_conv_b, v_conv_ln_g, v_conv_ln_b, v_w_out, v_b_out, v_ln1_g, v_ln1_b, v_w_ff1, v_b_ff1, v_w_ff2, v_b_ff2, v_ln2_g, v_ln2_b):
    given = dict(x=x, w_in=w_in, b_in=b_in, w_pool=w_pool, pool_scale=pool_scale, sgu_ln_g=sgu_ln_g, sgu_ln_b=sgu_ln_b, sgu_w=sgu_w, sgu_b=sgu_b, conv_w=conv_w, conv_b=conv_b, conv_ln_g=conv_ln_g, conv_ln_b=conv_ln_b, w_out=w_out, b_out=b_out, ln1_g=ln1_g, ln1_b=ln1_b, w_ff1=w_ff1, b_ff1=b_ff1, w_ff2=w_ff2, b_ff2=b_ff2, ln2_g=ln2_g, ln2_b=ln2_b, loss_target=loss_target, m_w_in=m_w_in, m_b_in=m_b_in, m_w_pool=m_w_pool, m_pool_scale=m_pool_scale, m_sgu_ln_g=m_sgu_ln_g, m_sgu_ln_b=m_sgu_ln_b, m_sgu_w=m_sgu_w, m_sgu_b=m_sgu_b, m_conv_w=m_conv_w, m_conv_b=m_conv_b, m_conv_ln_g=m_conv_ln_g, m_conv_ln_b=m_conv_ln_b, m_w_out=m_w_out, m_b_out=m_b_out, m_ln1_g=m_ln1_g, m_ln1_b=m_ln1_b, m_w_ff1=m_w_ff1, m_b_ff1=m_b_ff1, m_w_ff2=m_w_ff2, m_b_ff2=m_b_ff2, m_ln2_g=m_ln2_g, m_ln2_b=m_ln2_b, v_w_in=v_w_in, v_b_in=v_b_in, v_w_pool=v_w_pool, v_pool_scale=v_pool_scale, v_sgu_ln_g=v_sgu_ln_g, v_sgu_ln_b=v_sgu_ln_b, v_sgu_w=v_sgu_w, v_sgu_b=v_sgu_b, v_conv_w=v_conv_w, v_conv_b=v_conv_b, v_conv_ln_g=v_conv_ln_g, v_conv_ln_b=v_conv_ln_b, v_w_out=v_w_out, v_b_out=v_b_out, v_ln1_g=v_ln1_g, v_ln1_b=v_ln1_b, v_w_ff1=v_w_ff1, v_b_ff1=v_b_ff1, v_w_ff2=v_w_ff2, v_b_ff2=v_b_ff2, v_ln2_g=v_ln2_g, v_ln2_b=v_ln2_b)
    weights = {n: given[n] for n in TWIN_WEIGHTS}
    shared = {n: given[n] for n in SHARED_INPUTS}
    per_example = {n: given[n] for n in ['x']}
    grad_fn = _jax.value_and_grad(_loss, argnums=(0, 1))

    def one_microbatch(ex, loss_target):
        ex = dict(ex)
        diff = ex.pop(TWIN_DIFF_INPUT)
        return grad_fn(weights, diff, {**shared, **ex}, loss_target)

    if N_MICROBATCH == 1:
        loss, (grad_w, grad_x) = one_microbatch(per_example, given["loss_target"])
    else:
        def body(carry, xs):
            loss_sum, grad_sum = carry
            l_k, (gw_k, gx_k) = one_microbatch(xs[0], xs[1])
            with _jax.named_scope("update"):
                return (loss_sum + l_k, _jax.tree.map(_jnp.add, grad_sum, gw_k)), gx_k

        init = (_jnp.zeros((), _jnp.float32), _jax.tree.map(_jnp.zeros_like, weights))
        (loss, grad_w), grad_x = _jax.lax.scan(body, init, (per_example, given["loss_target"]))
    with _jax.named_scope("update"):
        delta_w, new_m, new_v = {}, {}, {}
        for n in TWIN_WEIGHTS:
            delta_w[n], new_m[n], new_v[n] = _adamw(weights[n], grad_w[n], given["m_" + n], given["v_" + n])
    return (loss, grad_x, *[grad_w[n] for n in TWIN_WEIGHTS], *[delta_w[n] for n in TWIN_WEIGHTS],
            *[new_m[n] for n in TWIN_WEIGHTS], *[new_v[n] for n in TWIN_WEIGHTS])
```

```python
import functools
import math

import jax
import jax.numpy as jnp
from jax import lax
from jax.experimental import pallas as pl
from jax.experimental.pallas import tpu as pltpu

F32 = jnp.float32
BF16 = jnp.bfloat16
MESH = pl.DeviceIdType.MESH

DEPTH = 2
POOL_WINDOWS = (2, 4, 8, 16)
POOL_WIDTH = 512
GROUP = 128
SGU_WIDTH = 768
SGU_HEADS = 6
CHUNK = 128
CONV_WIDTH = 768
CONV_KERNEL = 31
CONV_ROWS = 32
HALO = 32
COL_B = POOL_WIDTH
COL_C = POOL_WIDTH + 2 * SGU_WIDTH
IN_WIDTH = COL_C + 2 * CONV_WIDTH
ALPHA = (2 * DEPTH) ** 0.25
LN_EPS = 1e-5
ADAM_LR = 0.001
ADAM_B1 = 0.9
ADAM_B2 = 0.999
ADAM_EPS = 1e-08
ADAM_WD = 0.01
ADAM_STEP = 10
GELU_C = math.sqrt(2.0 / math.pi)
GELU_A = 0.044715

V7X_VMEM_BYTES = 64 * 2 ** 20
VMEM_LIMIT = 56 * 2 ** 20

BM_MM = 1024
BN_MM = 1024
BM_LN = 512
BK_MM = 1024
BS_TN = 1024
TS_MIX = 256
RB_CONV = 64
BR_EW = 512
EW_BLOCK_ELEMS = 2 ** 18


def _tile(n, pref):
    t = min(n, pref)
    assert n % t == 0, (n, pref)
    return t


def _params(n_grid):
    return pltpu.CompilerParams(dimension_semantics=("arbitrary",) * n_grid, vmem_limit_bytes=VMEM_LIMIT)


def _sigmoid(x):
    return 1.0 / (1.0 + jnp.exp(-x))


def _gelu(x):
    return 0.5 * x * (1.0 + jnp.tanh(GELU_C * (x + GELU_A * x * x * x)))


def _gelu_grad(x):
    t = jnp.tanh(GELU_C * (x + GELU_A * x * x * x))
    return 0.5 * (1.0 + t) + 0.5 * x * (1.0 - t * t) * GELU_C * (1.0 + 3.0 * GELU_A * x * x)


def _ln_stats(r):
    mu = jnp.mean(r, axis=-1, keepdims=True)
    xc = r - mu
    var = jnp.mean(xc * xc, axis=-1, keepdims=True)
    rstd = lax.rsqrt(var + LN_EPS)
    return xc * rstd, rstd


def _ln_bwd(dy, xhat, rstd, g):
    dxh = dy * g
    m1 = jnp.mean(dxh, axis=-1, keepdims=True)
    m2 = jnp.mean(dxh * xhat, axis=-1, keepdims=True)
    return rstd * (dxh - m1 - xhat * m2)


def _colsum(x):
    return jnp.sum(x, axis=0, keepdims=True)


def _accumulate(ref, val, first):
    @pl.when(first)
    def _():
        ref[...] = val

    @pl.when(jnp.logical_not(first))
    def _():
        ref[...] += val


def _matmul(name, grid, a, a_spec, b, b_spec, *, nt, extras, outs, epilogue, acc_shape=None):
    nk = grid[2]
    ne, no = len(extras), len(outs)
    dims = (((1,), (1,)), ((), ())) if nt else (((1,), (0,)), ((), ()))

    def body(*refs):
        a_ref, b_ref = refs[0], refs[1]
        ex = refs[2:2 + ne]
        out_refs = refs[2 + ne:2 + ne + no]
        ids = (pl.program_id(0), pl.program_id(1), pl.program_id(2))
        part = lax.dot_general(a_ref[...], b_ref[...], dims, preferred_element_type=F32)
        if nk == 1:
            epilogue(part, ex, out_refs, ids)
        else:
            acc_ref = refs[2 + ne + no]
            k = ids[2]

            @pl.when(k == 0)
            def _():
                acc_ref[...] = part

            @pl.when(k > 0)
            def _():
                acc_ref[...] += part

            @pl.when(k == nk - 1)
            def _():
                epilogue(acc_ref[...], ex, out_refs, ids)

    return pl.pallas_call(
        body,
        name=name,
        grid=grid,
        in_specs=[a_spec, b_spec] + [s for _, s in extras],
        out_specs=[s for _, s in outs],
        out_shape=[o for o, _ in outs],
        scratch_shapes=[pltpu.VMEM(acc_shape, F32)] if nk > 1 else [],
        compiler_params=_params(3),
    )(a, b, *[e for e, _ in extras])


def _matmul_tn(name, a, b, n_shards, shard_rows, shard_cols, row_sharded):
    s_len, ka = a.shape
    n = b.shape[1]
    assert (n_shards * shard_rows, shard_cols) == (ka, n) if row_sharded else (shard_rows, n_shards * shard_cols) == (ka, n)
    bs = _tile(s_len, BS_TN)
    bka = _tile(shard_rows, 2048)
    bn = _tile(shard_cols, BN_MM) if shard_cols % BN_MM == 0 else shard_cols
    ni, nj, ns = ka // bka, n // bn, s_len // bs
    per_shard_i = shard_rows // bka
    per_shard_j = shard_cols // bn

    if row_sharded:
        out_map = lambda i, j, s: (i // per_shard_i, i % per_shard_i, j)
    else:
        out_map = lambda i, j, s: (j // per_shard_j, i, j % per_shard_j)

    def body(a_ref, b_ref, o_ref):
        s = pl.program_id(2)
        part = lax.dot_general(a_ref[...], b_ref[...], (((0,), (0,)), ((), ())), preferred_element_type=F32)
        _accumulate(o_ref, part, s == 0)

    return pl.pallas_call(
        body,
        name=name,
        grid=(ni, nj, ns),
        in_specs=[pl.BlockSpec((bs, bka), lambda i, j, s: (s, i)), pl.BlockSpec((bs, bn), lambda i, j, s: (s, j))],
        out_specs=pl.BlockSpec((None, bka, bn), out_map),
        out_shape=jax.ShapeDtypeStruct((n_shards, shard_rows, shard_cols), F32),
        compiler_params=_params(3),
    )(a, b)


def _row(v):
    return v.reshape(1, -1)


def _proj(name, xb, w_g, b_in):
    s_len, d = xb.shape
    ncs = w_g.shape[2]
    bm = _tile(s_len, BM_MM)

    def epilogue(acc, ex, outs, ids):
        outs[0][...] = acc + ex[0][...]

    return _matmul(
        name, (s_len // bm, 4, 1),
        xb, pl.BlockSpec((bm, d), lambda i, j, k: (i, 0)),
        w_g, pl.BlockSpec((None, d, ncs), lambda i, j, k: (j, 0, 0)),
        nt=False,
        extras=[(_row(b_in), pl.BlockSpec((1, ncs), lambda i, j, k: (0, j)))],
        outs=[(jax.ShapeDtypeStruct((s_len, 4 * ncs), F32), pl.BlockSpec((bm, ncs), lambda i, j, k: (i, j)))],
        epilogue=epilogue,
    )[0]


def _mix_out(name, mixed, w_out_full, b_out, x0, g1, b1):
    s_len, d = mixed.shape
    bm = _tile(s_len, BM_LN // 2)
    row = pl.BlockSpec((1, d), lambda i, j, k: (0, 0))
    blk = pl.BlockSpec((bm, d), lambda i, j, k: (i, 0))

    def epilogue(acc, ex, outs, ids):
        r1 = ALPHA * ex[1][...] + (acc + ex[0][...])
        outs[0][...] = r1
        xhat, _ = _ln_stats(r1)
        outs[1][...] = (xhat * ex[2][...] + ex[3][...]).astype(BF16)

    return _matmul(
        name, (s_len // bm, 1, 1),
        mixed, blk,
        w_out_full, pl.BlockSpec((d, d), lambda i, j, k: (0, 0)),
        nt=False,
        extras=[(_row(b_out), row), (x0, blk), (_row(g1), row), (_row(b1), row)],
        outs=[(jax.ShapeDtypeStruct((s_len, d), F32), blk), (jax.ShapeDtypeStruct((s_len, d), BF16), blk)],
        epilogue=epilogue,
    )


def _ff1(name, x1b, w_g, b_ff1):
    s_len, d = x1b.shape
    ncs = w_g.shape[2]
    bm = _tile(s_len, BM_MM)
    bn = _tile(ncs, BN_MM)
    per = ncs // bn
    blk = pl.BlockSpec((bm, bn), lambda i, j, k: (i, j))

    def epilogue(acc, ex, outs, ids):
        zr = jnp.maximum(acc + ex[0][...], 0.0)
        outs[0][...] = (zr * zr).astype(BF16)
        outs[1][...] = zr.astype(BF16)

    shape = jax.ShapeDtypeStruct((s_len, 4 * ncs), BF16)
    return _matmul(
        name, (s_len // bm, 4 * per, 1),
        x1b, pl.BlockSpec((bm, d), lambda i, j, k: (i, 0)),
        w_g, pl.BlockSpec((None, d, bn), lambda i, j, k: (j // per, 0, j % per)),
        nt=False,
        extras=[(_row(b_ff1), pl.BlockSpec((1, bn), lambda i, j, k: (0, j)))],
        outs=[(shape, blk), (shape, blk)],
        epilogue=epilogue,
    )


def _ff2(name, hf, w2_full, b_ff2, r1, g1, b1, g2, b2):
    s_len, dff = hf.shape
    d = w2_full.shape[1]
    bm = _tile(s_len, BM_LN)
    bk = _tile(dff, BK_MM)
    row = pl.BlockSpec((1, d), lambda i, j, k: (0, 0))
    blk = pl.BlockSpec((bm, d), lambda i, j, k: (i, 0))

    def epilogue(acc, ex, outs, ids):
        xhat1, _ = _ln_stats(ex[1][...])
        r2 = ALPHA * (xhat1 * ex[2][...] + ex[3][...]) + (acc + ex[0][...])
        outs[0][...] = r2
        xhat2, _ = _ln_stats(r2)
        x2 = xhat2 * ex[4][...] + ex[5][...]
        outs[1][...] = x2
        outs[2][...] = x2.astype(BF16)

    return _matmul(
        name, (s_len // bm, 1, dff // bk),
        hf, pl.BlockSpec((bm, bk), lambda i, j, k: (i, k)),
        w2_full, pl.BlockSpec((bk, d), lambda i, j, k: (k, 0)),
        nt=False,
        extras=[(_row(b_ff2), row), (r1, blk), (_row(g1), row), (_row(b1), row), (_row(g2), row), (_row(b2), row)],
        outs=[(jax.ShapeDtypeStruct((s_len, d), F32), blk), (jax.ShapeDtypeStruct((s_len, d), F32), blk),
              (jax.ShapeDtypeStruct((s_len, d), BF16), blk)],
        epilogue=epilogue,
        acc_shape=(bm, d),
    )


def _dff_hidden(name, dr2b, w2_full, zr):
    s_len, d = dr2b.shape
    dff = w2_full.shape[0]
    bm = _tile(s_len, BM_MM)
    bn = _tile(dff, BN_MM)

    def epilogue(acc, ex, outs, ids):
        dz = acc * (2.0 * ex[0][...].astype(F32))
        outs[0][...] = dz.astype(BF16)
        _accumulate(outs[1], _colsum(dz), ids[1] == 0)

    return _matmul(
        name, (dff // bn, s_len // bm, 1),
        dr2b, pl.BlockSpec((bm, d), lambda j, i, k: (i, 0)),
        w2_full, pl.BlockSpec((bn, d), lambda j, i, k: (j, 0)),
        nt=True,
        extras=[(zr, pl.BlockSpec((bm, bn), lambda j, i, k: (i, j)))],
        outs=[(jax.ShapeDtypeStruct((s_len, dff), BF16), pl.BlockSpec((bm, bn), lambda j, i, k: (i, j))),
              (jax.ShapeDtypeStruct((1, dff), F32), pl.BlockSpec((1, bn), lambda j, i, k: (0, j)))],
        epilogue=epilogue,
    )


def _dx_sharded(name, dyb, w_g, resid):
    s_len = dyb.shape[0]
    d, ncs = w_g.shape[1], w_g.shape[2]
    bm = _tile(s_len, BM_LN)
    bk = _tile(ncs, BK_MM) if ncs % BK_MM == 0 else ncs
    per = ncs // bk
    blk = pl.BlockSpec((bm, d), lambda i, j, k: (i, 0))

    def epilogue(acc, ex, outs, ids):
        outs[0][...] = acc + ALPHA * ex[0][...]

    return _matmul(
        name, (s_len // bm, 1, 4 * per),
        dyb, pl.BlockSpec((bm, bk), lambda i, j, k: (i, k)),
        w_g, pl.BlockSpec((None, d, bk), lambda i, j, k: (k // per, 0, k % per)),
        nt=True,
        extras=[(resid, blk)],
        outs=[(jax.ShapeDtypeStruct((s_len, d), F32), blk)],
        epilogue=epilogue,
        acc_shape=(bm, d),
    )[0]


def _dmixed(name, dr1b, w_out_full):
    s_len, d = dr1b.shape
    bm = _tile(s_len, BM_LN)
    blk = pl.BlockSpec((bm, d), lambda i, j, k: (i, 0))

    def epilogue(acc, ex, outs, ids):
        outs[0][...] = acc

    return _matmul(
        name, (s_len // bm, 1, 1),
        dr1b, blk,
        w_out_full, pl.BlockSpec((d, d), lambda i, j, k: (0, 0)),
        nt=True, extras=[],
        outs=[(jax.ShapeDtypeStruct((s_len, d), F32), blk)],
        epilogue=epilogue,
    )[0]


def _ln_backward(name, r, g, *, dy=None, x=None, target=None):
    s_len, d = r.shape
    bm = _tile(s_len, BM_LN // 2)
    from_loss = dy is None
    blk = pl.BlockSpec((bm, d), lambda i: (i, 0))
    row = pl.BlockSpec((1, d), lambda i: (0, 0))

    def body(*refs):
        i = pl.program_id(0)
        if from_loss:
            x_ref, t_ref, r_ref, g_ref, dr_ref, drb_ref, dg_ref, db_ref, dbias_ref, loss_ref = refs
            diff = x_ref[...] - t_ref[...]
            dyv = diff * (1.0 / d)
            part = 0.5 * jnp.sum(jnp.sum(diff * diff, axis=1, keepdims=True) * (1.0 / d), axis=0, keepdims=True)
            _accumulate(loss_ref, jnp.broadcast_to(part, loss_ref.shape), i == 0)
        else:
            dy_ref, r_ref, g_ref, dr_ref, drb_ref, dg_ref, db_ref, dbias_ref = refs
            dyv = dy_ref[...]
        xhat, rstd = _ln_stats(r_ref[...])
        dr = _ln_bwd(dyv, xhat, rstd, g_ref[...])
        dr_ref[...] = dr
        drb_ref[...] = dr.astype(BF16)
        _accumulate(dg_ref, _colsum(dyv * xhat), i == 0)
        _accumulate(db_ref, _colsum(dyv), i == 0)
        _accumulate(dbias_ref, _colsum(dr), i == 0)

    ins = ([x, target] if from_loss else [dy]) + [r, _row(g)]
    in_specs = ([blk, blk] if from_loss else [blk]) + [blk, row]
    out_shape = [jax.ShapeDtypeStruct((s_len, d), F32), jax.ShapeDtypeStruct((s_len, d), BF16)] + \
                [jax.ShapeDtypeStruct((1, d), F32)] * 3
    out_specs = [blk, blk, row, row, row]
    if from_loss:
        out_shape.append(jax.ShapeDtypeStruct((8, 128), F32))
        out_specs.append(pl.BlockSpec((8, 128), lambda i: (0, 0)))
    return pl.pallas_call(body, name=name, grid=(s_len // bm,), in_specs=in_specs, out_specs=out_specs,
                          out_shape=out_shape, compiler_params=_params(1))(*ins)


def _rowwise(name, fn, ins, out_dtypes, rows_pref=BR_EW):
    r, c = ins[0].shape
    br = _tile(r, rows_pref)
    blk = pl.BlockSpec((br, c), lambda i: (i, 0))

    def body(*refs):
        res = fn(*[ref[...] for ref in refs[:len(ins)]])
        for o_ref, v in zip(refs[len(ins):], res):
            o_ref[...] = v.astype(o_ref.dtype)

    return pl.pallas_call(body, name=name, grid=(r // br,), in_specs=[blk] * len(ins),
                          out_specs=[blk] * len(out_dtypes),
                          out_shape=[jax.ShapeDtypeStruct((r, c), dt) for dt in out_dtypes],
                          compiler_params=_params(1))(*ins)


def _adamw_math(w, g, m, v):
    m = ADAM_B1 * m + (1.0 - ADAM_B1) * g
    v = ADAM_B2 * v + (1.0 - ADAM_B2) * (g * g)
    m_hat = m / (1.0 - ADAM_B1 ** ADAM_STEP)
    v_hat = v / (1.0 - ADAM_B2 ** ADAM_STEP)
    delta = -ADAM_LR * (m_hat / (jnp.sqrt(v_hat) + ADAM_EPS) + ADAM_WD * w)
    return delta, m, v


def _adamw(name, w, g, m, v):
    shape = w.shape
    c = shape[-1]
    flat = [a.reshape(-1, c) for a in (w, g, m, v)]
    rows = flat[0].shape[0]
    pref = max(8, 2 ** int(math.log2(EW_BLOCK_ELEMS // c)))
    res = _rowwise(name, _adamw_math, flat, [F32, F32, F32], rows_pref=pref if rows % pref == 0 else rows)
    return tuple(a.reshape(shape) for a in res)


def _pool_means(ext_ref, ts, tile_index):
    t_glob = tile_index * ts + lax.broadcasted_iota(jnp.int32, (ts, GROUP), 0)
    qs = []
    for g, win in enumerate(POOL_WINDOWS):
        cols = pl.ds(g * GROUP, GROUP)
        cur = ext_ref[pl.ds(16, ts), cols]
        acc = cur
        for j in range(1, win):
            acc = acc + ext_ref[pl.ds(16 - j, ts), cols]
        cnt = jnp.minimum(t_glob + 1, win).astype(F32)
        qs.append(acc / cnt - cur)
    return qs


def _masked_sgu_w(w_ref, h):
    r = lax.broadcasted_iota(jnp.int32, (CHUNK, CHUNK), 0)
    c = lax.broadcasted_iota(jnp.int32, (CHUNK, CHUNK), 1)
    return jnp.where(r >= c, w_ref[h], 0.0)


def _conv_taps(acc_init, w_ref, src_ref, ts, base):
    nrb = ts // RB_CONV
    blocks = []
    for cg in range(CONV_WIDTH // GROUP):
        cols = pl.ds(cg * GROUP, GROUP)
        accs = [acc_init(cg) for _ in range(nrb)]
        for k in range(CONV_KERNEL):
            wk = jnp.broadcast_to(w_ref[pl.ds(k, 1), cols], (RB_CONV, GROUP))
            for rb in range(nrb):
                accs[rb] = accs[rb] + wk * src_ref[pl.ds(rb * RB_CONV + base(k), RB_CONV), cols]
        for rb in range(nrb):
            blocks.append((rb * RB_CONV, cg * GROUP, accs[rb]))
    return blocks


def _mix_forward(name, p, lw):
    s_len = p.shape[0]
    ts = _tile(s_len, TS_MIX)
    per_halo = ts // HALO
    d = POOL_WIDTH + SGU_WIDTH + CONV_WIDTH

    def body(p_ref, ph_ref, wp_ref, ps_ref, slg_ref, slb_ref, ws_ref, sb_ref, cw_ref, cb_ref, clg_ref, clb_ref,
             mixed_ref, cv_ref, pool_ext, hh_ext):
        i = pl.program_id(0)
        keep = (i > 0).astype(F32)
        pool_ext[pl.ds(0, 16), :] = ph_ref[pl.ds(16, 16), pl.ds(0, POOL_WIDTH)] * keep
        pool_ext[pl.ds(16, ts), :] = p_ref[:, pl.ds(0, POOL_WIDTH)]
        qs = _pool_means(pool_ext, ts, i)
        for g in range(len(POOL_WINDOWS)):
            cols = pl.ds(g * GROUP, GROUP)
            e = jnp.dot(qs[g].astype(BF16), wp_ref[g].astype(BF16), preferred_element_type=F32)
            mixed_ref[:, cols] = (e * ps_ref[:, cols]).astype(BF16)
        uv = _gelu(p_ref[:, pl.ds(COL_B, 2 * SGU_WIDTH)])
        u = uv[:, :SGU_WIDTH]
        vhat, _ = _ln_stats(uv[:, SGU_WIDTH:])
        vn = (vhat * slg_ref[...] + slb_ref[...]).astype(BF16)
        for h in range(SGU_HEADS):
            wm = _masked_sgu_w(ws_ref, h).astype(BF16)
            for n in range(ts // CHUNK):
                rows = slice(n * CHUNK, (n + 1) * CHUNK)
                cols = slice(h * GROUP, (h + 1) * GROUP)
                mx = jnp.dot(wm, vn[rows, cols], preferred_element_type=F32) + sb_ref[h]
                mixed_ref[pl.ds(n * CHUNK, CHUNK), pl.ds(POOL_WIDTH + h * GROUP, GROUP)] = (u[rows, cols] * mx).astype(BF16)
        hh_ext[pl.ds(0, HALO), :] = (ph_ref[:, pl.ds(COL_C, CONV_WIDTH)]
                                     * _sigmoid(ph_ref[:, pl.ds(COL_C + CONV_WIDTH, CONV_WIDTH)])) * keep
        hh_ext[pl.ds(HALO, ts), :] = p_ref[:, pl.ds(COL_C, CONV_WIDTH)] * _sigmoid(p_ref[:, pl.ds(COL_C + CONV_WIDTH, CONV_WIDTH)])
        init = lambda cg: jnp.broadcast_to(cb_ref[:, pl.ds(cg * GROUP, GROUP)], (RB_CONV, GROUP))
        for r0, c0, blk in _conv_taps(init, cw_ref, hh_ext, ts, lambda k: k + HALO - (CONV_KERNEL - 1)):
            cv_ref[pl.ds(r0, RB_CONV), pl.ds(c0, GROUP)] = blk
        cvhat, _ = _ln_stats(cv_ref[...])
        cn = cvhat * clg_ref[...] + clb_ref[...]
        mixed_ref[:, pl.ds(POOL_WIDTH + SGU_WIDTH, CONV_WIDTH)] = (cn * _sigmoid(cn)).astype(BF16)

    full = lambda a: pl.BlockSpec(a.shape, lambda i: (0,) * a.ndim)
    weights = [lw["w_pool"], _row(lw["pool_scale"]), _row(lw["sgu_ln_g"]), _row(lw["sgu_ln_b"]), lw["sgu_w"],
               lw["sgu_bias_tile"], lw["conv_w_full"], _row(lw["conv_b"]), _row(lw["conv_ln_g"]), _row(lw["conv_ln_b"])]
    return pl.pallas_call(
        body, name=name, grid=(s_len // ts,),
        in_specs=[pl.BlockSpec((ts, IN_WIDTH), lambda i: (i, 0)),
                  pl.BlockSpec((HALO, IN_WIDTH), lambda i: (jnp.maximum(i * per_halo - 1, 0), 0))] + [full(a) for a in weights],
        out_specs=[pl.BlockSpec((ts, d), lambda i: (i, 0)), pl.BlockSpec((ts, CONV_WIDTH), lambda i: (i, 0))],
        out_shape=[jax.ShapeDtypeStruct((s_len, d), BF16), jax.ShapeDtypeStruct((s_len, CONV_WIDTH), F32)],
        scratch_shapes=[pltpu.VMEM((16 + ts, POOL_WIDTH), F32), pltpu.VMEM((HALO + ts, CONV_WIDTH), F32)],
        compiler_params=_params(1),
    )(p, p, *weights)


def _mix_backward(name, p, cv, dmixed, lw):
    s_len = p.shape[0]
    ts = _tile(s_len, TS_MIX)
    nt = s_len // ts
    per_halo = ts // HALO
    d = POOL_WIDTH + SGU_WIDTH + CONV_WIDTH
    nch = ts // CHUNK
    col_yc = POOL_WIDTH + SGU_WIDTH

    def body(p_ref, ph_ref, cv_ref, cvn_ref, dm_ref, dmn_ref, wp_ref, ps_ref, slg_ref, slb_ref, ws_ref, wst_ref, sb_ref,
             cw_ref, cwr_ref, clg_ref, clb_ref,
             dp_ref, dbin_ref, dwp_ref, dps_ref, dslg_ref, dslb_ref, dws_ref, dsb_ref, dcw_ref, dcb_ref, dclg_ref, dclb_ref,
             pool_ext, dq_ext, hh_ext, dcv_ext, dcw_acc):
        i = pl.program_id(0)
        first = i == 0
        keep_prev = (i > 0).astype(F32)
        keep_next = (i < nt - 1).astype(F32)

        pool_ext[pl.ds(0, 16), :] = ph_ref[pl.ds(16, 16), pl.ds(0, POOL_WIDTH)] * keep_prev
        pool_ext[pl.ds(16, ts), :] = p_ref[:, pl.ds(0, POOL_WIDTH)]
        qs = _pool_means(pool_ext, ts, i)
        t_ext = i * ts + lax.broadcasted_iota(jnp.int32, (ts + HALO, GROUP), 0)
        for g, win in enumerate(POOL_WINDOWS):
            cols = pl.ds(g * GROUP, GROUP)
            wpb = wp_ref[g].astype(BF16)
            qb = qs[g].astype(BF16)
            de = dm_ref[:, cols] * ps_ref[:, cols]
            e = jnp.dot(qb, wpb, preferred_element_type=F32)
            _accumulate(dps_ref.at[:, cols], _colsum(dm_ref[:, cols] * e), first)
            deb = de.astype(BF16)
            _accumulate(dwp_ref.at[g], lax.dot_general(qb, deb, (((0,), (0,)), ((), ())), preferred_element_type=F32), first)
            de_next = (dmn_ref[:, cols] * ps_ref[:, cols] * keep_next).astype(BF16)
            de_all = jnp.concatenate([deb, de_next], axis=0)
            dq = lax.dot_general(de_all, wpb, (((1,), (1,)), ((), ())), preferred_element_type=F32)
            inv = 1.0 / jnp.minimum(t_ext + 1, win).astype(F32)
            dq_ext[:, cols] = dq * inv
            acc = dq_ext[pl.ds(0, ts), cols]
            for j in range(1, win):
                acc = acc + dq_ext[pl.ds(j, ts), cols]
            dpa = acc - dq[:ts]
            dp_ref[:, cols] = dpa.astype(BF16)
            _accumulate(dbin_ref.at[:, cols], _colsum(dpa), first)

        pb = p_ref[:, pl.ds(COL_B, 2 * SGU_WIDTH)]
        uv = _gelu(pb)
        u = uv[:, :SGU_WIDTH]
        vhat, vrstd = _ln_stats(uv[:, SGU_WIDTH:])
        vn = (vhat * slg_ref[...] + slb_ref[...]).astype(BF16)
        dyb = dm_ref[:, pl.ds(POOL_WIDTH, SGU_WIDTH)]
        dmix = dyb * u
        dmixb = dmix.astype(BF16)
        du_parts, dvn_parts = [], []
        for h in range(SGU_HEADS):
            cols = slice(h * GROUP, (h + 1) * GROUP)
            wm = _masked_sgu_w(ws_ref, h).astype(BF16)
            wmt = _masked_sgu_w_t(wst_ref, h).astype(BF16)
            dws_h = jnp.zeros((CHUNK, CHUNK), F32)
            dsb_h = jnp.zeros((CHUNK, GROUP), F32)
            du_rows, dvn_rows = [], []
            for n in range(nch):
                rows = slice(n * CHUNK, (n + 1) * CHUNK)
                mx = jnp.dot(wm, vn[rows, cols], preferred_element_type=F32) + sb_ref[h]
                du_rows.append(dyb[rows, cols] * mx)
                dws_h = dws_h + lax.dot_general(dmixb[rows, cols], vn[rows, cols], (((1,), (1,)), ((), ())),
                                                preferred_element_type=F32)
                dsb_h = dsb_h + dmix[rows, cols]
                dvn_rows.append(jnp.dot(wmt, dmixb[rows, cols], preferred_element_type=F32))
            r = lax.broadcasted_iota(jnp.int32, (CHUNK, CHUNK), 0)
            c = lax.broadcasted_iota(jnp.int32, (CHUNK, CHUNK), 1)
            _accumulate(dws_ref.at[h], jnp.where(r >= c, dws_h, 0.0), first)
            _accumulate(dsb_ref.at[h], jnp.broadcast_to(jnp.sum(dsb_h, axis=1, keepdims=True), (CHUNK, GROUP)), first)
            du_parts.append(jnp.concatenate(du_rows, axis=0))
            dvn_parts.append(jnp.concatenate(dvn_rows, axis=0))
        du = jnp.concatenate(du_parts, axis=1)
        dvn = jnp.concatenate(dvn_parts, axis=1)
        _accumulate(dslg_ref, _colsum(dvn * vhat), first)
        _accumulate(dslb_ref, _colsum(dvn), first)
        dv = _ln_bwd(dvn, vhat, vrstd, slg_ref[...])
        dpb = jnp.concatenate([du, dv], axis=1) * _gelu_grad(pb)
        dp_ref[:, pl.ds(COL_B, 2 * SGU_WIDTH)] = dpb.astype(BF16)
        _accumulate(dbin_ref.at[:, pl.ds(COL_B, 2 * SGU_WIDTH)], _colsum(dpb), first)

        a_main = p_ref[:, pl.ds(COL_C, CONV_WIDTH)]
        sg_main = _sigmoid(p_ref[:, pl.ds(COL_C + CONV_WIDTH, CONV_WIDTH)])
        hh_ext[pl.ds(0, HALO), :] = (ph_ref[:, pl.ds(COL_C, CONV_WIDTH)]
                                     * _sigmoid(ph_ref[:, pl.ds(COL_C + CONV_WIDTH, CONV_WIDTH)])) * keep_prev
        hh_ext[pl.ds(HALO, ts), :] = a_main * sg_main

        def conv_ln_backward(cv_v, dyc_v):
            cvhat, crstd = _ln_stats(cv_v)
            cn = cvhat * clg_ref[...] + clb_ref[...]
            s = _sigmoid(cn)
            dcn = dyc_v * (s * (1.0 + cn * (1.0 - s)))
            return _ln_bwd(dcn, cvhat, crstd, clg_ref[...]), dcn, cvhat

        dcv, dcn, cvhat = conv_ln_backward(cv_ref[...], dm_ref[:, pl.ds(col_yc, CONV_WIDTH)])
        _accumulate(dclg_ref, _colsum(dcn * cvhat), first)
        _accumulate(dclb_ref, _colsum(dcn), first)
        _accumulate(dcb_ref, _colsum(dcv), first)
        dcv_next, _, _ = conv_ln_backward(cvn_ref[...], dmn_ref[:, pl.ds(col_yc, CONV_WIDTH)])
        dcv_ext[pl.ds(0, ts), :] = dcv
        dcv_ext[pl.ds(ts, HALO), :] = dcv_next * keep_next

        @pl.when(first)
        def _():
            dcw_acc[...] = jnp.zeros_like(dcw_acc)

        nrb = ts // RB_CONV
        for cg in range(CONV_WIDTH // GROUP):
            cols = pl.ds(cg * GROUP, GROUP)
            dblk = [dcv_ext[pl.ds(rb * RB_CONV, RB_CONV), cols] for rb in range(nrb)]
            for k in range(CONV_KERNEL):
                part = jnp.zeros((8, GROUP), F32)
                for rb in range(nrb):
                    prod = dblk[rb] * hh_ext[pl.ds(rb * RB_CONV + k + HALO - (CONV_KERNEL - 1), RB_CONV), cols]
                    part = part + jnp.sum(prod.reshape(RB_CONV // 8, 8, GROUP), axis=0)
                dcw_acc[k, :, cols] += part

        @pl.when(i == nt - 1)
        def _():
            dcw_ref[...] = jnp.sum(dcw_acc[...], axis=1)

        zero = lambda cg: jnp.zeros((RB_CONV, GROUP), F32)
        for r0, c0, blk in _conv_taps(zero, cwr_ref, dcv_ext, ts, lambda k: k):
            rows, cols = pl.ds(r0, RB_CONV), pl.ds(c0, GROUP)
            a_blk = p_ref[rows, pl.ds(COL_C + c0, GROUP)]
            s_blk = _sigmoid(p_ref[rows, pl.ds(COL_C + CONV_WIDTH + c0, GROUP)])
            da = blk * s_blk
            dg = blk * a_blk * s_blk * (1.0 - s_blk)
            dp_ref[rows, pl.ds(COL_C + c0, GROUP)] = da.astype(BF16)
            dp_ref[rows, pl.ds(COL_C + CONV_WIDTH + c0, GROUP)] = dg.astype(BF16)
            hh_ext[pl.ds(HALO + r0, RB_CONV), cols] = da
            dcv_ext[rows, cols] = dg
        _accumulate(dbin_ref.at[:, pl.ds(COL_C, CONV_WIDTH)], _colsum(hh_ext[pl.ds(HALO, ts), :]), first)
        _accumulate(dbin_ref.at[:, pl.ds(COL_C + CONV_WIDTH, CONV_WIDTH)], _colsum(dcv_ext[pl.ds(0, ts), :]), first)

    full = lambda a: pl.BlockSpec(a.shape, lambda i: (0,) * a.ndim)
    weights = [lw["w_pool"], _row(lw["pool_scale"]), _row(lw["sgu_ln_g"]), _row(lw["sgu_ln_b"]), lw["sgu_w"], lw["sgu_w_t"],
               lw["sgu_bias_tile"], lw["conv_w_full"], lw["conv_w_rev"], _row(lw["conv_ln_g"]), _row(lw["conv_ln_b"])]
    prev_halo = lambda i: (jnp.maximum(i * per_halo - 1, 0), 0)
    next_halo = lambda i: (jnp.minimum((i + 1) * per_halo, s_len // HALO - 1), 0)
    small = lambda shape: (jax.ShapeDtypeStruct(shape, F32), pl.BlockSpec(shape, lambda i: (0,) * len(shape)))
    outs = [(jax.ShapeDtypeStruct((s_len, IN_WIDTH), BF16), pl.BlockSpec((ts, IN_WIDTH), lambda i: (i, 0))),
            small((1, IN_WIDTH)), small((len(POOL_WINDOWS), GROUP, GROUP)), small((1, POOL_WIDTH)),
            small((1, SGU_WIDTH)), small((1, SGU_WIDTH)), small((SGU_HEADS, CHUNK, CHUNK)), small((SGU_HEADS, CHUNK, GROUP)),
            small((CONV_ROWS, CONV_WIDTH)), small((1, CONV_WIDTH)), small((1, CONV_WIDTH)), small((1, CONV_WIDTH))]
    return pl.pallas_call(
        body, name=name, grid=(nt,),
        in_specs=[pl.BlockSpec((ts, IN_WIDTH), lambda i: (i, 0)), pl.BlockSpec((HALO, IN_WIDTH), prev_halo),
                  pl.BlockSpec((ts, CONV_WIDTH), lambda i: (i, 0)), pl.BlockSpec((HALO, CONV_WIDTH), next_halo),
                  pl.BlockSpec((ts, d), lambda i: (i, 0)), pl.BlockSpec((HALO, d), next_halo)] + [full(a) for a in weights],
        out_specs=[s for _, s in outs],
        out_shape=[o for o, _ in outs],
        scratch_shapes=[pltpu.VMEM((16 + ts, POOL_WIDTH), F32), pltpu.VMEM((ts + HALO, POOL_WIDTH), F32),
                        pltpu.VMEM((HALO + ts, CONV_WIDTH), F32), pltpu.VMEM((ts + HALO, CONV_WIDTH), F32),
                        pltpu.VMEM((CONV_ROWS, 8, CONV_WIDTH), F32)],
        compiler_params=_params(1),
    )(p, p, cv, cv, dmixed, dmixed, *weights)


def _masked_sgu_w_t(wt_ref, h):
    r = lax.broadcasted_iota(jnp.int32, (CHUNK, CHUNK), 0)
    c = lax.broadcasted_iota(jnp.int32, (CHUNK, CHUNK), 1)
    return jnp.where(c >= r, wt_ref[h], 0.0)


HBM = pl.BlockSpec(memory_space=pltpu.HBM)
CHIP_FLIPS = ((1, 0), (0, 1), (1, 1))


def _place():
    return lax.axis_index("x"), lax.axis_index("y"), lax.axis_index("c")


def _half(ref, axis, which, size):
    idx = [slice(None)] * len(ref.shape)
    idx[axis] = pl.ds(which * size, size)
    return ref.at[tuple(idx)]


def _gather_weights(name, shards):
    n = len(shards)

    def body(*refs):
        src, dst = refs[:n], refs[n:2 * n]
        ici_send, ici_recv, d2d_send, d2d_recv, local_sem = refs[2 * n:]
        x, y, c = _place()
        me = 2 * x + y
        starts, forwards = [], []
        for a in range(n):
            rows = src[a].shape[1] // 2
            own = pltpu.make_async_copy(src[a], dst[a].at[:, me], local_sem.at[a])
            own.start()
            starts.append(own)
            for j, (fx, fy) in enumerate(CHIP_FLIPS):
                k = a * 3 + j
                cp = pltpu.make_async_remote_copy(
                    src_ref=_half(src[a], 1, c, rows), dst_ref=_half(dst[a].at[:, me], 1, c, rows),
                    send_sem=ici_send.at[k], recv_sem=ici_recv.at[k],
                    device_id=(x ^ fx, y ^ fy, c), device_id_type=MESH)
                cp.start()
                starts.append(cp)
        for a in range(n):
            rows = src[a].shape[1] // 2
            for j, (fx, fy) in enumerate(CHIP_FLIPS):
                k = a * 3 + j
                theirs = 2 * (x ^ fx) + (y ^ fy)
                landed = _half(dst[a].at[:, theirs], 1, c, rows)
                pltpu.make_async_remote_copy(
                    src_ref=landed, dst_ref=landed, send_sem=ici_send.at[k], recv_sem=ici_recv.at[k],
                    device_id=(x ^ fx, y ^ fy, c), device_id_type=MESH).wait_recv()
                fw = pltpu.make_async_remote_copy(
                    src_ref=landed, dst_ref=landed, send_sem=d2d_send.at[k], recv_sem=d2d_recv.at[k],
                    device_id=(x, y, 1 - c), device_id_type=MESH)
                fw.start()
                forwards.append(fw)
        for a in range(n):
            rows = src[a].shape[1] // 2
            for j, (fx, fy) in enumerate(CHIP_FLIPS):
                k = a * 3 + j
                theirs = 2 * (x ^ fx) + (y ^ fy)
                from_sibling = _half(dst[a].at[:, theirs], 1, 1 - c, rows)
                pltpu.make_async_remote_copy(
                    src_ref=from_sibling, dst_ref=from_sibling, send_sem=d2d_send.at[k], recv_sem=d2d_recv.at[k],
                    device_id=(x, y, 1 - c), device_id_type=MESH).wait_recv()
        for a in range(n):
            starts[a * 4].wait()
            for j in range(3):
                starts[a * 4 + 1 + j].wait_send()
        for fw in forwards:
            fw.wait_send()

    out_shape = [jax.ShapeDtypeStruct((s.shape[0], 4) + s.shape[1:], s.dtype) for s in shards]
    return pl.pallas_call(
        body, name=name, in_specs=[HBM] * n, out_specs=[HBM] * n, out_shape=out_shape,
        scratch_shapes=[pltpu.SemaphoreType.DMA((3 * n,)), pltpu.SemaphoreType.DMA((3 * n,)),
                        pltpu.SemaphoreType.DMA((3 * n,)), pltpu.SemaphoreType.DMA((3 * n,)),
                        pltpu.SemaphoreType.DMA((n,))],
    )(*shards)


def _exchange(name, copies):
    n = len(copies)
    out_ids = sorted({cp[5] for cp in copies})
    out_shapes = [next(cp[2] for cp in copies if cp[5] == o) for o in out_ids]
    srcs = [cp[0] for cp in copies]

    def body(*refs):
        src_refs = refs[:n]
        out_refs = refs[n:n + len(out_ids)]
        send_sems, recv_sems = refs[n + len(out_ids):]
        x, y, c = _place()
        started = []
        for k, (_, src_view, _, dst_view, peer, o) in enumerate(copies):
            src = src_view(src_refs[k], x, y, c)
            dst = dst_view(out_refs[out_ids.index(o)], x, y, c)
            if peer is None:
                cp = pltpu.make_async_copy(src, dst, send_sems.at[k])
            else:
                cp = pltpu.make_async_remote_copy(src_ref=src, dst_ref=dst, send_sem=send_sems.at[k],
                                                  recv_sem=recv_sems.at[k], device_id=peer(x, y, c), device_id_type=MESH)
            cp.start()
            started.append(cp)
        for cp in started:
            cp.wait()

    return pl.pallas_call(
        body, name=name, in_specs=[HBM] * n, out_specs=[HBM] * len(out_ids), out_shape=out_shapes,
        scratch_shapes=[pltpu.SemaphoreType.DMA((n,)), pltpu.SemaphoreType.DMA((n,))],
    )(*srcs)


def _sibling(x, y, c):
    return (x, y, 1 - c)


def _scalar_spec_call(name, fn, scalars, ins, in_blocks, out_shapes, out_blocks, grid):
    def body(s_ref, *refs):
        res = fn(*[r[...] for r in refs[:len(ins)]])
        for o_ref, v in zip(refs[len(ins):], res):
            o_ref[...] = v.astype(o_ref.dtype)

    return pl.pallas_call(
        body, name=name,
        grid_spec=pltpu.PrefetchScalarGridSpec(num_scalar_prefetch=1, grid=grid, in_specs=in_blocks, out_specs=out_blocks),
        out_shape=out_shapes, compiler_params=_params(len(grid)),
    )(scalars, *ins)


def _reduce_scatter(tag, grads, small):
    x, y, c = _place()
    me = 2 * x + y
    place = jnp.stack([c, me]).astype(jnp.int32)
    n = len(grads)
    half_rows = [g.shape[2] // 2 for g in grads]
    small_half = small.shape[0] // 2

    copies = []
    for a, g in enumerate(grads):
        hr = half_rows[a]
        copies.append((g, lambda ref, x, y, c, hr=hr: _half(ref, 2, 1 - c, hr),
                       jax.ShapeDtypeStruct(g.shape[:2] + (hr, g.shape[3]), F32), lambda ref, x, y, c: ref, _sibling, a))
    copies.append((small, lambda ref, x, y, c: _half(ref, 0, 1 - c, small_half),
                   jax.ShapeDtypeStruct((small_half, 128), F32), lambda ref, x, y, c: ref, _sibling, n))
    landed = _exchange(tag + "_sibling_in", copies)

    chip_sums = []
    for a, g in enumerate(grads):
        l_dim, _, _, cols = g.shape
        hr = half_rows[a]
        br = _tile(hr, BR_EW)
        chip_sums.append(_scalar_spec_call(
            f"{tag}_chip_sum{a}", lambda u, v: (u + v,), place, [g, landed[a]],
            [pl.BlockSpec((None, None, br, cols), lambda l, s, i, pr, hb=hr // br: (l, s, pr[0] * hb + i, 0)),
             pl.BlockSpec((None, None, br, cols), lambda l, s, i, pr: (l, s, i, 0))],
            [jax.ShapeDtypeStruct((l_dim, 4, hr, cols), BF16)],
            [pl.BlockSpec((None, None, br, cols), lambda l, s, i, pr: (l, s, i, 0))],
            (l_dim, 4, hr // br))[0])
    bs = _tile(small_half, BR_EW)
    small_sum = _scalar_spec_call(
        f"{tag}_chip_sum_small", lambda u, v: (u + v,), place, [small, landed[n]],
        [pl.BlockSpec((bs, 128), lambda i, pr, hb=small_half // bs: (pr[0] * hb + i, 0)), pl.BlockSpec((bs, 128), lambda i, pr: (i, 0))],
        [jax.ShapeDtypeStruct((small_half, 128), F32)], [pl.BlockSpec((bs, 128), lambda i, pr: (i, 0))],
        (small_half // bs,))[0]

    copies = []
    for a, h in enumerate(chip_sums):
        for j, (fx, fy) in enumerate(CHIP_FLIPS):
            copies.append((h, lambda ref, x, y, c, fx=fx, fy=fy: ref.at[:, 2 * (x ^ fx) + (y ^ fy)],
                           jax.ShapeDtypeStruct((3,) + h.shape[:1] + h.shape[2:], BF16),
                           lambda ref, x, y, c, j=j: ref.at[j],
                           lambda x, y, c, fx=fx, fy=fy: (x ^ fx, y ^ fy, c), a))
    for j, (fx, fy) in enumerate(CHIP_FLIPS):
        copies.append((small_sum, lambda ref, x, y, c: ref, jax.ShapeDtypeStruct((3, small_half, 128), F32),
                       lambda ref, x, y, c, j=j: ref.at[j], lambda x, y, c, fx=fx, fy=fy: (x ^ fx, y ^ fy, c), n))
    arrived = _exchange(tag + "_chips", copies)

    tree = lambda own, fx, fy, fxy: ((own.astype(F32) + fx.astype(F32)) + (fy.astype(F32) + fxy.astype(F32)),)
    reduced_half = []
    for a, h in enumerate(chip_sums):
        l_dim, _, hr, cols = h.shape
        br = _tile(hr, BR_EW)
        slot = lambda j: pl.BlockSpec((None, None, br, cols), lambda l, i, pr, j=j: (j, l, i, 0))
        reduced_half.append(_scalar_spec_call(
            f"{tag}_tree_sum{a}", tree, place, [h, arrived[a], arrived[a], arrived[a]],
            [pl.BlockSpec((None, None, br, cols), lambda l, i, pr: (l, pr[1], i, 0)), slot(0), slot(1), slot(2)],
            [jax.ShapeDtypeStruct((l_dim, hr, cols), F32)], [pl.BlockSpec((None, br, cols), lambda l, i, pr: (l, i, 0))],
            (l_dim, hr // br))[0])
    slot = lambda j: pl.BlockSpec((None, bs, 128), lambda i, pr, j=j: (j, i, 0))
    small_half_sum = _scalar_spec_call(
        f"{tag}_tree_sum_small", tree, place, [small_sum, arrived[n], arrived[n], arrived[n]],
        [pl.BlockSpec((bs, 128), lambda i, pr: (i, 0)), slot(0), slot(1), slot(2)],
        [jax.ShapeDtypeStruct((small_half, 128), F32)], [pl.BlockSpec((bs, 128), lambda i, pr: (i, 0))],
        (small_half // bs,))[0]

    copies = []
    for a, rh in enumerate(reduced_half):
        l_dim, hr, cols = rh.shape
        shape = jax.ShapeDtypeStruct((l_dim, 2, hr, cols), F32)
        for peer in (None, _sibling):
            copies.append((rh, lambda ref, x, y, c: ref, shape, lambda ref, x, y, c: ref.at[:, c], peer, a))
    for peer in (None, _sibling):
        copies.append((small_half_sum, lambda ref, x, y, c: ref, jax.ShapeDtypeStruct((2, small_half, 128), F32),
                       lambda ref, x, y, c: ref.at[c], peer, n))
    joined = _exchange(tag + "_sibling_out", copies)
    full = [j.reshape(j.shape[0], 2 * j.shape[2], j.shape[3]) for j in joined[:n]]
    return full, joined[n].reshape(2 * small_half, 128)


SHARDED = ("w_in", "w_out", "w_ff1", "w_ff2")
REPLICATED = ("b_in", "w_pool", "pool_scale", "sgu_ln_g", "sgu_ln_b", "sgu_w", "sgu_b", "conv_b", "conv_ln_g", "conv_ln_b",
              "b_out", "ln1_g", "ln1_b", "b_ff1", "b_ff2", "ln2_g", "ln2_b")
WEIGHTS = ("w_in", "b_in", "w_pool", "pool_scale", "sgu_ln_g", "sgu_ln_b", "sgu_w", "sgu_b", "conv_w", "conv_b", "conv_ln_g",
           "conv_ln_b", "w_out", "b_out", "ln1_g", "ln1_b", "w_ff1", "b_ff1", "w_ff2", "b_ff2", "ln2_g", "ln2_b")
PACK_ROWS = 1024


def _pack(arrays):
    flat = jnp.concatenate([a.reshape(-1) for a in arrays])
    rows = -(-flat.shape[0] // (128 * PACK_ROWS)) * PACK_ROWS
    return jnp.pad(flat, (0, rows * 128 - flat.shape[0])).reshape(rows, 128)


def _unpack(packed, like):
    flat = packed.reshape(-1)
    out, at = [], 0
    for a in like:
        out.append(flat[at:at + a.size].reshape(a.shape))
        at += a.size
    return out


def _cast_bf16(name, w):
    c = w.shape[-1]
    return _rowwise(name, lambda v: (v,), [w.reshape(-1, c)], [BF16])[0].reshape(w.shape)


def kernel(x, w_in, b_in, w_pool, pool_scale, sgu_ln_g, sgu_ln_b, sgu_w, sgu_b, conv_w, conv_b, conv_ln_g, conv_ln_b, w_out, b_out, ln1_g, ln1_b, w_ff1, b_ff1, w_ff2, b_ff2, ln2_g, ln2_b, loss_target, m_w_in, m_b_in, m_w_pool, m_pool_scale, m_sgu_ln_g, m_sgu_ln_b, m_sgu_w, m_sgu_b, m_conv_w, m_conv_b, m_conv_ln_g, m_conv_ln_b, m_w_out, m_b_out, m_ln1_g, m_ln1_b, m_w_ff1, m_b_ff1, m_w_ff2, m_b_ff2, m_ln2_g, m_ln2_b, v_w_in, v_b_in, v_w_pool, v_pool_scale, v_sgu_ln_g, v_sgu_ln_b, v_sgu_w, v_sgu_b, v_conv_w, v_conv_b, v_conv_ln_g, v_conv_ln_b, v_w_out, v_b_out, v_ln1_g, v_ln1_b, v_w_ff1, v_b_ff1, v_w_ff2, v_b_ff2, v_ln2_g, v_ln2_b):
    w = dict(w_in=w_in, b_in=b_in, w_pool=w_pool, pool_scale=pool_scale, sgu_ln_g=sgu_ln_g, sgu_ln_b=sgu_ln_b, sgu_w=sgu_w,
             sgu_b=sgu_b, conv_w=conv_w, conv_b=conv_b, conv_ln_g=conv_ln_g, conv_ln_b=conv_ln_b, w_out=w_out, b_out=b_out,
             ln1_g=ln1_g, ln1_b=ln1_b, w_ff1=w_ff1, b_ff1=b_ff1, w_ff2=w_ff2, b_ff2=b_ff2, ln2_g=ln2_g, ln2_b=ln2_b)
    m = dict(w_in=m_w_in, b_in=m_b_in, w_pool=m_w_pool, pool_scale=m_pool_scale, sgu_ln_g=m_sgu_ln_g, sgu_ln_b=m_sgu_ln_b,
             sgu_w=m_sgu_w, sgu_b=m_sgu_b, conv_w=m_conv_w, conv_b=m_conv_b, conv_ln_g=m_conv_ln_g, conv_ln_b=m_conv_ln_b,
             w_out=m_w_out, b_out=m_b_out, ln1_g=m_ln1_g, ln1_b=m_ln1_b, w_ff1=m_w_ff1, b_ff1=m_b_ff1, w_ff2=m_w_ff2,
             b_ff2=m_b_ff2, ln2_g=m_ln2_g, ln2_b=m_ln2_b)
    v = dict(w_in=v_w_in, b_in=v_b_in, w_pool=v_w_pool, pool_scale=v_pool_scale, sgu_ln_g=v_sgu_ln_g, sgu_ln_b=v_sgu_ln_b,
             sgu_w=v_sgu_w, sgu_b=v_sgu_b, conv_w=v_conv_w, conv_b=v_conv_b, conv_ln_g=v_conv_ln_g, conv_ln_b=v_conv_ln_b,
             w_out=v_w_out, b_out=v_b_out, ln1_g=v_ln1_g, ln1_b=v_ln1_b, w_ff1=v_w_ff1, b_ff1=v_b_ff1, w_ff2=v_w_ff2,
             b_ff2=v_b_ff2, ln2_g=v_ln2_g, ln2_b=v_ln2_b)
    assert x.shape[0] == 1 and x.shape[2] == POOL_WIDTH + SGU_WIDTH + CONV_WIDTH, x.shape
    xs, target = x[0], loss_target[0]
    s_len, d = xs.shape
    n_layers = w_in.shape[0]
    dff = 4 * w_ff1.shape[2]
    conv_shard = conv_w.shape[2]

    shards = [_cast_bf16("cast_" + n, w[n]) for n in SHARDED]
    conv_padded = jnp.pad(conv_w, ((0, 0), (0, CONV_ROWS - CONV_KERNEL), (0, 0)))
    gathered = dict(zip(SHARDED + ("conv_w",), _gather_weights("gather_weights", shards + [conv_padded])))
    conv_full = jnp.transpose(gathered["conv_w"], (0, 2, 1, 3)).reshape(n_layers, CONV_ROWS, CONV_WIDTH)
    conv_rev = jnp.pad(conv_full[:, CONV_KERNEL - 1::-1], ((0, 0), (0, CONV_ROWS - CONV_KERNEL), (0, 0)))

    def layer_weights(l):
        return dict(w_pool=w_pool[l], pool_scale=pool_scale[l], sgu_ln_g=sgu_ln_g[l], sgu_ln_b=sgu_ln_b[l], sgu_w=sgu_w[l],
                    sgu_w_t=jnp.transpose(sgu_w[l], (0, 2, 1)),
                    sgu_bias_tile=jnp.broadcast_to(sgu_b[l][:, :, None], (SGU_HEADS, CHUNK, GROUP)),
                    conv_w_full=conv_full[l], conv_w_rev=conv_rev[l], conv_b=conv_b[l], conv_ln_g=conv_ln_g[l],
                    conv_ln_b=conv_ln_b[l])

    saved = []
    x_cur, xb_cur = xs, _rowwise("cast_x", lambda t: (t,), [xs], [BF16])[0]
    for l in range(n_layers):
        lw = layer_weights(l)
        w_out_full = gathered["w_out"][l].reshape(d, d)
        w2_full = gathered["w_ff2"][l].reshape(dff, d)
        p = _proj(f"proj{l}", xb_cur, gathered["w_in"][l], b_in[l])
        mixed, cv = _mix_forward(f"mix_fwd{l}", p, lw)
        r1, x1b = _mix_out(f"mix_out{l}", mixed, w_out_full, b_out[l], x_cur, ln1_g[l], ln1_b[l])
        hf, zr = _ff1(f"ff1_{l}", x1b, gathered["w_ff1"][l], b_ff1[l])
        r2, x2, x2b = _ff2(f"ff2_{l}", hf, w2_full, b_ff2[l], r1, ln1_g[l], ln1_b[l], ln2_g[l], ln2_b[l])
        saved.append(dict(lw=lw, xb_in=xb_cur, p=p, mixed=mixed, cv=cv, r1=r1, x1b=x1b, hf=hf, zr=zr, r2=r2,
                          w_out_full=w_out_full, w2_full=w2_full))
        x_cur, xb_cur = x2, x2b

    grads = {n: [None] * n_layers for n in WEIGHTS}
    dx = None
    loss_tile = None
    for l in reversed(range(n_layers)):
        sv = saved[l]
        if dx is None:
            dr2, dr2b, dg2, db2, dbff2, loss_tile = _ln_backward(f"ln2_bwd{l}", sv["r2"], ln2_g[l], x=x_cur, target=target)
        else:
            dr2, dr2b, dg2, db2, dbff2 = _ln_backward(f"ln2_bwd{l}", sv["r2"], ln2_g[l], dy=dx)
        dzb, dbff1 = _dff_hidden(f"ff2_bwd{l}", dr2b, sv["w2_full"], sv["zr"])
        grads["w_ff2"][l] = _matmul_tn(f"dw_ff2_{l}", sv["hf"], dr2b, 4, dff // 4, d, True)
        grads["w_ff1"][l] = _matmul_tn(f"dw_ff1_{l}", sv["x1b"], dzb, 4, d, dff // 4, False)
        dx1 = _dx_sharded(f"ff1_bwd{l}", dzb, gathered["w_ff1"][l], dr2)
        dr1, dr1b, dg1, db1, dbout = _ln_backward(f"ln1_bwd{l}", sv["r1"], ln1_g[l], dy=dx1)
        dmixed = _dmixed(f"mix_out_bwd{l}", dr1b, sv["w_out_full"])
        grads["w_out"][l] = _matmul_tn(f"dw_out{l}", sv["mixed"], dr1b, 4, d // 4, d, True)
        (dp, dbin, dwp, dps, dslg, dslb, dws, dsb_tile, dcw, dcb, dclg, dclb) = _mix_backward(
            f"mix_bwd{l}", sv["p"], sv["cv"], dmixed, sv["lw"])
        grads["w_in"][l] = _matmul_tn(f"dw_in{l}", sv["xb_in"], dp, 4, d, IN_WIDTH // 4, False)
        dx = _dx_sharded(f"proj_bwd{l}", dp, gathered["w_in"][l], dr1)
        for name, g in (("b_in", dbin), ("w_pool", dwp), ("pool_scale", dps), ("sgu_ln_g", dslg), ("sgu_ln_b", dslb),
                        ("sgu_w", dws), ("sgu_b", dsb_tile[:, :, 0]), ("conv_w", dcw[:CONV_KERNEL]), ("conv_b", dcb),
                        ("conv_ln_g", dclg), ("conv_ln_b", dclb), ("b_out", dbout), ("ln1_g", dg1), ("ln1_b", db1),
                        ("b_ff1", dbff1), ("b_ff2", dbff2), ("ln2_g", dg2), ("ln2_b", db2)):
            grads[name][l] = g.reshape(w[name].shape[1:]) if name != "conv_w" else g

    loss = lax.psum(loss_tile[0, 0], ("x", "y", "c"))
    grad_x = dx[None]

    big = [jnp.stack(grads[n]) for n in SHARDED]
    small_like = [jnp.stack(grads[n]) for n in REPLICATED] + [jnp.stack(grads["conv_w"])]
    reduced_big, reduced_small = _reduce_scatter("grads", big, _pack(small_like))
    g_final = dict(zip(SHARDED, reduced_big))
    unpacked = _unpack(reduced_small, small_like)
    g_final.update(zip(REPLICATED, unpacked[:-1]))
    me = 2 * lax.axis_index("x") + lax.axis_index("y")
    g_final["conv_w"] = lax.dynamic_slice_in_dim(unpacked[-1], me * conv_shard, conv_shard, axis=2)

    delta, new_m, new_v = {}, {}, {}
    for n in SHARDED + ("conv_w",):
        delta[n], new_m[n], new_v[n] = _adamw("adamw_" + n, w[n], g_final[n], m[n], v[n])
    packed = [_pack([t[n] for n in REPLICATED]) for t in (w, g_final, m, v)]
    like = [w[n] for n in REPLICATED]
    for res, packed_out in zip((delta, new_m, new_v), _adamw("adamw_replicated", *packed)):
        res.update(zip(REPLICATED, _unpack(packed_out, like)))

    return (loss, grad_x, *[g_final[n] for n in WEIGHTS], *[delta[n] for n in WEIGHTS],
            *[new_m[n] for n in WEIGHTS], *[new_v[n] for n in WEIGHTS])
```

```python
import functools
import math

import jax
import jax.numpy as jnp
from jax import lax
from jax.experimental import pallas as pl
from jax.experimental.pallas import tpu as pltpu

F32 = jnp.float32
BF16 = jnp.bfloat16
MESH = pl.DeviceIdType.MESH

DEPTH = 2
POOL_WINDOWS = (2, 4, 8, 16)
POOL_WIDTH = 512
GROUP = 128
SGU_WIDTH = 768
SGU_HEADS = 6
CHUNK = 128
CONV_WIDTH = 768
CONV_KERNEL = 31
CONV_ROWS = 32
HALO = 32
COL_B = POOL_WIDTH
COL_C = POOL_WIDTH + 2 * SGU_WIDTH
IN_WIDTH = COL_C + 2 * CONV_WIDTH
ALPHA = (2 * DEPTH) ** 0.25
LN_EPS = 1e-5
ADAM_LR = 0.001
ADAM_B1 = 0.9
ADAM_B2 = 0.999
ADAM_EPS = 1e-08
ADAM_WD = 0.01
ADAM_STEP = 10
GELU_C = math.sqrt(2.0 / math.pi)
GELU_A = 0.044715

V7X_VMEM_BYTES = 64 * 2 ** 20
VMEM_LIMIT = 56 * 2 ** 20

BM_MM = 1024
BN_MM = 1024
BM_LN = 512
BK_MM = 1024
BS_TN = 1024
TS_MIX = 256
RB_CONV = 64
BR_EW = 512
EW_BLOCK_ELEMS = 2 ** 18


def _tile(n, pref):
    t = min(n, pref)
    assert n % t == 0, (n, pref)
    return t


def _params(n_grid):
    return pltpu.CompilerParams(dimension_semantics=("arbitrary",) * n_grid, vmem_limit_bytes=VMEM_LIMIT)


def _sigmoid(x):
    return 1.0 / (1.0 + jnp.exp(-x))


def _gelu(x):
    return 0.5 * x * (1.0 + jnp.tanh(GELU_C * (x + GELU_A * x * x * x)))


def _gelu_grad(x):
    t = jnp.tanh(GELU_C * (x + GELU_A * x * x * x))
    return 0.5 * (1.0 + t) + 0.5 * x * (1.0 - t * t) * GELU_C * (1.0 + 3.0 * GELU_A * x * x)


def _ln_stats(r):
    mu = jnp.mean(r, axis=-1, keepdims=True)
    xc = r - mu
    var = jnp.mean(xc * xc, axis=-1, keepdims=True)
    rstd = lax.rsqrt(var + LN_EPS)
    return xc * rstd, rstd


def _ln_bwd(dy, xhat, rstd, g):
    dxh = dy * g
    m1 = jnp.mean(dxh, axis=-1, keepdims=True)
    m2 = jnp.mean(dxh * xhat, axis=-1, keepdims=True)
    return rstd * (dxh - m1 - xhat * m2)


def _colsum(x):
    return jnp.sum(x, axis=0, keepdims=True)


def _accumulate(ref, val, first):
    @pl.when(first)
    def _():
        ref[...] = val

    @pl.when(jnp.logical_not(first))
    def _():
        ref[...] += val


def _matmul(name, grid, a, a_spec, b, b_spec, *, nt, extras, outs, epilogue, acc_shape=None):
    nk = grid[2]
    ne, no = len(extras), len(outs)
    dims = (((1,), (1,)), ((), ())) if nt else (((1,), (0,)), ((), ()))

    def body(*refs):
        a_ref, b_ref = refs[0], refs[1]
        ex = refs[2:2 + ne]
        out_refs = refs[2 + ne:2 + ne + no]
        ids = (pl.program_id(0), pl.program_id(1), pl.program_id(2))
        part = lax.dot_general(a_ref[...], b_ref[...], dims, preferred_element_type=F32)
        if nk == 1:
            epilogue(part, ex, out_refs, ids)
        else:
            acc_ref = refs[2 + ne + no]
            k = ids[2]

            @pl.when(k == 0)
            def _():
                acc_ref[...] = part

            @pl.when(k > 0)
            def _():
                acc_ref[...] += part

            @pl.when(k == nk - 1)
            def _():
                epilogue(acc_ref[...], ex, out_refs, ids)

    return pl.pallas_call(
        body,
        name=name,
        grid=grid,
        in_specs=[a_spec, b_spec] + [s for _, s in extras],
        out_specs=[s for _, s in outs],
        out_shape=[o for o, _ in outs],
        scratch_shapes=[pltpu.VMEM(acc_shape, F32)] if nk > 1 else [],
        compiler_params=_params(3),
    )(a, b, *[e for e, _ in extras])


def _matmul_tn(name, a, b, n_shards, shard_rows, shard_cols, row_sharded):
    s_len, ka = a.shape
    n = b.shape[1]
    assert (n_shards * shard_rows, shard_cols) == (ka, n) if row_sharded else (shard_rows, n_shards * shard_cols) == (ka, n)
    bs = _tile(s_len, BS_TN)
    bka = _tile(shard_rows, 2048)
    bn = _tile(shard_cols, BN_MM) if shard_cols % BN_MM == 0 else shard_cols
    ni, nj, ns = ka // bka, n // bn, s_len // bs
    per_shard_i = shard_rows // bka
    per_shard_j = shard_cols // bn

    if row_sharded:
        out_map = lambda i, j, s: (i // per_shard_i, i % per_shard_i, j)
    else:
        out_map = lambda i, j, s: (j // per_shard_j, i, j % per_shard_j)

    def body(a_ref, b_ref, o_ref):
        s = pl.program_id(2)
        part = lax.dot_general(a_ref[...], b_ref[...], (((0,), (0,)), ((), ())), preferred_element_type=F32)
        _accumulate(o_ref, part, s == 0)

    return pl.pallas_call(
        body,
        name=name,
        grid=(ni, nj, ns),
        in_specs=[pl.BlockSpec((bs, bka), lambda i, j, s: (s, i)), pl.BlockSpec((bs, bn), lambda i, j, s: (s, j))],
        out_specs=pl.BlockSpec((None, bka, bn), out_map),
        out_shape=jax.ShapeDtypeStruct((n_shards, shard_rows, shard_cols), F32),
        compiler_params=_params(3),
    )(a, b)


def _row(v):
    return v.reshape(1, -1)


def _proj(name, xb, w_g, b_in):
    s_len, d = xb.shape
    ncs = w_g.shape[2]
    bm = _tile(s_len, BM_MM)

    def epilogue(acc, ex, outs, ids):
        outs[0][...] = acc + ex[0][...]

    return _matmul(
        name, (s_len // bm, 4, 1),
        xb, pl.BlockSpec((bm, d), lambda i, j, k: (i, 0)),
        w_g, pl.BlockSpec((None, d, ncs), lambda i, j, k: (j, 0, 0)),
        nt=False,
        extras=[(_row(b_in), pl.BlockSpec((1, ncs), lambda i, j, k: (0, j)))],
        outs=[(jax.ShapeDtypeStruct((s_len, 4 * ncs), F32), pl.BlockSpec((bm, ncs), lambda i, j, k: (i, j)))],
        epilogue=epilogue,
    )[0]


def _mix_out(name, mixed, w_out_full, b_out, x0, g1, b1):
    s_len, d = mixed.shape
    bm = _tile(s_len, BM_LN // 2)
    row = pl.BlockSpec((1, d), lambda i, j, k: (0, 0))
    blk = pl.BlockSpec((bm, d), lambda i, j, k: (i, 0))

    def epilogue(acc, ex, outs, ids):
        r1 = ALPHA * ex[1][...] + (acc + ex[0][...])
        outs[0][...] = r1
        xhat, _ = _ln_stats(r1)
        outs[1][...] = (xhat * ex[2][...] + ex[3][...]).astype(BF16)

    return _matmul(
        name, (s_len // bm, 1, 1),
        mixed, blk,
        w_out_full, pl.BlockSpec((d, d), lambda i, j, k: (0, 0)),
        nt=False,
        extras=[(_row(b_out), row), (x0, blk), (_row(g1), row), (_row(b1), row)],
        outs=[(jax.ShapeDtypeStruct((s_len, d), F32), blk), (jax.ShapeDtypeStruct((s_len, d), BF16), blk)],
        epilogue=epilogue,
    )


def _ff1(name, x1b, w_g, b_ff1):
    s_len, d = x1b.shape
    ncs = w_g.shape[2]
    bm = _tile(s_len, BM_MM)
    bn = _tile(ncs, BN_MM)
    per = ncs // bn
    blk = pl.BlockSpec((bm, bn), lambda i, j, k: (i, j))

    def epilogue(acc, ex, outs, ids):
        zr = jnp.maximum(acc + ex[0][...], 0.0)
        outs[0][...] = (zr * zr).astype(BF16)
        outs[1][...] = zr.astype(BF16)

    shape = jax.ShapeDtypeStruct((s_len, 4 * ncs), BF16)
    return _matmul(
        name, (s_len // bm, 4 * per, 1),
        x1b, pl.BlockSpec((bm, d), lambda i, j, k: (i, 0)),
        w_g, pl.BlockSpec((None, d, bn), lambda i, j, k: (j // per, 0, j % per)),
        nt=False,
        extras=[(_row(b_ff1), pl.BlockSpec((1, bn), lambda i, j, k: (0, j)))],
        outs=[(shape, blk), (shape, blk)],
        epilogue=epilogue,
    )


def _ff2(name, hf, w2_full, b_ff2):
    s_len, dff = hf.shape
    d = w2_full.shape[1]
    bm = _tile(s_len, BM_MM)
    bk = _tile(dff, BK_MM)
    blk = pl.BlockSpec((bm, d), lambda i, j, k: (i, 0))

    def epilogue(acc, ex, outs, ids):
        outs[0][...] = acc + ex[0][...]

    return _matmul(
        name, (s_len // bm, 1, dff // bk),
        hf, pl.BlockSpec((bm, bk), lambda i, j, k: (i, k)),
        w2_full, pl.BlockSpec((bk, d), lambda i, j, k: (k, 0)),
        nt=False,
        extras=[(_row(b_ff2), pl.BlockSpec((1, d), lambda i, j, k: (0, 0)))],
        outs=[(jax.ShapeDtypeStruct((s_len, d), F32), blk)],
        epilogue=epilogue,
        acc_shape=(bm, d),
    )[0]


def _resid_ln2(name, r1, g1, b1, fo, g2, b2):
    s_len, d = r1.shape
    bm = _tile(s_len, BM_LN // 2)
    blk = pl.BlockSpec((bm, d), lambda i: (i, 0))
    row = pl.BlockSpec((1, d), lambda i: (0, 0))

    def body(r1_ref, g1_ref, b1_ref, fo_ref, g2_ref, b2_ref, r2_ref, x2_ref, x2b_ref):
        xhat1, _ = _ln_stats(r1_ref[...])
        r2 = ALPHA * (xhat1 * g1_ref[...] + b1_ref[...]) + fo_ref[...]
        r2_ref[...] = r2
        xhat2, _ = _ln_stats(r2)
        x2 = xhat2 * g2_ref[...] + b2_ref[...]
        x2_ref[...] = x2
        x2b_ref[...] = x2.astype(BF16)

    return pl.pallas_call(
        body, name=name, grid=(s_len // bm,), in_specs=[blk, row, row, blk, row, row], out_specs=[blk, blk, blk],
        out_shape=[jax.ShapeDtypeStruct((s_len, d), F32), jax.ShapeDtypeStruct((s_len, d), F32),
                   jax.ShapeDtypeStruct((s_len, d), BF16)],
        compiler_params=_params(1))(r1, _row(g1), _row(b1), fo, _row(g2), _row(b2))


def _dff_hidden(name, dr2b, w2_full, zr):
    s_len, d = dr2b.shape
    dff = w2_full.shape[0]
    bm = _tile(s_len, BM_MM)
    bn = _tile(dff, BN_MM)

    def epilogue(acc, ex, outs, ids):
        dz = acc * (2.0 * ex[0][...].astype(F32))
        outs[0][...] = dz.astype(BF16)
        _accumulate(outs[1], _colsum(dz), ids[1] == 0)

    return _matmul(
        name, (dff // bn, s_len // bm, 1),
        dr2b, pl.BlockSpec((bm, d), lambda j, i, k: (i, 0)),
        w2_full, pl.BlockSpec((bn, d), lambda j, i, k: (j, 0)),
        nt=True,
        extras=[(zr, pl.BlockSpec((bm, bn), lambda j, i, k: (i, j)))],
        outs=[(jax.ShapeDtypeStruct((s_len, dff), BF16), pl.BlockSpec((bm, bn), lambda j, i, k: (i, j))),
              (jax.ShapeDtypeStruct((1, dff), F32), pl.BlockSpec((1, bn), lambda j, i, k: (0, j)))],
        epilogue=epilogue,
    )


def _dx_sharded(name, dyb, w_g, resid=None):
    s_len = dyb.shape[0]
    d, ncs = w_g.shape[1], w_g.shape[2]
    bm = _tile(s_len, BM_MM if resid is None else BM_LN)
    bk = _tile(ncs, BK_MM) if ncs % BK_MM == 0 else ncs
    per = ncs // bk
    blk = pl.BlockSpec((bm, d), lambda i, j, k: (i, 0))

    def epilogue(acc, ex, outs, ids):
        outs[0][...] = acc if resid is None else acc + ALPHA * ex[0][...]

    return _matmul(
        name, (s_len // bm, 1, 4 * per),
        dyb, pl.BlockSpec((bm, bk), lambda i, j, k: (i, k)),
        w_g, pl.BlockSpec((None, d, bk), lambda i, j, k: (k // per, 0, k % per)),
        nt=True,
        extras=[] if resid is None else [(resid, blk)],
        outs=[(jax.ShapeDtypeStruct((s_len, d), F32), blk)],
        epilogue=epilogue,
        acc_shape=(bm, d),
    )[0]


def _dmixed(name, dr1b, w_out_full):
    s_len, d = dr1b.shape
    bm = _tile(s_len, BM_LN)
    blk = pl.BlockSpec((bm, d), lambda i, j, k: (i, 0))

    def epilogue(acc, ex, outs, ids):
        outs[0][...] = acc

    return _matmul(
        name, (s_len // bm, 1, 1),
        dr1b, blk,
        w_out_full, pl.BlockSpec((d, d), lambda i, j, k: (0, 0)),
        nt=True, extras=[],
        outs=[(jax.ShapeDtypeStruct((s_len, d), F32), blk)],
        epilogue=epilogue,
    )[0]


def _ln_backward(name, r, g, *, dy=None, resid=None, x=None, target=None):
    s_len, d = r.shape
    bm = _tile(s_len, BM_LN // 2)
    from_loss = dy is None
    blk = pl.BlockSpec((bm, d), lambda i: (i, 0))
    row = pl.BlockSpec((1, d), lambda i: (0, 0))

    def body(*refs):
        i = pl.program_id(0)
        if from_loss:
            x_ref, t_ref, r_ref, g_ref, dr_ref, drb_ref, dg_ref, db_ref, dbias_ref, loss_ref = refs
            diff = x_ref[...] - t_ref[...]
            dyv = diff * (1.0 / d)
            part = 0.5 * jnp.sum(jnp.sum(diff * diff, axis=1, keepdims=True) * (1.0 / d), axis=0, keepdims=True)
            _accumulate(loss_ref, jnp.broadcast_to(part, loss_ref.shape), i == 0)
        elif resid is not None:
            dy_ref, res_ref, r_ref, g_ref, dr_ref, drb_ref, dg_ref, db_ref, dbias_ref = refs
            dyv = dy_ref[...] + ALPHA * res_ref[...]
        else:
            dy_ref, r_ref, g_ref, dr_ref, drb_ref, dg_ref, db_ref, dbias_ref = refs
            dyv = dy_ref[...]
        xhat, rstd = _ln_stats(r_ref[...])
        dr = _ln_bwd(dyv, xhat, rstd, g_ref[...])
        dr_ref[...] = dr
        drb_ref[...] = dr.astype(BF16)
        _accumulate(dg_ref, _colsum(dyv * xhat), i == 0)
        _accumulate(db_ref, _colsum(dyv), i == 0)
        _accumulate(dbias_ref, _colsum(dr), i == 0)

    lead = [x, target] if from_loss else ([dy] if resid is None else [dy, resid])
    ins = lead + [r, _row(g)]
    in_specs = [blk] * len(lead) + [blk, row]
    out_shape = [jax.ShapeDtypeStruct((s_len, d), F32), jax.ShapeDtypeStruct((s_len, d), BF16)] + \
                [jax.ShapeDtypeStruct((1, d), F32)] * 3
    out_specs = [blk, blk, row, row, row]
    if from_loss:
        out_shape.append(jax.ShapeDtypeStruct((8, 128), F32))
        out_specs.append(pl.BlockSpec((8, 128), lambda i: (0, 0)))
    return pl.pallas_call(body, name=name, grid=(s_len // bm,), in_specs=in_specs, out_specs=out_specs,
                          out_shape=out_shape, compiler_params=_params(1))(*ins)


def _rowwise(name, fn, ins, out_dtypes, rows_pref=BR_EW):
    r, c = ins[0].shape
    br = _tile(r, rows_pref)
    blk = pl.BlockSpec((br, c), lambda i: (i, 0))

    def body(*refs):
        res = fn(*[ref[...] for ref in refs[:len(ins)]])
        for o_ref, v in zip(refs[len(ins):], res):
            o_ref[...] = v.astype(o_ref.dtype)

    return pl.pallas_call(body, name=name, grid=(r // br,), in_specs=[blk] * len(ins),
                          out_specs=[blk] * len(out_dtypes),
                          out_shape=[jax.ShapeDtypeStruct((r, c), dt) for dt in out_dtypes],
                          compiler_params=_params(1))(*ins)


def _adamw_math(w, g, m, v):
    m = ADAM_B1 * m + (1.0 - ADAM_B1) * g
    v = ADAM_B2 * v + (1.0 - ADAM_B2) * (g * g)
    m_hat = m / (1.0 - ADAM_B1 ** ADAM_STEP)
    v_hat = v / (1.0 - ADAM_B2 ** ADAM_STEP)
    delta = -ADAM_LR * (m_hat / (jnp.sqrt(v_hat) + ADAM_EPS) + ADAM_WD * w)
    return delta, m, v


def _adamw(name, w, g, m, v):
    shape = w.shape
    c = shape[-1]
    flat = [a.reshape(-1, c) for a in (w, g, m, v)]
    rows = flat[0].shape[0]
    pref = max(8, 2 ** int(math.log2(EW_BLOCK_ELEMS // c)))
    res = _rowwise(name, _adamw_math, flat, [F32, F32, F32], rows_pref=pref if rows % pref == 0 else rows)
    return tuple(a.reshape(shape) for a in res)


def _pool_means(ext_ref, ts, tile_index):
    t_glob = tile_index * ts + lax.broadcasted_iota(jnp.int32, (ts, GROUP), 0)
    qs = []
    for g, win in enumerate(POOL_WINDOWS):
        cols = pl.ds(g * GROUP, GROUP)
        cur = ext_ref[pl.ds(16, ts), cols]
        acc = cur
        for j in range(1, win):
            acc = acc + ext_ref[pl.ds(16 - j, ts), cols]
        cnt = jnp.minimum(t_glob + 1, win).astype(F32)
        qs.append(acc / cnt - cur)
    return qs


def _masked_sgu_w(w_ref, h):
    r = lax.broadcasted_iota(jnp.int32, (CHUNK, CHUNK), 0)
    c = lax.broadcasted_iota(jnp.int32, (CHUNK, CHUNK), 1)
    return jnp.where(r >= c, w_ref[h], 0.0)


def _conv_taps(acc_init, w_ref, src_ref, ts, base):
    nrb = ts // RB_CONV
    blocks = []
    for cg in range(CONV_WIDTH // GROUP):
        cols = pl.ds(cg * GROUP, GROUP)
        accs = [acc_init(cg) for _ in range(nrb)]
        for k in range(CONV_KERNEL):
            wk = jnp.broadcast_to(w_ref[pl.ds(k, 1), cols], (RB_CONV, GROUP))
            for rb in range(nrb):
                accs[rb] = accs[rb] + wk * src_ref[pl.ds(rb * RB_CONV + base(k), RB_CONV), cols]
        for rb in range(nrb):
            blocks.append((rb * RB_CONV, cg * GROUP, accs[rb]))
    return blocks


def _mix_forward(name, p, lw):
    s_len = p.shape[0]
    ts = _tile(s_len, TS_MIX)
    per_halo = ts // HALO
    d = POOL_WIDTH + SGU_WIDTH + CONV_WIDTH

    def body(p_ref, ph_ref, wp_ref, ps_ref, slg_ref, slb_ref, ws_ref, sb_ref, cw_ref, cb_ref, clg_ref, clb_ref,
             mixed_ref, cv_ref, pool_ext, hh_ext):
        i = pl.program_id(0)
        keep = (i > 0).astype(F32)
        pool_ext[pl.ds(0, 16), :] = ph_ref[pl.ds(16, 16), pl.ds(0, POOL_WIDTH)] * keep
        pool_ext[pl.ds(16, ts), :] = p_ref[:, pl.ds(0, POOL_WIDTH)]
        qs = _pool_means(pool_ext, ts, i)
        for g in range(len(POOL_WINDOWS)):
            cols = pl.ds(g * GROUP, GROUP)
            e = jnp.dot(qs[g].astype(BF16), wp_ref[g].astype(BF16), preferred_element_type=F32)
            mixed_ref[:, cols] = (e * ps_ref[:, cols]).astype(BF16)
        uv = _gelu(p_ref[:, pl.ds(COL_B, 2 * SGU_WIDTH)])
        u = uv[:, :SGU_WIDTH]
        vhat, _ = _ln_stats(uv[:, SGU_WIDTH:])
        vn = (vhat * slg_ref[...] + slb_ref[...]).astype(BF16)
        for h in range(SGU_HEADS):
            wm = _masked_sgu_w(ws_ref, h).astype(BF16)
            for n in range(ts // CHUNK):
                rows = slice(n * CHUNK, (n + 1) * CHUNK)
                cols = slice(h * GROUP, (h + 1) * GROUP)
                mx = jnp.dot(wm, vn[rows, cols], preferred_element_type=F32) + sb_ref[h]
                mixed_ref[pl.ds(n * CHUNK, CHUNK), pl.ds(POOL_WIDTH + h * GROUP, GROUP)] = (u[rows, cols] * mx).astype(BF16)
        hh_ext[pl.ds(0, HALO), :] = (ph_ref[:, pl.ds(COL_C, CONV_WIDTH)]
                                     * _sigmoid(ph_ref[:, pl.ds(COL_C + CONV_WIDTH, CONV_WIDTH)])) * keep
        hh_ext[pl.ds(HALO, ts), :] = p_ref[:, pl.ds(COL_C, CONV_WIDTH)] * _sigmoid(p_ref[:, pl.ds(COL_C + CONV_WIDTH, CONV_WIDTH)])
        init = lambda cg: jnp.broadcast_to(cb_ref[:, pl.ds(cg * GROUP, GROUP)], (RB_CONV, GROUP))
        for r0, c0, blk in _conv_taps(init, cw_ref, hh_ext, ts, lambda k: k + HALO - (CONV_KERNEL - 1)):
            cv_ref[pl.ds(r0, RB_CONV), pl.ds(c0, GROUP)] = blk
        cvhat, _ = _ln_stats(cv_ref[...])
        cn = cvhat * clg_ref[...] + clb_ref[...]
        mixed_ref[:, pl.ds(POOL_WIDTH + SGU_WIDTH, CONV_WIDTH)] = (cn * _sigmoid(cn)).astype(BF16)

    full = lambda a: pl.BlockSpec(a.shape, lambda i: (0,) * a.ndim)
    weights = [lw["w_pool"], _row(lw["pool_scale"]), _row(lw["sgu_ln_g"]), _row(lw["sgu_ln_b"]), lw["sgu_w"],
               lw["sgu_bias_tile"], lw["conv_w_full"], _row(lw["conv_b"]), _row(lw["conv_ln_g"]), _row(lw["conv_ln_b"])]
    return pl.pallas_call(
        body, name=name, grid=(s_len // ts,),
        in_specs=[pl.BlockSpec((ts, IN_WIDTH), lambda i: (i, 0)),
                  pl.BlockSpec((HALO, IN_WIDTH), lambda i: (jnp.maximum(i * per_halo - 1, 0), 0))] + [full(a) for a in weights],
        out_specs=[pl.BlockSpec((ts, d), lambda i: (i, 0)), pl.BlockSpec((ts, CONV_WIDTH), lambda i: (i, 0))],
        out_shape=[jax.ShapeDtypeStruct((s_len, d), BF16), jax.ShapeDtypeStruct((s_len, CONV_WIDTH), F32)],
        scratch_shapes=[pltpu.VMEM((16 + ts, POOL_WIDTH), F32), pltpu.VMEM((HALO + ts, CONV_WIDTH), F32)],
        compiler_params=_params(1),
    )(p, p, *weights)


def _mix_backward(name, p, cv, dmixed, lw):
    s_len = p.shape[0]
    ts = _tile(s_len, TS_MIX)
    nt = s_len // ts
    per_halo = ts // HALO
    d = POOL_WIDTH + SGU_WIDTH + CONV_WIDTH
    nch = ts // CHUNK
    col_yc = POOL_WIDTH + SGU_WIDTH

    def body(p_ref, ph_ref, cv_ref, cvn_ref, dm_ref, dmn_ref, wp_ref, ps_ref, slg_ref, slb_ref, ws_ref, wst_ref, sb_ref,
             cw_ref, cwr_ref, clg_ref, clb_ref,
             dp_ref, dbin_ref, dwp_ref, dps_ref, dslg_ref, dslb_ref, dws_ref, dsb_ref, dcw_ref, dcb_ref, dclg_ref, dclb_ref,
             pool_ext, dq_ext, hh_ext, dcv_ext, dcw_acc):
        i = pl.program_id(0)
        first = i == 0
        keep_prev = (i > 0).astype(F32)
        keep_next = (i < nt - 1).astype(F32)

        pool_ext[pl.ds(0, 16), :] = ph_ref[pl.ds(16, 16), pl.ds(0, POOL_WIDTH)] * keep_prev
        pool_ext[pl.ds(16, ts), :] = p_ref[:, pl.ds(0, POOL_WIDTH)]
        qs = _pool_means(pool_ext, ts, i)
        t_ext = i * ts + lax.broadcasted_iota(jnp.int32, (ts + HALO, GROUP), 0)
        for g, win in enumerate(POOL_WINDOWS):
            cols = pl.ds(g * GROUP, GROUP)
            wpb = wp_ref[g].astype(BF16)
            qb = qs[g].astype(BF16)
            de = dm_ref[:, cols] * ps_ref[:, cols]
            e = jnp.dot(qb, wpb, preferred_element_type=F32)
            _accumulate(dps_ref.at[:, cols], _colsum(dm_ref[:, cols] * e), first)
            deb = de.astype(BF16)
            _accumulate(dwp_ref.at[g], lax.dot_general(qb, deb, (((0,), (0,)), ((), ())), preferred_element_type=F32), first)
            de_next = (dmn_ref[:, cols] * ps_ref[:, cols] * keep_next).astype(BF16)
            de_all = jnp.concatenate([deb, de_next], axis=0)
            dq = lax.dot_general(de_all, wpb, (((1,), (1,)), ((), ())), preferred_element_type=F32)
            inv = 1.0 / jnp.minimum(t_ext + 1, win).astype(F32)
            dq_ext[:, cols] = dq * inv
            acc = dq_ext[pl.ds(0, ts), cols]
            for j in range(1, win):
                acc = acc + dq_ext[pl.ds(j, ts), cols]
            dpa = acc - dq[:ts]
            dp_ref[:, cols] = dpa.astype(BF16)
            _accumulate(dbin_ref.at[:, cols], _colsum(dpa), first)

        pb = p_ref[:, pl.ds(COL_B, 2 * SGU_WIDTH)]
        uv = _gelu(pb)
        u = uv[:, :SGU_WIDTH]
        vhat, vrstd = _ln_stats(uv[:, SGU_WIDTH:])
        vn = (vhat * slg_ref[...] + slb_ref[...]).astype(BF16)
        dyb = dm_ref[:, pl.ds(POOL_WIDTH, SGU_WIDTH)]
        dmix = dyb * u
        dmixb = dmix.astype(BF16)
        du_parts, dvn_parts = [], []
        for h in range(SGU_HEADS):
            cols = slice(h * GROUP, (h + 1) * GROUP)
            wm = _masked_sgu_w(ws_ref, h).astype(BF16)
            wmt = _masked_sgu_w_t(wst_ref, h).astype(BF16)
            dws_h = jnp.zeros((CHUNK, CHUNK), F32)
            dsb_h = jnp.zeros((CHUNK, GROUP), F32)
            du_rows, dvn_rows = [], []
            for n in range(nch):
                rows = slice(n * CHUNK, (n + 1) * CHUNK)
                mx = jnp.dot(wm, vn[rows, cols], preferred_element_type=F32) + sb_ref[h]
                du_rows.append(dyb[rows, cols] * mx)
                dws_h = dws_h + lax.dot_general(dmixb[rows, cols], vn[rows, cols], (((1,), (1,)), ((), ())),
                                                preferred_element_type=F32)
                dsb_h = dsb_h + dmix[rows, cols]
                dvn_rows.append(jnp.dot(wmt, dmixb[rows, cols], preferred_element_type=F32))
            r = lax.broadcasted_iota(jnp.int32, (CHUNK, CHUNK), 0)
            c = lax.broadcasted_iota(jnp.int32, (CHUNK, CHUNK), 1)
            _accumulate(dws_ref.at[h], jnp.where(r >= c, dws_h, 0.0), first)
            _accumulate(dsb_ref.at[h], jnp.broadcast_to(jnp.sum(dsb_h, axis=1, keepdims=True), (CHUNK, GROUP)), first)
            du_parts.append(jnp.concatenate(du_rows, axis=0))
            dvn_parts.append(jnp.concatenate(dvn_rows, axis=0))
        du = jnp.concatenate(du_parts, axis=1)
        dvn = jnp.concatenate(dvn_parts, axis=1)
        _accumulate(dslg_ref, _colsum(dvn * vhat), first)
        _accumulate(dslb_ref, _colsum(dvn), first)
        dv = _ln_bwd(dvn, vhat, vrstd, slg_ref[...])
        dpb = jnp.concatenate([du, dv], axis=1) * _gelu_grad(pb)
        dp_ref[:, pl.ds(COL_B, 2 * SGU_WIDTH)] = dpb.astype(BF16)
        _accumulate(dbin_ref.at[:, pl.ds(COL_B, 2 * SGU_WIDTH)], _colsum(dpb), first)

        a_main = p_ref[:, pl.ds(COL_C, CONV_WIDTH)]
        sg_main = _sigmoid(p_ref[:, pl.ds(COL_C + CONV_WIDTH, CONV_WIDTH)])
        hh_ext[pl.ds(0, HALO), :] = (ph_ref[:, pl.ds(COL_C, CONV_WIDTH)]
                                     * _sigmoid(ph_ref[:, pl.ds(COL_C + CONV_WIDTH, CONV_WIDTH)])) * keep_prev
        hh_ext[pl.ds(HALO, ts), :] = a_main * sg_main

        def conv_ln_backward(cv_v, dyc_v):
            cvhat, crstd = _ln_stats(cv_v)
            cn = cvhat * clg_ref[...] + clb_ref[...]
            s = _sigmoid(cn)
            dcn = dyc_v * (s * (1.0 + cn * (1.0 - s)))
            return _ln_bwd(dcn, cvhat, crstd, clg_ref[...]), dcn, cvhat

        dcv, dcn, cvhat = conv_ln_backward(cv_ref[...], dm_ref[:, pl.ds(col_yc, CONV_WIDTH)])
        _accumulate(dclg_ref, _colsum(dcn * cvhat), first)
        _accumulate(dclb_ref, _colsum(dcn), first)
        _accumulate(dcb_ref, _colsum(dcv), first)
        dcv_next, _, _ = conv_ln_backward(cvn_ref[...], dmn_ref[:, pl.ds(col_yc, CONV_WIDTH)])
        dcv_ext[pl.ds(0, ts), :] = dcv
        dcv_ext[pl.ds(ts, HALO), :] = dcv_next * keep_next

        @pl.when(first)
        def _():
            dcw_acc[...] = jnp.zeros_like(dcw_acc)

        nrb = ts // RB_CONV
        for cg in range(CONV_WIDTH // GROUP):
            cols = pl.ds(cg * GROUP, GROUP)
            dblk = [dcv_ext[pl.ds(rb * RB_CONV, RB_CONV), cols] for rb in range(nrb)]
            for k in range(CONV_KERNEL):
                part = jnp.zeros((8, GROUP), F32)
                for rb in range(nrb):
                    prod = dblk[rb] * hh_ext[pl.ds(rb * RB_CONV + k + HALO - (CONV_KERNEL - 1), RB_CONV), cols]
                    part = part + jnp.sum(prod.reshape(RB_CONV // 8, 8, GROUP), axis=0)
                dcw_acc[k, :, cols] += part

        @pl.when(i == nt - 1)
        def _():
            dcw_ref[...] = jnp.sum(dcw_acc[...], axis=1)

        zero = lambda cg: jnp.zeros((RB_CONV, GROUP), F32)
        for r0, c0, blk in _conv_taps(zero, cwr_ref, dcv_ext, ts, lambda k: k):
            rows, cols = pl.ds(r0, RB_CONV), pl.ds(c0, GROUP)
            a_blk = p_ref[rows, pl.ds(COL_C + c0, GROUP)]
            s_blk = _sigmoid(p_ref[rows, pl.ds(COL_C + CONV_WIDTH + c0, GROUP)])
            da = blk * s_blk
            dg = blk * a_blk * s_blk * (1.0 - s_blk)
            dp_ref[rows, pl.ds(COL_C + c0, GROUP)] = da.astype(BF16)
            dp_ref[rows, pl.ds(COL_C + CONV_WIDTH + c0, GROUP)] = dg.astype(BF16)
            hh_ext[pl.ds(HALO + r0, RB_CONV), cols] = da
            dcv_ext[rows, cols] = dg
        _accumulate(dbin_ref.at[:, pl.ds(COL_C, CONV_WIDTH)], _colsum(hh_ext[pl.ds(HALO, ts), :]), first)
        _accumulate(dbin_ref.at[:, pl.ds(COL_C + CONV_WIDTH, CONV_WIDTH)], _colsum(dcv_ext[pl.ds(0, ts), :]), first)

    full = lambda a: pl.BlockSpec(a.shape, lambda i: (0,) * a.ndim)
    weights = [lw["w_pool"], _row(lw["pool_scale"]), _row(lw["sgu_ln_g"]), _row(lw["sgu_ln_b"]), lw["sgu_w"], lw["sgu_w_t"],
               lw["sgu_bias_tile"], lw["conv_w_full"], lw["conv_w_rev"], _row(lw["conv_ln_g"]), _row(lw["conv_ln_b"])]
    prev_halo = lambda i: (jnp.maximum(i * per_halo - 1, 0), 0)
    next_halo = lambda i: (jnp.minimum((i + 1) * per_halo, s_len // HALO - 1), 0)
    small = lambda shape: (jax.ShapeDtypeStruct(shape, F32), pl.BlockSpec(shape, lambda i: (0,) * len(shape)))
    outs = [(jax.ShapeDtypeStruct((s_len, IN_WIDTH), BF16), pl.BlockSpec((ts, IN_WIDTH), lambda i: (i, 0))),
            small((1, IN_WIDTH)), small((len(POOL_WINDOWS), GROUP, GROUP)), small((1, POOL_WIDTH)),
            small((1, SGU_WIDTH)), small((1, SGU_WIDTH)), small((SGU_HEADS, CHUNK, CHUNK)), small((SGU_HEADS, CHUNK, GROUP)),
            small((CONV_ROWS, CONV_WIDTH)), small((1, CONV_WIDTH)), small((1, CONV_WIDTH)), small((1, CONV_WIDTH))]
    return pl.pallas_call(
        body, name=name, grid=(nt,),
        in_specs=[pl.BlockSpec((ts, IN_WIDTH), lambda i: (i, 0)), pl.BlockSpec((HALO, IN_WIDTH), prev_halo),
                  pl.BlockSpec((ts, CONV_WIDTH), lambda i: (i, 0)), pl.BlockSpec((HALO, CONV_WIDTH), next_halo),
                  pl.BlockSpec((ts, d), lambda i: (i, 0)), pl.BlockSpec((HALO, d), next_halo)] + [full(a) for a in weights],
        out_specs=[s for _, s in outs],
        out_shape=[o for o, _ in outs],
        scratch_shapes=[pltpu.VMEM((16 + ts, POOL_WIDTH), F32), pltpu.VMEM((ts + HALO, POOL_WIDTH), F32),
                        pltpu.VMEM((HALO + ts, CONV_WIDTH), F32), pltpu.VMEM((ts + HALO, CONV_WIDTH), F32),
                        pltpu.VMEM((CONV_ROWS, 8, CONV_WIDTH), F32)],
        compiler_params=_params(1),
    )(p, p, cv, cv, dmixed, dmixed, *weights)


def _masked_sgu_w_t(wt_ref, h):
    r = lax.broadcasted_iota(jnp.int32, (CHUNK, CHUNK), 0)
    c = lax.broadcasted_iota(jnp.int32, (CHUNK, CHUNK), 1)
    return jnp.where(c >= r, wt_ref[h], 0.0)


HBM = pl.BlockSpec(memory_space=pltpu.HBM)
CHIP_FLIPS = ((1, 0), (0, 1), (1, 1))


def _place():
    return lax.axis_index("x"), lax.axis_index("y"), lax.axis_index("c")


def _half(ref, axis, which, size):
    idx = [slice(None)] * len(ref.shape)
    idx[axis] = pl.ds(which * size, size)
    return ref.at[tuple(idx)]


def _exchange(name, sources, inplace, fresh, copies):
    n_src, n_in, n = len(sources), len(inplace), len(copies)
    n_out = n_in + len(fresh)

    def body(*refs):
        ins = refs[:n_src + n_in]
        outs = refs[n_src + n_in:n_src + n_in + n_out]
        send_sems, recv_sems = refs[n_src + n_in + n_out:]
        bufs = list(ins[:n_src]) + list(outs)
        x, y, c = _place()
        started = []
        for k, (src, src_view, dst, dst_view, peer) in enumerate(copies):
            cp = pltpu.make_async_remote_copy(
                src_ref=src_view(bufs[src], x, y, c), dst_ref=dst_view(bufs[dst], x, y, c),
                send_sem=send_sems.at[k], recv_sem=recv_sems.at[k], device_id=peer(x, y, c), device_id_type=MESH)
            cp.start()
            started.append(cp)
        for cp in started:
            cp.wait()

    out_shape = [jax.ShapeDtypeStruct(a.shape, a.dtype) for a in inplace] + list(fresh)
    return pl.pallas_call(
        body, name=name, in_specs=[HBM] * (n_src + n_in), out_specs=[HBM] * n_out, out_shape=out_shape,
        input_output_aliases={n_src + i: i for i in range(n_in)},
        scratch_shapes=[pltpu.SemaphoreType.DMA((n,)), pltpu.SemaphoreType.DMA((n,))],
    )(*sources, *inplace)


def _into_slot(name, src, layer, place, dtype):
    _, r, c = src.shape
    br = _tile(r, BR_EW) if r % BR_EW == 0 else r

    def body(pr_ref, s_ref, o_ref):
        o_ref[...] = s_ref[...].astype(dtype)

    return pl.pallas_call(
        body, name=name,
        grid_spec=pltpu.PrefetchScalarGridSpec(
            num_scalar_prefetch=1, grid=(r // br,),
            in_specs=[pl.BlockSpec((None, br, c), lambda i, pr: (layer, i, 0))],
            out_specs=pl.BlockSpec((None, br, c), lambda i, pr: (pr[1], i, 0))),
        out_shape=jax.ShapeDtypeStruct((4, r, c), dtype), compiler_params=_params(1),
    )(place, src)


def _gather_group(name, bufs):
    def own_half(ref, x, y, c):
        return _half(ref.at[2 * x + y], 0, c, ref.shape[1] // 2)

    ici, d2d = [], []
    for b in range(len(bufs)):
        for fx, fy in CHIP_FLIPS:
            ici.append((b, own_half, b, own_half, lambda x, y, c, fx=fx, fy=fy: (x ^ fx, y ^ fy, c)))
            landed = lambda ref, x, y, c, fx=fx, fy=fy: _half(ref.at[2 * (x ^ fx) + (y ^ fy)], 0, c, ref.shape[1] // 2)
            d2d.append((b, landed, b, landed, _sibling))
    bufs = _exchange(name + "_chips", [], bufs, [], ici)
    return _exchange(name + "_sibling", [], bufs, [], d2d)


def _sibling(x, y, c):
    return (x, y, 1 - c)


def _scalar_spec_call(name, fn, scalars, ins, in_blocks, out_shapes, out_blocks, grid):
    def body(s_ref, *refs):
        res = fn(*[r[...] for r in refs[:len(ins)]])
        for o_ref, v in zip(refs[len(ins):], res):
            o_ref[...] = v.astype(o_ref.dtype)

    return pl.pallas_call(
        body, name=name,
        grid_spec=pltpu.PrefetchScalarGridSpec(num_scalar_prefetch=1, grid=grid, in_specs=in_blocks, out_specs=out_blocks),
        out_shape=out_shapes, compiler_params=_params(len(grid)),
    )(scalars, *ins)


def _reduce(tag, place, items):
    n = len(items)
    whole = lambda ref, x, y, c: ref
    slot_of = lambda scatter: (lambda x, y: 2 * x + y) if scatter else (lambda x, y: 0)
    halves = [g.shape[1] // 2 for g, _, _ in items]

    fresh = [jax.ShapeDtypeStruct((g.shape[0], halves[a], g.shape[2]), F32) for a, (g, _, _) in enumerate(items)]
    copies = [(a, lambda ref, x, y, c, hr=halves[a]: _half(ref, 1, 1 - c, hr), n + a, whole, _sibling) for a in range(n)]
    landed = _exchange(tag + "_sibling_in", [g for g, _, _ in items], [], fresh, copies)

    chip_sums = []
    for a, (g, _, wire) in enumerate(items):
        ns, _, cols = g.shape
        hr = halves[a]
        br = _tile(hr, BR_EW)
        chip_sums.append(_scalar_spec_call(
            f"{tag}_chip_sum{a}", lambda u, v: (u + v,), place, [g, landed[a]],
            [pl.BlockSpec((None, br, cols), lambda s, i, pr, hb=hr // br: (s, pr[0] * hb + i, 0)),
             pl.BlockSpec((None, br, cols), lambda s, i, pr: (s, i, 0))],
            [jax.ShapeDtypeStruct((ns, hr, cols), wire)], [pl.BlockSpec((None, br, cols), lambda s, i, pr: (s, i, 0))],
            (ns, hr // br))[0])

    copies, fresh = [], []
    for a, (g, scatter, wire) in enumerate(items):
        slot = slot_of(scatter)
        fresh.append(jax.ShapeDtypeStruct((3, halves[a], g.shape[2]), wire))
        for j, (fx, fy) in enumerate(CHIP_FLIPS):
            copies.append((a, lambda ref, x, y, c, fx=fx, fy=fy, slot=slot: ref.at[slot(x ^ fx, y ^ fy)],
                           n + a, lambda ref, x, y, c, j=j: ref.at[j], lambda x, y, c, fx=fx, fy=fy: (x ^ fx, y ^ fy, c)))
    arrived = _exchange(tag + "_chips", chip_sums, [], fresh, copies)

    tree = lambda own, fx, fy, fxy: ((own.astype(F32) + fx.astype(F32)) + (fy.astype(F32) + fxy.astype(F32)),)
    mine = []
    for a, (g, scatter, _) in enumerate(items):
        hr, cols = halves[a], g.shape[2]
        br = _tile(hr, BR_EW)
        got = lambda j: pl.BlockSpec((None, br, cols), lambda i, pr, j=j: (j, i, 0))
        own = pl.BlockSpec((None, br, cols), (lambda i, pr: (pr[1], i, 0)) if scatter else (lambda i, pr: (0, i, 0)))
        mine.append(_scalar_spec_call(
            f"{tag}_tree_sum{a}", tree, place, [chip_sums[a], arrived[a], arrived[a], arrived[a]],
            [own, got(0), got(1), got(2)],
            [jax.ShapeDtypeStruct((hr, cols), F32)], [pl.BlockSpec((br, cols), lambda i, pr: (i, 0))], (hr // br,))[0])

    copies = [(a, whole, n + a, whole, _sibling) for a in range(n)]
    fresh = [jax.ShapeDtypeStruct(h.shape, F32) for h in mine]
    theirs = _exchange(tag + "_sibling_out", mine, [], fresh, copies)
    return list(zip(mine, theirs))


def _select_half(h, c, mine, theirs):
    return jnp.where(h == c, mine, theirs)


def _adamw_sharded(name, layer, place, w, m, v, g_mine, g_theirs, earlier):
    n_layers, r, c = w.shape
    hr = r // 2
    br = min(hr, max(8, 2 ** int(math.log2(EW_BLOCK_ELEMS // c))))
    assert hr % br == 0
    nb = hr // br
    full = pl.BlockSpec((None, br, c), lambda h, i, pr: (layer, h * nb + i, 0))
    half = pl.BlockSpec((br, c), lambda h, i, pr: (i, 0))
    n_alias = 0 if earlier is None else 4

    def body(pr_ref, w_ref, m_ref, v_ref, gm_ref, gt_ref, *rest):
        g_ref, d_ref, nm_ref, nv_ref = rest[n_alias:]
        g = _select_half(pl.program_id(0), pr_ref[0], gm_ref[...], gt_ref[...])
        delta, new_m, new_v = _adamw_math(w_ref[...], g, m_ref[...], v_ref[...])
        g_ref[...] = g
        d_ref[...] = delta
        nm_ref[...] = new_m
        nv_ref[...] = new_v

    return pl.pallas_call(
        body, name=name,
        grid_spec=pltpu.PrefetchScalarGridSpec(
            num_scalar_prefetch=1, grid=(2, nb),
            in_specs=[full, full, full, half, half] + [pl.BlockSpec(memory_space=pl.ANY)] * n_alias,
            out_specs=[full] * 4),
        out_shape=[jax.ShapeDtypeStruct((n_layers, r, c), F32)] * 4,
        input_output_aliases={6 + i: i for i in range(n_alias)},
        compiler_params=_params(2),
    )(place, w, m, v, g_mine, g_theirs, *(earlier or ()))


SHARDED = ("w_in", "w_out", "w_ff1", "w_ff2")
REPLICATED = ("b_in", "w_pool", "pool_scale", "sgu_ln_g", "sgu_ln_b", "sgu_w", "sgu_b", "conv_b", "conv_ln_g", "conv_ln_b",
              "b_out", "ln1_g", "ln1_b", "b_ff1", "b_ff2", "ln2_g", "ln2_b")
WEIGHTS = ("w_in", "b_in", "w_pool", "pool_scale", "sgu_ln_g", "sgu_ln_b", "sgu_w", "sgu_b", "conv_w", "conv_b", "conv_ln_g",
           "conv_ln_b", "w_out", "b_out", "ln1_g", "ln1_b", "w_ff1", "b_ff1", "w_ff2", "b_ff2", "ln2_g", "ln2_b")
PACK_ROWS = 1024


def _pack(arrays):
    flat = jnp.concatenate([a.reshape(-1) for a in arrays])
    rows = -(-flat.shape[0] // (128 * PACK_ROWS)) * PACK_ROWS
    return jnp.pad(flat, (0, rows * 128 - flat.shape[0])).reshape(rows, 128)


def _unpack(packed, like):
    flat = packed.reshape(-1)
    out, at = [], 0
    for a in like:
        out.append(flat[at:at + a.size].reshape(a.shape))
        at += a.size
    return out


def kernel(x, w_in, b_in, w_pool, pool_scale, sgu_ln_g, sgu_ln_b, sgu_w, sgu_b, conv_w, conv_b, conv_ln_g, conv_ln_b, w_out, b_out, ln1_g, ln1_b, w_ff1, b_ff1, w_ff2, b_ff2, ln2_g, ln2_b, loss_target, m_w_in, m_b_in, m_w_pool, m_pool_scale, m_sgu_ln_g, m_sgu_ln_b, m_sgu_w, m_sgu_b, m_conv_w, m_conv_b, m_conv_ln_g, m_conv_ln_b, m_w_out, m_b_out, m_ln1_g, m_ln1_b, m_w_ff1, m_b_ff1, m_w_ff2, m_b_ff2, m_ln2_g, m_ln2_b, v_w_in, v_b_in, v_w_pool, v_pool_scale, v_sgu_ln_g, v_sgu_ln_b, v_sgu_w, v_sgu_b, v_conv_w, v_conv_b, v_conv_ln_g, v_conv_ln_b, v_w_out, v_b_out, v_ln1_g, v_ln1_b, v_w_ff1, v_b_ff1, v_w_ff2, v_b_ff2, v_ln2_g, v_ln2_b):
    w = dict(w_in=w_in, b_in=b_in, w_pool=w_pool, pool_scale=pool_scale, sgu_ln_g=sgu_ln_g, sgu_ln_b=sgu_ln_b, sgu_w=sgu_w,
             sgu_b=sgu_b, conv_w=conv_w, conv_b=conv_b, conv_ln_g=conv_ln_g, conv_ln_b=conv_ln_b, w_out=w_out, b_out=b_out,
             ln1_g=ln1_g, ln1_b=ln1_b, w_ff1=w_ff1, b_ff1=b_ff1, w_ff2=w_ff2, b_ff2=b_ff2, ln2_g=ln2_g, ln2_b=ln2_b)
    m = dict(w_in=m_w_in, b_in=m_b_in, w_pool=m_w_pool, pool_scale=m_pool_scale, sgu_ln_g=m_sgu_ln_g, sgu_ln_b=m_sgu_ln_b,
             sgu_w=m_sgu_w, sgu_b=m_sgu_b, conv_w=m_conv_w, conv_b=m_conv_b, conv_ln_g=m_conv_ln_g, conv_ln_b=m_conv_ln_b,
             w_out=m_w_out, b_out=m_b_out, ln1_g=m_ln1_g, ln1_b=m_ln1_b, w_ff1=m_w_ff1, b_ff1=m_b_ff1, w_ff2=m_w_ff2,
             b_ff2=m_b_ff2, ln2_g=m_ln2_g, ln2_b=m_ln2_b)
    v = dict(w_in=v_w_in, b_in=v_b_in, w_pool=v_w_pool, pool_scale=v_pool_scale, sgu_ln_g=v_sgu_ln_g, sgu_ln_b=v_sgu_ln_b,
             sgu_w=v_sgu_w, sgu_b=v_sgu_b, conv_w=v_conv_w, conv_b=v_conv_b, conv_ln_g=v_conv_ln_g, conv_ln_b=v_conv_ln_b,
             w_out=v_w_out, b_out=v_b_out, ln1_g=v_ln1_g, ln1_b=v_ln1_b, w_ff1=v_w_ff1, b_ff1=v_b_ff1, w_ff2=v_w_ff2,
             b_ff2=v_b_ff2, ln2_g=v_ln2_g, ln2_b=v_ln2_b)
    assert x.shape[0] == 1 and x.shape[2] == POOL_WIDTH + SGU_WIDTH + CONV_WIDTH, x.shape
    xs, target = x[0], loss_target[0]
    s_len, d = xs.shape
    n_layers = w_in.shape[0]
    dff = 4 * w_ff1.shape[2]
    conv_shard = conv_w.shape[2]

    cx, cy, cc = _place()
    me = 2 * cx + cy
    place = jnp.stack([cc, me]).astype(jnp.int32)

    slot = {(n, l): _into_slot(f"cast_{n}{l}", w[n], l, place, BF16) for n in SHARDED for l in range(n_layers)}
    conv_padded = jnp.pad(conv_w, ((0, 0), (0, CONV_ROWS - CONV_KERNEL), (0, 0))).reshape(1, n_layers * CONV_ROWS, conv_shard)
    conv_slot = _into_slot("slot_conv_w", conv_padded, 0, place, F32)
    gathered = {}
    groups = [("gather_mix0", [("w_in", 0), ("w_out", 0), "conv_w"]), ("gather_ff0", [("w_ff1", 0), ("w_ff2", 0)])]
    groups += [(f"gather_layer{l}", [(n, l) for n in SHARDED]) for l in range(1, n_layers)]
    for name, keys in groups:
        bufs = _gather_group(name, [conv_slot if k == "conv_w" else slot[k] for k in keys])
        gathered.update(zip(keys, bufs))
    conv_full = jnp.transpose(gathered["conv_w"].reshape(4, n_layers, CONV_ROWS, conv_shard), (1, 2, 0, 3))
    conv_full = conv_full.reshape(n_layers, CONV_ROWS, CONV_WIDTH)
    conv_rev = jnp.pad(conv_full[:, CONV_KERNEL - 1::-1], ((0, 0), (0, CONV_ROWS - CONV_KERNEL), (0, 0)))

    def layer_weights(l):
        return dict(w_pool=w_pool[l], pool_scale=pool_scale[l], sgu_ln_g=sgu_ln_g[l], sgu_ln_b=sgu_ln_b[l], sgu_w=sgu_w[l],
                    sgu_w_t=jnp.transpose(sgu_w[l], (0, 2, 1)),
                    sgu_bias_tile=jnp.broadcast_to(sgu_b[l][:, :, None], (SGU_HEADS, CHUNK, GROUP)),
                    conv_w_full=conv_full[l], conv_w_rev=conv_rev[l], conv_b=conv_b[l], conv_ln_g=conv_ln_g[l],
                    conv_ln_b=conv_ln_b[l])

    saved = []
    x_cur, xb_cur = xs, _rowwise("cast_x", lambda t: (t,), [xs], [BF16])[0]
    for l in range(n_layers):
        lw = layer_weights(l)
        w_out_full = gathered[("w_out", l)].reshape(d, d)
        w2_full = gathered[("w_ff2", l)].reshape(dff, d)
        p = _proj(f"proj{l}", xb_cur, gathered[("w_in", l)], b_in[l])
        mixed, cv = _mix_forward(f"mix_fwd{l}", p, lw)
        r1, x1b = _mix_out(f"mix_out{l}", mixed, w_out_full, b_out[l], x_cur, ln1_g[l], ln1_b[l])
        hf, zr = _ff1(f"ff1_{l}", x1b, gathered[("w_ff1", l)], b_ff1[l])
        fo = _ff2(f"ff2_{l}", hf, w2_full, b_ff2[l])
        r2, x2, x2b = _resid_ln2(f"ln2_{l}", r1, ln1_g[l], ln1_b[l], fo, ln2_g[l], ln2_b[l])
        saved.append(dict(lw=lw, xb_in=xb_cur, p=p, mixed=mixed, cv=cv, r1=r1, x1b=x1b, hf=hf, zr=zr, r2=r2,
                          w_out_full=w_out_full, w2_full=w2_full))
        x_cur, xb_cur = x2, x2b

    grads = {n: [None] * n_layers for n in REPLICATED + ("conv_w",)}
    g_final, delta, new_m, new_v = {}, {}, {}, {}
    results = {n: None for n in SHARDED}
    dx_mm, dx_resid = None, None
    loss_tile = None
    small_like = None
    for l in reversed(range(n_layers)):
        sv = saved[l]
        if dx_mm is None:
            dr2, dr2b, dg2, db2, dbff2, loss_tile = _ln_backward(f"ln2_bwd{l}", sv["r2"], ln2_g[l], x=x_cur, target=target)
        else:
            dr2, dr2b, dg2, db2, dbff2 = _ln_backward(f"ln2_bwd{l}", sv["r2"], ln2_g[l], dy=dx_mm, resid=dx_resid)
        dzb, dbff1 = _dff_hidden(f"ff2_bwd{l}", dr2b, sv["w2_full"], sv["zr"])
        dw = {"w_ff2": _matmul_tn(f"dw_ff2_{l}", sv["hf"], dr2b, 4, dff // 4, d, True),
              "w_ff1": _matmul_tn(f"dw_ff1_{l}", sv["x1b"], dzb, 4, d, dff // 4, False)}
        dx1 = _dx_sharded(f"ff1_bwd{l}", dzb, gathered[("w_ff1", l)])
        dr1, dr1b, dg1, db1, dbout = _ln_backward(f"ln1_bwd{l}", sv["r1"], ln1_g[l], dy=dx1, resid=dr2)
        dmixed = _dmixed(f"mix_out_bwd{l}", dr1b, sv["w_out_full"])
        dw["w_out"] = _matmul_tn(f"dw_out{l}", sv["mixed"], dr1b, 4, d // 4, d, True)
        (dp, dbin, dwp, dps, dslg, dslb, dws, dsb_tile, dcw, dcb, dclg, dclb) = _mix_backward(
            f"mix_bwd{l}", sv["p"], sv["cv"], dmixed, sv["lw"])
        dw["w_in"] = _matmul_tn(f"dw_in{l}", sv["xb_in"], dp, 4, d, IN_WIDTH // 4, False)
        if l > 0:
            dx_mm, dx_resid = _dx_sharded(f"proj_bwd{l}", dp, gathered[("w_in", l)]), dr1
        else:
            grad_x = _dx_sharded(f"proj_bwd{l}", dp, gathered[("w_in", l)], dr1)[None]
        for name, g in (("b_in", dbin), ("w_pool", dwp), ("pool_scale", dps), ("sgu_ln_g", dslg), ("sgu_ln_b", dslb),
                        ("sgu_w", dws), ("sgu_b", dsb_tile[:, :, 0]), ("conv_w", dcw[:CONV_KERNEL]), ("conv_b", dcb),
                        ("conv_ln_g", dclg), ("conv_ln_b", dclb), ("b_out", dbout), ("ln1_g", dg1), ("ln1_b", db1),
                        ("b_ff1", dbff1), ("b_ff2", dbff2), ("ln2_g", dg2), ("ln2_b", db2)):
            grads[name][l] = g.reshape(w[name].shape[1:]) if name != "conv_w" else g

        items = [(dw[n], True, BF16) for n in SHARDED]
        if l == 0:
            small_like = [jnp.stack(grads[n]) for n in REPLICATED + ("conv_w",)]
            items.append((_pack(small_like)[None], False, F32))
        reduced = _reduce(f"grads{l}", place, items)
        for n, (mine, theirs) in zip(SHARDED, reduced):
            results[n] = _adamw_sharded(f"adamw_{n}{l}", l, place, w[n], m[n], v[n], mine, theirs, results[n])
        if l == 0:
            mine, theirs = reduced[-1]
            reduced_small = jnp.where(cc == 0, jnp.concatenate([mine, theirs]), jnp.concatenate([theirs, mine]))

    loss = lax.psum(loss_tile[0, 0], ("x", "y", "c"))
    for n in SHARDED:
        g_final[n], delta[n], new_m[n], new_v[n] = results[n]
    unpacked = _unpack(reduced_small, small_like)
    g_final.update(zip(REPLICATED, unpacked[:-1]))
    g_final["conv_w"] = lax.dynamic_slice_in_dim(unpacked[-1], me * conv_shard, conv_shard, axis=2)

    delta["conv_w"], new_m["conv_w"], new_v["conv_w"] = _adamw("adamw_conv_w", conv_w, g_final["conv_w"], m["conv_w"], v["conv_w"])
    packed = [_pack([t[n] for n in REPLICATED]) for t in (w, g_final, m, v)]
    like = [w[n] for n in REPLICATED]
    for res, packed_out in zip((delta, new_m, new_v), _adamw("adamw_replicated", *packed)):
        res.update(zip(REPLICATED, _unpack(packed_out, like)))

    return (loss, grad_x, *[g_final[n] for n in WEIGHTS], *[delta[n] for n in WEIGHTS],
            *[new_m[n] for n in WEIGHTS], *[new_v[n] for n in WEIGHTS])
```

```python
import functools
import math

import jax
import jax.numpy as jnp
from jax import lax
from jax.experimental import pallas as pl
from jax.experimental.pallas import tpu as pltpu

F32 = jnp.float32
BF16 = jnp.bfloat16
MESH = pl.DeviceIdType.MESH

DEPTH = 2
POOL_WINDOWS = (2, 4, 8, 16)
POOL_WIDTH = 512
GROUP = 128
SGU_WIDTH = 768
SGU_HEADS = 6
CHUNK = 128
CONV_WIDTH = 768
CONV_KERNEL = 31
CONV_ROWS = 32
HALO = 32
COL_B = POOL_WIDTH
COL_C = POOL_WIDTH + 2 * SGU_WIDTH
IN_WIDTH = COL_C + 2 * CONV_WIDTH
ALPHA = (2 * DEPTH) ** 0.25
LN_EPS = 1e-5
ADAM_LR = 0.001
ADAM_B1 = 0.9
ADAM_B2 = 0.999
ADAM_EPS = 1e-08
ADAM_WD = 0.01
ADAM_STEP = 10
GELU_C = math.sqrt(2.0 / math.pi)
GELU_A = 0.044715

V7X_VMEM_BYTES = 64 * 2 ** 20
VMEM_LIMIT = 56 * 2 ** 20

BM_MM = 1024
BN_MM = 1024
BM_LN = 512
BK_MM = 1024
BS_TN = 1024
TS_MIX = 256
RB_CONV = 64
BR_EW = 512
EW_BLOCK_ELEMS = 2 ** 18


def _tile(n, pref):
    t = min(n, pref)
    assert n % t == 0, (n, pref)
    return t


def _params(n_grid):
    return pltpu.CompilerParams(dimension_semantics=("arbitrary",) * n_grid, vmem_limit_bytes=VMEM_LIMIT)


def _sigmoid(x):
    return 1.0 / (1.0 + jnp.exp(-x))


def _gelu(x):
    return 0.5 * x * (1.0 + jnp.tanh(GELU_C * (x + GELU_A * x * x * x)))


def _gelu_grad(x):
    t = jnp.tanh(GELU_C * (x + GELU_A * x * x * x))
    return 0.5 * (1.0 + t) + 0.5 * x * (1.0 - t * t) * GELU_C * (1.0 + 3.0 * GELU_A * x * x)


def _ln_stats(r):
    mu = jnp.mean(r, axis=-1, keepdims=True)
    xc = r - mu
    var = jnp.mean(xc * xc, axis=-1, keepdims=True)
    rstd = lax.rsqrt(var + LN_EPS)
    return xc * rstd, rstd


def _ln_bwd(dy, xhat, rstd, g):
    dxh = dy * g
    m1 = jnp.mean(dxh, axis=-1, keepdims=True)
    m2 = jnp.mean(dxh * xhat, axis=-1, keepdims=True)
    return rstd * (dxh - m1 - xhat * m2)


def _colsum(x):
    return jnp.sum(x, axis=0, keepdims=True)


def _accumulate(ref, val, first):
    @pl.when(first)
    def _():
        ref[...] = val

    @pl.when(jnp.logical_not(first))
    def _():
        ref[...] += val


def _matmul(name, grid, a, a_spec, b, b_spec, *, nt, extras, outs, epilogue, acc_shape=None, after=None):
    nk = grid[2]
    if after is not None:
        extras = list(extras) + [(after, pl.BlockSpec(memory_space=pl.ANY))]
    ne, no = len(extras), len(outs)
    dims = (((1,), (1,)), ((), ())) if nt else (((1,), (0,)), ((), ()))

    def body(*refs):
        a_ref, b_ref = refs[0], refs[1]
        ex = refs[2:2 + ne]
        out_refs = refs[2 + ne:2 + ne + no]
        ids = (pl.program_id(0), pl.program_id(1), pl.program_id(2))
        part = lax.dot_general(a_ref[...], b_ref[...], dims, preferred_element_type=F32)
        if nk == 1:
            epilogue(part, ex, out_refs, ids)
        else:
            acc_ref = refs[2 + ne + no]
            k = ids[2]

            @pl.when(k == 0)
            def _():
                acc_ref[...] = part

            @pl.when(k > 0)
            def _():
                acc_ref[...] += part

            @pl.when(k == nk - 1)
            def _():
                epilogue(acc_ref[...], ex, out_refs, ids)

    return pl.pallas_call(
        body,
        name=name,
        grid=grid,
        in_specs=[a_spec, b_spec] + [s for _, s in extras],
        out_specs=[s for _, s in outs],
        out_shape=[o for o, _ in outs],
        scratch_shapes=[pltpu.VMEM(acc_shape, F32)] if nk > 1 else [],
        compiler_params=_params(3),
    )(a, b, *[e for e, _ in extras])


def _matmul_tn(name, a, b, n_shards, shard_rows, shard_cols, row_sharded):
    s_len, ka = a.shape
    n = b.shape[1]
    assert (n_shards * shard_rows, shard_cols) == (ka, n) if row_sharded else (shard_rows, n_shards * shard_cols) == (ka, n)
    bs = _tile(s_len, BS_TN)
    bka = _tile(shard_rows, 2048)
    bn = _tile(shard_cols, BN_MM) if shard_cols % BN_MM == 0 else shard_cols
    ni, nj, ns = ka // bka, n // bn, s_len // bs
    per_shard_i = shard_rows // bka
    per_shard_j = shard_cols // bn

    if row_sharded:
        out_map = lambda i, j, s: (i // per_shard_i, i % per_shard_i, j)
    else:
        out_map = lambda i, j, s: (j // per_shard_j, i, j % per_shard_j)

    def body(a_ref, b_ref, o_ref):
        s = pl.program_id(2)
        part = lax.dot_general(a_ref[...], b_ref[...], (((0,), (0,)), ((), ())), preferred_element_type=F32)
        _accumulate(o_ref, part, s == 0)

    return pl.pallas_call(
        body,
        name=name,
        grid=(ni, nj, ns),
        in_specs=[pl.BlockSpec((bs, bka), lambda i, j, s: (s, i)), pl.BlockSpec((bs, bn), lambda i, j, s: (s, j))],
        out_specs=pl.BlockSpec((None, bka, bn), out_map),
        out_shape=jax.ShapeDtypeStruct((n_shards, shard_rows, shard_cols), F32),
        compiler_params=_params(3),
    )(a, b)


def _row(v):
    return v.reshape(1, -1)


def _proj(name, xb, w_g, b_in, after=None):
    s_len, d = xb.shape
    ncs = w_g.shape[2]
    bm = _tile(s_len, BM_MM)

    def epilogue(acc, ex, outs, ids):
        outs[0][...] = acc + ex[0][...]

    return _matmul(
        name, (s_len // bm, 4, 1),
        xb, pl.BlockSpec((bm, d), lambda i, j, k: (i, 0)),
        w_g, pl.BlockSpec((None, d, ncs), lambda i, j, k: (j, 0, 0)),
        nt=False,
        extras=[(_row(b_in), pl.BlockSpec((1, ncs), lambda i, j, k: (0, j)))],
        outs=[(jax.ShapeDtypeStruct((s_len, 4 * ncs), F32), pl.BlockSpec((bm, ncs), lambda i, j, k: (i, j)))],
        epilogue=epilogue, after=after,
    )[0]


def _mix_out(name, mixed, w_out_full, b_out, x0, g1, b1):
    s_len, d = mixed.shape
    bm = _tile(s_len, BM_LN // 2)
    row = pl.BlockSpec((1, d), lambda i, j, k: (0, 0))
    blk = pl.BlockSpec((bm, d), lambda i, j, k: (i, 0))

    def epilogue(acc, ex, outs, ids):
        r1 = ALPHA * ex[1][...] + (acc + ex[0][...])
        outs[0][...] = r1
        xhat, _ = _ln_stats(r1)
        outs[1][...] = (xhat * ex[2][...] + ex[3][...]).astype(BF16)

    return _matmul(
        name, (s_len // bm, 1, 1),
        mixed, blk,
        w_out_full, pl.BlockSpec((d, d), lambda i, j, k: (0, 0)),
        nt=False,
        extras=[(_row(b_out), row), (x0, blk), (_row(g1), row), (_row(b1), row)],
        outs=[(jax.ShapeDtypeStruct((s_len, d), F32), blk), (jax.ShapeDtypeStruct((s_len, d), BF16), blk)],
        epilogue=epilogue,
    )


def _ff1(name, x1b, w_g, b_ff1, after=None):
    s_len, d = x1b.shape
    ncs = w_g.shape[2]
    bm = _tile(s_len, BM_MM)
    bn = _tile(ncs, BN_MM)
    per = ncs // bn
    blk = pl.BlockSpec((bm, bn), lambda i, j, k: (i, j))

    def epilogue(acc, ex, outs, ids):
        zr = jnp.maximum(acc + ex[0][...], 0.0)
        outs[0][...] = (zr * zr).astype(BF16)
        outs[1][...] = zr.astype(BF16)

    shape = jax.ShapeDtypeStruct((s_len, 4 * ncs), BF16)
    return _matmul(
        name, (s_len // bm, 4 * per, 1),
        x1b, pl.BlockSpec((bm, d), lambda i, j, k: (i, 0)),
        w_g, pl.BlockSpec((None, d, bn), lambda i, j, k: (j // per, 0, j % per)),
        nt=False,
        extras=[(_row(b_ff1), pl.BlockSpec((1, bn), lambda i, j, k: (0, j)))],
        outs=[(shape, blk), (shape, blk)],
        epilogue=epilogue, after=after,
    )


def _ff2(name, hf, w2_full, b_ff2):
    s_len, dff = hf.shape
    d = w2_full.shape[1]
    bm = _tile(s_len, BM_MM)
    bk = _tile(dff, BK_MM)
    blk = pl.BlockSpec((bm, d), lambda i, j, k: (i, 0))

    def epilogue(acc, ex, outs, ids):
        outs[0][...] = acc + ex[0][...]

    return _matmul(
        name, (s_len // bm, 1, dff // bk),
        hf, pl.BlockSpec((bm, bk), lambda i, j, k: (i, k)),
        w2_full, pl.BlockSpec((bk, d), lambda i, j, k: (k, 0)),
        nt=False,
        extras=[(_row(b_ff2), pl.BlockSpec((1, d), lambda i, j, k: (0, 0)))],
        outs=[(jax.ShapeDtypeStruct((s_len, d), F32), blk)],
        epilogue=epilogue,
        acc_shape=(bm, d),
    )[0]


def _resid_ln2(name, r1, g1, b1, fo, g2, b2):
    s_len, d = r1.shape
    bm = _tile(s_len, BM_LN // 2)
    blk = pl.BlockSpec((bm, d), lambda i: (i, 0))
    row = pl.BlockSpec((1, d), lambda i: (0, 0))

    def body(r1_ref, g1_ref, b1_ref, fo_ref, g2_ref, b2_ref, r2_ref, x2_ref, x2b_ref):
        xhat1, _ = _ln_stats(r1_ref[...])
        r2 = ALPHA * (xhat1 * g1_ref[...] + b1_ref[...]) + fo_ref[...]
        r2_ref[...] = r2
        xhat2, _ = _ln_stats(r2)
        x2 = xhat2 * g2_ref[...] + b2_ref[...]
        x2_ref[...] = x2
        x2b_ref[...] = x2.astype(BF16)

    return pl.pallas_call(
        body, name=name, grid=(s_len // bm,), in_specs=[blk, row, row, blk, row, row], out_specs=[blk, blk, blk],
        out_shape=[jax.ShapeDtypeStruct((s_len, d), F32), jax.ShapeDtypeStruct((s_len, d), F32),
                   jax.ShapeDtypeStruct((s_len, d), BF16)],
        compiler_params=_params(1))(r1, _row(g1), _row(b1), fo, _row(g2), _row(b2))


def _dff_hidden(name, dr2b, w2_full, zr):
    s_len, d = dr2b.shape
    dff = w2_full.shape[0]
    bm = _tile(s_len, BM_MM)
    bn = _tile(dff, BN_MM)

    def epilogue(acc, ex, outs, ids):
        dz = acc * (2.0 * ex[0][...].astype(F32))
        outs[0][...] = dz.astype(BF16)
        _accumulate(outs[1], _colsum(dz), ids[1] == 0)

    return _matmul(
        name, (dff // bn, s_len // bm, 1),
        dr2b, pl.BlockSpec((bm, d), lambda j, i, k: (i, 0)),
        w2_full, pl.BlockSpec((bn, d), lambda j, i, k: (j, 0)),
        nt=True,
        extras=[(zr, pl.BlockSpec((bm, bn), lambda j, i, k: (i, j)))],
        outs=[(jax.ShapeDtypeStruct((s_len, dff), BF16), pl.BlockSpec((bm, bn), lambda j, i, k: (i, j))),
              (jax.ShapeDtypeStruct((1, dff), F32), pl.BlockSpec((1, bn), lambda j, i, k: (0, j)))],
        epilogue=epilogue,
    )


def _dx_sharded(name, dyb, w_g, resid=None, after=None):
    s_len = dyb.shape[0]
    d, ncs = w_g.shape[1], w_g.shape[2]
    bm = _tile(s_len, BM_MM if resid is None else BM_LN)
    bk = _tile(ncs, BK_MM) if ncs % BK_MM == 0 else ncs
    per = ncs // bk
    blk = pl.BlockSpec((bm, d), lambda i, j, k: (i, 0))

    def epilogue(acc, ex, outs, ids):
        outs[0][...] = acc if resid is None else acc + ALPHA * ex[0][...]

    return _matmul(
        name, (s_len // bm, 1, 4 * per),
        dyb, pl.BlockSpec((bm, bk), lambda i, j, k: (i, k)),
        w_g, pl.BlockSpec((None, d, bk), lambda i, j, k: (k // per, 0, k % per)),
        nt=True,
        extras=[] if resid is None else [(resid, blk)],
        outs=[(jax.ShapeDtypeStruct((s_len, d), F32), blk)],
        epilogue=epilogue,
        acc_shape=(bm, d), after=after,
    )[0]


def _dmixed(name, dr1b, w_out_full):
    s_len, d = dr1b.shape
    bm = _tile(s_len, BM_LN)
    blk = pl.BlockSpec((bm, d), lambda i, j, k: (i, 0))

    def epilogue(acc, ex, outs, ids):
        outs[0][...] = acc

    return _matmul(
        name, (s_len // bm, 1, 1),
        dr1b, blk,
        w_out_full, pl.BlockSpec((d, d), lambda i, j, k: (0, 0)),
        nt=True, extras=[],
        outs=[(jax.ShapeDtypeStruct((s_len, d), F32), blk)],
        epilogue=epilogue,
    )[0]


def _ln_backward(name, r, g, *, dy=None, resid=None, x=None, target=None):
    s_len, d = r.shape
    bm = _tile(s_len, BM_LN // 2)
    from_loss = dy is None
    blk = pl.BlockSpec((bm, d), lambda i: (i, 0))
    row = pl.BlockSpec((1, d), lambda i: (0, 0))

    def body(*refs):
        i = pl.program_id(0)
        if from_loss:
            x_ref, t_ref, r_ref, g_ref, dr_ref, drb_ref, dg_ref, db_ref, dbias_ref, loss_ref = refs
            diff = x_ref[...] - t_ref[...]
            dyv = diff * (1.0 / d)
            part = 0.5 * jnp.sum(jnp.sum(diff * diff, axis=1, keepdims=True) * (1.0 / d), axis=0, keepdims=True)
            _accumulate(loss_ref, jnp.broadcast_to(part, loss_ref.shape), i == 0)
        elif resid is not None:
            dy_ref, res_ref, r_ref, g_ref, dr_ref, drb_ref, dg_ref, db_ref, dbias_ref = refs
            dyv = dy_ref[...] + ALPHA * res_ref[...]
        else:
            dy_ref, r_ref, g_ref, dr_ref, drb_ref, dg_ref, db_ref, dbias_ref = refs
            dyv = dy_ref[...]
        xhat, rstd = _ln_stats(r_ref[...])
        dr = _ln_bwd(dyv, xhat, rstd, g_ref[...])
        dr_ref[...] = dr
        drb_ref[...] = dr.astype(BF16)
        _accumulate(dg_ref, _colsum(dyv * xhat), i == 0)
        _accumulate(db_ref, _colsum(dyv), i == 0)
        _accumulate(dbias_ref, _colsum(dr), i == 0)

    lead = [x, target] if from_loss else ([dy] if resid is None else [dy, resid])
    ins = lead + [r, _row(g)]
    in_specs = [blk] * len(lead) + [blk, row]
    out_shape = [jax.ShapeDtypeStruct((s_len, d), F32), jax.ShapeDtypeStruct((s_len, d), BF16)] + \
                [jax.ShapeDtypeStruct((1, d), F32)] * 3
    out_specs = [blk, blk, row, row, row]
    if from_loss:
        out_shape.append(jax.ShapeDtypeStruct((8, 128), F32))
        out_specs.append(pl.BlockSpec((8, 128), lambda i: (0, 0)))
    return pl.pallas_call(body, name=name, grid=(s_len // bm,), in_specs=in_specs, out_specs=out_specs,
                          out_shape=out_shape, compiler_params=_params(1))(*ins)


def _rowwise(name, fn, ins, out_dtypes, rows_pref=BR_EW):
    r, c = ins[0].shape
    br = _tile(r, rows_pref)
    blk = pl.BlockSpec((br, c), lambda i: (i, 0))

    def body(*refs):
        res = fn(*[ref[...] for ref in refs[:len(ins)]])
        for o_ref, v in zip(refs[len(ins):], res):
            o_ref[...] = v.astype(o_ref.dtype)

    return pl.pallas_call(body, name=name, grid=(r // br,), in_specs=[blk] * len(ins),
                          out_specs=[blk] * len(out_dtypes),
                          out_shape=[jax.ShapeDtypeStruct((r, c), dt) for dt in out_dtypes],
                          compiler_params=_params(1))(*ins)


def _adamw_math(w, g, m, v):
    m = ADAM_B1 * m + (1.0 - ADAM_B1) * g
    v = ADAM_B2 * v + (1.0 - ADAM_B2) * (g * g)
    m_hat = m / (1.0 - ADAM_B1 ** ADAM_STEP)
    v_hat = v / (1.0 - ADAM_B2 ** ADAM_STEP)
    delta = -ADAM_LR * (m_hat / (jnp.sqrt(v_hat) + ADAM_EPS) + ADAM_WD * w)
    return delta, m, v


def _adamw(name, w, g, m, v):
    shape = w.shape
    c = shape[-1]
    flat = [a.reshape(-1, c) for a in (w, g, m, v)]
    rows = flat[0].shape[0]
    pref = max(8, 2 ** int(math.log2(EW_BLOCK_ELEMS // c)))
    res = _rowwise(name, _adamw_math, flat, [F32, F32, F32], rows_pref=pref if rows % pref == 0 else rows)
    return tuple(a.reshape(shape) for a in res)


def _pool_means(ext_ref, ts, tile_index):
    t_glob = tile_index * ts + lax.broadcasted_iota(jnp.int32, (ts, GROUP), 0)
    qs = []
    for g, win in enumerate(POOL_WINDOWS):
        cols = pl.ds(g * GROUP, GROUP)
        cur = ext_ref[pl.ds(16, ts), cols]
        acc = cur
        for j in range(1, win):
            acc = acc + ext_ref[pl.ds(16 - j, ts), cols]
        cnt = jnp.minimum(t_glob + 1, win).astype(F32)
        qs.append(acc / cnt - cur)
    return qs


def _masked_sgu_w(w_ref, h):
    r = lax.broadcasted_iota(jnp.int32, (CHUNK, CHUNK), 0)
    c = lax.broadcasted_iota(jnp.int32, (CHUNK, CHUNK), 1)
    return jnp.where(r >= c, w_ref[h], 0.0)


def _conv_taps(acc_init, w_ref, src_ref, ts, base):
    nrb = ts // RB_CONV
    blocks = []
    for cg in range(CONV_WIDTH // GROUP):
        cols = pl.ds(cg * GROUP, GROUP)
        accs = [acc_init(cg) for _ in range(nrb)]
        for k in range(CONV_KERNEL):
            wk = jnp.broadcast_to(w_ref[pl.ds(k, 1), cols], (RB_CONV, GROUP))
            for rb in range(nrb):
                accs[rb] = accs[rb] + wk * src_ref[pl.ds(rb * RB_CONV + base(k), RB_CONV), cols]
        for rb in range(nrb):
            blocks.append((rb * RB_CONV, cg * GROUP, accs[rb]))
    return blocks


def _mix_forward(name, p, lw):
    s_len = p.shape[0]
    ts = _tile(s_len, TS_MIX)
    per_halo = ts // HALO
    d = POOL_WIDTH + SGU_WIDTH + CONV_WIDTH

    def body(p_ref, ph_ref, wp_ref, ps_ref, slg_ref, slb_ref, ws_ref, sb_ref, cw_ref, cb_ref, clg_ref, clb_ref,
             mixed_ref, cv_ref, pool_ext, hh_ext):
        i = pl.program_id(0)
        keep = (i > 0).astype(F32)
        pool_ext[pl.ds(0, 16), :] = ph_ref[pl.ds(16, 16), pl.ds(0, POOL_WIDTH)] * keep
        pool_ext[pl.ds(16, ts), :] = p_ref[:, pl.ds(0, POOL_WIDTH)]
        qs = _pool_means(pool_ext, ts, i)
        for g in range(len(POOL_WINDOWS)):
            cols = pl.ds(g * GROUP, GROUP)
            e = jnp.dot(qs[g].astype(BF16), wp_ref[g].astype(BF16), preferred_element_type=F32)
            mixed_ref[:, cols] = (e * ps_ref[:, cols]).astype(BF16)
        uv = _gelu(p_ref[:, pl.ds(COL_B, 2 * SGU_WIDTH)])
        u = uv[:, :SGU_WIDTH]
        vhat, _ = _ln_stats(uv[:, SGU_WIDTH:])
        vn = (vhat * slg_ref[...] + slb_ref[...]).astype(BF16)
        for h in range(SGU_HEADS):
            wm = _masked_sgu_w(ws_ref, h).astype(BF16)
            for n in range(ts // CHUNK):
                rows = slice(n * CHUNK, (n + 1) * CHUNK)
                cols = slice(h * GROUP, (h + 1) * GROUP)
                mx = jnp.dot(wm, vn[rows, cols], preferred_element_type=F32) + sb_ref[h]
                mixed_ref[pl.ds(n * CHUNK, CHUNK), pl.ds(POOL_WIDTH + h * GROUP, GROUP)] = (u[rows, cols] * mx).astype(BF16)
        hh_ext[pl.ds(0, HALO), :] = (ph_ref[:, pl.ds(COL_C, CONV_WIDTH)]
                                     * _sigmoid(ph_ref[:, pl.ds(COL_C + CONV_WIDTH, CONV_WIDTH)])) * keep
        hh_ext[pl.ds(HALO, ts), :] = p_ref[:, pl.ds(COL_C, CONV_WIDTH)] * _sigmoid(p_ref[:, pl.ds(COL_C + CONV_WIDTH, CONV_WIDTH)])
        init = lambda cg: jnp.broadcast_to(cb_ref[:, pl.ds(cg * GROUP, GROUP)], (RB_CONV, GROUP))
        for r0, c0, blk in _conv_taps(init, cw_ref, hh_ext, ts, lambda k: k + HALO - (CONV_KERNEL - 1)):
            cv_ref[pl.ds(r0, RB_CONV), pl.ds(c0, GROUP)] = blk
        cvhat, _ = _ln_stats(cv_ref[...])
        cn = cvhat * clg_ref[...] + clb_ref[...]
        mixed_ref[:, pl.ds(POOL_WIDTH + SGU_WIDTH, CONV_WIDTH)] = (cn * _sigmoid(cn)).astype(BF16)

    full = lambda a: pl.BlockSpec(a.shape, lambda i: (0,) * a.ndim)
    weights = [lw["w_pool"], _row(lw["pool_scale"]), _row(lw["sgu_ln_g"]), _row(lw["sgu_ln_b"]), lw["sgu_w"],
               lw["sgu_bias_tile"], lw["conv_w_full"], _row(lw["conv_b"]), _row(lw["conv_ln_g"]), _row(lw["conv_ln_b"])]
    return pl.pallas_call(
        body, name=name, grid=(s_len // ts,),
        in_specs=[pl.BlockSpec((ts, IN_WIDTH), lambda i: (i, 0)),
                  pl.BlockSpec((HALO, IN_WIDTH), lambda i: (jnp.maximum(i * per_halo - 1, 0), 0))] + [full(a) for a in weights],
        out_specs=[pl.BlockSpec((ts, d), lambda i: (i, 0)), pl.BlockSpec((ts, CONV_WIDTH), lambda i: (i, 0))],
        out_shape=[jax.ShapeDtypeStruct((s_len, d), BF16), jax.ShapeDtypeStruct((s_len, CONV_WIDTH), F32)],
        scratch_shapes=[pltpu.VMEM((16 + ts, POOL_WIDTH), F32), pltpu.VMEM((HALO + ts, CONV_WIDTH), F32)],
        compiler_params=_params(1),
    )(p, p, *weights)


def _mix_backward(name, p, cv, dmixed, lw):
    s_len = p.shape[0]
    ts = _tile(s_len, TS_MIX)
    nt = s_len // ts
    per_halo = ts // HALO
    d = POOL_WIDTH + SGU_WIDTH + CONV_WIDTH
    nch = ts // CHUNK
    col_yc = POOL_WIDTH + SGU_WIDTH

    def body(p_ref, ph_ref, cv_ref, cvn_ref, dm_ref, dmn_ref, wp_ref, ps_ref, slg_ref, slb_ref, ws_ref, wst_ref, sb_ref,
             cw_ref, cwr_ref, clg_ref, clb_ref,
             dp_ref, dbin_ref, dwp_ref, dps_ref, dslg_ref, dslb_ref, dws_ref, dsb_ref, dcw_ref, dcb_ref, dclg_ref, dclb_ref,
             pool_ext, dq_ext, hh_ext, dcv_ext, dcw_acc):
        i = pl.program_id(0)
        first = i == 0
        keep_prev = (i > 0).astype(F32)
        keep_next = (i < nt - 1).astype(F32)

        pool_ext[pl.ds(0, 16), :] = ph_ref[pl.ds(16, 16), pl.ds(0, POOL_WIDTH)] * keep_prev
        pool_ext[pl.ds(16, ts), :] = p_ref[:, pl.ds(0, POOL_WIDTH)]
        qs = _pool_means(pool_ext, ts, i)
        t_ext = i * ts + lax.broadcasted_iota(jnp.int32, (ts + HALO, GROUP), 0)
        for g, win in enumerate(POOL_WINDOWS):
            cols = pl.ds(g * GROUP, GROUP)
            wpb = wp_ref[g].astype(BF16)
            qb = qs[g].astype(BF16)
            de = dm_ref[:, cols] * ps_ref[:, cols]
            e = jnp.dot(qb, wpb, preferred_element_type=F32)
            _accumulate(dps_ref.at[:, cols], _colsum(dm_ref[:, cols] * e), first)
            deb = de.astype(BF16)
            _accumulate(dwp_ref.at[g], lax.dot_general(qb, deb, (((0,), (0,)), ((), ())), preferred_element_type=F32), first)
            de_next = (dmn_ref[:, cols] * ps_ref[:, cols] * keep_next).astype(BF16)
            de_all = jnp.concatenate([deb, de_next], axis=0)
            dq = lax.dot_general(de_all, wpb, (((1,), (1,)), ((), ())), preferred_element_type=F32)
            inv = 1.0 / jnp.minimum(t_ext + 1, win).astype(F32)
            dq_ext[:, cols] = dq * inv
            acc = dq_ext[pl.ds(0, ts), cols]
            for j in range(1, win):
                acc = acc + dq_ext[pl.ds(j, ts), cols]
            dpa = acc - dq[:ts]
            dp_ref[:, cols] = dpa.astype(BF16)
            _accumulate(dbin_ref.at[:, cols], _colsum(dpa), first)

        pb = p_ref[:, pl.ds(COL_B, 2 * SGU_WIDTH)]
        uv = _gelu(pb)
        u = uv[:, :SGU_WIDTH]
        vhat, vrstd = _ln_stats(uv[:, SGU_WIDTH:])
        vn = (vhat * slg_ref[...] + slb_ref[...]).astype(BF16)
        dyb = dm_ref[:, pl.ds(POOL_WIDTH, SGU_WIDTH)]
        dmix = dyb * u
        dmixb = dmix.astype(BF16)
        du_parts, dvn_parts = [], []
        for h in range(SGU_HEADS):
            cols = slice(h * GROUP, (h + 1) * GROUP)
            wm = _masked_sgu_w(ws_ref, h).astype(BF16)
            wmt = _masked_sgu_w_t(wst_ref, h).astype(BF16)
            dws_h = jnp.zeros((CHUNK, CHUNK), F32)
            dsb_h = jnp.zeros((CHUNK, GROUP), F32)
            du_rows, dvn_rows = [], []
            for n in range(nch):
                rows = slice(n * CHUNK, (n + 1) * CHUNK)
                mx = jnp.dot(wm, vn[rows, cols], preferred_element_type=F32) + sb_ref[h]
                du_rows.append(dyb[rows, cols] * mx)
                dws_h = dws_h + lax.dot_general(dmixb[rows, cols], vn[rows, cols], (((1,), (1,)), ((), ())),
                                                preferred_element_type=F32)
                dsb_h = dsb_h + dmix[rows, cols]
                dvn_rows.append(jnp.dot(wmt, dmixb[rows, cols], preferred_element_type=F32))
            r = lax.broadcasted_iota(jnp.int32, (CHUNK, CHUNK), 0)
            c = lax.broadcasted_iota(jnp.int32, (CHUNK, CHUNK), 1)
            _accumulate(dws_ref.at[h], jnp.where(r >= c, dws_h, 0.0), first)
            _accumulate(dsb_ref.at[h], jnp.broadcast_to(jnp.sum(dsb_h, axis=1, keepdims=True), (CHUNK, GROUP)), first)
            du_parts.append(jnp.concatenate(du_rows, axis=0))
            dvn_parts.append(jnp.concatenate(dvn_rows, axis=0))
        du = jnp.concatenate(du_parts, axis=1)
        dvn = jnp.concatenate(dvn_parts, axis=1)
        _accumulate(dslg_ref, _colsum(dvn * vhat), first)
        _accumulate(dslb_ref, _colsum(dvn), first)
        dv = _ln_bwd(dvn, vhat, vrstd, slg_ref[...])
        dpb = jnp.concatenate([du, dv], axis=1) * _gelu_grad(pb)
        dp_ref[:, pl.ds(COL_B, 2 * SGU_WIDTH)] = dpb.astype(BF16)
        _accumulate(dbin_ref.at[:, pl.ds(COL_B, 2 * SGU_WIDTH)], _colsum(dpb), first)

        a_main = p_ref[:, pl.ds(COL_C, CONV_WIDTH)]
        sg_main = _sigmoid(p_ref[:, pl.ds(COL_C + CONV_WIDTH, CONV_WIDTH)])
        hh_ext[pl.ds(0, HALO), :] = (ph_ref[:, pl.ds(COL_C, CONV_WIDTH)]
                                     * _sigmoid(ph_ref[:, pl.ds(COL_C + CONV_WIDTH, CONV_WIDTH)])) * keep_prev
        hh_ext[pl.ds(HALO, ts), :] = a_main * sg_main

        def conv_ln_backward(cv_v, dyc_v):
            cvhat, crstd = _ln_stats(cv_v)
            cn = cvhat * clg_ref[...] + clb_ref[...]
            s = _sigmoid(cn)
            dcn = dyc_v * (s * (1.0 + cn * (1.0 - s)))
            return _ln_bwd(dcn, cvhat, crstd, clg_ref[...]), dcn, cvhat

        dcv, dcn, cvhat = conv_ln_backward(cv_ref[...], dm_ref[:, pl.ds(col_yc, CONV_WIDTH)])
        _accumulate(dclg_ref, _colsum(dcn * cvhat), first)
        _accumulate(dclb_ref, _colsum(dcn), first)
        _accumulate(dcb_ref, _colsum(dcv), first)
        dcv_next, _, _ = conv_ln_backward(cvn_ref[...], dmn_ref[:, pl.ds(col_yc, CONV_WIDTH)])
        dcv_ext[pl.ds(0, ts), :] = dcv
        dcv_ext[pl.ds(ts, HALO), :] = dcv_next * keep_next

        @pl.when(first)
        def _():
            dcw_acc[...] = jnp.zeros_like(dcw_acc)

        nrb = ts // RB_CONV
        for cg in range(CONV_WIDTH // GROUP):
            cols = pl.ds(cg * GROUP, GROUP)
            dblk = [dcv_ext[pl.ds(rb * RB_CONV, RB_CONV), cols] for rb in range(nrb)]
            for k in range(CONV_KERNEL):
                part = jnp.zeros((8, GROUP), F32)
                for rb in range(nrb):
                    prod = dblk[rb] * hh_ext[pl.ds(rb * RB_CONV + k + HALO - (CONV_KERNEL - 1), RB_CONV), cols]
                    part = part + jnp.sum(prod.reshape(RB_CONV // 8, 8, GROUP), axis=0)
                dcw_acc[k, :, cols] += part

        @pl.when(i == nt - 1)
        def _():
            dcw_ref[...] = jnp.sum(dcw_acc[...], axis=1)

        zero = lambda cg: jnp.zeros((RB_CONV, GROUP), F32)
        for r0, c0, blk in _conv_taps(zero, cwr_ref, dcv_ext, ts, lambda k: k):
            rows, cols = pl.ds(r0, RB_CONV), pl.ds(c0, GROUP)
            a_blk = p_ref[rows, pl.ds(COL_C + c0, GROUP)]
            s_blk = _sigmoid(p_ref[rows, pl.ds(COL_C + CONV_WIDTH + c0, GROUP)])
            da = blk * s_blk
            dg = blk * a_blk * s_blk * (1.0 - s_blk)
            dp_ref[rows, pl.ds(COL_C + c0, GROUP)] = da.astype(BF16)
            dp_ref[rows, pl.ds(COL_C + CONV_WIDTH + c0, GROUP)] = dg.astype(BF16)
            hh_ext[pl.ds(HALO + r0, RB_CONV), cols] = da
            dcv_ext[rows, cols] = dg
        _accumulate(dbin_ref.at[:, pl.ds(COL_C, CONV_WIDTH)], _colsum(hh_ext[pl.ds(HALO, ts), :]), first)
        _accumulate(dbin_ref.at[:, pl.ds(COL_C + CONV_WIDTH, CONV_WIDTH)], _colsum(dcv_ext[pl.ds(0, ts), :]), first)

    full = lambda a: pl.BlockSpec(a.shape, lambda i: (0,) * a.ndim)
    weights = [lw["w_pool"], _row(lw["pool_scale"]), _row(lw["sgu_ln_g"]), _row(lw["sgu_ln_b"]), lw["sgu_w"], lw["sgu_w_t"],
               lw["sgu_bias_tile"], lw["conv_w_full"], lw["conv_w_rev"], _row(lw["conv_ln_g"]), _row(lw["conv_ln_b"])]
    prev_halo = lambda i: (jnp.maximum(i * per_halo - 1, 0), 0)
    next_halo = lambda i: (jnp.minimum((i + 1) * per_halo, s_len // HALO - 1), 0)
    small = lambda shape: (jax.ShapeDtypeStruct(shape, F32), pl.BlockSpec(shape, lambda i: (0,) * len(shape)))
    outs = [(jax.ShapeDtypeStruct((s_len, IN_WIDTH), BF16), pl.BlockSpec((ts, IN_WIDTH), lambda i: (i, 0))),
            small((1, IN_WIDTH)), small((len(POOL_WINDOWS), GROUP, GROUP)), small((1, POOL_WIDTH)),
            small((1, SGU_WIDTH)), small((1, SGU_WIDTH)), small((SGU_HEADS, CHUNK, CHUNK)), small((SGU_HEADS, CHUNK, GROUP)),
            small((CONV_ROWS, CONV_WIDTH)), small((1, CONV_WIDTH)), small((1, CONV_WIDTH)), small((1, CONV_WIDTH))]
    return pl.pallas_call(
        body, name=name, grid=(nt,),
        in_specs=[pl.BlockSpec((ts, IN_WIDTH), lambda i: (i, 0)), pl.BlockSpec((HALO, IN_WIDTH), prev_halo),
                  pl.BlockSpec((ts, CONV_WIDTH), lambda i: (i, 0)), pl.BlockSpec((HALO, CONV_WIDTH), next_halo),
                  pl.BlockSpec((ts, d), lambda i: (i, 0)), pl.BlockSpec((HALO, d), next_halo)] + [full(a) for a in weights],
        out_specs=[s for _, s in outs],
        out_shape=[o for o, _ in outs],
        scratch_shapes=[pltpu.VMEM((16 + ts, POOL_WIDTH), F32), pltpu.VMEM((ts + HALO, POOL_WIDTH), F32),
                        pltpu.VMEM((HALO + ts, CONV_WIDTH), F32), pltpu.VMEM((ts + HALO, CONV_WIDTH), F32),
                        pltpu.VMEM((CONV_ROWS, 8, CONV_WIDTH), F32)],
        compiler_params=_params(1),
    )(p, p, cv, cv, dmixed, dmixed, *weights)


def _masked_sgu_w_t(wt_ref, h):
    r = lax.broadcasted_iota(jnp.int32, (CHUNK, CHUNK), 0)
    c = lax.broadcasted_iota(jnp.int32, (CHUNK, CHUNK), 1)
    return jnp.where(c >= r, wt_ref[h], 0.0)


HBM = pl.BlockSpec(memory_space=pltpu.HBM)
CHIP_FLIPS = ((1, 0), (0, 1), (1, 1))


def _place():
    return lax.axis_index("x"), lax.axis_index("y"), lax.axis_index("c")


def _half(ref, axis, which, size):
    idx = [slice(None)] * len(ref.shape)
    idx[axis] = pl.ds(which * size, size)
    return ref.at[tuple(idx)]


def _exchange(name, sources, inplace, fresh, copies):
    n_src, n_in, n = len(sources), len(inplace), len(copies)
    n_out = n_in + len(fresh)

    def body(*refs):
        ins = refs[:n_src + n_in]
        outs = refs[n_src + n_in:n_src + n_in + n_out]
        send_sems, recv_sems = refs[n_src + n_in + n_out:]
        started = _descriptors(copies, list(ins[:n_src]) + list(outs), send_sems, recv_sems)
        for cp in started:
            cp.start()
        for cp in started:
            cp.wait()

    out_shape = [jax.ShapeDtypeStruct(a.shape, a.dtype) for a in inplace] + list(fresh)
    return pl.pallas_call(
        body, name=name, in_specs=[HBM] * (n_src + n_in), out_specs=[HBM] * n_out, out_shape=out_shape,
        input_output_aliases={n_src + i: i for i in range(n_in)},
        scratch_shapes=[pltpu.SemaphoreType.DMA((n,)), pltpu.SemaphoreType.DMA((n,))],
    )(*sources, *inplace)


def _descriptors(copies, bufs, send_sems, recv_sems):
    x, y, c = _place()
    return [pltpu.make_async_remote_copy(
        src_ref=src_view(bufs[src], x, y, c), dst_ref=dst_view(bufs[dst], x, y, c),
        send_sem=send_sems.at[k], recv_sem=recv_sems.at[k], device_id=peer(x, y, c), device_id_type=MESH)
        for k, (src, src_view, dst, dst_view, peer) in enumerate(copies)]


SEM = pl.BlockSpec(memory_space=pltpu.SEMAPHORE)
IN_FLIGHT = pltpu.CompilerParams(has_side_effects=pltpu.SideEffectType.DATAFLOW_SIDE_EFFECTING)


def _exchange_start(name, sources, inplace, fresh, copies):
    n_src, n = len(sources), len(copies)
    landing = [lax.empty(f.shape, f.dtype) for f in fresh]
    bufs = [pltpu.with_memory_space_constraint(a, pltpu.HBM) for a in (*inplace, *landing)]
    srcs = [pltpu.with_memory_space_constraint(a, pltpu.HBM) for a in sources]
    n_buf = len(bufs)

    def body(*refs):
        ins = refs[:n_src]
        send_sems, recv_sems = refs[n_src + n_buf], refs[n_src + n_buf + 1]
        outs = refs[n_src + n_buf + 2:n_src + 2 * n_buf + 2]
        token = refs[n_src + 2 * n_buf + 2]
        for cp in _descriptors(copies, list(ins) + list(outs), send_sems, recv_sems):
            cp.start()
        token[...] = jnp.zeros_like(token)

    res = pl.pallas_call(
        body, name=name + "_start",
        out_shape=(pltpu.SemaphoreType.DMA((n,)), pltpu.SemaphoreType.DMA((n,)),
                   *[pltpu.HBM(b.shape, b.dtype) for b in bufs], jax.ShapeDtypeStruct((8, 128), F32)),
        in_specs=[HBM] * (n_src + n_buf),
        out_specs=(SEM, SEM, *[HBM] * n_buf, pl.BlockSpec(memory_space=pltpu.VMEM)),
        input_output_aliases={n_src + i: 2 + i for i in range(n_buf)},
        compiler_params=IN_FLIGHT,
    )(*srcs, *bufs)
    return dict(name=name, sources=srcs, sems=res[:2], bufs=list(res[2:2 + n_buf]), token=res[2 + n_buf], copies=copies)


def _exchange_wait(handle, after):
    srcs, bufs, copies = handle["sources"], handle["bufs"], handle["copies"]
    n_src, n_buf = len(srcs), len(bufs)

    def body(*refs):
        ins = refs[:n_src]
        send_sems, recv_sems = refs[n_src + n_buf], refs[n_src + n_buf + 1]
        outs = refs[n_src + n_buf + 3:]
        for cp in _descriptors(copies, list(ins) + list(outs), send_sems, recv_sems):
            cp.wait()

    return list(pl.pallas_call(
        body, name=handle["name"] + "_wait",
        out_shape=tuple(pltpu.HBM(b.shape, b.dtype) for b in bufs),
        in_specs=[HBM] * (n_src + n_buf) + [SEM, SEM, pl.BlockSpec(memory_space=pl.ANY)],
        out_specs=tuple([HBM] * n_buf),
        input_output_aliases={n_src + i: i for i in range(n_buf)},
        compiler_params=IN_FLIGHT,
    )(*srcs, *bufs, *handle["sems"], after))


def _into_slot(name, src, layer, place, dtype):
    _, r, c = src.shape
    br = _tile(r, BR_EW) if r % BR_EW == 0 else r

    def body(pr_ref, s_ref, o_ref):
        o_ref[...] = s_ref[...].astype(dtype)

    return pl.pallas_call(
        body, name=name,
        grid_spec=pltpu.PrefetchScalarGridSpec(
            num_scalar_prefetch=1, grid=(r // br,),
            in_specs=[pl.BlockSpec((None, br, c), lambda i, pr: (layer, i, 0))],
            out_specs=pl.BlockSpec((None, br, c), lambda i, pr: (pr[1], i, 0))),
        out_shape=jax.ShapeDtypeStruct((4, r, c), dtype), compiler_params=_params(1),
    )(place, src)


def _gather_group(name, bufs):
    ici, d2d = _gather_copies(len(bufs))
    bufs = _exchange(name + "_chips", [], bufs, [], ici)
    return _exchange(name + "_sibling", [], bufs, [], d2d)


def _gather_copies(n_bufs):
    def own_half(ref, x, y, c):
        return _half(ref.at[2 * x + y], 0, c, ref.shape[1] // 2)

    ici, d2d = [], []
    for b in range(n_bufs):
        for fx, fy in CHIP_FLIPS:
            ici.append((b, own_half, b, own_half, lambda x, y, c, fx=fx, fy=fy: (x ^ fx, y ^ fy, c)))
            landed = lambda ref, x, y, c, fx=fx, fy=fy: _half(ref.at[2 * (x ^ fx) + (y ^ fy)], 0, c, ref.shape[1] // 2)
            d2d.append((b, landed, b, landed, _sibling))
    return ici, d2d


def _gather_start(name, bufs):
    return _exchange_start(name + "_chips", [], bufs, [], _gather_copies(len(bufs))[0])


def _gather_finish(name, handle, after):
    bufs = _exchange_wait(handle, after)
    return _exchange(name + "_sibling", [], bufs, [], _gather_copies(len(bufs))[1])


def _sibling(x, y, c):
    return (x, y, 1 - c)


def _scalar_spec_call(name, fn, scalars, ins, in_blocks, out_shapes, out_blocks, grid):
    def body(s_ref, *refs):
        res = fn(*[r[...] for r in refs[:len(ins)]])
        for o_ref, v in zip(refs[len(ins):], res):
            o_ref[...] = v.astype(o_ref.dtype)

    return pl.pallas_call(
        body, name=name,
        grid_spec=pltpu.PrefetchScalarGridSpec(num_scalar_prefetch=1, grid=grid, in_specs=in_blocks, out_specs=out_blocks),
        out_shape=out_shapes, compiler_params=_params(len(grid)),
    )(scalars, *ins)


def _reduce_begin(tag, place, items, in_flight):
    n = len(items)
    whole = lambda ref, x, y, c: ref
    slot_of = lambda scatter: (lambda x, y: 2 * x + y) if scatter else (lambda x, y: 0)
    halves = [g.shape[1] // 2 for g, _, _ in items]

    fresh = [jax.ShapeDtypeStruct((g.shape[0], halves[a], g.shape[2]), F32) for a, (g, _, _) in enumerate(items)]
    copies = [(a, lambda ref, x, y, c, hr=halves[a]: _half(ref, 1, 1 - c, hr), n + a, whole, _sibling) for a in range(n)]
    landed = _exchange(tag + "_sibling_in", [g for g, _, _ in items], [], fresh, copies)

    chip_sums = []
    for a, (g, _, wire) in enumerate(items):
        ns, _, cols = g.shape
        hr = halves[a]
        br = _tile(hr, BR_EW)
        chip_sums.append(_scalar_spec_call(
            f"{tag}_chip_sum{a}", lambda u, v: (u + v,), place, [g, landed[a]],
            [pl.BlockSpec((None, br, cols), lambda s, i, pr, hb=hr // br: (s, pr[0] * hb + i, 0)),
             pl.BlockSpec((None, br, cols), lambda s, i, pr: (s, i, 0))],
            [jax.ShapeDtypeStruct((ns, hr, cols), wire)], [pl.BlockSpec((None, br, cols), lambda s, i, pr: (s, i, 0))],
            (ns, hr // br))[0])

    copies, fresh = [], []
    for a, (g, scatter, wire) in enumerate(items):
        slot = slot_of(scatter)
        fresh.append(jax.ShapeDtypeStruct((3, halves[a], g.shape[2]), wire))
        for j, (fx, fy) in enumerate(CHIP_FLIPS):
            copies.append((a, lambda ref, x, y, c, fx=fx, fy=fy, slot=slot: ref.at[slot(x ^ fx, y ^ fy)],
                           n + a, lambda ref, x, y, c, j=j: ref.at[j], lambda x, y, c, fx=fx, fy=fy: (x ^ fx, y ^ fy, c)))
    state = dict(tag=tag, place=place, items=items, chip_sums=chip_sums)
    if in_flight:
        state["handle"] = _exchange_start(tag + "_chips", chip_sums, [], fresh, copies)
        state["token"] = state["handle"]["token"]
    else:
        state["arrived"] = _exchange(tag + "_chips", chip_sums, [], fresh, copies)
    return state


def _reduce_end(state, after=None):
    tag, place, items, chip_sums = state["tag"], state["place"], state["items"], state["chip_sums"]
    n = len(items)
    whole = lambda ref, x, y, c: ref
    halves = [g.shape[1] // 2 for g, _, _ in items]
    arrived = _exchange_wait(state["handle"], after) if "handle" in state else state["arrived"]

    tree = lambda own, fx, fy, fxy: ((own.astype(F32) + fx.astype(F32)) + (fy.astype(F32) + fxy.astype(F32)),)
    mine = []
    for a, (g, scatter, _) in enumerate(items):
        hr, cols = halves[a], g.shape[2]
        br = _tile(hr, BR_EW)
        got = lambda j: pl.BlockSpec((None, br, cols), lambda i, pr, j=j: (j, i, 0))
        own = pl.BlockSpec((None, br, cols), (lambda i, pr: (pr[1], i, 0)) if scatter else (lambda i, pr: (0, i, 0)))
        mine.append(_scalar_spec_call(
            f"{tag}_tree_sum{a}", tree, place, [chip_sums[a], arrived[a], arrived[a], arrived[a]],
            [own, got(0), got(1), got(2)],
            [jax.ShapeDtypeStruct((hr, cols), F32)], [pl.BlockSpec((br, cols), lambda i, pr: (i, 0))], (hr // br,))[0])

    copies = [(a, whole, n + a, whole, _sibling) for a in range(n)]
    fresh = [jax.ShapeDtypeStruct(h.shape, F32) for h in mine]
    theirs = _exchange(tag + "_sibling_out", mine, [], fresh, copies)
    return list(zip(mine, theirs))


def _select_half(h, c, mine, theirs):
    return jnp.where(h == c, mine, theirs)


def _adamw_sharded(name, layer, place, w, m, v, g_mine, g_theirs, earlier):
    n_layers, r, c = w.shape
    hr = r // 2
    br = min(hr, max(8, 2 ** int(math.log2(EW_BLOCK_ELEMS // c))))
    assert hr % br == 0
    nb = hr // br
    full = pl.BlockSpec((None, br, c), lambda h, i, pr: (layer, h * nb + i, 0))
    half = pl.BlockSpec((br, c), lambda h, i, pr: (i, 0))
    n_alias = 0 if earlier is None else 4

    def body(pr_ref, w_ref, m_ref, v_ref, gm_ref, gt_ref, *rest):
        g_ref, d_ref, nm_ref, nv_ref = rest[n_alias:]
        g = _select_half(pl.program_id(0), pr_ref[0], gm_ref[...], gt_ref[...])
        delta, new_m, new_v = _adamw_math(w_ref[...], g, m_ref[...], v_ref[...])
        g_ref[...] = g
        d_ref[...] = delta
        nm_ref[...] = new_m
        nv_ref[...] = new_v

    return pl.pallas_call(
        body, name=name,
        grid_spec=pltpu.PrefetchScalarGridSpec(
            num_scalar_prefetch=1, grid=(2, nb),
            in_specs=[full, full, full, half, half] + [pl.BlockSpec(memory_space=pl.ANY)] * n_alias,
            out_specs=[full] * 4),
        out_shape=[jax.ShapeDtypeStruct((n_layers, r, c), F32)] * 4,
        input_output_aliases={6 + i: i for i in range(n_alias)},
        compiler_params=_params(2),
    )(place, w, m, v, g_mine, g_theirs, *(earlier or ()))


SHARDED = ("w_in", "w_out", "w_ff1", "w_ff2")
REPLICATED = ("b_in", "w_pool", "pool_scale", "sgu_ln_g", "sgu_ln_b", "sgu_w", "sgu_b", "conv_b", "conv_ln_g", "conv_ln_b",
              "b_out", "ln1_g", "ln1_b", "b_ff1", "b_ff2", "ln2_g", "ln2_b")
WEIGHTS = ("w_in", "b_in", "w_pool", "pool_scale", "sgu_ln_g", "sgu_ln_b", "sgu_w", "sgu_b", "conv_w", "conv_b", "conv_ln_g",
           "conv_ln_b", "w_out", "b_out", "ln1_g", "ln1_b", "w_ff1", "b_ff1", "w_ff2", "b_ff2", "ln2_g", "ln2_b")
PACK_ROWS = 1024


def _pack(arrays):
    flat = jnp.concatenate([a.reshape(-1) for a in arrays])
    rows = -(-flat.shape[0] // (128 * PACK_ROWS)) * PACK_ROWS
    return jnp.pad(flat, (0, rows * 128 - flat.shape[0])).reshape(rows, 128)


def _unpack(packed, like):
    flat = packed.reshape(-1)
    out, at = [], 0
    for a in like:
        out.append(flat[at:at + a.size].reshape(a.shape))
        at += a.size
    return out


def kernel(x, w_in, b_in, w_pool, pool_scale, sgu_ln_g, sgu_ln_b, sgu_w, sgu_b, conv_w, conv_b, conv_ln_g, conv_ln_b, w_out, b_out, ln1_g, ln1_b, w_ff1, b_ff1, w_ff2, b_ff2, ln2_g, ln2_b, loss_target, m_w_in, m_b_in, m_w_pool, m_pool_scale, m_sgu_ln_g, m_sgu_ln_b, m_sgu_w, m_sgu_b, m_conv_w, m_conv_b, m_conv_ln_g, m_conv_ln_b, m_w_out, m_b_out, m_ln1_g, m_ln1_b, m_w_ff1, m_b_ff1, m_w_ff2, m_b_ff2, m_ln2_g, m_ln2_b, v_w_in, v_b_in, v_w_pool, v_pool_scale, v_sgu_ln_g, v_sgu_ln_b, v_sgu_w, v_sgu_b, v_conv_w, v_conv_b, v_conv_ln_g, v_conv_ln_b, v_w_out, v_b_out, v_ln1_g, v_ln1_b, v_w_ff1, v_b_ff1, v_w_ff2, v_b_ff2, v_ln2_g, v_ln2_b):
    w = dict(w_in=w_in, b_in=b_in, w_pool=w_pool, pool_scale=pool_scale, sgu_ln_g=sgu_ln_g, sgu_ln_b=sgu_ln_b, sgu_w=sgu_w,
             sgu_b=sgu_b, conv_w=conv_w, conv_b=conv_b, conv_ln_g=conv_ln_g, conv_ln_b=conv_ln_b, w_out=w_out, b_out=b_out,
             ln1_g=ln1_g, ln1_b=ln1_b, w_ff1=w_ff1, b_ff1=b_ff1, w_ff2=w_ff2, b_ff2=b_ff2, ln2_g=ln2_g, ln2_b=ln2_b)
    m = dict(w_in=m_w_in, b_in=m_b_in, w_pool=m_w_pool, pool_scale=m_pool_scale, sgu_ln_g=m_sgu_ln_g, sgu_ln_b=m_sgu_ln_b,
             sgu_w=m_sgu_w, sgu_b=m_sgu_b, conv_w=m_conv_w, conv_b=m_conv_b, conv_ln_g=m_conv_ln_g, conv_ln_b=m_conv_ln_b,
             w_out=m_w_out, b_out=m_b_out, ln1_g=m_ln1_g, ln1_b=m_ln1_b, w_ff1=m_w_ff1, b_ff1=m_b_ff1, w_ff2=m_w_ff2,
             b_ff2=m_b_ff2, ln2_g=m_ln2_g, ln2_b=m_ln2_b)
    v = dict(w_in=v_w_in, b_in=v_b_in, w_pool=v_w_pool, pool_scale=v_pool_scale, sgu_ln_g=v_sgu_ln_g, sgu_ln_b=v_sgu_ln_b,
             sgu_w=v_sgu_w, sgu_b=v_sgu_b, conv_w=v_conv_w, conv_b=v_conv_b, conv_ln_g=v_conv_ln_g, conv_ln_b=v_conv_ln_b,
             w_out=v_w_out, b_out=v_b_out, ln1_g=v_ln1_g, ln1_b=v_ln1_b, w_ff1=v_w_ff1, b_ff1=v_b_ff1, w_ff2=v_w_ff2,
             b_ff2=v_b_ff2, ln2_g=v_ln2_g, ln2_b=v_ln2_b)
    assert x.shape[0] == 1 and x.shape[2] == POOL_WIDTH + SGU_WIDTH + CONV_WIDTH, x.shape
    xs, target = x[0], loss_target[0]
    s_len, d = xs.shape
    n_layers = w_in.shape[0]
    dff = 4 * w_ff1.shape[2]
    conv_shard = conv_w.shape[2]

    cx, cy, cc = _place()
    me = 2 * cx + cy
    place = jnp.stack([cc, me]).astype(jnp.int32)

    slot = {(n, l): _into_slot(f"cast_{n}{l}", w[n], l, place, BF16) for n in SHARDED for l in range(n_layers)}
    conv_padded = jnp.pad(conv_w, ((0, 0), (0, CONV_ROWS - CONV_KERNEL), (0, 0))).reshape(1, n_layers * CONV_ROWS, conv_shard)
    conv_slot = _into_slot("slot_conv_w", conv_padded, 0, place, F32)
    slot["conv_w"] = conv_slot
    gathered = {}
    mix0 = [("w_in", 0), ("w_out", 0), "conv_w"]
    ff0 = [("w_ff1", 0), ("w_ff2", 0)]
    gathered.update(zip(mix0, _gather_group("gather_mix0", [slot[k] for k in mix0])))
    conv_full = jnp.transpose(gathered["conv_w"].reshape(4, n_layers, CONV_ROWS, conv_shard), (1, 2, 0, 3))
    conv_full = conv_full.reshape(n_layers, CONV_ROWS, CONV_WIDTH)
    conv_rev = jnp.pad(conv_full[:, CONV_KERNEL - 1::-1], ((0, 0), (0, CONV_ROWS - CONV_KERNEL), (0, 0)))

    def layer_weights(l):
        return dict(w_pool=w_pool[l], pool_scale=pool_scale[l], sgu_ln_g=sgu_ln_g[l], sgu_ln_b=sgu_ln_b[l], sgu_w=sgu_w[l],
                    sgu_w_t=jnp.transpose(sgu_w[l], (0, 2, 1)),
                    sgu_bias_tile=jnp.broadcast_to(sgu_b[l][:, :, None], (SGU_HEADS, CHUNK, GROUP)),
                    conv_w_full=conv_full[l], conv_w_rev=conv_rev[l], conv_b=conv_b[l], conv_ln_g=conv_ln_g[l],
                    conv_ln_b=conv_ln_b[l])

    saved = []
    x_cur, xb_cur = xs, _rowwise("cast_x", lambda t: (t,), [xs], [BF16])[0]
    for l in range(n_layers):
        lw = layer_weights(l)
        in_flight = _gather_start("gather_ff0", [slot[k] for k in ff0]) if l == 0 else None
        w_out_full = gathered[("w_out", l)].reshape(d, d)
        p = _proj(f"proj{l}", xb_cur, gathered[("w_in", l)], b_in[l], after=in_flight and in_flight["token"])
        mixed, cv = _mix_forward(f"mix_fwd{l}", p, lw)
        r1, x1b = _mix_out(f"mix_out{l}", mixed, w_out_full, b_out[l], x_cur, ln1_g[l], ln1_b[l])
        if l == 0:
            gathered.update(zip(ff0, _gather_finish("gather_ff0", in_flight, x1b)))
        nxt = [(n, l + 1) for n in SHARDED] if l + 1 < n_layers else None
        in_flight = _gather_start(f"gather_layer{l + 1}", [slot[k] for k in nxt]) if nxt else None
        w2_full = gathered[("w_ff2", l)].reshape(dff, d)
        hf, zr = _ff1(f"ff1_{l}", x1b, gathered[("w_ff1", l)], b_ff1[l], after=in_flight and in_flight["token"])
        fo = _ff2(f"ff2_{l}", hf, w2_full, b_ff2[l])
        r2, x2, x2b = _resid_ln2(f"ln2_{l}", r1, ln1_g[l], ln1_b[l], fo, ln2_g[l], ln2_b[l])
        if nxt:
            gathered.update(zip(nxt, _gather_finish(f"gather_layer{l + 1}", in_flight, x2b)))
        saved.append(dict(lw=lw, xb_in=xb_cur, p=p, mixed=mixed, cv=cv, r1=r1, x1b=x1b, hf=hf, zr=zr, r2=r2,
                          w_out_full=w_out_full, w2_full=w2_full))
        x_cur, xb_cur = x2, x2b

    grads = {n: [None] * n_layers for n in REPLICATED + ("conv_w",)}
    g_final, delta, new_m, new_v = {}, {}, {}, {}
    results = {n: None for n in SHARDED}
    dx_mm, dx_resid = None, None
    loss_tile = None
    pending = None

    def finish_reduce(begun, after):
        names, layer, state = begun
        reduced = _reduce_end(state, after)
        for n, (mine, theirs) in zip(names, reduced):
            results[n] = _adamw_sharded(f"adamw_{n}{layer}", layer, place, w[n], m[n], v[n], mine, theirs, results[n])
        return reduced

    for l in reversed(range(n_layers)):
        sv = saved[l]
        if dx_mm is None:
            dr2, dr2b, dg2, db2, dbff2, loss_tile = _ln_backward(f"ln2_bwd{l}", sv["r2"], ln2_g[l], x=x_cur, target=target)
        else:
            dr2, dr2b, dg2, db2, dbff2 = _ln_backward(f"ln2_bwd{l}", sv["r2"], ln2_g[l], dy=dx_mm, resid=dx_resid)
        dzb, dbff1 = _dff_hidden(f"ff2_bwd{l}", dr2b, sv["w2_full"], sv["zr"])
        dw = {"w_ff2": _matmul_tn(f"dw_ff2_{l}", sv["hf"], dr2b, 4, dff // 4, d, True),
              "w_ff1": _matmul_tn(f"dw_ff1_{l}", sv["x1b"], dzb, 4, d, dff // 4, False)}
        if pending is not None:
            finish_reduce(pending, dw["w_ff1"])
            pending = None
        ff_red = _reduce_begin(f"grads_ff{l}", place, [(dw[n], True, BF16) for n in ("w_ff1", "w_ff2")], True)
        dx1 = _dx_sharded(f"ff1_bwd{l}", dzb, gathered[("w_ff1", l)], after=ff_red["token"])
        dr1, dr1b, dg1, db1, dbout = _ln_backward(f"ln1_bwd{l}", sv["r1"], ln1_g[l], dy=dx1, resid=dr2)
        dmixed = _dmixed(f"mix_out_bwd{l}", dr1b, sv["w_out_full"])
        dw["w_out"] = _matmul_tn(f"dw_out{l}", sv["mixed"], dr1b, 4, d // 4, d, True)
        (dp, dbin, dwp, dps, dslg, dslb, dws, dsb_tile, dcw, dcb, dclg, dclb) = _mix_backward(
            f"mix_bwd{l}", sv["p"], sv["cv"], dmixed, sv["lw"])
        dw["w_in"] = _matmul_tn(f"dw_in{l}", sv["xb_in"], dp, 4, d, IN_WIDTH // 4, False)
        finish_reduce((("w_ff1", "w_ff2"), l, ff_red), dw["w_in"])
        for name, g in (("b_in", dbin), ("w_pool", dwp), ("pool_scale", dps), ("sgu_ln_g", dslg), ("sgu_ln_b", dslb),
                        ("sgu_w", dws), ("sgu_b", dsb_tile[:, :, 0]), ("conv_w", dcw[:CONV_KERNEL]), ("conv_b", dcb),
                        ("conv_ln_g", dclg), ("conv_ln_b", dclb), ("b_out", dbout), ("ln1_g", dg1), ("ln1_b", db1),
                        ("b_ff1", dbff1), ("b_ff2", dbff2), ("ln2_g", dg2), ("ln2_b", db2)):
            grads[name][l] = g.reshape(w[name].shape[1:]) if name != "conv_w" else g

        items = [(dw[n], True, BF16) for n in ("w_in", "w_out")]
        if l > 0:
            mix_red = _reduce_begin(f"grads_mix{l}", place, items, True)
            dx_mm, dx_resid = _dx_sharded(f"proj_bwd{l}", dp, gathered[("w_in", l)], after=mix_red["token"]), dr1
            pending = (("w_in", "w_out"), l, mix_red)
        else:
            grad_x = _dx_sharded(f"proj_bwd{l}", dp, gathered[("w_in", l)], dr1)[None]
            small_like = [jnp.stack(grads[n]) for n in REPLICATED + ("conv_w",)]
            items.append((_pack(small_like)[None], False, F32))
            mix_red = _reduce_begin(f"grads_mix{l}", place, items, False)
            mine, theirs = finish_reduce((("w_in", "w_out"), l, mix_red), None)[-1]
            reduced_small = jnp.where(cc == 0, jnp.concatenate([mine, theirs]), jnp.concatenate([theirs, mine]))

    loss = lax.psum(loss_tile[0, 0], ("x", "y", "c"))
    for n in SHARDED:
        g_final[n], delta[n], new_m[n], new_v[n] = results[n]
    unpacked = _unpack(reduced_small, small_like)
    g_final.update(zip(REPLICATED, unpacked[:-1]))
    g_final["conv_w"] = lax.dynamic_slice_in_dim(unpacked[-1], me * conv_shard, conv_shard, axis=2)

    delta["conv_w"], new_m["conv_w"], new_v["conv_w"] = _adamw("adamw_conv_w", conv_w, g_final["conv_w"], m["conv_w"], v["conv_w"])
    packed = [_pack([t[n] for n in REPLICATED]) for t in (w, g_final, m, v)]
    like = [w[n] for n in REPLICATED]
    for res, packed_out in zip((delta, new_m, new_v), _adamw("adamw_replicated", *packed)):
        res.update(zip(REPLICATED, _unpack(packed_out, like)))

    return (loss, grad_x, *[g_final[n] for n in WEIGHTS], *[delta[n] for n in WEIGHTS],
            *[new_m[n] for n in WEIGHTS], *[new_v[n] for n in WEIGHTS])
```

```python
import functools
import math

import jax
import jax.numpy as jnp
from jax import lax
from jax.experimental import pallas as pl
from jax.experimental.pallas import tpu as pltpu

F32 = jnp.float32
BF16 = jnp.bfloat16
MESH = pl.DeviceIdType.MESH

DEPTH = 2
POOL_WINDOWS = (2, 4, 8, 16)
POOL_WIDTH = 512
GROUP = 128
SGU_WIDTH = 768
SGU_HEADS = 6
CHUNK = 128
CONV_WIDTH = 768
CONV_KERNEL = 31
CONV_ROWS = 32
HALO = 32
COL_B = POOL_WIDTH
COL_C = POOL_WIDTH + 2 * SGU_WIDTH
IN_WIDTH = COL_C + 2 * CONV_WIDTH
ALPHA = (2 * DEPTH) ** 0.25
LN_EPS = 1e-5
ADAM_LR = 0.001
ADAM_B1 = 0.9
ADAM_B2 = 0.999
ADAM_EPS = 1e-08
ADAM_WD = 0.01
ADAM_STEP = 10
GELU_C = math.sqrt(2.0 / math.pi)
GELU_A = 0.044715

V7X_VMEM_BYTES = 64 * 2 ** 20
VMEM_LIMIT = 56 * 2 ** 20

BM_MM = 1024
BN_MM = 1024
BM_LN = 512
BK_MM = 1024
BK_WIDE = 2048
BS_TN = 2048
TN_OUT_BYTES = 8 * 2 ** 20
TS_MIX = 256
RB_CONV = 64
BR_EW = 512
EW_BLOCK_ELEMS = 2 ** 18


def _tile(n, pref):
    t = min(n, pref)
    assert n % t == 0, (n, pref)
    return t


def _params(n_grid):
    return pltpu.CompilerParams(dimension_semantics=("arbitrary",) * n_grid, vmem_limit_bytes=VMEM_LIMIT)


def _sigmoid(x):
    return 1.0 / (1.0 + jnp.exp(-x))


def _gelu(x):
    return 0.5 * x * (1.0 + jnp.tanh(GELU_C * (x + GELU_A * x * x * x)))


def _gelu_grad(x):
    t = jnp.tanh(GELU_C * (x + GELU_A * x * x * x))
    return 0.5 * (1.0 + t) + 0.5 * x * (1.0 - t * t) * GELU_C * (1.0 + 3.0 * GELU_A * x * x)


def _ln_stats(r):
    mu = jnp.mean(r, axis=-1, keepdims=True)
    xc = r - mu
    var = jnp.mean(xc * xc, axis=-1, keepdims=True)
    rstd = lax.rsqrt(var + LN_EPS)
    return xc * rstd, rstd


def _ln_bwd(dy, xhat, rstd, g):
    dxh = dy * g
    m1 = jnp.mean(dxh, axis=-1, keepdims=True)
    m2 = jnp.mean(dxh * xhat, axis=-1, keepdims=True)
    return rstd * (dxh - m1 - xhat * m2)


def _colsum(x):
    return jnp.sum(x, axis=0, keepdims=True)


def _accumulate(ref, val, first):
    @pl.when(first)
    def _():
        ref[...] = val

    @pl.when(jnp.logical_not(first))
    def _():
        ref[...] += val


def _matmul(name, grid, a, a_spec, b, b_spec, *, nt, extras, outs, epilogue, acc_shape=None, after=None):
    nk = grid[2]
    if after is not None:
        extras = list(extras) + [(after, pl.BlockSpec(memory_space=pl.ANY))]
    ne, no = len(extras), len(outs)
    dims = (((1,), (1,)), ((), ())) if nt else (((1,), (0,)), ((), ()))

    def body(*refs):
        a_ref, b_ref = refs[0], refs[1]
        ex = refs[2:2 + ne]
        out_refs = refs[2 + ne:2 + ne + no]
        ids = (pl.program_id(0), pl.program_id(1), pl.program_id(2))
        part = lax.dot_general(a_ref[...], b_ref[...], dims, preferred_element_type=F32)
        if nk == 1:
            epilogue(part, ex, out_refs, ids)
        elif acc_shape is None:
            k = ids[2]

            @pl.when(k == 0)
            def _():
                epilogue(part, ex, out_refs, ids)

            @pl.when(k > 0)
            def _():
                out_refs[0][...] += part
        else:
            acc_ref = refs[2 + ne + no]
            k = ids[2]

            @pl.when(k == 0)
            def _():
                acc_ref[...] = part

            @pl.when(k > 0)
            def _():
                acc_ref[...] += part

            @pl.when(k == nk - 1)
            def _():
                epilogue(acc_ref[...], ex, out_refs, ids)

    return pl.pallas_call(
        body,
        name=name,
        grid=grid,
        in_specs=[a_spec, b_spec] + [s for _, s in extras],
        out_specs=[s for _, s in outs],
        out_shape=[o for o, _ in outs],
        scratch_shapes=[pltpu.VMEM(acc_shape, F32)] if nk > 1 and acc_shape is not None else [],
        compiler_params=_params(3),
    )(a, b, *[e for e, _ in extras])


def _matmul_tn(name, a, b, n_shards, shard_rows, shard_cols, row_sharded):
    s_len, ka = a.shape
    n = b.shape[1]
    assert (n_shards * shard_rows, shard_cols) == (ka, n) if row_sharded else (shard_rows, n_shards * shard_cols) == (ka, n)
    bs = _tile(s_len, BS_TN)
    bka = _tile(shard_rows, 2048)
    bn = next((t for t in (2 * BN_MM, BN_MM) if shard_cols % t == 0 and bka * t * 4 <= TN_OUT_BYTES), shard_cols)
    ni, nj, ns = ka // bka, n // bn, s_len // bs
    per_shard_i = shard_rows // bka
    per_shard_j = shard_cols // bn

    if row_sharded:
        out_map = lambda i, j, s: (i // per_shard_i, i % per_shard_i, j)
    else:
        out_map = lambda i, j, s: (j // per_shard_j, i, j % per_shard_j)

    def body(a_ref, b_ref, o_ref):
        s = pl.program_id(2)
        part = lax.dot_general(a_ref[...], b_ref[...], (((0,), (0,)), ((), ())), preferred_element_type=F32)
        _accumulate(o_ref, part, s == 0)

    return pl.pallas_call(
        body,
        name=name,
        grid=(ni, nj, ns),
        in_specs=[pl.BlockSpec((bs, bka), lambda i, j, s: (s, i)), pl.BlockSpec((bs, bn), lambda i, j, s: (s, j))],
        out_specs=pl.BlockSpec((None, bka, bn), out_map),
        out_shape=jax.ShapeDtypeStruct((n_shards, shard_rows, shard_cols), F32),
        compiler_params=_params(3),
    )(a, b)


def _row(v):
    return v.reshape(1, -1)


def _proj(name, xb, w_g, b_in, after=None):
    s_len, d = xb.shape
    ncs = w_g.shape[2]
    bm = _tile(s_len, BM_MM)

    def epilogue(acc, ex, outs, ids):
        outs[0][...] = acc + ex[0][...]

    return _matmul(
        name, (s_len // bm, 4, 1),
        xb, pl.BlockSpec((bm, d), lambda i, j, k: (i, 0)),
        w_g, pl.BlockSpec((None, d, ncs), lambda i, j, k: (j, 0, 0)),
        nt=False,
        extras=[(_row(b_in), pl.BlockSpec((1, ncs), lambda i, j, k: (0, j)))],
        outs=[(jax.ShapeDtypeStruct((s_len, 4 * ncs), F32), pl.BlockSpec((bm, ncs), lambda i, j, k: (i, j)))],
        epilogue=epilogue, after=after,
    )[0]


def _mix_out(name, mixed, w_out_full, b_out, x0, g1, b1):
    s_len, d = mixed.shape
    bm = _tile(s_len, BM_LN // 2)
    row = pl.BlockSpec((1, d), lambda i, j, k: (0, 0))
    blk = pl.BlockSpec((bm, d), lambda i, j, k: (i, 0))

    def epilogue(acc, ex, outs, ids):
        r1 = ALPHA * ex[1][...] + (acc + ex[0][...])
        outs[0][...] = r1
        xhat, _ = _ln_stats(r1)
        outs[1][...] = (xhat * ex[2][...] + ex[3][...]).astype(BF16)

    return _matmul(
        name, (s_len // bm, 1, 1),
        mixed, blk,
        w_out_full, pl.BlockSpec((d, d), lambda i, j, k: (0, 0)),
        nt=False,
        extras=[(_row(b_out), row), (x0, blk), (_row(g1), row), (_row(b1), row)],
        outs=[(jax.ShapeDtypeStruct((s_len, d), F32), blk), (jax.ShapeDtypeStruct((s_len, d), BF16), blk)],
        epilogue=epilogue,
    )


def _ff1(name, x1b, w_g, b_ff1, after=None):
    s_len, d = x1b.shape
    ncs = w_g.shape[2]
    bm = _tile(s_len, BM_MM)
    bn = _tile(ncs, BN_MM)
    per = ncs // bn
    blk = pl.BlockSpec((bm, bn), lambda i, j, k: (i, j))

    def epilogue(acc, ex, outs, ids):
        zr = jnp.maximum(acc + ex[0][...], 0.0)
        outs[0][...] = (zr * zr).astype(BF16)
        outs[1][...] = zr.astype(BF16)

    shape = jax.ShapeDtypeStruct((s_len, 4 * ncs), BF16)
    return _matmul(
        name, (s_len // bm, 4 * per, 1),
        x1b, pl.BlockSpec((bm, d), lambda i, j, k: (i, 0)),
        w_g, pl.BlockSpec((None, d, bn), lambda i, j, k: (j // per, 0, j % per)),
        nt=False,
        extras=[(_row(b_ff1), pl.BlockSpec((1, bn), lambda i, j, k: (0, j)))],
        outs=[(shape, blk), (shape, blk)],
        epilogue=epilogue, after=after,
    )


def _ff2(name, hf, w2_full, b_ff2):
    s_len, dff = hf.shape
    d = w2_full.shape[1]
    bm = _tile(s_len, BM_MM)
    bk = _tile(dff, BK_WIDE)
    blk = pl.BlockSpec((bm, d), lambda i, j, k: (i, 0))

    def epilogue(acc, ex, outs, ids):
        outs[0][...] = acc + ex[0][...]

    return _matmul(
        name, (s_len // bm, 1, dff // bk),
        hf, pl.BlockSpec((bm, bk), lambda i, j, k: (i, k)),
        w2_full, pl.BlockSpec((bk, d), lambda i, j, k: (k, 0)),
        nt=False,
        extras=[(_row(b_ff2), pl.BlockSpec((1, d), lambda i, j, k: (0, 0)))],
        outs=[(jax.ShapeDtypeStruct((s_len, d), F32), blk)],
        epilogue=epilogue,
    )[0]


def _resid_ln2(name, r1, g1, b1, fo, g2, b2):
    s_len, d = r1.shape
    bm = _tile(s_len, BM_LN // 2)
    blk = pl.BlockSpec((bm, d), lambda i: (i, 0))
    row = pl.BlockSpec((1, d), lambda i: (0, 0))

    def body(r1_ref, g1_ref, b1_ref, fo_ref, g2_ref, b2_ref, r2_ref, x2_ref, x2b_ref):
        xhat1, _ = _ln_stats(r1_ref[...])
        r2 = ALPHA * (xhat1 * g1_ref[...] + b1_ref[...]) + fo_ref[...]
        r2_ref[...] = r2
        xhat2, _ = _ln_stats(r2)
        x2 = xhat2 * g2_ref[...] + b2_ref[...]
        x2_ref[...] = x2
        x2b_ref[...] = x2.astype(BF16)

    return pl.pallas_call(
        body, name=name, grid=(s_len // bm,), in_specs=[blk, row, row, blk, row, row], out_specs=[blk, blk, blk],
        out_shape=[jax.ShapeDtypeStruct((s_len, d), F32), jax.ShapeDtypeStruct((s_len, d), F32),
                   jax.ShapeDtypeStruct((s_len, d), BF16)],
        compiler_params=_params(1))(r1, _row(g1), _row(b1), fo, _row(g2), _row(b2))


def _dff_hidden(name, dr2b, w2_full, zr):
    s_len, d = dr2b.shape
    dff = w2_full.shape[0]
    bm = _tile(s_len, BM_MM)
    bn = _tile(dff, BN_MM)

    def epilogue(acc, ex, outs, ids):
        dz = acc * (2.0 * ex[0][...].astype(F32))
        outs[0][...] = dz.astype(BF16)
        _accumulate(outs[1], _colsum(dz), ids[1] == 0)

    return _matmul(
        name, (dff // bn, s_len // bm, 1),
        dr2b, pl.BlockSpec((bm, d), lambda j, i, k: (i, 0)),
        w2_full, pl.BlockSpec((bn, d), lambda j, i, k: (j, 0)),
        nt=True,
        extras=[(zr, pl.BlockSpec((bm, bn), lambda j, i, k: (i, j)))],
        outs=[(jax.ShapeDtypeStruct((s_len, dff), BF16), pl.BlockSpec((bm, bn), lambda j, i, k: (i, j))),
              (jax.ShapeDtypeStruct((1, dff), F32), pl.BlockSpec((1, bn), lambda j, i, k: (0, j)))],
        epilogue=epilogue,
    )


def _dx_sharded(name, dyb, w_g, resid=None, after=None):
    s_len = dyb.shape[0]
    d, ncs = w_g.shape[1], w_g.shape[2]
    bm = _tile(s_len, BM_MM if resid is None else BM_LN)
    bk = _tile(ncs, BK_WIDE) if ncs % BK_MM == 0 else ncs
    per = ncs // bk
    blk = pl.BlockSpec((bm, d), lambda i, j, k: (i, 0))

    def epilogue(acc, ex, outs, ids):
        outs[0][...] = acc if resid is None else acc + ALPHA * ex[0][...]

    return _matmul(
        name, (s_len // bm, 1, 4 * per),
        dyb, pl.BlockSpec((bm, bk), lambda i, j, k: (i, k)),
        w_g, pl.BlockSpec((None, d, bk), lambda i, j, k: (k // per, 0, k % per)),
        nt=True,
        extras=[] if resid is None else [(resid, blk)],
        outs=[(jax.ShapeDtypeStruct((s_len, d), F32), blk)],
        epilogue=epilogue, after=after,
    )[0]


def _dmixed(name, dr1b, w_out_full):
    s_len, d = dr1b.shape
    bm = _tile(s_len, BM_LN)
    blk = pl.BlockSpec((bm, d), lambda i, j, k: (i, 0))

    def epilogue(acc, ex, outs, ids):
        outs[0][...] = acc

    return _matmul(
        name, (s_len // bm, 1, 1),
        dr1b, blk,
        w_out_full, pl.BlockSpec((d, d), lambda i, j, k: (0, 0)),
        nt=True, extras=[],
        outs=[(jax.ShapeDtypeStruct((s_len, d), F32), blk)],
        epilogue=epilogue,
    )[0]


def _ln_backward(name, r, g, *, dy=None, resid=None, x=None, target=None, after=None):
    s_len, d = r.shape
    bm = _tile(s_len, BM_LN // 2)
    from_loss = dy is None
    blk = pl.BlockSpec((bm, d), lambda i: (i, 0))
    row = pl.BlockSpec((1, d), lambda i: (0, 0))

    def body(*refs):
        i = pl.program_id(0)
        if after is not None:
            refs = refs[1:]
        if from_loss:
            x_ref, t_ref, r_ref, g_ref, dr_ref, drb_ref, dg_ref, db_ref, dbias_ref, loss_ref = refs
            diff = x_ref[...] - t_ref[...]
            dyv = diff * (1.0 / d)
            part = 0.5 * jnp.sum(jnp.sum(diff * diff, axis=1, keepdims=True) * (1.0 / d), axis=0, keepdims=True)
            _accumulate(loss_ref, jnp.broadcast_to(part, loss_ref.shape), i == 0)
        elif resid is not None:
            dy_ref, res_ref, r_ref, g_ref, dr_ref, drb_ref, dg_ref, db_ref, dbias_ref = refs
            dyv = dy_ref[...] + ALPHA * res_ref[...]
        else:
            dy_ref, r_ref, g_ref, dr_ref, drb_ref, dg_ref, db_ref, dbias_ref = refs
            dyv = dy_ref[...]
        xhat, rstd = _ln_stats(r_ref[...])
        dr = _ln_bwd(dyv, xhat, rstd, g_ref[...])
        dr_ref[...] = dr
        drb_ref[...] = dr.astype(BF16)
        _accumulate(dg_ref, _colsum(dyv * xhat), i == 0)
        _accumulate(db_ref, _colsum(dyv), i == 0)
        _accumulate(dbias_ref, _colsum(dr), i == 0)

    lead = [x, target] if from_loss else ([dy] if resid is None else [dy, resid])
    ins = lead + [r, _row(g)]
    in_specs = [blk] * len(lead) + [blk, row]
    if after is not None:
        ins, in_specs = [after] + ins, [pl.BlockSpec(memory_space=pl.ANY)] + in_specs
    out_shape = [jax.ShapeDtypeStruct((s_len, d), F32), jax.ShapeDtypeStruct((s_len, d), BF16)] + \
                [jax.ShapeDtypeStruct((1, d), F32)] * 3
    out_specs = [blk, blk, row, row, row]
    if from_loss:
        out_shape.append(jax.ShapeDtypeStruct((8, 128), F32))
        out_specs.append(pl.BlockSpec((8, 128), lambda i: (0, 0)))
    return pl.pallas_call(body, name=name, grid=(s_len // bm,), in_specs=in_specs, out_specs=out_specs,
                          out_shape=out_shape, compiler_params=_params(1))(*ins)


def _rowwise(name, fn, ins, out_dtypes, rows_pref=BR_EW):
    r, c = ins[0].shape
    br = _tile(r, rows_pref)
    blk = pl.BlockSpec((br, c), lambda i: (i, 0))

    def body(*refs):
        res = fn(*[ref[...] for ref in refs[:len(ins)]])
        for o_ref, v in zip(refs[len(ins):], res):
            o_ref[...] = v.astype(o_ref.dtype)

    return pl.pallas_call(body, name=name, grid=(r // br,), in_specs=[blk] * len(ins),
                          out_specs=[blk] * len(out_dtypes),
                          out_shape=[jax.ShapeDtypeStruct((r, c), dt) for dt in out_dtypes],
                          compiler_params=_params(1))(*ins)


def _adamw_math(w, g, m, v):
    m = ADAM_B1 * m + (1.0 - ADAM_B1) * g
    v = ADAM_B2 * v + (1.0 - ADAM_B2) * (g * g)
    m_hat = m / (1.0 - ADAM_B1 ** ADAM_STEP)
    v_hat = v / (1.0 - ADAM_B2 ** ADAM_STEP)
    delta = -ADAM_LR * (m_hat / (jnp.sqrt(v_hat) + ADAM_EPS) + ADAM_WD * w)
    return delta, m, v


def _adamw(name, w, g, m, v):
    shape = w.shape
    c = shape[-1]
    flat = [a.reshape(-1, c) for a in (w, g, m, v)]
    rows = flat[0].shape[0]
    pref = max(8, 2 ** int(math.log2(EW_BLOCK_ELEMS // c)))
    res = _rowwise(name, _adamw_math, flat, [F32, F32, F32], rows_pref=pref if rows % pref == 0 else rows)
    return tuple(a.reshape(shape) for a in res)


def _pool_means(ext_ref, ts, tile_index):
    t_glob = tile_index * ts + lax.broadcasted_iota(jnp.int32, (ts, GROUP), 0)
    qs = []
    for g, win in enumerate(POOL_WINDOWS):
        cols = pl.ds(g * GROUP, GROUP)
        cur = ext_ref[pl.ds(16, ts), cols]
        acc = cur
        for j in range(1, win):
            acc = acc + ext_ref[pl.ds(16 - j, ts), cols]
        cnt = jnp.minimum(t_glob + 1, win).astype(F32)
        qs.append(acc / cnt - cur)
    return qs


def _masked_sgu_w(w_ref, h):
    r = lax.broadcasted_iota(jnp.int32, (CHUNK, CHUNK), 0)
    c = lax.broadcasted_iota(jnp.int32, (CHUNK, CHUNK), 1)
    return jnp.where(r >= c, w_ref[h], 0.0)


def _conv_taps(acc_init, w_ref, src_ref, ts, base):
    nrb = ts // RB_CONV
    blocks = []
    for cg in range(CONV_WIDTH // GROUP):
        cols = pl.ds(cg * GROUP, GROUP)
        accs = [acc_init(cg) for _ in range(nrb)]
        for k in range(CONV_KERNEL):
            wk = jnp.broadcast_to(w_ref[pl.ds(k, 1), cols], (RB_CONV, GROUP))
            for rb in range(nrb):
                accs[rb] = accs[rb] + wk * src_ref[pl.ds(rb * RB_CONV + base(k), RB_CONV), cols]
        for rb in range(nrb):
            blocks.append((rb * RB_CONV, cg * GROUP, accs[rb]))
    return blocks


def _mix_forward(name, p, lw):
    s_len = p.shape[0]
    ts = _tile(s_len, TS_MIX)
    per_halo = ts // HALO
    d = POOL_WIDTH + SGU_WIDTH + CONV_WIDTH

    def body(p_ref, ph_ref, wp_ref, ps_ref, slg_ref, slb_ref, ws_ref, sb_ref, cw_ref, cb_ref, clg_ref, clb_ref,
             mixed_ref, cv_ref, pool_ext, hh_ext):
        i = pl.program_id(0)
        keep = (i > 0).astype(F32)
        pool_ext[pl.ds(0, 16), :] = ph_ref[pl.ds(16, 16), pl.ds(0, POOL_WIDTH)] * keep
        pool_ext[pl.ds(16, ts), :] = p_ref[:, pl.ds(0, POOL_WIDTH)]
        qs = _pool_means(pool_ext, ts, i)
        for g in range(len(POOL_WINDOWS)):
            cols = pl.ds(g * GROUP, GROUP)
            e = jnp.dot(qs[g].astype(BF16), wp_ref[g].astype(BF16), preferred_element_type=F32)
            mixed_ref[:, cols] = (e * ps_ref[:, cols]).astype(BF16)
        uv = _gelu(p_ref[:, pl.ds(COL_B, 2 * SGU_WIDTH)])
        u = uv[:, :SGU_WIDTH]
        vhat, _ = _ln_stats(uv[:, SGU_WIDTH:])
        vn = (vhat * slg_ref[...] + slb_ref[...]).astype(BF16)
        for h in range(SGU_HEADS):
            wm = _masked_sgu_w(ws_ref, h).astype(BF16)
            for n in range(ts // CHUNK):
                rows = slice(n * CHUNK, (n + 1) * CHUNK)
                cols = slice(h * GROUP, (h + 1) * GROUP)
                mx = jnp.dot(wm, vn[rows, cols], preferred_element_type=F32) + sb_ref[h]
                mixed_ref[pl.ds(n * CHUNK, CHUNK), pl.ds(POOL_WIDTH + h * GROUP, GROUP)] = (u[rows, cols] * mx).astype(BF16)
        hh_ext[pl.ds(0, HALO), :] = (ph_ref[:, pl.ds(COL_C, CONV_WIDTH)]
                                     * _sigmoid(ph_ref[:, pl.ds(COL_C + CONV_WIDTH, CONV_WIDTH)])) * keep
        hh_ext[pl.ds(HALO, ts), :] = p_ref[:, pl.ds(COL_C, CONV_WIDTH)] * _sigmoid(p_ref[:, pl.ds(COL_C + CONV_WIDTH, CONV_WIDTH)])
        init = lambda cg: jnp.broadcast_to(cb_ref[:, pl.ds(cg * GROUP, GROUP)], (RB_CONV, GROUP))
        for r0, c0, blk in _conv_taps(init, cw_ref, hh_ext, ts, lambda k: k + HALO - (CONV_KERNEL - 1)):
            cv_ref[pl.ds(r0, RB_CONV), pl.ds(c0, GROUP)] = blk
        cvhat, _ = _ln_stats(cv_ref[...])
        cn = cvhat * clg_ref[...] + clb_ref[...]
        mixed_ref[:, pl.ds(POOL_WIDTH + SGU_WIDTH, CONV_WIDTH)] = (cn * _sigmoid(cn)).astype(BF16)

    full = lambda a: pl.BlockSpec(a.shape, lambda i: (0,) * a.ndim)
    weights = [lw["w_pool"], _row(lw["pool_scale"]), _row(lw["sgu_ln_g"]), _row(lw["sgu_ln_b"]), lw["sgu_w"],
               lw["sgu_bias_tile"], lw["conv_w_full"], _row(lw["conv_b"]), _row(lw["conv_ln_g"]), _row(lw["conv_ln_b"])]
    return pl.pallas_call(
        body, name=name, grid=(s_len // ts,),
        in_specs=[pl.BlockSpec((ts, IN_WIDTH), lambda i: (i, 0)),
                  pl.BlockSpec((HALO, IN_WIDTH), lambda i: (jnp.maximum(i * per_halo - 1, 0), 0))] + [full(a) for a in weights],
        out_specs=[pl.BlockSpec((ts, d), lambda i: (i, 0)), pl.BlockSpec((ts, CONV_WIDTH), lambda i: (i, 0))],
        out_shape=[jax.ShapeDtypeStruct((s_len, d), BF16), jax.ShapeDtypeStruct((s_len, CONV_WIDTH), F32)],
        scratch_shapes=[pltpu.VMEM((16 + ts, POOL_WIDTH), F32), pltpu.VMEM((HALO + ts, CONV_WIDTH), F32)],
        compiler_params=_params(1),
    )(p, p, *weights)


def _mix_backward(name, p, cv, dmixed, lw):
    s_len = p.shape[0]
    ts = _tile(s_len, TS_MIX)
    nt = s_len // ts
    per_halo = ts // HALO
    d = POOL_WIDTH + SGU_WIDTH + CONV_WIDTH
    nch = ts // CHUNK
    col_yc = POOL_WIDTH + SGU_WIDTH

    def body(p_ref, ph_ref, cv_ref, cvn_ref, dm_ref, dmn_ref, wp_ref, ps_ref, slg_ref, slb_ref, ws_ref, wst_ref, sb_ref,
             cw_ref, cwr_ref, clg_ref, clb_ref,
             dp_ref, dbin_ref, dwp_ref, dps_ref, dslg_ref, dslb_ref, dws_ref, dsb_ref, dcw_ref, dcb_ref, dclg_ref, dclb_ref,
             pool_ext, dq_ext, hh_ext, dcv_ext, dcw_acc):
        i = pl.program_id(0)
        first = i == 0
        keep_prev = (i > 0).astype(F32)
        keep_next = (i < nt - 1).astype(F32)

        pool_ext[pl.ds(0, 16), :] = ph_ref[pl.ds(16, 16), pl.ds(0, POOL_WIDTH)] * keep_prev
        pool_ext[pl.ds(16, ts), :] = p_ref[:, pl.ds(0, POOL_WIDTH)]
        qs = _pool_means(pool_ext, ts, i)
        t_ext = i * ts + lax.broadcasted_iota(jnp.int32, (ts + HALO, GROUP), 0)
        for g, win in enumerate(POOL_WINDOWS):
            cols = pl.ds(g * GROUP, GROUP)
            wpb = wp_ref[g].astype(BF16)
            qb = qs[g].astype(BF16)
            de = dm_ref[:, cols] * ps_ref[:, cols]
            e = jnp.dot(qb, wpb, preferred_element_type=F32)
            _accumulate(dps_ref.at[:, cols], _colsum(dm_ref[:, cols] * e), first)
            deb = de.astype(BF16)
            _accumulate(dwp_ref.at[g], lax.dot_general(qb, deb, (((0,), (0,)), ((), ())), preferred_element_type=F32), first)
            de_next = (dmn_ref[:, cols] * ps_ref[:, cols] * keep_next).astype(BF16)
            de_all = jnp.concatenate([deb, de_next], axis=0)
            dq = lax.dot_general(de_all, wpb, (((1,), (1,)), ((), ())), preferred_element_type=F32)
            inv = 1.0 / jnp.minimum(t_ext + 1, win).astype(F32)
            dq_ext[:, cols] = dq * inv
            acc = dq_ext[pl.ds(0, ts), cols]
            for j in range(1, win):
                acc = acc + dq_ext[pl.ds(j, ts), cols]
            dpa = acc - dq[:ts]
            dp_ref[:, cols] = dpa.astype(BF16)
            _accumulate(dbin_ref.at[:, cols], _colsum(dpa), first)

        pb = p_ref[:, pl.ds(COL_B, 2 * SGU_WIDTH)]
        uv = _gelu(pb)
        u = uv[:, :SGU_WIDTH]
        vhat, vrstd = _ln_stats(uv[:, SGU_WIDTH:])
        vn = (vhat * slg_ref[...] + slb_ref[...]).astype(BF16)
        dyb = dm_ref[:, pl.ds(POOL_WIDTH, SGU_WIDTH)]
        dmix = dyb * u
        dmixb = dmix.astype(BF16)
        du_parts, dvn_parts = [], []
        for h in range(SGU_HEADS):
            cols = slice(h * GROUP, (h + 1) * GROUP)
            wm = _masked_sgu_w(ws_ref, h).astype(BF16)
            wmt = _masked_sgu_w_t(wst_ref, h).astype(BF16)
            dws_h = jnp.zeros((CHUNK, CHUNK), F32)
            dsb_h = jnp.zeros((CHUNK, GROUP), F32)
            du_rows, dvn_rows = [], []
            for n in range(nch):
                rows = slice(n * CHUNK, (n + 1) * CHUNK)
                mx = jnp.dot(wm, vn[rows, cols], preferred_element_type=F32) + sb_ref[h]
                du_rows.append(dyb[rows, cols] * mx)
                dws_h = dws_h + lax.dot_general(dmixb[rows, cols], vn[rows, cols], (((1,), (1,)), ((), ())),
                                                preferred_element_type=F32)
                dsb_h = dsb_h + dmix[rows, cols]
                dvn_rows.append(jnp.dot(wmt, dmixb[rows, cols], preferred_element_type=F32))
            r = lax.broadcasted_iota(jnp.int32, (CHUNK, CHUNK), 0)
            c = lax.broadcasted_iota(jnp.int32, (CHUNK, CHUNK), 1)
            _accumulate(dws_ref.at[h], jnp.where(r >= c, dws_h, 0.0), first)
            _accumulate(dsb_ref.at[h], jnp.broadcast_to(jnp.sum(dsb_h, axis=1, keepdims=True), (CHUNK, GROUP)), first)
            du_parts.append(jnp.concatenate(du_rows, axis=0))
            dvn_parts.append(jnp.concatenate(dvn_rows, axis=0))
        du = jnp.concatenate(du_parts, axis=1)
        dvn = jnp.concatenate(dvn_parts, axis=1)
        _accumulate(dslg_ref, _colsum(dvn * vhat), first)
        _accumulate(dslb_ref, _colsum(dvn), first)
        dv = _ln_bwd(dvn, vhat, vrstd, slg_ref[...])
        dpb = jnp.concatenate([du, dv], axis=1) * _gelu_grad(pb)
        dp_ref[:, pl.ds(COL_B, 2 * SGU_WIDTH)] = dpb.astype(BF16)
        _accumulate(dbin_ref.at[:, pl.ds(COL_B, 2 * SGU_WIDTH)], _colsum(dpb), first)

        a_main = p_ref[:, pl.ds(COL_C, CONV_WIDTH)]
        sg_main = _sigmoid(p_ref[:, pl.ds(COL_C + CONV_WIDTH, CONV_WIDTH)])
        hh_ext[pl.ds(0, HALO), :] = (ph_ref[:, pl.ds(COL_C, CONV_WIDTH)]
                                     * _sigmoid(ph_ref[:, pl.ds(COL_C + CONV_WIDTH, CONV_WIDTH)])) * keep_prev
        hh_ext[pl.ds(HALO, ts), :] = a_main * sg_main

        def conv_ln_backward(cv_v, dyc_v):
            cvhat, crstd = _ln_stats(cv_v)
            cn = cvhat * clg_ref[...] + clb_ref[...]
            s = _sigmoid(cn)
            dcn = dyc_v * (s * (1.0 + cn * (1.0 - s)))
            return _ln_bwd(dcn, cvhat, crstd, clg_ref[...]), dcn, cvhat

        dcv, dcn, cvhat = conv_ln_backward(cv_ref[...], dm_ref[:, pl.ds(col_yc, CONV_WIDTH)])
        _accumulate(dclg_ref, _colsum(dcn * cvhat), first)
        _accumulate(dclb_ref, _colsum(dcn), first)
        _accumulate(dcb_ref, _colsum(dcv), first)
        dcv_next, _, _ = conv_ln_backward(cvn_ref[...], dmn_ref[:, pl.ds(col_yc, CONV_WIDTH)])
        dcv_ext[pl.ds(0, ts), :] = dcv
        dcv_ext[pl.ds(ts, HALO), :] = dcv_next * keep_next

        @pl.when(first)
        def _():
            dcw_acc[...] = jnp.zeros_like(dcw_acc)

        nrb = ts // RB_CONV
        for cg in range(CONV_WIDTH // GROUP):
            cols = pl.ds(cg * GROUP, GROUP)
            dblk = [dcv_ext[pl.ds(rb * RB_CONV, RB_CONV), cols] for rb in range(nrb)]
            for k in range(CONV_KERNEL):
                part = jnp.zeros((8, GROUP), F32)
                for rb in range(nrb):
                    prod = dblk[rb] * hh_ext[pl.ds(rb * RB_CONV + k + HALO - (CONV_KERNEL - 1), RB_CONV), cols]
                    part = part + jnp.sum(prod.reshape(RB_CONV // 8, 8, GROUP), axis=0)
                dcw_acc[k, :, cols] += part

        @pl.when(i == nt - 1)
        def _():
            dcw_ref[...] = jnp.sum(dcw_acc[...], axis=1)

        zero = lambda cg: jnp.zeros((RB_CONV, GROUP), F32)
        for r0, c0, blk in _conv_taps(zero, cwr_ref, dcv_ext, ts, lambda k: k):
            rows, cols = pl.ds(r0, RB_CONV), pl.ds(c0, GROUP)
            a_blk = p_ref[rows, pl.ds(COL_C + c0, GROUP)]
            s_blk = _sigmoid(p_ref[rows, pl.ds(COL_C + CONV_WIDTH + c0, GROUP)])
            da = blk * s_blk
            dg = blk * a_blk * s_blk * (1.0 - s_blk)
            dp_ref[rows, pl.ds(COL_C + c0, GROUP)] = da.astype(BF16)
            dp_ref[rows, pl.ds(COL_C + CONV_WIDTH + c0, GROUP)] = dg.astype(BF16)
            hh_ext[pl.ds(HALO + r0, RB_CONV), cols] = da
            dcv_ext[rows, cols] = dg
        _accumulate(dbin_ref.at[:, pl.ds(COL_C, CONV_WIDTH)], _colsum(hh_ext[pl.ds(HALO, ts), :]), first)
        _accumulate(dbin_ref.at[:, pl.ds(COL_C + CONV_WIDTH, CONV_WIDTH)], _colsum(dcv_ext[pl.ds(0, ts), :]), first)

    full = lambda a: pl.BlockSpec(a.shape, lambda i: (0,) * a.ndim)
    weights = [lw["w_pool"], _row(lw["pool_scale"]), _row(lw["sgu_ln_g"]), _row(lw["sgu_ln_b"]), lw["sgu_w"], lw["sgu_w_t"],
               lw["sgu_bias_tile"], lw["conv_w_full"], lw["conv_w_rev"], _row(lw["conv_ln_g"]), _row(lw["conv_ln_b"])]
    prev_halo = lambda i: (jnp.maximum(i * per_halo - 1, 0), 0)
    next_halo = lambda i: (jnp.minimum((i + 1) * per_halo, s_len // HALO - 1), 0)
    small = lambda shape: (jax.ShapeDtypeStruct(shape, F32), pl.BlockSpec(shape, lambda i: (0,) * len(shape)))
    outs = [(jax.ShapeDtypeStruct((s_len, IN_WIDTH), BF16), pl.BlockSpec((ts, IN_WIDTH), lambda i: (i, 0))),
            small((1, IN_WIDTH)), small((len(POOL_WINDOWS), GROUP, GROUP)), small((1, POOL_WIDTH)),
            small((1, SGU_WIDTH)), small((1, SGU_WIDTH)), small((SGU_HEADS, CHUNK, CHUNK)), small((SGU_HEADS, CHUNK, GROUP)),
            small((CONV_ROWS, CONV_WIDTH)), small((1, CONV_WIDTH)), small((1, CONV_WIDTH)), small((1, CONV_WIDTH))]
    return pl.pallas_call(
        body, name=name, grid=(nt,),
        in_specs=[pl.BlockSpec((ts, IN_WIDTH), lambda i: (i, 0)), pl.BlockSpec((HALO, IN_WIDTH), prev_halo),
                  pl.BlockSpec((ts, CONV_WIDTH), lambda i: (i, 0)), pl.BlockSpec((HALO, CONV_WIDTH), next_halo),
                  pl.BlockSpec((ts, d), lambda i: (i, 0)), pl.BlockSpec((HALO, d), next_halo)] + [full(a) for a in weights],
        out_specs=[s for _, s in outs],
        out_shape=[o for o, _ in outs],
        scratch_shapes=[pltpu.VMEM((16 + ts, POOL_WIDTH), F32), pltpu.VMEM((ts + HALO, POOL_WIDTH), F32),
                        pltpu.VMEM((HALO + ts, CONV_WIDTH), F32), pltpu.VMEM((ts + HALO, CONV_WIDTH), F32),
                        pltpu.VMEM((CONV_ROWS, 8, CONV_WIDTH), F32)],
        compiler_params=_params(1),
    )(p, p, cv, cv, dmixed, dmixed, *weights)


def _masked_sgu_w_t(wt_ref, h):
    r = lax.broadcasted_iota(jnp.int32, (CHUNK, CHUNK), 0)
    c = lax.broadcasted_iota(jnp.int32, (CHUNK, CHUNK), 1)
    return jnp.where(c >= r, wt_ref[h], 0.0)


HBM = pl.BlockSpec(memory_space=pltpu.HBM)
CHIP_FLIPS = ((1, 0), (0, 1), (1, 1))


def _place():
    return lax.axis_index("x"), lax.axis_index("y"), lax.axis_index("c")


def _half(ref, axis, which, size):
    idx = [slice(None)] * len(ref.shape)
    idx[axis] = pl.ds(which * size, size)
    return ref.at[tuple(idx)]


def _exchange(name, sources, inplace, fresh, copies):
    n_src, n_in, n = len(sources), len(inplace), len(copies)
    n_out = n_in + len(fresh)

    def body(*refs):
        ins = refs[:n_src + n_in]
        outs = refs[n_src + n_in:n_src + n_in + n_out]
        send_sems, recv_sems = refs[n_src + n_in + n_out:]
        started = _descriptors(copies, list(ins[:n_src]) + list(outs), send_sems, recv_sems)
        for cp in started:
            cp.start()
        for cp in started:
            cp.wait()

    out_shape = [jax.ShapeDtypeStruct(a.shape, a.dtype) for a in inplace] + list(fresh)
    return pl.pallas_call(
        body, name=name, in_specs=[HBM] * (n_src + n_in), out_specs=[HBM] * n_out, out_shape=out_shape,
        input_output_aliases={n_src + i: i for i in range(n_in)},
        scratch_shapes=[pltpu.SemaphoreType.DMA((n,)), pltpu.SemaphoreType.DMA((n,))],
    )(*sources, *inplace)


def _descriptors(copies, bufs, send_sems, recv_sems):
    x, y, c = _place()
    return [pltpu.make_async_remote_copy(
        src_ref=src_view(bufs[src], x, y, c), dst_ref=dst_view(bufs[dst], x, y, c),
        send_sem=send_sems.at[k], recv_sem=recv_sems.at[k], device_id=peer(x, y, c), device_id_type=MESH)
        for k, (src, src_view, dst, dst_view, peer) in enumerate(copies)]


SEM = pl.BlockSpec(memory_space=pltpu.SEMAPHORE)
IN_FLIGHT = pltpu.CompilerParams(has_side_effects=pltpu.SideEffectType.DATAFLOW_SIDE_EFFECTING)


def _exchange_start(name, sources, inplace, fresh, copies):
    n_src, n = len(sources), len(copies)
    landing = [lax.empty(f.shape, f.dtype) for f in fresh]
    bufs = [pltpu.with_memory_space_constraint(a, pltpu.HBM) for a in (*inplace, *landing)]
    srcs = [pltpu.with_memory_space_constraint(a, pltpu.HBM) for a in sources]
    n_buf = len(bufs)

    def body(*refs):
        ins = refs[:n_src]
        send_sems, recv_sems = refs[n_src + n_buf], refs[n_src + n_buf + 1]
        outs = refs[n_src + n_buf + 2:n_src + 2 * n_buf + 2]
        token = refs[n_src + 2 * n_buf + 2]
        for cp in _descriptors(copies, list(ins) + list(outs), send_sems, recv_sems):
            cp.start()
        token[...] = jnp.zeros_like(token)

    res = pl.pallas_call(
        body, name=name + "_start",
        out_shape=(pltpu.SemaphoreType.DMA((n,)), pltpu.SemaphoreType.DMA((n,)),
                   *[pltpu.HBM(b.shape, b.dtype) for b in bufs], jax.ShapeDtypeStruct((8, 128), F32)),
        in_specs=[HBM] * (n_src + n_buf),
        out_specs=(SEM, SEM, *[HBM] * n_buf, pl.BlockSpec(memory_space=pltpu.VMEM)),
        input_output_aliases={n_src + i: 2 + i for i in range(n_buf)},
        compiler_params=IN_FLIGHT,
    )(*srcs, *bufs)
    return dict(name=name, sources=srcs, sems=res[:2], bufs=list(res[2:2 + n_buf]), token=res[2 + n_buf], copies=copies)


def _exchange_wait(handle, after):
    srcs, bufs, copies = handle["sources"], handle["bufs"], handle["copies"]
    n_src, n_buf = len(srcs), len(bufs)

    def body(*refs):
        ins = refs[:n_src]
        send_sems, recv_sems = refs[n_src + n_buf], refs[n_src + n_buf + 1]
        outs = refs[n_src + n_buf + 3:]
        for cp in _descriptors(copies, list(ins) + list(outs), send_sems, recv_sems):
            cp.wait()

    return list(pl.pallas_call(
        body, name=handle["name"] + "_wait",
        out_shape=tuple(pltpu.HBM(b.shape, b.dtype) for b in bufs),
        in_specs=[HBM] * (n_src + n_buf) + [SEM, SEM, pl.BlockSpec(memory_space=pl.ANY)],
        out_specs=tuple([HBM] * n_buf),
        input_output_aliases={n_src + i: i for i in range(n_buf)},
        compiler_params=IN_FLIGHT,
    )(*srcs, *bufs, *handle["sems"], after))


def _into_slot(name, src, layer, place, dtype):
    _, r, c = src.shape
    br = _tile(r, BR_EW) if r % BR_EW == 0 else r

    def body(pr_ref, s_ref, o_ref):
        o_ref[...] = s_ref[...].astype(dtype)

    return pl.pallas_call(
        body, name=name,
        grid_spec=pltpu.PrefetchScalarGridSpec(
            num_scalar_prefetch=1, grid=(r // br,),
            in_specs=[pl.BlockSpec((None, br, c), lambda i, pr: (layer, i, 0))],
            out_specs=pl.BlockSpec((None, br, c), lambda i, pr: (pr[1], i, 0))),
        out_shape=jax.ShapeDtypeStruct((4, r, c), dtype), compiler_params=_params(1),
    )(place, src)


def _gather_group(name, bufs):
    ici, d2d = _gather_copies(len(bufs))
    bufs = _exchange(name + "_chips", [], bufs, [], ici)
    return _exchange(name + "_sibling", [], bufs, [], d2d)


def _gather_copies(n_bufs):
    def own_half(ref, x, y, c):
        return _half(ref.at[2 * x + y], 0, c, ref.shape[1] // 2)

    ici, d2d = [], []
    for b in range(n_bufs):
        for fx, fy in CHIP_FLIPS:
            ici.append((b, own_half, b, own_half, lambda x, y, c, fx=fx, fy=fy: (x ^ fx, y ^ fy, c)))
            landed = lambda ref, x, y, c, fx=fx, fy=fy: _half(ref.at[2 * (x ^ fx) + (y ^ fy)], 0, c, ref.shape[1] // 2)
            d2d.append((b, landed, b, landed, _sibling))
    return ici, d2d


def _gather_start(name, bufs):
    return _exchange_start(name + "_chips", [], bufs, [], _gather_copies(len(bufs))[0])


def _gather_finish(name, handle, after):
    bufs = _exchange_wait(handle, after)
    return _exchange(name + "_sibling", [], bufs, [], _gather_copies(len(bufs))[1])


def _sibling(x, y, c):
    return (x, y, 1 - c)


def _scalar_spec_call(name, fn, scalars, ins, in_blocks, out_shapes, out_blocks, grid):
    def body(s_ref, *refs):
        res = fn(*[r[...] for r in refs[:len(ins)]])
        for o_ref, v in zip(refs[len(ins):], res):
            o_ref[...] = v.astype(o_ref.dtype)

    return pl.pallas_call(
        body, name=name,
        grid_spec=pltpu.PrefetchScalarGridSpec(num_scalar_prefetch=1, grid=grid, in_specs=in_blocks, out_specs=out_blocks),
        out_shape=out_shapes, compiler_params=_params(len(grid)),
    )(scalars, *ins)


def _reduce_begin(tag, place, items):
    n = len(items)
    whole = lambda ref, x, y, c: ref
    halves = [g.shape[1] // 2 for g, _, _ in items]
    fresh = [jax.ShapeDtypeStruct((g.shape[0], halves[a], g.shape[2]), F32) for a, (g, _, _) in enumerate(items)]
    copies = [(a, lambda ref, x, y, c, hr=halves[a]: _half(ref, 1, 1 - c, hr), n + a, whole, _sibling) for a in range(n)]
    handle = _exchange_start(tag + "_sibling_in", [g for g, _, _ in items], [], fresh, copies)
    return dict(tag=tag, place=place, items=items, handle=handle, token=handle["token"])


def _reduce_between_chips(state, after):
    tag, place, items = state["tag"], state["place"], state["items"]
    n = len(items)
    slot_of = lambda scatter: (lambda x, y: 2 * x + y) if scatter else (lambda x, y: 0)
    halves = [g.shape[1] // 2 for g, _, _ in items]
    landed = _exchange_wait(state["handle"], after)

    chip_sums = []
    for a, (g, _, wire) in enumerate(items):
        ns, _, cols = g.shape
        hr = halves[a]
        br = _tile(hr, BR_EW)
        chip_sums.append(_scalar_spec_call(
            f"{tag}_chip_sum{a}", lambda u, v: (u + v,), place, [g, landed[a]],
            [pl.BlockSpec((None, br, cols), lambda s, i, pr, hb=hr // br: (s, pr[0] * hb + i, 0)),
             pl.BlockSpec((None, br, cols), lambda s, i, pr: (s, i, 0))],
            [jax.ShapeDtypeStruct((ns, hr, cols), wire)], [pl.BlockSpec((None, br, cols), lambda s, i, pr: (s, i, 0))],
            (ns, hr // br))[0])

    copies, fresh = [], []
    for a, (g, scatter, wire) in enumerate(items):
        slot = slot_of(scatter)
        fresh.append(jax.ShapeDtypeStruct((3, halves[a], g.shape[2]), wire))
        for j, (fx, fy) in enumerate(CHIP_FLIPS):
            copies.append((a, lambda ref, x, y, c, fx=fx, fy=fy, slot=slot: ref.at[slot(x ^ fx, y ^ fy)],
                           n + a, lambda ref, x, y, c, j=j: ref.at[j], lambda x, y, c, fx=fx, fy=fy: (x ^ fx, y ^ fy, c)))
    handle = _exchange_start(tag + "_chips", chip_sums, [], fresh, copies)
    return dict(tag=tag, place=place, items=items, chip_sums=chip_sums, handle=handle, token=handle["token"])


def _reduce_end(state, after):
    tag, place, items, chip_sums = state["tag"], state["place"], state["items"], state["chip_sums"]
    n = len(items)
    whole = lambda ref, x, y, c: ref
    halves = [g.shape[1] // 2 for g, _, _ in items]
    arrived = _exchange_wait(state["handle"], after)

    tree = lambda own, fx, fy, fxy: ((own.astype(F32) + fx.astype(F32)) + (fy.astype(F32) + fxy.astype(F32)),)
    mine = []
    for a, (g, scatter, _) in enumerate(items):
        hr, cols = halves[a], g.shape[2]
        br = _tile(hr, BR_EW)
        got = lambda j: pl.BlockSpec((None, br, cols), lambda i, pr, j=j: (j, i, 0))
        own = pl.BlockSpec((None, br, cols), (lambda i, pr: (pr[1], i, 0)) if scatter else (lambda i, pr: (0, i, 0)))
        mine.append(_scalar_spec_call(
            f"{tag}_tree_sum{a}", tree, place, [chip_sums[a], arrived[a], arrived[a], arrived[a]],
            [own, got(0), got(1), got(2)],
            [jax.ShapeDtypeStruct((hr, cols), F32)], [pl.BlockSpec((br, cols), lambda i, pr: (i, 0))], (hr // br,))[0])

    copies = [(a, whole, n + a, whole, _sibling) for a in range(n)]
    fresh = [jax.ShapeDtypeStruct(h.shape, F32) for h in mine]
    theirs = _exchange(tag + "_sibling_out", mine, [], fresh, copies)
    return list(zip(mine, theirs))


def _select_half(h, c, mine, theirs):
    return jnp.where(h == c, mine, theirs)


def _adamw_sharded(name, layer, place, w, m, v, g_mine, g_theirs, earlier):
    n_layers, r, c = w.shape
    hr = r // 2
    br = min(hr, max(8, 2 ** int(math.log2(EW_BLOCK_ELEMS // c))))
    assert hr % br == 0
    nb = hr // br
    full = pl.BlockSpec((None, br, c), lambda h, i, pr: (layer, h * nb + i, 0))
    half = pl.BlockSpec((br, c), lambda h, i, pr: (i, 0))
    n_alias = 0 if earlier is None else 4

    def body(pr_ref, w_ref, m_ref, v_ref, gm_ref, gt_ref, *rest):
        g_ref, d_ref, nm_ref, nv_ref = rest[n_alias:]
        g = _select_half(pl.program_id(0), pr_ref[0], gm_ref[...], gt_ref[...])
        delta, new_m, new_v = _adamw_math(w_ref[...], g, m_ref[...], v_ref[...])
        g_ref[...] = g
        d_ref[...] = delta
        nm_ref[...] = new_m
        nv_ref[...] = new_v

    return pl.pallas_call(
        body, name=name,
        grid_spec=pltpu.PrefetchScalarGridSpec(
            num_scalar_prefetch=1, grid=(2, nb),
            in_specs=[full, full, full, half, half] + [pl.BlockSpec(memory_space=pl.ANY)] * n_alias,
            out_specs=[full] * 4),
        out_shape=[jax.ShapeDtypeStruct((n_layers, r, c), F32)] * 4,
        input_output_aliases={6 + i: i for i in range(n_alias)},
        compiler_params=_params(2),
    )(place, w, m, v, g_mine, g_theirs, *(earlier or ()))


SHARDED = ("w_in", "w_out", "w_ff1", "w_ff2")
REPLICATED = ("b_in", "w_pool", "pool_scale", "sgu_ln_g", "sgu_ln_b", "sgu_w", "sgu_b", "conv_b", "conv_ln_g", "conv_ln_b",
              "b_out", "ln1_g", "ln1_b", "b_ff1", "b_ff2", "ln2_g", "ln2_b")
WEIGHTS = ("w_in", "b_in", "w_pool", "pool_scale", "sgu_ln_g", "sgu_ln_b", "sgu_w", "sgu_b", "conv_w", "conv_b", "conv_ln_g",
           "conv_ln_b", "w_out", "b_out", "ln1_g", "ln1_b", "w_ff1", "b_ff1", "w_ff2", "b_ff2", "ln2_g", "ln2_b")
PACK_ROWS = 1024


def _pack(arrays):
    flat = jnp.concatenate([a.reshape(-1) for a in arrays])
    rows = -(-flat.shape[0] // (128 * PACK_ROWS)) * PACK_ROWS
    return jnp.pad(flat, (0, rows * 128 - flat.shape[0])).reshape(rows, 128)


def _unpack(packed, like):
    flat = packed.reshape(-1)
    out, at = [], 0
    for a in like:
        out.append(flat[at:at + a.size].reshape(a.shape))
        at += a.size
    return out


def kernel(x, w_in, b_in, w_pool, pool_scale, sgu_ln_g, sgu_ln_b, sgu_w, sgu_b, conv_w, conv_b, conv_ln_g, conv_ln_b, w_out, b_out, ln1_g, ln1_b, w_ff1, b_ff1, w_ff2, b_ff2, ln2_g, ln2_b, loss_target, m_w_in, m_b_in, m_w_pool, m_pool_scale, m_sgu_ln_g, m_sgu_ln_b, m_sgu_w, m_sgu_b, m_conv_w, m_conv_b, m_conv_ln_g, m_conv_ln_b, m_w_out, m_b_out, m_ln1_g, m_ln1_b, m_w_ff1, m_b_ff1, m_w_ff2, m_b_ff2, m_ln2_g, m_ln2_b, v_w_in, v_b_in, v_w_pool, v_pool_scale, v_sgu_ln_g, v_sgu_ln_b, v_sgu_w, v_sgu_b, v_conv_w, v_conv_b, v_conv_ln_g, v_conv_ln_b, v_w_out, v_b_out, v_ln1_g, v_ln1_b, v_w_ff1, v_b_ff1, v_w_ff2, v_b_ff2, v_ln2_g, v_ln2_b):
    w = dict(w_in=w_in, b_in=b_in, w_pool=w_pool, pool_scale=pool_scale, sgu_ln_g=sgu_ln_g, sgu_ln_b=sgu_ln_b, sgu_w=sgu_w,
             sgu_b=sgu_b, conv_w=conv_w, conv_b=conv_b, conv_ln_g=conv_ln_g, conv_ln_b=conv_ln_b, w_out=w_out, b_out=b_out,
             ln1_g=ln1_g, ln1_b=ln1_b, w_ff1=w_ff1, b_ff1=b_ff1, w_ff2=w_ff2, b_ff2=b_ff2, ln2_g=ln2_g, ln2_b=ln2_b)
    m = dict(w_in=m_w_in, b_in=m_b_in, w_pool=m_w_pool, pool_scale=m_pool_scale, sgu_ln_g=m_sgu_ln_g, sgu_ln_b=m_sgu_ln_b,
             sgu_w=m_sgu_w, sgu_b=m_sgu_b, conv_w=m_conv_w, conv_b=m_conv_b, conv_ln_g=m_conv_ln_g, conv_ln_b=m_conv_ln_b,
             w_out=m_w_out, b_out=m_b_out, ln1_g=m_ln1_g, ln1_b=m_ln1_b, w_ff1=m_w_ff1, b_ff1=m_b_ff1, w_ff2=m_w_ff2,
             b_ff2=m_b_ff2, ln2_g=m_ln2_g, ln2_b=m_ln2_b)
    v = dict(w_in=v_w_in, b_in=v_b_in, w_pool=v_w_pool, pool_scale=v_pool_scale, sgu_ln_g=v_sgu_ln_g, sgu_ln_b=v_sgu_ln_b,
             sgu_w=v_sgu_w, sgu_b=v_sgu_b, conv_w=v_conv_w, conv_b=v_conv_b, conv_ln_g=v_conv_ln_g, conv_ln_b=v_conv_ln_b,
             w_out=v_w_out, b_out=v_b_out, ln1_g=v_ln1_g, ln1_b=v_ln1_b, w_ff1=v_w_ff1, b_ff1=v_b_ff1, w_ff2=v_w_ff2,
             b_ff2=v_b_ff2, ln2_g=v_ln2_g, ln2_b=v_ln2_b)
    assert x.shape[0] == 1 and x.shape[2] == POOL_WIDTH + SGU_WIDTH + CONV_WIDTH, x.shape
    xs, target = x[0], loss_target[0]
    s_len, d = xs.shape
    n_layers = w_in.shape[0]
    dff = 4 * w_ff1.shape[2]
    conv_shard = conv_w.shape[2]

    cx, cy, cc = _place()
    me = 2 * cx + cy
    place = jnp.stack([cc, me]).astype(jnp.int32)

    slot = {(n, l): _into_slot(f"cast_{n}{l}", w[n], l, place, BF16) for n in SHARDED for l in range(n_layers)}
    conv_padded = jnp.pad(conv_w, ((0, 0), (0, CONV_ROWS - CONV_KERNEL), (0, 0))).reshape(1, n_layers * CONV_ROWS, conv_shard)
    conv_slot = _into_slot("slot_conv_w", conv_padded, 0, place, F32)
    slot["conv_w"] = conv_slot
    gathered = {}
    mix0 = [("w_in", 0), ("w_out", 0), "conv_w"]
    ff0 = [("w_ff1", 0), ("w_ff2", 0)]
    gathered.update(zip(mix0, _gather_group("gather_mix0", [slot[k] for k in mix0])))
    conv_full = jnp.transpose(gathered["conv_w"].reshape(4, n_layers, CONV_ROWS, conv_shard), (1, 2, 0, 3))
    conv_full = conv_full.reshape(n_layers, CONV_ROWS, CONV_WIDTH)
    conv_rev = jnp.pad(conv_full[:, CONV_KERNEL - 1::-1], ((0, 0), (0, CONV_ROWS - CONV_KERNEL), (0, 0)))

    def layer_weights(l):
        return dict(w_pool=w_pool[l], pool_scale=pool_scale[l], sgu_ln_g=sgu_ln_g[l], sgu_ln_b=sgu_ln_b[l], sgu_w=sgu_w[l],
                    sgu_w_t=jnp.transpose(sgu_w[l], (0, 2, 1)),
                    sgu_bias_tile=jnp.broadcast_to(sgu_b[l][:, :, None], (SGU_HEADS, CHUNK, GROUP)),
                    conv_w_full=conv_full[l], conv_w_rev=conv_rev[l], conv_b=conv_b[l], conv_ln_g=conv_ln_g[l],
                    conv_ln_b=conv_ln_b[l])

    saved = []
    x_cur, xb_cur = xs, _rowwise("cast_x", lambda t: (t,), [xs], [BF16])[0]
    for l in range(n_layers):
        lw = layer_weights(l)
        in_flight = _gather_start("gather_ff0", [slot[k] for k in ff0]) if l == 0 else None
        w_out_full = gathered[("w_out", l)].reshape(d, d)
        p = _proj(f"proj{l}", xb_cur, gathered[("w_in", l)], b_in[l], after=in_flight and in_flight["token"])
        mixed, cv = _mix_forward(f"mix_fwd{l}", p, lw)
        r1, x1b = _mix_out(f"mix_out{l}", mixed, w_out_full, b_out[l], x_cur, ln1_g[l], ln1_b[l])
        if l == 0:
            gathered.update(zip(ff0, _gather_finish("gather_ff0", in_flight, x1b)))
        nxt = [(n, l + 1) for n in SHARDED] if l + 1 < n_layers else None
        in_flight = _gather_start(f"gather_layer{l + 1}", [slot[k] for k in nxt]) if nxt else None
        w2_full = gathered[("w_ff2", l)].reshape(dff, d)
        hf, zr = _ff1(f"ff1_{l}", x1b, gathered[("w_ff1", l)], b_ff1[l], after=in_flight and in_flight["token"])
        fo = _ff2(f"ff2_{l}", hf, w2_full, b_ff2[l])
        r2, x2, x2b = _resid_ln2(f"ln2_{l}", r1, ln1_g[l], ln1_b[l], fo, ln2_g[l], ln2_b[l])
        if nxt:
            gathered.update(zip(nxt, _gather_finish(f"gather_layer{l + 1}", in_flight, x2b)))
        saved.append(dict(lw=lw, xb_in=xb_cur, p=p, mixed=mixed, cv=cv, r1=r1, x1b=x1b, hf=hf, zr=zr, r2=r2,
                          w_out_full=w_out_full, w2_full=w2_full))
        x_cur, xb_cur = x2, x2b

    grads = {n: [None] * n_layers for n in REPLICATED + ("conv_w",)}
    g_final, delta, new_m, new_v = {}, {}, {}, {}
    results = {n: None for n in SHARDED}
    dx_mm, dx_resid = None, None
    loss_tile = None
    pending = None

    def finish_reduce(begun, after):
        names, layer, state = begun
        reduced = _reduce_end(state, after)
        for n, (mine, theirs) in zip(names, reduced):
            results[n] = _adamw_sharded(f"adamw_{n}{layer}", layer, place, w[n], m[n], v[n], mine, theirs, results[n])
        return reduced

    for l in reversed(range(n_layers)):
        sv = saved[l]
        if dx_mm is None:
            dr2, dr2b, dg2, db2, dbff2, loss_tile = _ln_backward(f"ln2_bwd{l}", sv["r2"], ln2_g[l], x=x_cur, target=target)
        else:
            dr2, dr2b, dg2, db2, dbff2 = _ln_backward(f"ln2_bwd{l}", sv["r2"], ln2_g[l], dy=dx_mm, resid=dx_resid)
        dzb, dbff1 = _dff_hidden(f"ff2_bwd{l}", dr2b, sv["w2_full"], sv["zr"])
        dw = {"w_ff2": _matmul_tn(f"dw_ff2_{l}", sv["hf"], dr2b, 4, dff // 4, d, True),
              "w_ff1": _matmul_tn(f"dw_ff1_{l}", sv["x1b"], dzb, 4, d, dff // 4, False)}
        if pending is not None:
            finish_reduce(pending, dw["w_ff1"])
            pending = None
        ff_red = _reduce_begin(f"grads_ff{l}", place, [(dw[n], True, BF16) for n in ("w_ff1", "w_ff2")])
        dx1 = _dx_sharded(f"ff1_bwd{l}", dzb, gathered[("w_ff1", l)], after=ff_red["token"])
        ff_red = _reduce_between_chips(ff_red, dx1)
        dr1, dr1b, dg1, db1, dbout = _ln_backward(f"ln1_bwd{l}", sv["r1"], ln1_g[l], dy=dx1, resid=dr2, after=ff_red["token"])
        dmixed = _dmixed(f"mix_out_bwd{l}", dr1b, sv["w_out_full"])
        dw["w_out"] = _matmul_tn(f"dw_out{l}", sv["mixed"], dr1b, 4, d // 4, d, True)
        (dp, dbin, dwp, dps, dslg, dslb, dws, dsb_tile, dcw, dcb, dclg, dclb) = _mix_backward(
            f"mix_bwd{l}", sv["p"], sv["cv"], dmixed, sv["lw"])
        dw["w_in"] = _matmul_tn(f"dw_in{l}", sv["xb_in"], dp, 4, d, IN_WIDTH // 4, False)
        for name, g in (("b_in", dbin), ("w_pool", dwp), ("pool_scale", dps), ("sgu_ln_g", dslg), ("sgu_ln_b", dslb),
                        ("sgu_w", dws), ("sgu_b", dsb_tile[:, :, 0]), ("conv_w", dcw[:CONV_KERNEL]), ("conv_b", dcb),
                        ("conv_ln_g", dclg), ("conv_ln_b", dclb), ("b_out", dbout), ("ln1_g", dg1), ("ln1_b", db1),
                        ("b_ff1", dbff1), ("b_ff2", dbff2), ("ln2_g", dg2), ("ln2_b", db2)):
            grads[name][l] = g.reshape(w[name].shape[1:]) if name != "conv_w" else g

        items = [(dw[n], True, BF16) for n in ("w_in", "w_out")]
        if l == 0:
            small_like = [jnp.stack(grads[n]) for n in REPLICATED + ("conv_w",)]
            items.append((_pack(small_like)[None], False, F32))
        mix_red = _reduce_begin(f"grads_mix{l}", place, items)
        finish_reduce((("w_ff1", "w_ff2"), l, ff_red), mix_red["token"])
        mix_red = _reduce_between_chips(mix_red, results["w_ff1"][0])
        if l > 0:
            dx_mm, dx_resid = _dx_sharded(f"proj_bwd{l}", dp, gathered[("w_in", l)], after=mix_red["token"]), dr1
            pending = (("w_in", "w_out"), l, mix_red)
        else:
            grad_x = _dx_sharded(f"proj_bwd{l}", dp, gathered[("w_in", l)], dr1, after=mix_red["token"])
            mine, theirs = finish_reduce((("w_in", "w_out"), l, mix_red), grad_x)[-1]
            grad_x = grad_x[None]
            reduced_small = jnp.where(cc == 0, jnp.concatenate([mine, theirs]), jnp.concatenate([theirs, mine]))

    loss = lax.psum(loss_tile[0, 0], ("x", "y", "c"))
    for n in SHARDED:
        g_final[n], delta[n], new_m[n], new_v[n] = results[n]
    unpacked = _unpack(reduced_small, small_like)
    g_final.update(zip(REPLICATED, unpacked[:-1]))
    g_final["conv_w"] = lax.dynamic_slice_in_dim(unpacked[-1], me * conv_shard, conv_shard, axis=2)

    delta["conv_w"], new_m["conv_w"], new_v["conv_w"] = _adamw("adamw_conv_w", conv_w, g_final["conv_w"], m["conv_w"], v["conv_w"])
    packed = [_pack([t[n] for n in REPLICATED]) for t in (w, g_final, m, v)]
    like = [w[n] for n in REPLICATED]
    for res, packed_out in zip((delta, new_m, new_v), _adamw("adamw_replicated", *packed)):
        res.update(zip(REPLICATED, _unpack(packed_out, like)))

    return (loss, grad_x, *[g_final[n] for n in WEIGHTS], *[delta[n] for n in WEIGHTS],
            *[new_m[n] for n in WEIGHTS], *[new_v[n] for n in WEIGHTS])
```

```python
import functools
import math

import jax
import jax.numpy as jnp
from jax import lax
from jax.experimental import pallas as pl
from jax.experimental.pallas import tpu as pltpu

F32 = jnp.float32
BF16 = jnp.bfloat16
MESH = pl.DeviceIdType.MESH

DEPTH = 2
POOL_WINDOWS = (2, 4, 8, 16)
POOL_WIDTH = 512
GROUP = 128
SGU_WIDTH = 768
SGU_HEADS = 6
CHUNK = 128
CONV_WIDTH = 768
CONV_KERNEL = 31
CONV_ROWS = 32
HALO = 32
COL_B = POOL_WIDTH
COL_C = POOL_WIDTH + 2 * SGU_WIDTH
IN_WIDTH = COL_C + 2 * CONV_WIDTH
ALPHA = (2 * DEPTH) ** 0.25
LN_EPS = 1e-5
ADAM_LR = 0.001
ADAM_B1 = 0.9
ADAM_B2 = 0.999
ADAM_EPS = 1e-08
ADAM_WD = 0.01
ADAM_STEP = 10
GELU_C = math.sqrt(2.0 / math.pi)
GELU_A = 0.044715

V7X_VMEM_BYTES = 64 * 2 ** 20
VMEM_LIMIT = 56 * 2 ** 20

BM_MM = 1024
BN_MM = 1024
BM_LN = 512
BK_MM = 1024
BK_WIDE = 2048
BS_TN = 2048
TN_OUT_BYTES = 8 * 2 ** 20
TS_MIX = 256
RB_CONV = 64
BR_EW = 512
EW_BLOCK_ELEMS = 2 ** 18


def _tile(n, pref):
    t = min(n, pref)
    assert n % t == 0, (n, pref)
    return t


def _params(n_grid):
    return pltpu.CompilerParams(dimension_semantics=("arbitrary",) * n_grid, vmem_limit_bytes=VMEM_LIMIT)


def _sigmoid(x):
    return 1.0 / (1.0 + jnp.exp(-x))


def _gelu(x):
    return 0.5 * x * (1.0 + jnp.tanh(GELU_C * (x + GELU_A * x * x * x)))


def _gelu_with_grad(x):
    t = jnp.tanh(GELU_C * (x + GELU_A * x * x * x))
    half = 0.5 * (1.0 + t)
    return x * half, half + 0.5 * x * (1.0 - t * t) * GELU_C * (1.0 + 3.0 * GELU_A * x * x)


def _ln_stats(r):
    mu = jnp.mean(r, axis=-1, keepdims=True)
    xc = r - mu
    var = jnp.mean(xc * xc, axis=-1, keepdims=True)
    rstd = lax.rsqrt(var + LN_EPS)
    return xc * rstd, rstd


def _ln_bwd(dy, xhat, rstd, g):
    dxh = dy * g
    m1 = jnp.mean(dxh, axis=-1, keepdims=True)
    m2 = jnp.mean(dxh * xhat, axis=-1, keepdims=True)
    return rstd * (dxh - m1 - xhat * m2)


def _colsum(x):
    return jnp.sum(x, axis=0, keepdims=True)


def _accumulate(ref, val, first):
    @pl.when(first)
    def _():
        ref[...] = val

    @pl.when(jnp.logical_not(first))
    def _():
        ref[...] += val


def _matmul(name, grid, a, a_spec, b, b_spec, *, nt, extras, outs, epilogue, acc_shape=None, after=None):
    nk = grid[2]
    if after is not None:
        extras = list(extras) + [(after, pl.BlockSpec(memory_space=pl.ANY))]
    ne, no = len(extras), len(outs)
    dims = (((1,), (1,)), ((), ())) if nt else (((1,), (0,)), ((), ()))

    def body(*refs):
        a_ref, b_ref = refs[0], refs[1]
        ex = refs[2:2 + ne]
        out_refs = refs[2 + ne:2 + ne + no]
        ids = (pl.program_id(0), pl.program_id(1), pl.program_id(2))
        part = lax.dot_general(a_ref[...], b_ref[...], dims, preferred_element_type=F32)
        if nk == 1:
            epilogue(part, ex, out_refs, ids)
        elif acc_shape is None:
            k = ids[2]

            @pl.when(k == 0)
            def _():
                epilogue(part, ex, out_refs, ids)

            @pl.when(k > 0)
            def _():
                out_refs[0][...] += part
        else:
            acc_ref = refs[2 + ne + no]
            k = ids[2]

            @pl.when(k == 0)
            def _():
                acc_ref[...] = part

            @pl.when(k > 0)
            def _():
                acc_ref[...] += part

            @pl.when(k == nk - 1)
            def _():
                epilogue(acc_ref[...], ex, out_refs, ids)

    return pl.pallas_call(
        body,
        name=name,
        grid=grid,
        in_specs=[a_spec, b_spec] + [s for _, s in extras],
        out_specs=[s for _, s in outs],
        out_shape=[o for o, _ in outs],
        scratch_shapes=[pltpu.VMEM(acc_shape, F32)] if nk > 1 and acc_shape is not None else [],
        compiler_params=_params(3),
    )(a, b, *[e for e, _ in extras])


def _matmul_tn(name, a, b, n_shards, shard_rows, shard_cols, row_sharded):
    s_len, ka = a.shape
    n = b.shape[1]
    assert (n_shards * shard_rows, shard_cols) == (ka, n) if row_sharded else (shard_rows, n_shards * shard_cols) == (ka, n)
    bs = _tile(s_len, BS_TN)
    bka = _tile(shard_rows, 2048)
    bn = next((t for t in (2 * BN_MM, BN_MM) if shard_cols % t == 0 and bka * t * 4 <= TN_OUT_BYTES), shard_cols)
    ni, nj, ns = ka // bka, n // bn, s_len // bs
    per_shard_i = shard_rows // bka
    per_shard_j = shard_cols // bn

    if row_sharded:
        out_map = lambda i, j, s: (i // per_shard_i, i % per_shard_i, j)
    else:
        out_map = lambda i, j, s: (j // per_shard_j, i, j % per_shard_j)

    def body(a_ref, b_ref, o_ref):
        s = pl.program_id(2)
        part = lax.dot_general(a_ref[...], b_ref[...], (((0,), (0,)), ((), ())), preferred_element_type=F32)
        _accumulate(o_ref, part, s == 0)

    return pl.pallas_call(
        body,
        name=name,
        grid=(ni, nj, ns),
        in_specs=[pl.BlockSpec((bs, bka), lambda i, j, s: (s, i)), pl.BlockSpec((bs, bn), lambda i, j, s: (s, j))],
        out_specs=pl.BlockSpec((None, bka, bn), out_map),
        out_shape=jax.ShapeDtypeStruct((n_shards, shard_rows, shard_cols), F32),
        compiler_params=_params(3),
    )(a, b)


def _row(v):
    return v.reshape(1, -1)


def _proj(name, xb, w_g, b_in, after=None):
    s_len, d = xb.shape
    ncs = w_g.shape[2]
    bm = _tile(s_len, BM_MM)

    def epilogue(acc, ex, outs, ids):
        outs[0][...] = acc + ex[0][...]

    return _matmul(
        name, (s_len // bm, 4, 1),
        xb, pl.BlockSpec((bm, d), lambda i, j, k: (i, 0)),
        w_g, pl.BlockSpec((None, d, ncs), lambda i, j, k: (j, 0, 0)),
        nt=False,
        extras=[(_row(b_in), pl.BlockSpec((1, ncs), lambda i, j, k: (0, j)))],
        outs=[(jax.ShapeDtypeStruct((s_len, 4 * ncs), F32), pl.BlockSpec((bm, ncs), lambda i, j, k: (i, j)))],
        epilogue=epilogue, after=after,
    )[0]


def _mix_out(name, mixed, w_out_full, b_out, x0, g1, b1):
    s_len, d = mixed.shape
    bm = _tile(s_len, BM_LN // 2)
    row = pl.BlockSpec((1, d), lambda i, j, k: (0, 0))
    blk = pl.BlockSpec((bm, d), lambda i, j, k: (i, 0))

    def epilogue(acc, ex, outs, ids):
        r1 = ALPHA * ex[1][...] + (acc + ex[0][...])
        outs[0][...] = r1
        xhat, _ = _ln_stats(r1)
        outs[1][...] = (xhat * ex[2][...] + ex[3][...]).astype(BF16)

    return _matmul(
        name, (s_len // bm, 1, 1),
        mixed, blk,
        w_out_full, pl.BlockSpec((d, d), lambda i, j, k: (0, 0)),
        nt=False,
        extras=[(_row(b_out), row), (x0, blk), (_row(g1), row), (_row(b1), row)],
        outs=[(jax.ShapeDtypeStruct((s_len, d), F32), blk), (jax.ShapeDtypeStruct((s_len, d), BF16), blk)],
        epilogue=epilogue,
    )


def _ff1(name, x1b, w_g, b_ff1, after=None):
    s_len, d = x1b.shape
    ncs = w_g.shape[2]
    bm = _tile(s_len, BM_MM)
    bn = _tile(ncs, BN_MM)
    per = ncs // bn
    blk = pl.BlockSpec((bm, bn), lambda i, j, k: (i, j))

    def epilogue(acc, ex, outs, ids):
        zr = jnp.maximum(acc + ex[0][...], 0.0)
        outs[0][...] = (zr * zr).astype(BF16)
        outs[1][...] = zr.astype(BF16)

    shape = jax.ShapeDtypeStruct((s_len, 4 * ncs), BF16)
    return _matmul(
        name, (s_len // bm, 4 * per, 1),
        x1b, pl.BlockSpec((bm, d), lambda i, j, k: (i, 0)),
        w_g, pl.BlockSpec((None, d, bn), lambda i, j, k: (j // per, 0, j % per)),
        nt=False,
        extras=[(_row(b_ff1), pl.BlockSpec((1, bn), lambda i, j, k: (0, j)))],
        outs=[(shape, blk), (shape, blk)],
        epilogue=epilogue, after=after,
    )


def _ff2(name, hf, w2_full, b_ff2):
    s_len, dff = hf.shape
    d = w2_full.shape[1]
    bm = _tile(s_len, BM_MM)
    bk = _tile(dff, BK_WIDE)
    blk = pl.BlockSpec((bm, d), lambda i, j, k: (i, 0))

    def epilogue(acc, ex, outs, ids):
        outs[0][...] = acc + ex[0][...]

    return _matmul(
        name, (s_len // bm, 1, dff // bk),
        hf, pl.BlockSpec((bm, bk), lambda i, j, k: (i, k)),
        w2_full, pl.BlockSpec((bk, d), lambda i, j, k: (k, 0)),
        nt=False,
        extras=[(_row(b_ff2), pl.BlockSpec((1, d), lambda i, j, k: (0, 0)))],
        outs=[(jax.ShapeDtypeStruct((s_len, d), F32), blk)],
        epilogue=epilogue,
    )[0]


def _resid_ln2(name, r1, g1, b1, fo, g2, b2, after=None):
    s_len, d = r1.shape
    bm = _tile(s_len, BM_LN // 2)
    blk = pl.BlockSpec((bm, d), lambda i: (i, 0))
    row = pl.BlockSpec((1, d), lambda i: (0, 0))
    order = [] if after is None else [after]

    def body(*refs):
        r1_ref, g1_ref, b1_ref, fo_ref, g2_ref, b2_ref, r2_ref, x2_ref, x2b_ref = refs[len(order):]
        xhat1, _ = _ln_stats(r1_ref[...])
        r2 = ALPHA * (xhat1 * g1_ref[...] + b1_ref[...]) + fo_ref[...]
        r2_ref[...] = r2
        xhat2, _ = _ln_stats(r2)
        x2 = xhat2 * g2_ref[...] + b2_ref[...]
        x2_ref[...] = x2
        x2b_ref[...] = x2.astype(BF16)

    return pl.pallas_call(
        body, name=name, grid=(s_len // bm,),
        in_specs=[pl.BlockSpec(memory_space=pl.ANY)] * len(order) + [blk, row, row, blk, row, row], out_specs=[blk, blk, blk],
        out_shape=[jax.ShapeDtypeStruct((s_len, d), F32), jax.ShapeDtypeStruct((s_len, d), F32),
                   jax.ShapeDtypeStruct((s_len, d), BF16)],
        compiler_params=_params(1))(*order, r1, _row(g1), _row(b1), fo, _row(g2), _row(b2))


def _dff_hidden(name, dr2b, w2_full, zr):
    s_len, d = dr2b.shape
    dff = w2_full.shape[0]
    bm = _tile(s_len, BM_MM)
    bn = _tile(dff, BN_MM)

    def epilogue(acc, ex, outs, ids):
        dz = acc * (2.0 * ex[0][...].astype(F32))
        outs[0][...] = dz.astype(BF16)
        _accumulate(outs[1], _colsum(dz), ids[1] == 0)

    return _matmul(
        name, (dff // bn, s_len // bm, 1),
        dr2b, pl.BlockSpec((bm, d), lambda j, i, k: (i, 0)),
        w2_full, pl.BlockSpec((bn, d), lambda j, i, k: (j, 0)),
        nt=True,
        extras=[(zr, pl.BlockSpec((bm, bn), lambda j, i, k: (i, j)))],
        outs=[(jax.ShapeDtypeStruct((s_len, dff), BF16), pl.BlockSpec((bm, bn), lambda j, i, k: (i, j))),
              (jax.ShapeDtypeStruct((1, dff), F32), pl.BlockSpec((1, bn), lambda j, i, k: (0, j)))],
        epilogue=epilogue,
    )


def _dx_sharded(name, dyb, w_g, resid=None, after=None):
    s_len = dyb.shape[0]
    d, ncs = w_g.shape[1], w_g.shape[2]
    bm = _tile(s_len, BM_MM if resid is None else BM_LN)
    bk = _tile(ncs, BK_WIDE) if ncs % BK_MM == 0 else ncs
    per = ncs // bk
    blk = pl.BlockSpec((bm, d), lambda i, j, k: (i, 0))

    def epilogue(acc, ex, outs, ids):
        outs[0][...] = acc if resid is None else acc + ALPHA * ex[0][...]

    return _matmul(
        name, (s_len // bm, 1, 4 * per),
        dyb, pl.BlockSpec((bm, bk), lambda i, j, k: (i, k)),
        w_g, pl.BlockSpec((None, d, bk), lambda i, j, k: (k // per, 0, k % per)),
        nt=True,
        extras=[] if resid is None else [(resid, blk)],
        outs=[(jax.ShapeDtypeStruct((s_len, d), F32), blk)],
        epilogue=epilogue, after=after,
    )[0]


def _dmixed(name, dr1b, w_out_full):
    s_len, d = dr1b.shape
    bm = _tile(s_len, BM_LN)
    blk = pl.BlockSpec((bm, d), lambda i, j, k: (i, 0))

    def epilogue(acc, ex, outs, ids):
        outs[0][...] = acc

    return _matmul(
        name, (s_len // bm, 1, 1),
        dr1b, blk,
        w_out_full, pl.BlockSpec((d, d), lambda i, j, k: (0, 0)),
        nt=True, extras=[],
        outs=[(jax.ShapeDtypeStruct((s_len, d), F32), blk)],
        epilogue=epilogue,
    )[0]


def _ln_backward(name, r, g, *, dy=None, resid=None, last=None, after=None):
    from_loss = last is not None
    s_len, d = last[0].shape if from_loss else r.shape
    bm = _tile(s_len, BM_LN // 2)
    blk = pl.BlockSpec((bm, d), lambda i: (i, 0))
    row = pl.BlockSpec((1, d), lambda i: (0, 0))

    def body(*refs):
        i = pl.program_id(0)
        if after is not None:
            refs = refs[1:]
        if from_loss:
            r1_ref, fo_ref, t_ref, g1_ref, b1_ref, b_ref, g_ref, dr_ref, drb_ref, dg_ref, db_ref, dbias_ref, loss_ref = refs
            xhat1, _ = _ln_stats(r1_ref[...])
            rv = ALPHA * (xhat1 * g1_ref[...] + b1_ref[...]) + fo_ref[...]
            xhat, rstd = _ln_stats(rv)
            diff = (xhat * g_ref[...] + b_ref[...]) - t_ref[...]
            dyv = diff * (1.0 / d)
            part = 0.5 * jnp.sum(jnp.sum(diff * diff, axis=1, keepdims=True) * (1.0 / d), axis=0, keepdims=True)
            _accumulate(loss_ref, jnp.broadcast_to(part, loss_ref.shape), i == 0)
        else:
            if resid is not None:
                dy_ref, res_ref, r_ref, g_ref, dr_ref, drb_ref, dg_ref, db_ref, dbias_ref = refs
                dyv = dy_ref[...] + ALPHA * res_ref[...]
            else:
                dy_ref, r_ref, g_ref, dr_ref, drb_ref, dg_ref, db_ref, dbias_ref = refs
                dyv = dy_ref[...]
            xhat, rstd = _ln_stats(r_ref[...])
        dr = _ln_bwd(dyv, xhat, rstd, g_ref[...])
        dr_ref[...] = dr
        drb_ref[...] = dr.astype(BF16)
        _accumulate(dg_ref, _colsum(dyv * xhat), i == 0)
        _accumulate(db_ref, _colsum(dyv), i == 0)
        _accumulate(dbias_ref, _colsum(dr), i == 0)

    if from_loss:
        r1, g1, b1, fo, b, target = last
        ins = [r1, fo, target, _row(g1), _row(b1), _row(b), _row(g)]
        in_specs = [blk] * 3 + [row] * 4
    else:
        lead = [dy] if resid is None else [dy, resid]
        ins = lead + [r, _row(g)]
        in_specs = [blk] * len(lead) + [blk, row]
    if after is not None:
        ins, in_specs = [after] + ins, [pl.BlockSpec(memory_space=pl.ANY)] + in_specs
    out_shape = [jax.ShapeDtypeStruct((s_len, d), F32), jax.ShapeDtypeStruct((s_len, d), BF16)] + \
                [jax.ShapeDtypeStruct((1, d), F32)] * 3
    out_specs = [blk, blk, row, row, row]
    if from_loss:
        out_shape.append(jax.ShapeDtypeStruct((8, 128), F32))
        out_specs.append(pl.BlockSpec((8, 128), lambda i: (0, 0)))
    return pl.pallas_call(body, name=name, grid=(s_len // bm,), in_specs=in_specs, out_specs=out_specs,
                          out_shape=out_shape, compiler_params=_params(1))(*ins)


def _rowwise(name, fn, ins, out_dtypes, rows_pref=BR_EW):
    r, c = ins[0].shape
    br = _tile(r, rows_pref)
    blk = pl.BlockSpec((br, c), lambda i: (i, 0))

    def body(*refs):
        res = fn(*[ref[...] for ref in refs[:len(ins)]])
        for o_ref, v in zip(refs[len(ins):], res):
            o_ref[...] = v.astype(o_ref.dtype)

    return pl.pallas_call(body, name=name, grid=(r // br,), in_specs=[blk] * len(ins),
                          out_specs=[blk] * len(out_dtypes),
                          out_shape=[jax.ShapeDtypeStruct((r, c), dt) for dt in out_dtypes],
                          compiler_params=_params(1))(*ins)


def _adamw_math(w, g, m, v):
    m = ADAM_B1 * m + (1.0 - ADAM_B1) * g
    v = ADAM_B2 * v + (1.0 - ADAM_B2) * (g * g)
    m_hat = m / (1.0 - ADAM_B1 ** ADAM_STEP)
    v_hat = v / (1.0 - ADAM_B2 ** ADAM_STEP)
    delta = -ADAM_LR * (m_hat / (jnp.sqrt(v_hat) + ADAM_EPS) + ADAM_WD * w)
    return delta, m, v


def _adamw(name, w, g, m, v):
    shape = w.shape
    c = shape[-1]
    flat = [a.reshape(-1, c) for a in (w, g, m, v)]
    rows = flat[0].shape[0]
    pref = max(8, 2 ** int(math.log2(EW_BLOCK_ELEMS // c)))
    res = _rowwise(name, _adamw_math, flat, [F32, F32, F32], rows_pref=pref if rows % pref == 0 else rows)
    return tuple(a.reshape(shape) for a in res)


def _pool_means(ext_ref, ts, tile_index):
    t_glob = tile_index * ts + lax.broadcasted_iota(jnp.int32, (ts, GROUP), 0)
    qs = []
    for g, win in enumerate(POOL_WINDOWS):
        cols = pl.ds(g * GROUP, GROUP)
        cur = ext_ref[pl.ds(16, ts), cols]
        acc = cur
        for j in range(1, win):
            acc = acc + ext_ref[pl.ds(16 - j, ts), cols]
        cnt = jnp.minimum(t_glob + 1, win).astype(F32)
        qs.append(acc / cnt - cur)
    return qs


def _masked_sgu_w(w_ref, h):
    r = lax.broadcasted_iota(jnp.int32, (CHUNK, CHUNK), 0)
    c = lax.broadcasted_iota(jnp.int32, (CHUNK, CHUNK), 1)
    return jnp.where(r >= c, w_ref[h], 0.0)


SUBLANES = 8


def _fill_shifted(src_ref, shifted_ref, ts):
    rows = ts + HALO - SUBLANES
    for b in range(1, SUBLANES):
        shifted_ref[b - 1] = src_ref[pl.ds(b, rows), :]


def _shifted_rows(src_ref, shifted_ref, offset, rows, cols):
    a, b = divmod(offset, SUBLANES)
    if b == 0:
        return src_ref[pl.ds(offset, rows), cols]
    return shifted_ref[b - 1, pl.ds(a * SUBLANES, rows), cols]


def _conv_taps(acc_init, w_ref, src_ref, shifted_ref, ts, base):
    nrb = ts // RB_CONV
    blocks = []
    for cg in range(CONV_WIDTH // GROUP):
        cols = pl.ds(cg * GROUP, GROUP)
        accs = [acc_init(cg) for _ in range(nrb)]
        for k in range(CONV_KERNEL):
            wk = jnp.broadcast_to(w_ref[pl.ds(k, 1), cols], (RB_CONV, GROUP))
            for rb in range(nrb):
                accs[rb] = accs[rb] + wk * _shifted_rows(src_ref, shifted_ref, rb * RB_CONV + base(k), RB_CONV, cols)
        for rb in range(nrb):
            blocks.append((rb * RB_CONV, cg * GROUP, accs[rb]))
    return blocks


def _mix_forward(name, p, lw):
    s_len = p.shape[0]
    ts = _tile(s_len, TS_MIX)
    per_halo = ts // HALO
    d = POOL_WIDTH + SGU_WIDTH + CONV_WIDTH

    def body(p_ref, ph_ref, wp_ref, ps_ref, slg_ref, slb_ref, ws_ref, sb_ref, cw_ref, cb_ref, clg_ref, clb_ref,
             mixed_ref, cv_ref, pool_ext, hh_ext, shifted):
        i = pl.program_id(0)
        keep = (i > 0).astype(F32)
        pool_ext[pl.ds(0, 16), :] = ph_ref[pl.ds(16, 16), pl.ds(0, POOL_WIDTH)] * keep
        pool_ext[pl.ds(16, ts), :] = p_ref[:, pl.ds(0, POOL_WIDTH)]
        qs = _pool_means(pool_ext, ts, i)
        for g in range(len(POOL_WINDOWS)):
            cols = pl.ds(g * GROUP, GROUP)
            e = jnp.dot(qs[g].astype(BF16), wp_ref[g].astype(BF16), preferred_element_type=F32)
            mixed_ref[:, cols] = (e * ps_ref[:, cols]).astype(BF16)
        uv = _gelu(p_ref[:, pl.ds(COL_B, 2 * SGU_WIDTH)])
        u = uv[:, :SGU_WIDTH]
        vhat, _ = _ln_stats(uv[:, SGU_WIDTH:])
        vn = (vhat * slg_ref[...] + slb_ref[...]).astype(BF16)
        for h in range(SGU_HEADS):
            wm = _masked_sgu_w(ws_ref, h).astype(BF16)
            for n in range(ts // CHUNK):
                rows = slice(n * CHUNK, (n + 1) * CHUNK)
                cols = slice(h * GROUP, (h + 1) * GROUP)
                mx = jnp.dot(wm, vn[rows, cols], preferred_element_type=F32) + sb_ref[h]
                mixed_ref[pl.ds(n * CHUNK, CHUNK), pl.ds(POOL_WIDTH + h * GROUP, GROUP)] = (u[rows, cols] * mx).astype(BF16)
        hh_ext[pl.ds(0, HALO), :] = (ph_ref[:, pl.ds(COL_C, CONV_WIDTH)]
                                     * _sigmoid(ph_ref[:, pl.ds(COL_C + CONV_WIDTH, CONV_WIDTH)])) * keep
        hh_ext[pl.ds(HALO, ts), :] = p_ref[:, pl.ds(COL_C, CONV_WIDTH)] * _sigmoid(p_ref[:, pl.ds(COL_C + CONV_WIDTH, CONV_WIDTH)])
        init = lambda cg: jnp.broadcast_to(cb_ref[:, pl.ds(cg * GROUP, GROUP)], (RB_CONV, GROUP))
        _fill_shifted(hh_ext, shifted, ts)
        for r0, c0, blk in _conv_taps(init, cw_ref, hh_ext, shifted, ts, lambda k: k + HALO - (CONV_KERNEL - 1)):
            cv_ref[pl.ds(r0, RB_CONV), pl.ds(c0, GROUP)] = blk
        cvhat, _ = _ln_stats(cv_ref[...])
        cn = cvhat * clg_ref[...] + clb_ref[...]
        mixed_ref[:, pl.ds(POOL_WIDTH + SGU_WIDTH, CONV_WIDTH)] = (cn * _sigmoid(cn)).astype(BF16)

    full = lambda a: pl.BlockSpec(a.shape, lambda i: (0,) * a.ndim)
    weights = [lw["w_pool"], _row(lw["pool_scale"]), _row(lw["sgu_ln_g"]), _row(lw["sgu_ln_b"]), lw["sgu_w"],
               lw["sgu_bias_tile"], lw["conv_w_full"], _row(lw["conv_b"]), _row(lw["conv_ln_g"]), _row(lw["conv_ln_b"])]
    return pl.pallas_call(
        body, name=name, grid=(s_len // ts,),
        in_specs=[pl.BlockSpec((ts, IN_WIDTH), lambda i: (i, 0)),
                  pl.BlockSpec((HALO, IN_WIDTH), lambda i: (jnp.maximum(i * per_halo - 1, 0), 0))] + [full(a) for a in weights],
        out_specs=[pl.BlockSpec((ts, d), lambda i: (i, 0)), pl.BlockSpec((ts, CONV_WIDTH), lambda i: (i, 0))],
        out_shape=[jax.ShapeDtypeStruct((s_len, d), BF16), jax.ShapeDtypeStruct((s_len, CONV_WIDTH), F32)],
        scratch_shapes=[pltpu.VMEM((16 + ts, POOL_WIDTH), F32), pltpu.VMEM((HALO + ts, CONV_WIDTH), F32),
                        pltpu.VMEM((SUBLANES - 1, ts + HALO - SUBLANES, CONV_WIDTH), F32)],
        compiler_params=_params(1),
    )(p, p, *weights)


def _mix_backward(name, p, cv, dmixed, lw):
    s_len = p.shape[0]
    ts = _tile(s_len, TS_MIX)
    nt = s_len // ts
    per_halo = ts // HALO
    d = POOL_WIDTH + SGU_WIDTH + CONV_WIDTH
    nch = ts // CHUNK
    col_yc = POOL_WIDTH + SGU_WIDTH

    def body(p_ref, ph_ref, cv_ref, cvn_ref, dm_ref, dmn_ref, wp_ref, ps_ref, slg_ref, slb_ref, ws_ref, wst_ref, sb_ref,
             cw_ref, cwr_ref, clg_ref, clb_ref,
             dp_ref, dbin_ref, dwp_ref, dps_ref, dslg_ref, dslb_ref, dws_ref, dsb_ref, dcw_ref, dcb_ref, dclg_ref, dclb_ref,
             pool_ext, dq_ext, hh_ext, dcv_ext, dcw_acc, shifted):
        i = pl.program_id(0)
        first = i == 0
        keep_prev = (i > 0).astype(F32)
        keep_next = (i < nt - 1).astype(F32)

        pool_ext[pl.ds(0, 16), :] = ph_ref[pl.ds(16, 16), pl.ds(0, POOL_WIDTH)] * keep_prev
        pool_ext[pl.ds(16, ts), :] = p_ref[:, pl.ds(0, POOL_WIDTH)]
        qs = _pool_means(pool_ext, ts, i)
        t_ext = i * ts + lax.broadcasted_iota(jnp.int32, (ts + HALO, GROUP), 0)
        for g, win in enumerate(POOL_WINDOWS):
            cols = pl.ds(g * GROUP, GROUP)
            wpb = wp_ref[g].astype(BF16)
            qb = qs[g].astype(BF16)
            de = dm_ref[:, cols] * ps_ref[:, cols]
            e = jnp.dot(qb, wpb, preferred_element_type=F32)
            _accumulate(dps_ref.at[:, cols], _colsum(dm_ref[:, cols] * e), first)
            deb = de.astype(BF16)
            _accumulate(dwp_ref.at[g], lax.dot_general(qb, deb, (((0,), (0,)), ((), ())), preferred_element_type=F32), first)
            de_next = (dmn_ref[:, cols] * ps_ref[:, cols] * keep_next).astype(BF16)
            de_all = jnp.concatenate([deb, de_next], axis=0)
            dq = lax.dot_general(de_all, wpb, (((1,), (1,)), ((), ())), preferred_element_type=F32)
            inv = 1.0 / jnp.minimum(t_ext + 1, win).astype(F32)
            dq_ext[:, cols] = dq * inv
            acc = dq_ext[pl.ds(0, ts), cols]
            for j in range(1, win):
                acc = acc + dq_ext[pl.ds(j, ts), cols]
            dpa = acc - dq[:ts]
            dp_ref[:, cols] = dpa.astype(BF16)
            _accumulate(dbin_ref.at[:, cols], _colsum(dpa), first)

        pb = p_ref[:, pl.ds(COL_B, 2 * SGU_WIDTH)]
        uv, duv = _gelu_with_grad(pb)
        u = uv[:, :SGU_WIDTH]
        vhat, vrstd = _ln_stats(uv[:, SGU_WIDTH:])
        vn = (vhat * slg_ref[...] + slb_ref[...]).astype(BF16)
        dyb = dm_ref[:, pl.ds(POOL_WIDTH, SGU_WIDTH)]
        dmix = dyb * u
        dmixb = dmix.astype(BF16)
        du_parts, dvn_parts = [], []
        for h in range(SGU_HEADS):
            cols = slice(h * GROUP, (h + 1) * GROUP)
            wm = _masked_sgu_w(ws_ref, h).astype(BF16)
            wmt = _masked_sgu_w_t(wst_ref, h).astype(BF16)
            dws_h = jnp.zeros((CHUNK, CHUNK), F32)
            dsb_h = jnp.zeros((CHUNK, GROUP), F32)
            du_rows, dvn_rows = [], []
            for n in range(nch):
                rows = slice(n * CHUNK, (n + 1) * CHUNK)
                mx = jnp.dot(wm, vn[rows, cols], preferred_element_type=F32) + sb_ref[h]
                du_rows.append(dyb[rows, cols] * mx)
                dws_h = dws_h + lax.dot_general(dmixb[rows, cols], vn[rows, cols], (((1,), (1,)), ((), ())),
                                                preferred_element_type=F32)
                dsb_h = dsb_h + dmix[rows, cols]
                dvn_rows.append(jnp.dot(wmt, dmixb[rows, cols], preferred_element_type=F32))
            r = lax.broadcasted_iota(jnp.int32, (CHUNK, CHUNK), 0)
            c = lax.broadcasted_iota(jnp.int32, (CHUNK, CHUNK), 1)
            _accumulate(dws_ref.at[h], jnp.where(r >= c, dws_h, 0.0), first)
            _accumulate(dsb_ref.at[h], jnp.broadcast_to(jnp.sum(dsb_h, axis=1, keepdims=True), (CHUNK, GROUP)), first)
            du_parts.append(jnp.concatenate(du_rows, axis=0))
            dvn_parts.append(jnp.concatenate(dvn_rows, axis=0))
        du = jnp.concatenate(du_parts, axis=1)
        dvn = jnp.concatenate(dvn_parts, axis=1)
        _accumulate(dslg_ref, _colsum(dvn * vhat), first)
        _accumulate(dslb_ref, _colsum(dvn), first)
        dv = _ln_bwd(dvn, vhat, vrstd, slg_ref[...])
        dpb = jnp.concatenate([du, dv], axis=1) * duv
        dp_ref[:, pl.ds(COL_B, 2 * SGU_WIDTH)] = dpb.astype(BF16)
        _accumulate(dbin_ref.at[:, pl.ds(COL_B, 2 * SGU_WIDTH)], _colsum(dpb), first)

        a_main = p_ref[:, pl.ds(COL_C, CONV_WIDTH)]
        sg_main = _sigmoid(p_ref[:, pl.ds(COL_C + CONV_WIDTH, CONV_WIDTH)])
        hh_ext[pl.ds(0, HALO), :] = (ph_ref[:, pl.ds(COL_C, CONV_WIDTH)]
                                     * _sigmoid(ph_ref[:, pl.ds(COL_C + CONV_WIDTH, CONV_WIDTH)])) * keep_prev
        hh_ext[pl.ds(HALO, ts), :] = a_main * sg_main

        def conv_ln_backward(cv_v, dyc_v):
            cvhat, crstd = _ln_stats(cv_v)
            cn = cvhat * clg_ref[...] + clb_ref[...]
            s = _sigmoid(cn)
            dcn = dyc_v * (s * (1.0 + cn * (1.0 - s)))
            return _ln_bwd(dcn, cvhat, crstd, clg_ref[...]), dcn, cvhat

        dcv, dcn, cvhat = conv_ln_backward(cv_ref[...], dm_ref[:, pl.ds(col_yc, CONV_WIDTH)])
        _accumulate(dclg_ref, _colsum(dcn * cvhat), first)
        _accumulate(dclb_ref, _colsum(dcn), first)
        _accumulate(dcb_ref, _colsum(dcv), first)
        dcv_next, _, _ = conv_ln_backward(cvn_ref[...], dmn_ref[:, pl.ds(col_yc, CONV_WIDTH)])
        dcv_ext[pl.ds(0, ts), :] = dcv
        dcv_ext[pl.ds(ts, HALO), :] = dcv_next * keep_next

        @pl.when(first)
        def _():
            dcw_acc[...] = jnp.zeros_like(dcw_acc)

        nrb = ts // RB_CONV
        _fill_shifted(hh_ext, shifted, ts)
        for cg in range(CONV_WIDTH // GROUP):
            cols = pl.ds(cg * GROUP, GROUP)
            dblk = [dcv_ext[pl.ds(rb * RB_CONV, RB_CONV), cols] for rb in range(nrb)]
            for k in range(CONV_KERNEL):
                part = jnp.zeros((8, GROUP), F32)
                for rb in range(nrb):
                    offset = rb * RB_CONV + k + HALO - (CONV_KERNEL - 1)
                    prod = dblk[rb] * _shifted_rows(hh_ext, shifted, offset, RB_CONV, cols)
                    part = part + jnp.sum(prod.reshape(RB_CONV // 8, 8, GROUP), axis=0)
                dcw_acc[k, :, cols] += part

        @pl.when(i == nt - 1)
        def _():
            dcw_ref[...] = jnp.sum(dcw_acc[...], axis=1)

        zero = lambda cg: jnp.zeros((RB_CONV, GROUP), F32)
        _fill_shifted(dcv_ext, shifted, ts)
        for r0, c0, blk in _conv_taps(zero, cwr_ref, dcv_ext, shifted, ts, lambda k: k):
            rows, cols = pl.ds(r0, RB_CONV), pl.ds(c0, GROUP)
            a_blk = p_ref[rows, pl.ds(COL_C + c0, GROUP)]
            s_blk = _sigmoid(p_ref[rows, pl.ds(COL_C + CONV_WIDTH + c0, GROUP)])
            da = blk * s_blk
            dg = blk * a_blk * s_blk * (1.0 - s_blk)
            dp_ref[rows, pl.ds(COL_C + c0, GROUP)] = da.astype(BF16)
            dp_ref[rows, pl.ds(COL_C + CONV_WIDTH + c0, GROUP)] = dg.astype(BF16)
            hh_ext[pl.ds(HALO + r0, RB_CONV), cols] = da
            dcv_ext[rows, cols] = dg
        _accumulate(dbin_ref.at[:, pl.ds(COL_C, CONV_WIDTH)], _colsum(hh_ext[pl.ds(HALO, ts), :]), first)
        _accumulate(dbin_ref.at[:, pl.ds(COL_C + CONV_WIDTH, CONV_WIDTH)], _colsum(dcv_ext[pl.ds(0, ts), :]), first)

    full = lambda a: pl.BlockSpec(a.shape, lambda i: (0,) * a.ndim)
    weights = [lw["w_pool"], _row(lw["pool_scale"]), _row(lw["sgu_ln_g"]), _row(lw["sgu_ln_b"]), lw["sgu_w"], lw["sgu_w_t"],
               lw["sgu_bias_tile"], lw["conv_w_full"], lw["conv_w_rev"], _row(lw["conv_ln_g"]), _row(lw["conv_ln_b"])]
    prev_halo = lambda i: (jnp.maximum(i * per_halo - 1, 0), 0)
    next_halo = lambda i: (jnp.minimum((i + 1) * per_halo, s_len // HALO - 1), 0)
    small = lambda shape: (jax.ShapeDtypeStruct(shape, F32), pl.BlockSpec(shape, lambda i: (0,) * len(shape)))
    outs = [(jax.ShapeDtypeStruct((s_len, IN_WIDTH), BF16), pl.BlockSpec((ts, IN_WIDTH), lambda i: (i, 0))),
            small((1, IN_WIDTH)), small((len(POOL_WINDOWS), GROUP, GROUP)), small((1, POOL_WIDTH)),
            small((1, SGU_WIDTH)), small((1, SGU_WIDTH)), small((SGU_HEADS, CHUNK, CHUNK)), small((SGU_HEADS, CHUNK, GROUP)),
            small((CONV_ROWS, CONV_WIDTH)), small((1, CONV_WIDTH)), small((1, CONV_WIDTH)), small((1, CONV_WIDTH))]
    return pl.pallas_call(
        body, name=name, grid=(nt,),
        in_specs=[pl.BlockSpec((ts, IN_WIDTH), lambda i: (i, 0)), pl.BlockSpec((HALO, IN_WIDTH), prev_halo),
                  pl.BlockSpec((ts, CONV_WIDTH), lambda i: (i, 0)), pl.BlockSpec((HALO, CONV_WIDTH), next_halo),
                  pl.BlockSpec((ts, d), lambda i: (i, 0)), pl.BlockSpec((HALO, d), next_halo)] + [full(a) for a in weights],
        out_specs=[s for _, s in outs],
        out_shape=[o for o, _ in outs],
        scratch_shapes=[pltpu.VMEM((16 + ts, POOL_WIDTH), F32), pltpu.VMEM((ts + HALO, POOL_WIDTH), F32),
                        pltpu.VMEM((HALO + ts, CONV_WIDTH), F32), pltpu.VMEM((ts + HALO, CONV_WIDTH), F32),
                        pltpu.VMEM((CONV_ROWS, 8, CONV_WIDTH), F32),
                        pltpu.VMEM((SUBLANES - 1, ts + HALO - SUBLANES, CONV_WIDTH), F32)],
        compiler_params=_params(1),
    )(p, p, cv, cv, dmixed, dmixed, *weights)


def _masked_sgu_w_t(wt_ref, h):
    r = lax.broadcasted_iota(jnp.int32, (CHUNK, CHUNK), 0)
    c = lax.broadcasted_iota(jnp.int32, (CHUNK, CHUNK), 1)
    return jnp.where(c >= r, wt_ref[h], 0.0)


HBM = pl.BlockSpec(memory_space=pltpu.HBM)
CHIP_FLIPS = ((1, 0), (0, 1), (1, 1))


def _place():
    return lax.axis_index("x"), lax.axis_index("y"), lax.axis_index("c")


def _half(ref, axis, which, size):
    idx = [slice(None)] * len(ref.shape)
    idx[axis] = pl.ds(which * size, size)
    return ref.at[tuple(idx)]


def _exchange(name, sources, inplace, fresh, copies):
    n_src, n_in, n = len(sources), len(inplace), len(copies)
    n_out = n_in + len(fresh)

    def body(*refs):
        ins = refs[:n_src + n_in]
        outs = refs[n_src + n_in:n_src + n_in + n_out]
        send_sems, recv_sems = refs[n_src + n_in + n_out:]
        started = _descriptors(copies, list(ins[:n_src]) + list(outs), send_sems, recv_sems)
        for cp in started:
            cp.start()
        for cp in started:
            cp.wait()

    out_shape = [jax.ShapeDtypeStruct(a.shape, a.dtype) for a in inplace] + list(fresh)
    return pl.pallas_call(
        body, name=name, in_specs=[HBM] * (n_src + n_in), out_specs=[HBM] * n_out, out_shape=out_shape,
        input_output_aliases={n_src + i: i for i in range(n_in)},
        scratch_shapes=[pltpu.SemaphoreType.DMA((n,)), pltpu.SemaphoreType.DMA((n,))],
    )(*sources, *inplace)


def _descriptors(copies, bufs, send_sems, recv_sems):
    x, y, c = _place()
    return [pltpu.make_async_remote_copy(
        src_ref=src_view(bufs[src], x, y, c), dst_ref=dst_view(bufs[dst], x, y, c),
        send_sem=send_sems.at[k], recv_sem=recv_sems.at[k], device_id=peer(x, y, c), device_id_type=MESH)
        for k, (src, src_view, dst, dst_view, peer) in enumerate(copies)]


SEM = pl.BlockSpec(memory_space=pltpu.SEMAPHORE)
IN_FLIGHT = pltpu.CompilerParams(has_side_effects=pltpu.SideEffectType.DATAFLOW_SIDE_EFFECTING)


def _exchange_start(name, sources, inplace, fresh, copies):
    n_src, n = len(sources), len(copies)
    landing = [lax.empty(f.shape, f.dtype) for f in fresh]
    bufs = [pltpu.with_memory_space_constraint(a, pltpu.HBM) for a in (*inplace, *landing)]
    srcs = [pltpu.with_memory_space_constraint(a, pltpu.HBM) for a in sources]
    n_buf = len(bufs)

    def body(*refs):
        ins = refs[:n_src]
        send_sems, recv_sems = refs[n_src + n_buf], refs[n_src + n_buf + 1]
        outs = refs[n_src + n_buf + 2:n_src + 2 * n_buf + 2]
        token = refs[n_src + 2 * n_buf + 2]
        for cp in _descriptors(copies, list(ins) + list(outs), send_sems, recv_sems):
            cp.start()
        token[...] = jnp.zeros_like(token)

    res = pl.pallas_call(
        body, name=name + "_start",
        out_shape=(pltpu.SemaphoreType.DMA((n,)), pltpu.SemaphoreType.DMA((n,)),
                   *[pltpu.HBM(b.shape, b.dtype) for b in bufs], jax.ShapeDtypeStruct((8, 128), F32)),
        in_specs=[HBM] * (n_src + n_buf),
        out_specs=(SEM, SEM, *[HBM] * n_buf, pl.BlockSpec(memory_space=pltpu.VMEM)),
        input_output_aliases={n_src + i: 2 + i for i in range(n_buf)},
        compiler_params=IN_FLIGHT,
    )(*srcs, *bufs)
    return dict(name=name, sources=srcs, sems=res[:2], bufs=list(res[2:2 + n_buf]), token=res[2 + n_buf], copies=copies)


def _exchange_wait(handle, after):
    srcs, bufs, copies = handle["sources"], handle["bufs"], handle["copies"]
    n_src, n_buf = len(srcs), len(bufs)

    def body(*refs):
        ins = refs[:n_src]
        send_sems, recv_sems = refs[n_src + n_buf], refs[n_src + n_buf + 1]
        outs = refs[n_src + n_buf + 3:]
        for cp in _descriptors(copies, list(ins) + list(outs), send_sems, recv_sems):
            cp.wait()

    return list(pl.pallas_call(
        body, name=handle["name"] + "_wait",
        out_shape=tuple(pltpu.HBM(b.shape, b.dtype) for b in bufs),
        in_specs=[HBM] * (n_src + n_buf) + [SEM, SEM, pl.BlockSpec(memory_space=pl.ANY)],
        out_specs=tuple([HBM] * n_buf),
        input_output_aliases={n_src + i: i for i in range(n_buf)},
        compiler_params=IN_FLIGHT,
    )(*srcs, *bufs, *handle["sems"], after))


def _into_slot(name, src, layer, place, dtype, after=None):
    _, r, c = src.shape
    br = _tile(r, BR_EW) if r % BR_EW == 0 else r
    order = [] if after is None else [after]

    def body(pr_ref, s_ref, *rest):
        rest[-1][...] = s_ref[...].astype(dtype)

    return pl.pallas_call(
        body, name=name,
        grid_spec=pltpu.PrefetchScalarGridSpec(
            num_scalar_prefetch=1, grid=(r // br,),
            in_specs=[pl.BlockSpec((None, br, c), lambda i, pr: (layer, i, 0))] + [pl.BlockSpec(memory_space=pl.ANY)] * len(order),
            out_specs=pl.BlockSpec((None, br, c), lambda i, pr: (pr[1], i, 0))),
        out_shape=jax.ShapeDtypeStruct((4, r, c), dtype), compiler_params=_params(1),
    )(place, src, *order)


def _gather_group(name, bufs):
    ici, d2d = _gather_copies(len(bufs))
    bufs = _exchange(name + "_chips", [], bufs, [], ici)
    return _exchange(name + "_sibling", [], bufs, [], d2d)


def _gather_copies(n_bufs):
    def own_half(ref, x, y, c):
        return _half(ref.at[2 * x + y], 0, c, ref.shape[1] // 2)

    ici, d2d = [], []
    for b in range(n_bufs):
        for fx, fy in CHIP_FLIPS:
            ici.append((b, own_half, b, own_half, lambda x, y, c, fx=fx, fy=fy: (x ^ fx, y ^ fy, c)))
            landed = lambda ref, x, y, c, fx=fx, fy=fy: _half(ref.at[2 * (x ^ fx) + (y ^ fy)], 0, c, ref.shape[1] // 2)
            d2d.append((b, landed, b, landed, _sibling))
    return ici, d2d


def _gather_start(name, bufs):
    return _exchange_start(name + "_chips", [], bufs, [], _gather_copies(len(bufs))[0])


def _gather_finish(name, handle, after):
    bufs = _exchange_wait(handle, after)
    return _exchange(name + "_sibling", [], bufs, [], _gather_copies(len(bufs))[1])


def _gather_pass_on(name, handle, after):
    bufs = _exchange_wait(handle, after)
    return _exchange_start(name + "_sibling", [], bufs, [], _gather_copies(len(bufs))[1])


def _sibling(x, y, c):
    return (x, y, 1 - c)


def _scalar_spec_call(name, fn, scalars, ins, in_blocks, out_shapes, out_blocks, grid):
    def body(s_ref, *refs):
        res = fn(*[r[...] for r in refs[:len(ins)]])
        for o_ref, v in zip(refs[len(ins):], res):
            o_ref[...] = v.astype(o_ref.dtype)

    return pl.pallas_call(
        body, name=name,
        grid_spec=pltpu.PrefetchScalarGridSpec(num_scalar_prefetch=1, grid=grid, in_specs=in_blocks, out_specs=out_blocks),
        out_shape=out_shapes, compiler_params=_params(len(grid)),
    )(scalars, *ins)


def _reduce_begin(tag, place, items):
    n = len(items)
    whole = lambda ref, x, y, c: ref
    halves = [g.shape[1] // 2 for g, _, _ in items]
    fresh = [jax.ShapeDtypeStruct((g.shape[0], halves[a], g.shape[2]), F32) for a, (g, _, _) in enumerate(items)]
    copies = [(a, lambda ref, x, y, c, hr=halves[a]: _half(ref, 1, 1 - c, hr), n + a, whole, _sibling) for a in range(n)]
    handle = _exchange_start(tag + "_sibling_in", [g for g, _, _ in items], [], fresh, copies)
    return dict(tag=tag, place=place, items=items, handle=handle, token=handle["token"])


def _reduce_between_chips(state, after):
    tag, place, items = state["tag"], state["place"], state["items"]
    n = len(items)
    slot_of = lambda scatter: (lambda x, y: 2 * x + y) if scatter else (lambda x, y: 0)
    halves = [g.shape[1] // 2 for g, _, _ in items]
    landed = _exchange_wait(state["handle"], after)

    chip_sums = []
    for a, (g, _, wire) in enumerate(items):
        ns, _, cols = g.shape
        hr = halves[a]
        br = _tile(hr, BR_EW)
        chip_sums.append(_scalar_spec_call(
            f"{tag}_chip_sum{a}", lambda u, v: (u + v,), place, [g, landed[a]],
            [pl.BlockSpec((None, br, cols), lambda s, i, pr, hb=hr // br: (s, pr[0] * hb + i, 0)),
             pl.BlockSpec((None, br, cols), lambda s, i, pr: (s, i, 0))],
            [jax.ShapeDtypeStruct((ns, hr, cols), wire)], [pl.BlockSpec((None, br, cols), lambda s, i, pr: (s, i, 0))],
            (ns, hr // br))[0])

    copies, fresh = [], []
    for a, (g, scatter, wire) in enumerate(items):
        slot = slot_of(scatter)
        fresh.append(jax.ShapeDtypeStruct((3, halves[a], g.shape[2]), wire))
        for j, (fx, fy) in enumerate(CHIP_FLIPS):
            copies.append((a, lambda ref, x, y, c, fx=fx, fy=fy, slot=slot: ref.at[slot(x ^ fx, y ^ fy)],
                           n + a, lambda ref, x, y, c, j=j: ref.at[j], lambda x, y, c, fx=fx, fy=fy: (x ^ fx, y ^ fy, c)))
    handle = _exchange_start(tag + "_chips", chip_sums, [], fresh, copies)
    return dict(tag=tag, place=place, items=items, chip_sums=chip_sums, handle=handle, token=handle["token"])


def _reduce_end(state, after):
    tag, place, items, chip_sums = state["tag"], state["place"], state["items"], state["chip_sums"]
    n = len(items)
    whole = lambda ref, x, y, c: ref
    halves = [g.shape[1] // 2 for g, _, _ in items]
    arrived = _exchange_wait(state["handle"], after)

    tree = lambda own, fx, fy, fxy: ((own.astype(F32) + fx.astype(F32)) + (fy.astype(F32) + fxy.astype(F32)),)
    mine = []
    for a, (g, scatter, _) in enumerate(items):
        hr, cols = halves[a], g.shape[2]
        br = _tile(hr, BR_EW)
        got = lambda j: pl.BlockSpec((None, br, cols), lambda i, pr, j=j: (j, i, 0))
        own = pl.BlockSpec((None, br, cols), (lambda i, pr: (pr[1], i, 0)) if scatter else (lambda i, pr: (0, i, 0)))
        mine.append(_scalar_spec_call(
            f"{tag}_tree_sum{a}", tree, place, [chip_sums[a], arrived[a], arrived[a], arrived[a]],
            [own, got(0), got(1), got(2)],
            [jax.ShapeDtypeStruct((hr, cols), F32)], [pl.BlockSpec((br, cols), lambda i, pr: (i, 0))], (hr // br,))[0])

    copies = [(a, whole, n + a, whole, _sibling) for a in range(n)]
    fresh = [jax.ShapeDtypeStruct(h.shape, F32) for h in mine]
    theirs = _exchange(tag + "_sibling_out", mine, [], fresh, copies)
    return list(zip(mine, theirs))


def _select_half(h, c, mine, theirs):
    return jnp.where(h == c, mine, theirs)


def _adamw_sharded(name, layer, place, w, m, v, g_mine, g_theirs, earlier):
    n_layers, r, c = w.shape
    hr = r // 2
    br = min(hr, max(8, 2 ** int(math.log2(EW_BLOCK_ELEMS // c))))
    assert hr % br == 0
    nb = hr // br
    full = pl.BlockSpec((None, br, c), lambda h, i, pr: (layer, h * nb + i, 0))
    half = pl.BlockSpec((br, c), lambda h, i, pr: (i, 0))
    n_alias = 0 if earlier is None else 4

    def body(pr_ref, w_ref, m_ref, v_ref, gm_ref, gt_ref, *rest):
        g_ref, d_ref, nm_ref, nv_ref = rest[n_alias:]
        g = _select_half(pl.program_id(0), pr_ref[0], gm_ref[...], gt_ref[...])
        delta, new_m, new_v = _adamw_math(w_ref[...], g, m_ref[...], v_ref[...])
        g_ref[...] = g
        d_ref[...] = delta
        nm_ref[...] = new_m
        nv_ref[...] = new_v

    return pl.pallas_call(
        body, name=name,
        grid_spec=pltpu.PrefetchScalarGridSpec(
            num_scalar_prefetch=1, grid=(2, nb),
            in_specs=[full, full, full, half, half] + [pl.BlockSpec(memory_space=pl.ANY)] * n_alias,
            out_specs=[full] * 4),
        out_shape=[jax.ShapeDtypeStruct((n_layers, r, c), F32)] * 4,
        input_output_aliases={6 + i: i for i in range(n_alias)},
        compiler_params=_params(2),
    )(place, w, m, v, g_mine, g_theirs, *(earlier or ()))


SHARDED = ("w_in", "w_out", "w_ff1", "w_ff2")
REPLICATED = ("b_in", "w_pool", "pool_scale", "sgu_ln_g", "sgu_ln_b", "sgu_w", "sgu_b", "conv_b", "conv_ln_g", "conv_ln_b",
              "b_out", "ln1_g", "ln1_b", "b_ff1", "b_ff2", "ln2_g", "ln2_b")
WEIGHTS = ("w_in", "b_in", "w_pool", "pool_scale", "sgu_ln_g", "sgu_ln_b", "sgu_w", "sgu_b", "conv_w", "conv_b", "conv_ln_g",
           "conv_ln_b", "w_out", "b_out", "ln1_g", "ln1_b", "w_ff1", "b_ff1", "w_ff2", "b_ff2", "ln2_g", "ln2_b")
PACK_ROWS = 1024


def _pack(arrays):
    flat = jnp.concatenate([a.reshape(-1) for a in arrays])
    rows = -(-flat.shape[0] // (128 * PACK_ROWS)) * PACK_ROWS
    return jnp.pad(flat, (0, rows * 128 - flat.shape[0])).reshape(rows, 128)


def _unpack(packed, like):
    flat = packed.reshape(-1)
    out, at = [], 0
    for a in like:
        out.append(flat[at:at + a.size].reshape(a.shape))
        at += a.size
    return out


def kernel(x, w_in, b_in, w_pool, pool_scale, sgu_ln_g, sgu_ln_b, sgu_w, sgu_b, conv_w, conv_b, conv_ln_g, conv_ln_b, w_out, b_out, ln1_g, ln1_b, w_ff1, b_ff1, w_ff2, b_ff2, ln2_g, ln2_b, loss_target, m_w_in, m_b_in, m_w_pool, m_pool_scale, m_sgu_ln_g, m_sgu_ln_b, m_sgu_w, m_sgu_b, m_conv_w, m_conv_b, m_conv_ln_g, m_conv_ln_b, m_w_out, m_b_out, m_ln1_g, m_ln1_b, m_w_ff1, m_b_ff1, m_w_ff2, m_b_ff2, m_ln2_g, m_ln2_b, v_w_in, v_b_in, v_w_pool, v_pool_scale, v_sgu_ln_g, v_sgu_ln_b, v_sgu_w, v_sgu_b, v_conv_w, v_conv_b, v_conv_ln_g, v_conv_ln_b, v_w_out, v_b_out, v_ln1_g, v_ln1_b, v_w_ff1, v_b_ff1, v_w_ff2, v_b_ff2, v_ln2_g, v_ln2_b):
    w = dict(w_in=w_in, b_in=b_in, w_pool=w_pool, pool_scale=pool_scale, sgu_ln_g=sgu_ln_g, sgu_ln_b=sgu_ln_b, sgu_w=sgu_w,
             sgu_b=sgu_b, conv_w=conv_w, conv_b=conv_b, conv_ln_g=conv_ln_g, conv_ln_b=conv_ln_b, w_out=w_out, b_out=b_out,
             ln1_g=ln1_g, ln1_b=ln1_b, w_ff1=w_ff1, b_ff1=b_ff1, w_ff2=w_ff2, b_ff2=b_ff2, ln2_g=ln2_g, ln2_b=ln2_b)
    m = dict(w_in=m_w_in, b_in=m_b_in, w_pool=m_w_pool, pool_scale=m_pool_scale, sgu_ln_g=m_sgu_ln_g, sgu_ln_b=m_sgu_ln_b,
             sgu_w=m_sgu_w, sgu_b=m_sgu_b, conv_w=m_conv_w, conv_b=m_conv_b, conv_ln_g=m_conv_ln_g, conv_ln_b=m_conv_ln_b,
             w_out=m_w_out, b_out=m_b_out, ln1_g=m_ln1_g, ln1_b=m_ln1_b, w_ff1=m_w_ff1, b_ff1=m_b_ff1, w_ff2=m_w_ff2,
             b_ff2=m_b_ff2, ln2_g=m_ln2_g, ln2_b=m_ln2_b)
    v = dict(w_in=v_w_in, b_in=v_b_in, w_pool=v_w_pool, pool_scale=v_pool_scale, sgu_ln_g=v_sgu_ln_g, sgu_ln_b=v_sgu_ln_b,
             sgu_w=v_sgu_w, sgu_b=v_sgu_b, conv_w=v_conv_w, conv_b=v_conv_b, conv_ln_g=v_conv_ln_g, conv_ln_b=v_conv_ln_b,
             w_out=v_w_out, b_out=v_b_out, ln1_g=v_ln1_g, ln1_b=v_ln1_b, w_ff1=v_w_ff1, b_ff1=v_b_ff1, w_ff2=v_w_ff2,
             b_ff2=v_b_ff2, ln2_g=v_ln2_g, ln2_b=v_ln2_b)
    assert x.shape[0] == 1 and x.shape[2] == POOL_WIDTH + SGU_WIDTH + CONV_WIDTH, x.shape
    xs, target = x[0], loss_target[0]
    s_len, d = xs.shape
    n_layers = w_in.shape[0]
    dff = 4 * w_ff1.shape[2]
    conv_shard = conv_w.shape[2]

    cx, cy, cc = _place()
    me = 2 * cx + cy
    place = jnp.stack([cc, me]).astype(jnp.int32)

    def cast(key, after=None):
        return _into_slot(f"cast_{key[0]}{key[1]}", w[key[0]], key[1], place, BF16, after=after)

    mix0 = [("w_in", 0), ("w_out", 0), "conv_w"]
    ff0 = [("w_ff1", 0), ("w_ff2", 0)]
    conv_padded = jnp.pad(conv_w, ((0, 0), (0, CONV_ROWS - CONV_KERNEL), (0, 0))).reshape(1, n_layers * CONV_ROWS, conv_shard)
    slot = {"conv_w": _into_slot("slot_conv_w", conv_padded, 0, place, F32), ("w_in", 0): cast(("w_in", 0)),
            ("w_out", 0): cast(("w_out", 0))}
    mix0_flight = _gather_start("gather_mix0", [slot[k] for k in mix0])
    slot.update({k: cast(k, mix0_flight["token"]) for k in ff0})
    ff0_flight = _gather_start("gather_ff0", [slot[k] for k in ff0])
    slot.update({(n, l): cast((n, l), ff0_flight["token"]) for n in SHARDED for l in range(1, n_layers)})
    xb0 = _rowwise("cast_x", lambda t: (t,), [xs], [BF16])[0]
    gathered = dict(zip(mix0, _gather_finish("gather_mix0", mix0_flight, slot[(SHARDED[-1], n_layers - 1)])))
    conv_full = jnp.transpose(gathered["conv_w"].reshape(4, n_layers, CONV_ROWS, conv_shard), (1, 2, 0, 3))
    conv_full = conv_full.reshape(n_layers, CONV_ROWS, CONV_WIDTH)
    conv_rev = jnp.pad(conv_full[:, CONV_KERNEL - 1::-1], ((0, 0), (0, CONV_ROWS - CONV_KERNEL), (0, 0)))

    def layer_weights(l):
        return dict(w_pool=w_pool[l], pool_scale=pool_scale[l], sgu_ln_g=sgu_ln_g[l], sgu_ln_b=sgu_ln_b[l], sgu_w=sgu_w[l],
                    sgu_w_t=jnp.transpose(sgu_w[l], (0, 2, 1)),
                    sgu_bias_tile=jnp.broadcast_to(sgu_b[l][:, :, None], (SGU_HEADS, CHUNK, GROUP)),
                    conv_w_full=conv_full[l], conv_w_rev=conv_rev[l], conv_b=conv_b[l], conv_ln_g=conv_ln_g[l],
                    conv_ln_b=conv_ln_b[l])

    saved = []
    x_cur, xb_cur = xs, xb0
    for l in range(n_layers):
        lw = layer_weights(l)
        w_out_full = gathered[("w_out", l)].reshape(d, d)
        p = _proj(f"proj{l}", xb_cur, gathered[("w_in", l)], b_in[l], after=ff0_flight["token"] if l == 0 else None)
        mixed, cv = _mix_forward(f"mix_fwd{l}", p, lw)
        r1, x1b = _mix_out(f"mix_out{l}", mixed, w_out_full, b_out[l], x_cur, ln1_g[l], ln1_b[l])
        if l == 0:
            gathered.update(zip(ff0, _gather_finish("gather_ff0", ff0_flight, x1b)))
        nxt = [(n, l + 1) for n in SHARDED] if l + 1 < n_layers else None
        in_flight = _gather_start(f"gather_layer{l + 1}", [slot[k] for k in nxt]) if nxt else None
        w2_full = gathered[("w_ff2", l)].reshape(dff, d)
        hf, zr = _ff1(f"ff1_{l}", x1b, gathered[("w_ff1", l)], b_ff1[l], after=in_flight and in_flight["token"])
        fo = _ff2(f"ff2_{l}", hf, w2_full, b_ff2[l])
        sv = dict(lw=lw, xb_in=xb_cur, p=p, mixed=mixed, cv=cv, r1=r1, x1b=x1b, hf=hf, zr=zr, fo=fo,
                  w_out_full=w_out_full, w2_full=w2_full)
        if nxt:
            in_flight = _gather_pass_on(f"gather_layer{l + 1}", in_flight, fo)
            sv["r2"], x_cur, xb_cur = _resid_ln2(f"ln2_{l}", r1, ln1_g[l], ln1_b[l], fo, ln2_g[l], ln2_b[l],
                                                 after=in_flight["token"])
            gathered.update(zip(nxt, _exchange_wait(in_flight, xb_cur)))
        saved.append(sv)

    grads = {n: [None] * n_layers for n in REPLICATED + ("conv_w",)}
    g_final, delta, new_m, new_v = {}, {}, {}, {}
    results = {n: None for n in SHARDED}
    dx_mm, dx_resid = None, None
    loss_tile = None
    pending = None

    def finish_reduce(begun, after):
        names, layer, state = begun
        reduced = _reduce_end(state, after)
        for n, (mine, theirs) in zip(names, reduced):
            results[n] = _adamw_sharded(f"adamw_{n}{layer}", layer, place, w[n], m[n], v[n], mine, theirs, results[n])
        return reduced

    for l in reversed(range(n_layers)):
        sv = saved[l]
        if dx_mm is None:
            dr2, dr2b, dg2, db2, dbff2, loss_tile = _ln_backward(
                f"ln2_bwd{l}", None, ln2_g[l], last=(sv["r1"], ln1_g[l], ln1_b[l], sv["fo"], ln2_b[l], target))
        else:
            dr2, dr2b, dg2, db2, dbff2 = _ln_backward(f"ln2_bwd{l}", sv["r2"], ln2_g[l], dy=dx_mm, resid=dx_resid)
        dzb, dbff1 = _dff_hidden(f"ff2_bwd{l}", dr2b, sv["w2_full"], sv["zr"])
        dw = {"w_ff2": _matmul_tn(f"dw_ff2_{l}", sv["hf"], dr2b, 4, dff // 4, d, True),
              "w_ff1": _matmul_tn(f"dw_ff1_{l}", sv["x1b"], dzb, 4, d, dff // 4, False)}
        if pending is not None:
            finish_reduce(pending, dw["w_ff1"])
            pending = None
        ff_red = _reduce_begin(f"grads_ff{l}", place, [(dw[n], True, BF16) for n in ("w_ff1", "w_ff2")])
        dx1 = _dx_sharded(f"ff1_bwd{l}", dzb, gathered[("w_ff1", l)], after=ff_red["token"])
        ff_red = _reduce_between_chips(ff_red, dx1)
        dr1, dr1b, dg1, db1, dbout = _ln_backward(f"ln1_bwd{l}", sv["r1"], ln1_g[l], dy=dx1, resid=dr2, after=ff_red["token"])
        dmixed = _dmixed(f"mix_out_bwd{l}", dr1b, sv["w_out_full"])
        dw["w_out"] = _matmul_tn(f"dw_out{l}", sv["mixed"], dr1b, 4, d // 4, d, True)
        (dp, dbin, dwp, dps, dslg, dslb, dws, dsb_tile, dcw, dcb, dclg, dclb) = _mix_backward(
            f"mix_bwd{l}", sv["p"], sv["cv"], dmixed, sv["lw"])
        dw["w_in"] = _matmul_tn(f"dw_in{l}", sv["xb_in"], dp, 4, d, IN_WIDTH // 4, False)
        for name, g in (("b_in", dbin), ("w_pool", dwp), ("pool_scale", dps), ("sgu_ln_g", dslg), ("sgu_ln_b", dslb),
                        ("sgu_w", dws), ("sgu_b", dsb_tile[:, :, 0]), ("conv_w", dcw[:CONV_KERNEL]), ("conv_b", dcb),
                        ("conv_ln_g", dclg), ("conv_ln_b", dclb), ("b_out", dbout), ("ln1_g", dg1), ("ln1_b", db1),
                        ("b_ff1", dbff1), ("b_ff2", dbff2), ("ln2_g", dg2), ("ln2_b", db2)):
            grads[name][l] = g.reshape(w[name].shape[1:]) if name != "conv_w" else g

        items = [(dw[n], True, BF16) for n in ("w_in", "w_out")]
        if l == 0:
            small_like = [jnp.stack(grads[n]) for n in REPLICATED + ("conv_w",)]
            items.append((_pack(small_like)[None], False, F32))
        mix_red = _reduce_begin(f"grads_mix{l}", place, items)
        finish_reduce((("w_ff1", "w_ff2"), l, ff_red), mix_red["token"])
        mix_red = _reduce_between_chips(mix_red, results["w_ff1"][0])
        if l > 0:
            dx_mm, dx_resid = _dx_sharded(f"proj_bwd{l}", dp, gathered[("w_in", l)], after=mix_red["token"]), dr1
            pending = (("w_in", "w_out"), l, mix_red)
        else:
            grad_x = _dx_sharded(f"proj_bwd{l}", dp, gathered[("w_in", l)], dr1, after=mix_red["token"])
            mine, theirs = finish_reduce((("w_in", "w_out"), l, mix_red), grad_x)[-1]
            grad_x = grad_x[None]
            reduced_small = jnp.where(cc == 0, jnp.concatenate([mine, theirs]), jnp.concatenate([theirs, mine]))

    loss = lax.psum(loss_tile[0, 0], ("x", "y", "c"))
    for n in SHARDED:
        g_final[n], delta[n], new_m[n], new_v[n] = results[n]
    unpacked = _unpack(reduced_small, small_like)
    g_final.update(zip(REPLICATED, unpacked[:-1]))
    g_final["conv_w"] = lax.dynamic_slice_in_dim(unpacked[-1], me * conv_shard, conv_shard, axis=2)

    delta["conv_w"], new_m["conv_w"], new_v["conv_w"] = _adamw("adamw_conv_w", conv_w, g_final["conv_w"], m["conv_w"], v["conv_w"])
    packed = [_pack([t[n] for n in REPLICATED]) for t in (w, g_final, m, v)]
    like = [w[n] for n in REPLICATED]
    for res, packed_out in zip((delta, new_m, new_v), _adamw("adamw_replicated", *packed)):
        res.update(zip(REPLICATED, _unpack(packed_out, like)))

    return (loss, grad_x, *[g_final[n] for n in WEIGHTS], *[delta[n] for n in WEIGHTS],
            *[new_m[n] for n in WEIGHTS], *[new_v[n] for n in WEIGHTS])
```

```python
import functools
import math

import jax
import jax.numpy as jnp
from jax import lax
from jax.experimental import pallas as pl
from jax.experimental.pallas import tpu as pltpu

F32 = jnp.float32
BF16 = jnp.bfloat16
MESH = pl.DeviceIdType.MESH

DEPTH = 2
POOL_WINDOWS = (2, 4, 8, 16)
POOL_WIDTH = 512
GROUP = 128
SGU_WIDTH = 768
SGU_HEADS = 6
CHUNK = 128
CONV_WIDTH = 768
CONV_KERNEL = 31
CONV_ROWS = 32
HALO = 32
COL_B = POOL_WIDTH
COL_C = POOL_WIDTH + 2 * SGU_WIDTH
IN_WIDTH = COL_C + 2 * CONV_WIDTH
ALPHA = (2 * DEPTH) ** 0.25
LN_EPS = 1e-5
ADAM_LR = 0.001
ADAM_B1 = 0.9
ADAM_B2 = 0.999
ADAM_EPS = 1e-08
ADAM_WD = 0.01
ADAM_STEP = 10
GELU_C = math.sqrt(2.0 / math.pi)
GELU_A = 0.044715

V7X_VMEM_BYTES = 64 * 2 ** 20
VMEM_LIMIT = 56 * 2 ** 20

BM_MM = 1024
BN_MM = 1024
BM_LN = 512
BK_MM = 1024
BK_WIDE = 2048
BS_TN = 2048
TN_OUT_BYTES = 8 * 2 ** 20
TS_MIX = 256
RB_CONV = 64
CONV_LIVE_BLOCKS = 2
BR_EW = 512
EW_BLOCK_ELEMS = 2 ** 18


def _tile(n, pref):
    t = min(n, pref)
    assert n % t == 0, (n, pref)
    return t


def _params(n_grid):
    return pltpu.CompilerParams(dimension_semantics=("arbitrary",) * n_grid, vmem_limit_bytes=VMEM_LIMIT)


def _sigmoid(x):
    return 1.0 / (1.0 + jnp.exp(-x))


def _gelu(x):
    return 0.5 * x * (1.0 + jnp.tanh(GELU_C * (x + GELU_A * x * x * x)))


def _gelu_with_grad(x):
    t = jnp.tanh(GELU_C * (x + GELU_A * x * x * x))
    half = 0.5 * (1.0 + t)
    return x * half, half + 0.5 * x * (1.0 - t * t) * GELU_C * (1.0 + 3.0 * GELU_A * x * x)


def _ln_stats(r):
    mu = jnp.mean(r, axis=-1, keepdims=True)
    xc = r - mu
    var = jnp.mean(xc * xc, axis=-1, keepdims=True)
    rstd = lax.rsqrt(var + LN_EPS)
    return xc * rstd, rstd


def _ln_bwd(dy, xhat, rstd, g):
    dxh = dy * g
    m1 = jnp.mean(dxh, axis=-1, keepdims=True)
    m2 = jnp.mean(dxh * xhat, axis=-1, keepdims=True)
    return rstd * (dxh - m1 - xhat * m2)


def _colsum(x):
    return jnp.sum(x, axis=0, keepdims=True)


def _accumulate(ref, val, first):
    @pl.when(first)
    def _():
        ref[...] = val

    @pl.when(jnp.logical_not(first))
    def _():
        ref[...] += val


def _matmul(name, grid, a, a_spec, b, b_spec, *, nt, extras, outs, epilogue, acc_shape=None, after=None):
    nk = grid[2]
    if after is not None:
        extras = list(extras) + [(after, pl.BlockSpec(memory_space=pl.ANY))]
    ne, no = len(extras), len(outs)
    dims = (((1,), (1,)), ((), ())) if nt else (((1,), (0,)), ((), ()))

    def body(*refs):
        a_ref, b_ref = refs[0], refs[1]
        ex = refs[2:2 + ne]
        out_refs = refs[2 + ne:2 + ne + no]
        ids = (pl.program_id(0), pl.program_id(1), pl.program_id(2))
        part = lax.dot_general(a_ref[...], b_ref[...], dims, preferred_element_type=F32)
        if nk == 1:
            epilogue(part, ex, out_refs, ids)
        elif acc_shape is None:
            k = ids[2]

            @pl.when(k == 0)
            def _():
                epilogue(part, ex, out_refs, ids)

            @pl.when(k > 0)
            def _():
                out_refs[0][...] += part
        else:
            acc_ref = refs[2 + ne + no]
            k = ids[2]

            @pl.when(k == 0)
            def _():
                acc_ref[...] = part

            @pl.when(k > 0)
            def _():
                acc_ref[...] += part

            @pl.when(k == nk - 1)
            def _():
                epilogue(acc_ref[...], ex, out_refs, ids)

    return pl.pallas_call(
        body,
        name=name,
        grid=grid,
        in_specs=[a_spec, b_spec] + [s for _, s in extras],
        out_specs=[s for _, s in outs],
        out_shape=[o for o, _ in outs],
        scratch_shapes=[pltpu.VMEM(acc_shape, F32)] if nk > 1 and acc_shape is not None else [],
        compiler_params=_params(3),
    )(a, b, *[e for e, _ in extras])


def _matmul_tn(name, a, b, n_shards, shard_rows, shard_cols, row_sharded):
    s_len, ka = a.shape
    n = b.shape[1]
    assert (n_shards * shard_rows, shard_cols) == (ka, n) if row_sharded else (shard_rows, n_shards * shard_cols) == (ka, n)
    bs = _tile(s_len, BS_TN)
    bka = _tile(shard_rows, 2048)
    bn = next((t for t in (2 * BN_MM, BN_MM) if shard_cols % t == 0 and bka * t * 4 <= TN_OUT_BYTES), shard_cols)
    ni, nj, ns = ka // bka, n // bn, s_len // bs
    per_shard_i = shard_rows // bka
    per_shard_j = shard_cols // bn

    if row_sharded:
        out_map = lambda i, j, s: (i // per_shard_i, i % per_shard_i, j)
    else:
        out_map = lambda i, j, s: (j // per_shard_j, i, j % per_shard_j)

    def body(a_ref, b_ref, o_ref):
        s = pl.program_id(2)
        part = lax.dot_general(a_ref[...], b_ref[...], (((0,), (0,)), ((), ())), preferred_element_type=F32)
        _accumulate(o_ref, part, s == 0)

    return pl.pallas_call(
        body,
        name=name,
        grid=(ni, nj, ns),
        in_specs=[pl.BlockSpec((bs, bka), lambda i, j, s: (s, i)), pl.BlockSpec((bs, bn), lambda i, j, s: (s, j))],
        out_specs=pl.BlockSpec((None, bka, bn), out_map),
        out_shape=jax.ShapeDtypeStruct((n_shards, shard_rows, shard_cols), F32),
        compiler_params=_params(3),
    )(a, b)


def _row(v):
    return v.reshape(1, -1)


def _proj(name, xb, w_g, b_in, after=None):
    s_len, d = xb.shape
    ncs = w_g.shape[2]
    bm = _tile(s_len, BM_MM)

    def epilogue(acc, ex, outs, ids):
        outs[0][...] = acc + ex[0][...]

    return _matmul(
        name, (s_len // bm, 4, 1),
        xb, pl.BlockSpec((bm, d), lambda i, j, k: (i, 0)),
        w_g, pl.BlockSpec((None, d, ncs), lambda i, j, k: (j, 0, 0)),
        nt=False,
        extras=[(_row(b_in), pl.BlockSpec((1, ncs), lambda i, j, k: (0, j)))],
        outs=[(jax.ShapeDtypeStruct((s_len, 4 * ncs), F32), pl.BlockSpec((bm, ncs), lambda i, j, k: (i, j)))],
        epilogue=epilogue, after=after,
    )[0]


def _mix_out(name, mixed, w_out_full, b_out, x0, g1, b1):
    s_len, d = mixed.shape
    bm = _tile(s_len, BM_LN // 2)
    row = pl.BlockSpec((1, d), lambda i, j, k: (0, 0))
    blk = pl.BlockSpec((bm, d), lambda i, j, k: (i, 0))

    def epilogue(acc, ex, outs, ids):
        r1 = ALPHA * ex[1][...] + (acc + ex[0][...])
        outs[0][...] = r1
        xhat, _ = _ln_stats(r1)
        outs[1][...] = (xhat * ex[2][...] + ex[3][...]).astype(BF16)

    return _matmul(
        name, (s_len // bm, 1, 1),
        mixed, blk,
        w_out_full, pl.BlockSpec((d, d), lambda i, j, k: (0, 0)),
        nt=False,
        extras=[(_row(b_out), row), (x0, blk), (_row(g1), row), (_row(b1), row)],
        outs=[(jax.ShapeDtypeStruct((s_len, d), F32), blk), (jax.ShapeDtypeStruct((s_len, d), BF16), blk)],
        epilogue=epilogue,
    )


def _ff1(name, x1b, w_g, b_ff1, after=None):
    s_len, d = x1b.shape
    ncs = w_g.shape[2]
    bm = _tile(s_len, BM_MM)
    bn = _tile(ncs, BN_MM)
    per = ncs // bn
    blk = pl.BlockSpec((bm, bn), lambda i, j, k: (i, j))

    def epilogue(acc, ex, outs, ids):
        zr = jnp.maximum(acc + ex[0][...], 0.0)
        outs[0][...] = (zr * zr).astype(BF16)
        outs[1][...] = zr.astype(BF16)

    shape = jax.ShapeDtypeStruct((s_len, 4 * ncs), BF16)
    return _matmul(
        name, (s_len // bm, 4 * per, 1),
        x1b, pl.BlockSpec((bm, d), lambda i, j, k: (i, 0)),
        w_g, pl.BlockSpec((None, d, bn), lambda i, j, k: (j // per, 0, j % per)),
        nt=False,
        extras=[(_row(b_ff1), pl.BlockSpec((1, bn), lambda i, j, k: (0, j)))],
        outs=[(shape, blk), (shape, blk)],
        epilogue=epilogue, after=after,
    )


def _ff2(name, hf, w2_full, b_ff2):
    s_len, dff = hf.shape
    d = w2_full.shape[1]
    bm = _tile(s_len, BM_MM)
    bk = _tile(dff, BK_WIDE)
    blk = pl.BlockSpec((bm, d), lambda i, j, k: (i, 0))

    def epilogue(acc, ex, outs, ids):
        outs[0][...] = acc + ex[0][...]

    return _matmul(
        name, (s_len // bm, 1, dff // bk),
        hf, pl.BlockSpec((bm, bk), lambda i, j, k: (i, k)),
        w2_full, pl.BlockSpec((bk, d), lambda i, j, k: (k, 0)),
        nt=False,
        extras=[(_row(b_ff2), pl.BlockSpec((1, d), lambda i, j, k: (0, 0)))],
        outs=[(jax.ShapeDtypeStruct((s_len, d), F32), blk)],
        epilogue=epilogue,
    )[0]


def _resid_ln2(name, r1, g1, b1, fo, g2, b2, after=None):
    s_len, d = r1.shape
    bm = _tile(s_len, BM_LN // 2)
    blk = pl.BlockSpec((bm, d), lambda i: (i, 0))
    row = pl.BlockSpec((1, d), lambda i: (0, 0))
    order = [] if after is None else [after]

    def body(*refs):
        r1_ref, g1_ref, b1_ref, fo_ref, g2_ref, b2_ref, r2_ref, x2_ref, x2b_ref = refs[len(order):]
        xhat1, _ = _ln_stats(r1_ref[...])
        r2 = ALPHA * (xhat1 * g1_ref[...] + b1_ref[...]) + fo_ref[...]
        r2_ref[...] = r2
        xhat2, _ = _ln_stats(r2)
        x2 = xhat2 * g2_ref[...] + b2_ref[...]
        x2_ref[...] = x2
        x2b_ref[...] = x2.astype(BF16)

    return pl.pallas_call(
        body, name=name, grid=(s_len // bm,),
        in_specs=[pl.BlockSpec(memory_space=pl.ANY)] * len(order) + [blk, row, row, blk, row, row], out_specs=[blk, blk, blk],
        out_shape=[jax.ShapeDtypeStruct((s_len, d), F32), jax.ShapeDtypeStruct((s_len, d), F32),
                   jax.ShapeDtypeStruct((s_len, d), BF16)],
        compiler_params=_params(1))(*order, r1, _row(g1), _row(b1), fo, _row(g2), _row(b2))


def _dff_hidden(name, dr2b, w2_full, zr):
    s_len, d = dr2b.shape
    dff = w2_full.shape[0]
    bm = _tile(s_len, BM_MM)
    bn = _tile(dff, BN_MM)

    def epilogue(acc, ex, outs, ids):
        dz = acc * (2.0 * ex[0][...].astype(F32))
        outs[0][...] = dz.astype(BF16)
        _accumulate(outs[1], _colsum(dz), ids[1] == 0)

    return _matmul(
        name, (dff // bn, s_len // bm, 1),
        dr2b, pl.BlockSpec((bm, d), lambda j, i, k: (i, 0)),
        w2_full, pl.BlockSpec((bn, d), lambda j, i, k: (j, 0)),
        nt=True,
        extras=[(zr, pl.BlockSpec((bm, bn), lambda j, i, k: (i, j)))],
        outs=[(jax.ShapeDtypeStruct((s_len, dff), BF16), pl.BlockSpec((bm, bn), lambda j, i, k: (i, j))),
              (jax.ShapeDtypeStruct((1, dff), F32), pl.BlockSpec((1, bn), lambda j, i, k: (0, j)))],
        epilogue=epilogue,
    )


def _dx_sharded(name, dyb, w_g, resid=None, after=None):
    s_len = dyb.shape[0]
    d, ncs = w_g.shape[1], w_g.shape[2]
    bm = _tile(s_len, BM_MM if resid is None else BM_LN)
    bk = _tile(ncs, BK_WIDE) if ncs % BK_MM == 0 else ncs
    per = ncs // bk
    blk = pl.BlockSpec((bm, d), lambda i, j, k: (i, 0))

    def epilogue(acc, ex, outs, ids):
        outs[0][...] = acc if resid is None else acc + ALPHA * ex[0][...]

    return _matmul(
        name, (s_len // bm, 1, 4 * per),
        dyb, pl.BlockSpec((bm, bk), lambda i, j, k: (i, k)),
        w_g, pl.BlockSpec((None, d, bk), lambda i, j, k: (k // per, 0, k % per)),
        nt=True,
        extras=[] if resid is None else [(resid, blk)],
        outs=[(jax.ShapeDtypeStruct((s_len, d), F32), blk)],
        epilogue=epilogue, after=after,
    )[0]


def _dmixed(name, dr1b, w_out_full):
    s_len, d = dr1b.shape
    bm = _tile(s_len, BM_LN)
    blk = pl.BlockSpec((bm, d), lambda i, j, k: (i, 0))

    def epilogue(acc, ex, outs, ids):
        outs[0][...] = acc

    return _matmul(
        name, (s_len // bm, 1, 1),
        dr1b, blk,
        w_out_full, pl.BlockSpec((d, d), lambda i, j, k: (0, 0)),
        nt=True, extras=[],
        outs=[(jax.ShapeDtypeStruct((s_len, d), F32), blk)],
        epilogue=epilogue,
    )[0]


def _ln_backward(name, r, g, *, dy=None, resid=None, last=None, after=None):
    from_loss = last is not None
    s_len, d = last[0].shape if from_loss else r.shape
    bm = _tile(s_len, BM_LN // 2)
    blk = pl.BlockSpec((bm, d), lambda i: (i, 0))
    row = pl.BlockSpec((1, d), lambda i: (0, 0))

    def body(*refs):
        i = pl.program_id(0)
        if after is not None:
            refs = refs[1:]
        if from_loss:
            r1_ref, fo_ref, t_ref, g1_ref, b1_ref, b_ref, g_ref, dr_ref, drb_ref, dg_ref, db_ref, dbias_ref, loss_ref = refs
            xhat1, _ = _ln_stats(r1_ref[...])
            rv = ALPHA * (xhat1 * g1_ref[...] + b1_ref[...]) + fo_ref[...]
            xhat, rstd = _ln_stats(rv)
            diff = (xhat * g_ref[...] + b_ref[...]) - t_ref[...]
            dyv = diff * (1.0 / d)
            part = 0.5 * jnp.sum(jnp.sum(diff * diff, axis=1, keepdims=True) * (1.0 / d), axis=0, keepdims=True)
            _accumulate(loss_ref, jnp.broadcast_to(part, loss_ref.shape), i == 0)
        else:
            if resid is not None:
                dy_ref, res_ref, r_ref, g_ref, dr_ref, drb_ref, dg_ref, db_ref, dbias_ref = refs
                dyv = dy_ref[...] + ALPHA * res_ref[...]
            else:
                dy_ref, r_ref, g_ref, dr_ref, drb_ref, dg_ref, db_ref, dbias_ref = refs
                dyv = dy_ref[...]
            xhat, rstd = _ln_stats(r_ref[...])
        dr = _ln_bwd(dyv, xhat, rstd, g_ref[...])
        dr_ref[...] = dr
        drb_ref[...] = dr.astype(BF16)
        _accumulate(dg_ref, _colsum(dyv * xhat), i == 0)
        _accumulate(db_ref, _colsum(dyv), i == 0)
        _accumulate(dbias_ref, _colsum(dr), i == 0)

    if from_loss:
        r1, g1, b1, fo, b, target = last
        ins = [r1, fo, target, _row(g1), _row(b1), _row(b), _row(g)]
        in_specs = [blk] * 3 + [row] * 4
    else:
        lead = [dy] if resid is None else [dy, resid]
        ins = lead + [r, _row(g)]
        in_specs = [blk] * len(lead) + [blk, row]
    if after is not None:
        ins, in_specs = [after] + ins, [pl.BlockSpec(memory_space=pl.ANY)] + in_specs
    out_shape = [jax.ShapeDtypeStruct((s_len, d), F32), jax.ShapeDtypeStruct((s_len, d), BF16)] + \
                [jax.ShapeDtypeStruct((1, d), F32)] * 3
    out_specs = [blk, blk, row, row, row]
    if from_loss:
        out_shape.append(jax.ShapeDtypeStruct((8, 128), F32))
        out_specs.append(pl.BlockSpec((8, 128), lambda i: (0, 0)))
    return pl.pallas_call(body, name=name, grid=(s_len // bm,), in_specs=in_specs, out_specs=out_specs,
                          out_shape=out_shape, compiler_params=_params(1))(*ins)


def _rowwise(name, fn, ins, out_dtypes, rows_pref=BR_EW):
    r, c = ins[0].shape
    br = _tile(r, rows_pref)
    blk = pl.BlockSpec((br, c), lambda i: (i, 0))

    def body(*refs):
        res = fn(*[ref[...] for ref in refs[:len(ins)]])
        for o_ref, v in zip(refs[len(ins):], res):
            o_ref[...] = v.astype(o_ref.dtype)

    return pl.pallas_call(body, name=name, grid=(r // br,), in_specs=[blk] * len(ins),
                          out_specs=[blk] * len(out_dtypes),
                          out_shape=[jax.ShapeDtypeStruct((r, c), dt) for dt in out_dtypes],
                          compiler_params=_params(1))(*ins)


def _adamw_math(w, g, m, v):
    m = ADAM_B1 * m + (1.0 - ADAM_B1) * g
    v = ADAM_B2 * v + (1.0 - ADAM_B2) * (g * g)
    m_hat = m / (1.0 - ADAM_B1 ** ADAM_STEP)
    v_hat = v / (1.0 - ADAM_B2 ** ADAM_STEP)
    delta = -ADAM_LR * (m_hat / (jnp.sqrt(v_hat) + ADAM_EPS) + ADAM_WD * w)
    return delta, m, v


def _adamw(name, w, g, m, v):
    shape = w.shape
    c = shape[-1]
    flat = [a.reshape(-1, c) for a in (w, g, m, v)]
    rows = flat[0].shape[0]
    pref = max(8, 2 ** int(math.log2(EW_BLOCK_ELEMS // c)))
    res = _rowwise(name, _adamw_math, flat, [F32, F32, F32], rows_pref=pref if rows % pref == 0 else rows)
    return tuple(a.reshape(shape) for a in res)


def _pool_means(ext_ref, ts, tile_index):
    t_glob = tile_index * ts + lax.broadcasted_iota(jnp.int32, (ts, GROUP), 0)
    qs = []
    for g, win in enumerate(POOL_WINDOWS):
        cols = pl.ds(g * GROUP, GROUP)
        cur = ext_ref[pl.ds(16, ts), cols]
        acc = cur
        for j in range(1, win):
            acc = acc + ext_ref[pl.ds(16 - j, ts), cols]
        cnt = jnp.minimum(t_glob + 1, win).astype(F32)
        qs.append(acc / cnt - cur)
    return qs


def _masked_sgu_w(w_ref, h):
    r = lax.broadcasted_iota(jnp.int32, (CHUNK, CHUNK), 0)
    c = lax.broadcasted_iota(jnp.int32, (CHUNK, CHUNK), 1)
    return jnp.where(r >= c, w_ref[h], 0.0)


SUBLANES = 8


def _fill_shifted(src_ref, shifted_ref, ts):
    rows = ts + HALO - SUBLANES
    for b in range(1, SUBLANES):
        shifted_ref[b - 1] = src_ref[pl.ds(b, rows), :]


def _shifted_rows(src_ref, shifted_ref, offset, rows, cols):
    a, b = divmod(offset, SUBLANES)
    if b == 0:
        return src_ref[pl.ds(offset, rows), cols]
    return shifted_ref[b - 1, pl.ds(a * SUBLANES, rows), cols]


def _conv_taps(acc_init, w_ref, src_ref, shifted_ref, ts, base, emit):
    nrb = ts // RB_CONV
    for cg in range(CONV_WIDTH // GROUP):
        cols = pl.ds(cg * GROUP, GROUP)
        for rb0 in range(0, nrb, CONV_LIVE_BLOCKS):
            group = range(rb0, min(rb0 + CONV_LIVE_BLOCKS, nrb))
            accs = {rb: acc_init(cg) for rb in group}
            for k in range(CONV_KERNEL):
                wk = jnp.broadcast_to(w_ref[pl.ds(k, 1), cols], (RB_CONV, GROUP))
                for rb in group:
                    accs[rb] = accs[rb] + wk * _shifted_rows(src_ref, shifted_ref, rb * RB_CONV + base(k), RB_CONV, cols)
            for rb in group:
                emit(rb * RB_CONV, cg * GROUP, accs[rb])


def _mix_forward(name, p, lw):
    s_len = p.shape[0]
    ts = _tile(s_len, TS_MIX)
    per_halo = ts // HALO
    d = POOL_WIDTH + SGU_WIDTH + CONV_WIDTH

    def body(p_ref, ph_ref, wp_ref, ps_ref, slg_ref, slb_ref, ws_ref, sb_ref, cw_ref, cb_ref, clg_ref, clb_ref,
             mixed_ref, cv_ref, pool_ext, hh_ext, shifted):
        i = pl.program_id(0)
        keep = (i > 0).astype(F32)
        pool_ext[pl.ds(0, 16), :] = ph_ref[pl.ds(16, 16), pl.ds(0, POOL_WIDTH)] * keep
        pool_ext[pl.ds(16, ts), :] = p_ref[:, pl.ds(0, POOL_WIDTH)]
        qs = _pool_means(pool_ext, ts, i)
        for g in range(len(POOL_WINDOWS)):
            cols = pl.ds(g * GROUP, GROUP)
            e = jnp.dot(qs[g].astype(BF16), wp_ref[g].astype(BF16), preferred_element_type=F32)
            mixed_ref[:, cols] = (e * ps_ref[:, cols]).astype(BF16)
        uv = _gelu(p_ref[:, pl.ds(COL_B, 2 * SGU_WIDTH)])
        u = uv[:, :SGU_WIDTH]
        vhat, _ = _ln_stats(uv[:, SGU_WIDTH:])
        vn = (vhat * slg_ref[...] + slb_ref[...]).astype(BF16)
        for h in range(SGU_HEADS):
            wm = _masked_sgu_w(ws_ref, h).astype(BF16)
            for n in range(ts // CHUNK):
                rows = slice(n * CHUNK, (n + 1) * CHUNK)
                cols = slice(h * GROUP, (h + 1) * GROUP)
                mx = jnp.dot(wm, vn[rows, cols], preferred_element_type=F32) + sb_ref[h]
                mixed_ref[pl.ds(n * CHUNK, CHUNK), pl.ds(POOL_WIDTH + h * GROUP, GROUP)] = (u[rows, cols] * mx).astype(BF16)
        hh_ext[pl.ds(0, HALO), :] = (ph_ref[:, pl.ds(COL_C, CONV_WIDTH)]
                                     * _sigmoid(ph_ref[:, pl.ds(COL_C + CONV_WIDTH, CONV_WIDTH)])) * keep
        hh_ext[pl.ds(HALO, ts), :] = p_ref[:, pl.ds(COL_C, CONV_WIDTH)] * _sigmoid(p_ref[:, pl.ds(COL_C + CONV_WIDTH, CONV_WIDTH)])
        init = lambda cg: jnp.broadcast_to(cb_ref[:, pl.ds(cg * GROUP, GROUP)], (RB_CONV, GROUP))
        _fill_shifted(hh_ext, shifted, ts)
        def store_cv(r0, c0, blk):
            cv_ref[pl.ds(r0, RB_CONV), pl.ds(c0, GROUP)] = blk

        _conv_taps(init, cw_ref, hh_ext, shifted, ts, lambda k: k + HALO - (CONV_KERNEL - 1), store_cv)
        cvhat, _ = _ln_stats(cv_ref[...])
        cn = cvhat * clg_ref[...] + clb_ref[...]
        mixed_ref[:, pl.ds(POOL_WIDTH + SGU_WIDTH, CONV_WIDTH)] = (cn * _sigmoid(cn)).astype(BF16)

    full = lambda a: pl.BlockSpec(a.shape, lambda i: (0,) * a.ndim)
    weights = [lw["w_pool"], _row(lw["pool_scale"]), _row(lw["sgu_ln_g"]), _row(lw["sgu_ln_b"]), lw["sgu_w"],
               lw["sgu_bias_tile"], lw["conv_w_full"], _row(lw["conv_b"]), _row(lw["conv_ln_g"]), _row(lw["conv_ln_b"])]
    return pl.pallas_call(
        body, name=name, grid=(s_len // ts,),
        in_specs=[pl.BlockSpec((ts, IN_WIDTH), lambda i: (i, 0)),
                  pl.BlockSpec((HALO, IN_WIDTH), lambda i: (jnp.maximum(i * per_halo - 1, 0), 0))] + [full(a) for a in weights],
        out_specs=[pl.BlockSpec((ts, d), lambda i: (i, 0)), pl.BlockSpec((ts, CONV_WIDTH), lambda i: (i, 0))],
        out_shape=[jax.ShapeDtypeStruct((s_len, d), BF16), jax.ShapeDtypeStruct((s_len, CONV_WIDTH), F32)],
        scratch_shapes=[pltpu.VMEM((16 + ts, POOL_WIDTH), F32), pltpu.VMEM((HALO + ts, CONV_WIDTH), F32),
                        pltpu.VMEM((SUBLANES - 1, ts + HALO - SUBLANES, CONV_WIDTH), F32)],
        compiler_params=_params(1),
    )(p, p, *weights)


def _mix_backward(name, p, cv, dmixed, lw):
    s_len = p.shape[0]
    ts = _tile(s_len, TS_MIX)
    nt = s_len // ts
    per_halo = ts // HALO
    d = POOL_WIDTH + SGU_WIDTH + CONV_WIDTH
    nch = ts // CHUNK
    col_yc = POOL_WIDTH + SGU_WIDTH

    def body(p_ref, ph_ref, cv_ref, cvn_ref, dm_ref, dmn_ref, wp_ref, ps_ref, slg_ref, slb_ref, ws_ref, wst_ref, sb_ref,
             cw_ref, cwr_ref, clg_ref, clb_ref,
             dp_ref, dbin_ref, dwp_ref, dps_ref, dslg_ref, dslb_ref, dws_ref, dsb_ref, dcw_ref, dcb_ref, dclg_ref, dclb_ref,
             pool_ext, dq_ext, hh_ext, dcv_ext, dcw_acc, shifted):
        i = pl.program_id(0)
        first = i == 0
        keep_prev = (i > 0).astype(F32)
        keep_next = (i < nt - 1).astype(F32)

        pool_ext[pl.ds(0, 16), :] = ph_ref[pl.ds(16, 16), pl.ds(0, POOL_WIDTH)] * keep_prev
        pool_ext[pl.ds(16, ts), :] = p_ref[:, pl.ds(0, POOL_WIDTH)]
        qs = _pool_means(pool_ext, ts, i)
        t_ext = i * ts + lax.broadcasted_iota(jnp.int32, (ts + HALO, GROUP), 0)
        for g, win in enumerate(POOL_WINDOWS):
            cols = pl.ds(g * GROUP, GROUP)
            wpb = wp_ref[g].astype(BF16)
            qb = qs[g].astype(BF16)
            de = dm_ref[:, cols] * ps_ref[:, cols]
            e = jnp.dot(qb, wpb, preferred_element_type=F32)
            _accumulate(dps_ref.at[:, cols], _colsum(dm_ref[:, cols] * e), first)
            deb = de.astype(BF16)
            _accumulate(dwp_ref.at[g], lax.dot_general(qb, deb, (((0,), (0,)), ((), ())), preferred_element_type=F32), first)
            de_next = (dmn_ref[:, cols] * ps_ref[:, cols] * keep_next).astype(BF16)
            de_all = jnp.concatenate([deb, de_next], axis=0)
            dq = lax.dot_general(de_all, wpb, (((1,), (1,)), ((), ())), preferred_element_type=F32)
            inv = 1.0 / jnp.minimum(t_ext + 1, win).astype(F32)
            dq_ext[:, cols] = dq * inv
            acc = dq_ext[pl.ds(0, ts), cols]
            for j in range(1, win):
                acc = acc + dq_ext[pl.ds(j, ts), cols]
            dpa = acc - dq[:ts]
            dp_ref[:, cols] = dpa.astype(BF16)
            _accumulate(dbin_ref.at[:, cols], _colsum(dpa), first)

        pb = p_ref[:, pl.ds(COL_B, 2 * SGU_WIDTH)]
        uv, duv = _gelu_with_grad(pb)
        u = uv[:, :SGU_WIDTH]
        vhat, vrstd = _ln_stats(uv[:, SGU_WIDTH:])
        vn = (vhat * slg_ref[...] + slb_ref[...]).astype(BF16)
        dyb = dm_ref[:, pl.ds(POOL_WIDTH, SGU_WIDTH)]
        dmix = dyb * u
        dmixb = dmix.astype(BF16)
        du_parts, dvn_parts = [], []
        for h in range(SGU_HEADS):
            cols = slice(h * GROUP, (h + 1) * GROUP)
            wm = _masked_sgu_w(ws_ref, h).astype(BF16)
            wmt = _masked_sgu_w_t(wst_ref, h).astype(BF16)
            dws_h = jnp.zeros((CHUNK, CHUNK), F32)
            dsb_h = jnp.zeros((CHUNK, GROUP), F32)
            du_rows, dvn_rows = [], []
            for n in range(nch):
                rows = slice(n * CHUNK, (n + 1) * CHUNK)
                mx = jnp.dot(wm, vn[rows, cols], preferred_element_type=F32) + sb_ref[h]
                du_rows.append(dyb[rows, cols] * mx)
                dws_h = dws_h + lax.dot_general(dmixb[rows, cols], vn[rows, cols], (((1,), (1,)), ((), ())),
                                                preferred_element_type=F32)
                dsb_h = dsb_h + dmix[rows, cols]
                dvn_rows.append(jnp.dot(wmt, dmixb[rows, cols], preferred_element_type=F32))
            r = lax.broadcasted_iota(jnp.int32, (CHUNK, CHUNK), 0)
            c = lax.broadcasted_iota(jnp.int32, (CHUNK, CHUNK), 1)
            _accumulate(dws_ref.at[h], jnp.where(r >= c, dws_h, 0.0), first)
            _accumulate(dsb_ref.at[h], jnp.broadcast_to(jnp.sum(dsb_h, axis=1, keepdims=True), (CHUNK, GROUP)), first)
            du_parts.append(jnp.concatenate(du_rows, axis=0))
            dvn_parts.append(jnp.concatenate(dvn_rows, axis=0))
        du = jnp.concatenate(du_parts, axis=1)
        dvn = jnp.concatenate(dvn_parts, axis=1)
        _accumulate(dslg_ref, _colsum(dvn * vhat), first)
        _accumulate(dslb_ref, _colsum(dvn), first)
        dv = _ln_bwd(dvn, vhat, vrstd, slg_ref[...])
        dpb = jnp.concatenate([du, dv], axis=1) * duv
        dp_ref[:, pl.ds(COL_B, 2 * SGU_WIDTH)] = dpb.astype(BF16)
        _accumulate(dbin_ref.at[:, pl.ds(COL_B, 2 * SGU_WIDTH)], _colsum(dpb), first)

        a_main = p_ref[:, pl.ds(COL_C, CONV_WIDTH)]
        sg_main = _sigmoid(p_ref[:, pl.ds(COL_C + CONV_WIDTH, CONV_WIDTH)])
        hh_ext[pl.ds(HALO, ts), :] = a_main * sg_main

        def conv_ln_backward(cv_v, dyc_v):
            cvhat, crstd = _ln_stats(cv_v)
            cn = cvhat * clg_ref[...] + clb_ref[...]
            s = _sigmoid(cn)
            dcn = dyc_v * (s * (1.0 + cn * (1.0 - s)))
            return _ln_bwd(dcn, cvhat, crstd, clg_ref[...]), dcn, cvhat

        dcv, dcn, cvhat = conv_ln_backward(cv_ref[...], dm_ref[:, pl.ds(col_yc, CONV_WIDTH)])
        _accumulate(dclg_ref, _colsum(dcn * cvhat), first)
        _accumulate(dclb_ref, _colsum(dcn), first)
        _accumulate(dcb_ref, _colsum(dcv), first)
        dcv_next, _, _ = conv_ln_backward(cvn_ref[...], dmn_ref[:, pl.ds(col_yc, CONV_WIDTH)])
        dcv_ext[pl.ds(0, ts), :] = dcv
        dcv_ext[pl.ds(ts, HALO), :] = dcv_next * keep_next

        @pl.when(first)
        def _():
            dcw_acc[...] = jnp.zeros_like(dcw_acc)

        nrb = ts // RB_CONV
        _fill_shifted(dcv_ext, shifted, ts)
        for cg in range(CONV_WIDTH // GROUP):
            cols = pl.ds(cg * GROUP, GROUP)
            for rb0 in range(0, nrb, CONV_LIVE_BLOCKS):
                group = range(rb0, min(rb0 + CONV_LIVE_BLOCKS, nrb))
                hblk = {rb: hh_ext[pl.ds(HALO + rb * RB_CONV, RB_CONV), cols] for rb in group}
                for k in range(CONV_KERNEL):
                    part = jnp.zeros((8, GROUP), F32)
                    for rb in group:
                        offset = rb * RB_CONV + CONV_KERNEL - 1 - k
                        prod = hblk[rb] * _shifted_rows(dcv_ext, shifted, offset, RB_CONV, cols)
                        part = part + jnp.sum(prod.reshape(RB_CONV // 8, 8, GROUP), axis=0)
                    dcw_acc[k, :, cols] += part

        @pl.when(i == nt - 1)
        def _():
            dcw_ref[...] = jnp.sum(dcw_acc[...], axis=1)

        zero = lambda cg: jnp.zeros((RB_CONV, GROUP), F32)
        def glu_backward(r0, c0, blk):
            rows, cols = pl.ds(r0, RB_CONV), pl.ds(c0, GROUP)
            a_blk = p_ref[rows, pl.ds(COL_C + c0, GROUP)]
            s_blk = _sigmoid(p_ref[rows, pl.ds(COL_C + CONV_WIDTH + c0, GROUP)])
            da = blk * s_blk
            dg = blk * a_blk * s_blk * (1.0 - s_blk)
            dp_ref[rows, pl.ds(COL_C + c0, GROUP)] = da.astype(BF16)
            dp_ref[rows, pl.ds(COL_C + CONV_WIDTH + c0, GROUP)] = dg.astype(BF16)
            hh_ext[pl.ds(HALO + r0, RB_CONV), cols] = da
            dcv_ext[rows, cols] = dg

        _conv_taps(zero, cwr_ref, dcv_ext, shifted, ts, lambda k: k, glu_backward)
        _accumulate(dbin_ref.at[:, pl.ds(COL_C, CONV_WIDTH)], _colsum(hh_ext[pl.ds(HALO, ts), :]), first)
        _accumulate(dbin_ref.at[:, pl.ds(COL_C + CONV_WIDTH, CONV_WIDTH)], _colsum(dcv_ext[pl.ds(0, ts), :]), first)

    full = lambda a: pl.BlockSpec(a.shape, lambda i: (0,) * a.ndim)
    weights = [lw["w_pool"], _row(lw["pool_scale"]), _row(lw["sgu_ln_g"]), _row(lw["sgu_ln_b"]), lw["sgu_w"], lw["sgu_w_t"],
               lw["sgu_bias_tile"], lw["conv_w_full"], lw["conv_w_rev"], _row(lw["conv_ln_g"]), _row(lw["conv_ln_b"])]
    prev_halo = lambda i: (jnp.maximum(i * per_halo - 1, 0), 0)
    next_halo = lambda i: (jnp.minimum((i + 1) * per_halo, s_len // HALO - 1), 0)
    small = lambda shape: (jax.ShapeDtypeStruct(shape, F32), pl.BlockSpec(shape, lambda i: (0,) * len(shape)))
    outs = [(jax.ShapeDtypeStruct((s_len, IN_WIDTH), BF16), pl.BlockSpec((ts, IN_WIDTH), lambda i: (i, 0))),
            small((1, IN_WIDTH)), small((len(POOL_WINDOWS), GROUP, GROUP)), small((1, POOL_WIDTH)),
            small((1, SGU_WIDTH)), small((1, SGU_WIDTH)), small((SGU_HEADS, CHUNK, CHUNK)), small((SGU_HEADS, CHUNK, GROUP)),
            small((CONV_ROWS, CONV_WIDTH)), small((1, CONV_WIDTH)), small((1, CONV_WIDTH)), small((1, CONV_WIDTH))]
    return pl.pallas_call(
        body, name=name, grid=(nt,),
        in_specs=[pl.BlockSpec((ts, IN_WIDTH), lambda i: (i, 0)), pl.BlockSpec((HALO, IN_WIDTH), prev_halo),
                  pl.BlockSpec((ts, CONV_WIDTH), lambda i: (i, 0)), pl.BlockSpec((HALO, CONV_WIDTH), next_halo),
                  pl.BlockSpec((ts, d), lambda i: (i, 0)), pl.BlockSpec((HALO, d), next_halo)] + [full(a) for a in weights],
        out_specs=[s for _, s in outs],
        out_shape=[o for o, _ in outs],
        scratch_shapes=[pltpu.VMEM((16 + ts, POOL_WIDTH), F32), pltpu.VMEM((ts + HALO, POOL_WIDTH), F32),
                        pltpu.VMEM((HALO + ts, CONV_WIDTH), F32), pltpu.VMEM((ts + HALO, CONV_WIDTH), F32),
                        pltpu.VMEM((CONV_ROWS, 8, CONV_WIDTH), F32),
                        pltpu.VMEM((SUBLANES - 1, ts + HALO - SUBLANES, CONV_WIDTH), F32)],
        compiler_params=_params(1),
    )(p, p, cv, cv, dmixed, dmixed, *weights)


def _masked_sgu_w_t(wt_ref, h):
    r = lax.broadcasted_iota(jnp.int32, (CHUNK, CHUNK), 0)
    c = lax.broadcasted_iota(jnp.int32, (CHUNK, CHUNK), 1)
    return jnp.where(c >= r, wt_ref[h], 0.0)


HBM = pl.BlockSpec(memory_space=pltpu.HBM)
CHIP_FLIPS = ((1, 0), (0, 1), (1, 1))


def _place():
    return lax.axis_index("x"), lax.axis_index("y"), lax.axis_index("c")


def _half(ref, axis, which, size):
    idx = [slice(None)] * len(ref.shape)
    idx[axis] = pl.ds(which * size, size)
    return ref.at[tuple(idx)]


def _exchange(name, sources, inplace, fresh, copies):
    n_src, n_in, n = len(sources), len(inplace), len(copies)
    n_out = n_in + len(fresh)

    def body(*refs):
        ins = refs[:n_src + n_in]
        outs = refs[n_src + n_in:n_src + n_in + n_out]
        send_sems, recv_sems = refs[n_src + n_in + n_out:]
        started = _descriptors(copies, list(ins[:n_src]) + list(outs), send_sems, recv_sems)
        for cp in started:
            cp.start()
        for cp in started:
            cp.wait()

    out_shape = [jax.ShapeDtypeStruct(a.shape, a.dtype) for a in inplace] + list(fresh)
    return pl.pallas_call(
        body, name=name, in_specs=[HBM] * (n_src + n_in), out_specs=[HBM] * n_out, out_shape=out_shape,
        input_output_aliases={n_src + i: i for i in range(n_in)},
        scratch_shapes=[pltpu.SemaphoreType.DMA((n,)), pltpu.SemaphoreType.DMA((n,))],
    )(*sources, *inplace)


def _descriptors(copies, bufs, send_sems, recv_sems):
    x, y, c = _place()
    return [pltpu.make_async_remote_copy(
        src_ref=src_view(bufs[src], x, y, c), dst_ref=dst_view(bufs[dst], x, y, c),
        send_sem=send_sems.at[k], recv_sem=recv_sems.at[k], device_id=peer(x, y, c), device_id_type=MESH)
        for k, (src, src_view, dst, dst_view, peer) in enumerate(copies)]


SEM = pl.BlockSpec(memory_space=pltpu.SEMAPHORE)
IN_FLIGHT = pltpu.CompilerParams(has_side_effects=pltpu.SideEffectType.DATAFLOW_SIDE_EFFECTING)


def _exchange_start(name, sources, inplace, fresh, copies):
    n_src, n = len(sources), len(copies)
    landing = [lax.empty(f.shape, f.dtype) for f in fresh]
    bufs = [pltpu.with_memory_space_constraint(a, pltpu.HBM) for a in (*inplace, *landing)]
    srcs = [pltpu.with_memory_space_constraint(a, pltpu.HBM) for a in sources]
    n_buf = len(bufs)

    def body(*refs):
        ins = refs[:n_src]
        send_sems, recv_sems = refs[n_src + n_buf], refs[n_src + n_buf + 1]
        outs = refs[n_src + n_buf + 2:n_src + 2 * n_buf + 2]
        token = refs[n_src + 2 * n_buf + 2]
        for cp in _descriptors(copies, list(ins) + list(outs), send_sems, recv_sems):
            cp.start()
        token[...] = jnp.zeros_like(token)

    res = pl.pallas_call(
        body, name=name + "_start",
        out_shape=(pltpu.SemaphoreType.DMA((n,)), pltpu.SemaphoreType.DMA((n,)),
                   *[pltpu.HBM(b.shape, b.dtype) for b in bufs], jax.ShapeDtypeStruct((8, 128), F32)),
        in_specs=[HBM] * (n_src + n_buf),
        out_specs=(SEM, SEM, *[HBM] * n_buf, pl.BlockSpec(memory_space=pltpu.VMEM)),
        input_output_aliases={n_src + i: 2 + i for i in range(n_buf)},
        compiler_params=IN_FLIGHT,
    )(*srcs, *bufs)
    return dict(name=name, sources=srcs, sems=res[:2], bufs=list(res[2:2 + n_buf]), token=res[2 + n_buf], copies=copies)


def _exchange_wait(handle, after):
    srcs, bufs, copies = handle["sources"], handle["bufs"], handle["copies"]
    n_src, n_buf = len(srcs), len(bufs)

    def body(*refs):
        ins = refs[:n_src]
        send_sems, recv_sems = refs[n_src + n_buf], refs[n_src + n_buf + 1]
        outs = refs[n_src + n_buf + 3:]
        for cp in _descriptors(copies, list(ins) + list(outs), send_sems, recv_sems):
            cp.wait()

    return list(pl.pallas_call(
        body, name=handle["name"] + "_wait",
        out_shape=tuple(pltpu.HBM(b.shape, b.dtype) for b in bufs),
        in_specs=[HBM] * (n_src + n_buf) + [SEM, SEM, pl.BlockSpec(memory_space=pl.ANY)],
        out_specs=tuple([HBM] * n_buf),
        input_output_aliases={n_src + i: i for i in range(n_buf)},
        compiler_params=IN_FLIGHT,
    )(*srcs, *bufs, *handle["sems"], after))


def _into_slot(name, src, layer, place, dtype, after=None):
    _, r, c = src.shape
    br = _tile(r, BR_EW) if r % BR_EW == 0 else r
    order = [] if after is None else [after]

    def body(pr_ref, s_ref, *rest):
        rest[-1][...] = s_ref[...].astype(dtype)

    return pl.pallas_call(
        body, name=name,
        grid_spec=pltpu.PrefetchScalarGridSpec(
            num_scalar_prefetch=1, grid=(r // br,),
            in_specs=[pl.BlockSpec((None, br, c), lambda i, pr: (layer, i, 0))] + [pl.BlockSpec(memory_space=pl.ANY)] * len(order),
            out_specs=pl.BlockSpec((None, br, c), lambda i, pr: (pr[1], i, 0))),
        out_shape=jax.ShapeDtypeStruct((4, r, c), dtype), compiler_params=_params(1),
    )(place, src, *order)


def _gather_group(name, bufs):
    ici, d2d = _gather_copies(len(bufs))
    bufs = _exchange(name + "_chips", [], bufs, [], ici)
    return _exchange(name + "_sibling", [], bufs, [], d2d)


def _gather_copies(n_bufs):
    def own_half(ref, x, y, c):
        return _half(ref.at[2 * x + y], 0, c, ref.shape[1] // 2)

    ici, d2d = [], []
    for b in range(n_bufs):
        for fx, fy in CHIP_FLIPS:
            ici.append((b, own_half, b, own_half, lambda x, y, c, fx=fx, fy=fy: (x ^ fx, y ^ fy, c)))
            landed = lambda ref, x, y, c, fx=fx, fy=fy: _half(ref.at[2 * (x ^ fx) + (y ^ fy)], 0, c, ref.shape[1] // 2)
            d2d.append((b, landed, b, landed, _sibling))
    return ici, d2d


def _gather_start(name, bufs):
    return _exchange_start(name + "_chips", [], bufs, [], _gather_copies(len(bufs))[0])


def _gather_finish(name, handle, after):
    bufs = _exchange_wait(handle, after)
    return _exchange(name + "_sibling", [], bufs, [], _gather_copies(len(bufs))[1])


def _gather_pass_on(name, handle, after):
    bufs = _exchange_wait(handle, after)
    return _exchange_start(name + "_sibling", [], bufs, [], _gather_copies(len(bufs))[1])


def _sibling(x, y, c):
    return (x, y, 1 - c)


def _scalar_spec_call(name, fn, scalars, ins, in_blocks, out_shapes, out_blocks, grid):
    def body(s_ref, *refs):
        res = fn(*[r[...] for r in refs[:len(ins)]])
        for o_ref, v in zip(refs[len(ins):], res):
            o_ref[...] = v.astype(o_ref.dtype)

    return pl.pallas_call(
        body, name=name,
        grid_spec=pltpu.PrefetchScalarGridSpec(num_scalar_prefetch=1, grid=grid, in_specs=in_blocks, out_specs=out_blocks),
        out_shape=out_shapes, compiler_params=_params(len(grid)),
    )(scalars, *ins)


def _reduce_begin(tag, place, items):
    n = len(items)
    whole = lambda ref, x, y, c: ref
    halves = [g.shape[1] // 2 for g, _, _ in items]
    fresh = [jax.ShapeDtypeStruct((g.shape[0], halves[a], g.shape[2]), F32) for a, (g, _, _) in enumerate(items)]
    copies = [(a, lambda ref, x, y, c, hr=halves[a]: _half(ref, 1, 1 - c, hr), n + a, whole, _sibling) for a in range(n)]
    handle = _exchange_start(tag + "_sibling_in", [g for g, _, _ in items], [], fresh, copies)
    return dict(tag=tag, place=place, items=items, handle=handle, token=handle["token"])


def _reduce_between_chips(state, after):
    tag, place, items = state["tag"], state["place"], state["items"]
    n = len(items)
    slot_of = lambda scatter: (lambda x, y: 2 * x + y) if scatter else (lambda x, y: 0)
    halves = [g.shape[1] // 2 for g, _, _ in items]
    landed = _exchange_wait(state["handle"], after)

    chip_sums = []
    for a, (g, _, wire) in enumerate(items):
        ns, _, cols = g.shape
        hr = halves[a]
        br = _tile(hr, BR_EW)
        chip_sums.append(_scalar_spec_call(
            f"{tag}_chip_sum{a}", lambda u, v: (u + v,), place, [g, landed[a]],
            [pl.BlockSpec((None, br, cols), lambda s, i, pr, hb=hr // br: (s, pr[0] * hb + i, 0)),
             pl.BlockSpec((None, br, cols), lambda s, i, pr: (s, i, 0))],
            [jax.ShapeDtypeStruct((ns, hr, cols), wire)], [pl.BlockSpec((None, br, cols), lambda s, i, pr: (s, i, 0))],
            (ns, hr // br))[0])

    copies, fresh = [], []
    for a, (g, scatter, wire) in enumerate(items):
        slot = slot_of(scatter)
        fresh.append(jax.ShapeDtypeStruct((3, halves[a], g.shape[2]), wire))
        for j, (fx, fy) in enumerate(CHIP_FLIPS):
            copies.append((a, lambda ref, x, y, c, fx=fx, fy=fy, slot=slot: ref.at[slot(x ^ fx, y ^ fy)],
                           n + a, lambda ref, x, y, c, j=j: ref.at[j], lambda x, y, c, fx=fx, fy=fy: (x ^ fx, y ^ fy, c)))
    handle = _exchange_start(tag + "_chips", chip_sums, [], fresh, copies)
    return dict(tag=tag, place=place, items=items, chip_sums=chip_sums, handle=handle, token=handle["token"])


def _reduce_swap(state, after):
    tag, place, items, chip_sums = state["tag"], state["place"], state["items"], state["chip_sums"]
    n = len(items)
    whole = lambda ref, x, y, c: ref
    halves = [g.shape[1] // 2 for g, _, _ in items]
    arrived = _exchange_wait(state["handle"], after)

    tree = lambda own, fx, fy, fxy: ((own.astype(F32) + fx.astype(F32)) + (fy.astype(F32) + fxy.astype(F32)),)
    mine = []
    for a, (g, scatter, _) in enumerate(items):
        hr, cols = halves[a], g.shape[2]
        br = _tile(hr, BR_EW)
        got = lambda j: pl.BlockSpec((None, br, cols), lambda i, pr, j=j: (j, i, 0))
        own = pl.BlockSpec((None, br, cols), (lambda i, pr: (pr[1], i, 0)) if scatter else (lambda i, pr: (0, i, 0)))
        mine.append(_scalar_spec_call(
            f"{tag}_tree_sum{a}", tree, place, [chip_sums[a], arrived[a], arrived[a], arrived[a]],
            [own, got(0), got(1), got(2)],
            [jax.ShapeDtypeStruct((hr, cols), F32)], [pl.BlockSpec((br, cols), lambda i, pr: (i, 0))], (hr // br,))[0])

    copies = [(a, whole, n + a, whole, _sibling) for a in range(n)]
    fresh = [jax.ShapeDtypeStruct(h.shape, F32) for h in mine]
    handle = _exchange_start(tag + "_sibling_out", mine, [], fresh, copies)
    return dict(mine=mine, handle=handle, token=handle["token"])


def _reduce_end(state, after):
    return list(zip(state["mine"], _exchange_wait(state["handle"], after)))


def _select_half(h, c, mine, theirs):
    return jnp.where(h == c, mine, theirs)


def _adamw_sharded(name, layer, place, w, m, v, g_mine, g_theirs, earlier):
    n_layers, r, c = w.shape
    hr = r // 2
    br = min(hr, max(8, 2 ** int(math.log2(EW_BLOCK_ELEMS // c))))
    assert hr % br == 0
    nb = hr // br
    full = pl.BlockSpec((None, br, c), lambda h, i, pr: (layer, h * nb + i, 0))
    half = pl.BlockSpec((br, c), lambda h, i, pr: (i, 0))
    n_alias = 0 if earlier is None else 4

    def body(pr_ref, w_ref, m_ref, v_ref, gm_ref, gt_ref, *rest):
        g_ref, d_ref, nm_ref, nv_ref = rest[n_alias:]
        g = _select_half(pl.program_id(0), pr_ref[0], gm_ref[...], gt_ref[...])
        delta, new_m, new_v = _adamw_math(w_ref[...], g, m_ref[...], v_ref[...])
        g_ref[...] = g
        d_ref[...] = delta
        nm_ref[...] = new_m
        nv_ref[...] = new_v

    return pl.pallas_call(
        body, name=name,
        grid_spec=pltpu.PrefetchScalarGridSpec(
            num_scalar_prefetch=1, grid=(2, nb),
            in_specs=[full, full, full, half, half] + [pl.BlockSpec(memory_space=pl.ANY)] * n_alias,
            out_specs=[full] * 4),
        out_shape=[jax.ShapeDtypeStruct((n_layers, r, c), F32)] * 4,
        input_output_aliases={6 + i: i for i in range(n_alias)},
        compiler_params=_params(2),
    )(place, w, m, v, g_mine, g_theirs, *(earlier or ()))


SHARDED = ("w_in", "w_out", "w_ff1", "w_ff2")
REPLICATED = ("b_in", "w_pool", "pool_scale", "sgu_ln_g", "sgu_ln_b", "sgu_w", "sgu_b", "conv_b", "conv_ln_g", "conv_ln_b",
              "b_out", "ln1_g", "ln1_b", "b_ff1", "b_ff2", "ln2_g", "ln2_b")
WEIGHTS = ("w_in", "b_in", "w_pool", "pool_scale", "sgu_ln_g", "sgu_ln_b", "sgu_w", "sgu_b", "conv_w", "conv_b", "conv_ln_g",
           "conv_ln_b", "w_out", "b_out", "ln1_g", "ln1_b", "w_ff1", "b_ff1", "w_ff2", "b_ff2", "ln2_g", "ln2_b")
PACK_ROWS = 1024


def _pack(arrays):
    flat = jnp.concatenate([a.reshape(-1) for a in arrays])
    rows = -(-flat.shape[0] // (128 * PACK_ROWS)) * PACK_ROWS
    return jnp.pad(flat, (0, rows * 128 - flat.shape[0])).reshape(rows, 128)


def _unpack(packed, like):
    flat = packed.reshape(-1)
    out, at = [], 0
    for a in like:
        out.append(flat[at:at + a.size].reshape(a.shape))
        at += a.size
    return out


def kernel(x, w_in, b_in, w_pool, pool_scale, sgu_ln_g, sgu_ln_b, sgu_w, sgu_b, conv_w, conv_b, conv_ln_g, conv_ln_b, w_out, b_out, ln1_g, ln1_b, w_ff1, b_ff1, w_ff2, b_ff2, ln2_g, ln2_b, loss_target, m_w_in, m_b_in, m_w_pool, m_pool_scale, m_sgu_ln_g, m_sgu_ln_b, m_sgu_w, m_sgu_b, m_conv_w, m_conv_b, m_conv_ln_g, m_conv_ln_b, m_w_out, m_b_out, m_ln1_g, m_ln1_b, m_w_ff1, m_b_ff1, m_w_ff2, m_b_ff2, m_ln2_g, m_ln2_b, v_w_in, v_b_in, v_w_pool, v_pool_scale, v_sgu_ln_g, v_sgu_ln_b, v_sgu_w, v_sgu_b, v_conv_w, v_conv_b, v_conv_ln_g, v_conv_ln_b, v_w_out, v_b_out, v_ln1_g, v_ln1_b, v_w_ff1, v_b_ff1, v_w_ff2, v_b_ff2, v_ln2_g, v_ln2_b):
    w = dict(w_in=w_in, b_in=b_in, w_pool=w_pool, pool_scale=pool_scale, sgu_ln_g=sgu_ln_g, sgu_ln_b=sgu_ln_b, sgu_w=sgu_w,
             sgu_b=sgu_b, conv_w=conv_w, conv_b=conv_b, conv_ln_g=conv_ln_g, conv_ln_b=conv_ln_b, w_out=w_out, b_out=b_out,
             ln1_g=ln1_g, ln1_b=ln1_b, w_ff1=w_ff1, b_ff1=b_ff1, w_ff2=w_ff2, b_ff2=b_ff2, ln2_g=ln2_g, ln2_b=ln2_b)
    m = dict(w_in=m_w_in, b_in=m_b_in, w_pool=m_w_pool, pool_scale=m_pool_scale, sgu_ln_g=m_sgu_ln_g, sgu_ln_b=m_sgu_ln_b,
             sgu_w=m_sgu_w, sgu_b=m_sgu_b, conv_w=m_conv_w, conv_b=m_conv_b, conv_ln_g=m_conv_ln_g, conv_ln_b=m_conv_ln_b,
             w_out=m_w_out, b_out=m_b_out, ln1_g=m_ln1_g, ln1_b=m_ln1_b, w_ff1=m_w_ff1, b_ff1=m_b_ff1, w_ff2=m_w_ff2,
             b_ff2=m_b_ff2, ln2_g=m_ln2_g, ln2_b=m_ln2_b)
    v = dict(w_in=v_w_in, b_in=v_b_in, w_pool=v_w_pool, pool_scale=v_pool_scale, sgu_ln_g=v_sgu_ln_g, sgu_ln_b=v_sgu_ln_b,
             sgu_w=v_sgu_w, sgu_b=v_sgu_b, conv_w=v_conv_w, conv_b=v_conv_b, conv_ln_g=v_conv_ln_g, conv_ln_b=v_conv_ln_b,
             w_out=v_w_out, b_out=v_b_out, ln1_g=v_ln1_g, ln1_b=v_ln1_b, w_ff1=v_w_ff1, b_ff1=v_b_ff1, w_ff2=v_w_ff2,
             b_ff2=v_b_ff2, ln2_g=v_ln2_g, ln2_b=v_ln2_b)
    assert x.shape[0] == 1 and x.shape[2] == POOL_WIDTH + SGU_WIDTH + CONV_WIDTH, x.shape
    xs, target = x[0], loss_target[0]
    s_len, d = xs.shape
    n_layers = w_in.shape[0]
    dff = 4 * w_ff1.shape[2]
    conv_shard = conv_w.shape[2]

    cx, cy, cc = _place()
    me = 2 * cx + cy
    place = jnp.stack([cc, me]).astype(jnp.int32)

    def cast(key, after=None):
        return _into_slot(f"cast_{key[0]}{key[1]}", w[key[0]], key[1], place, BF16, after=after)

    mix0 = [("w_in", 0), ("w_out", 0), "conv_w"]
    ff0 = [("w_ff1", 0), ("w_ff2", 0)]
    conv_padded = jnp.pad(conv_w, ((0, 0), (0, CONV_ROWS - CONV_KERNEL), (0, 0))).reshape(1, n_layers * CONV_ROWS, conv_shard)
    slot = {"conv_w": _into_slot("slot_conv_w", conv_padded, 0, place, F32), ("w_in", 0): cast(("w_in", 0)),
            ("w_out", 0): cast(("w_out", 0))}
    mix0_flight = _gather_start("gather_mix0", [slot[k] for k in mix0])
    slot.update({k: cast(k, mix0_flight["token"]) for k in ff0})
    ff0_flight = _gather_start("gather_ff0", [slot[k] for k in ff0])
    slot.update({(n, l): cast((n, l), ff0_flight["token"]) for n in SHARDED for l in range(1, n_layers)})
    xb0 = _rowwise("cast_x", lambda t: (t,), [xs], [BF16])[0]
    gathered = dict(zip(mix0, _gather_finish("gather_mix0", mix0_flight, slot[(SHARDED[-1], n_layers - 1)])))
    conv_full = jnp.transpose(gathered["conv_w"].reshape(4, n_layers, CONV_ROWS, conv_shard), (1, 2, 0, 3))
    conv_full = conv_full.reshape(n_layers, CONV_ROWS, CONV_WIDTH)
    conv_rev = jnp.pad(conv_full[:, CONV_KERNEL - 1::-1], ((0, 0), (0, CONV_ROWS - CONV_KERNEL), (0, 0)))

    def layer_weights(l):
        return dict(w_pool=w_pool[l], pool_scale=pool_scale[l], sgu_ln_g=sgu_ln_g[l], sgu_ln_b=sgu_ln_b[l], sgu_w=sgu_w[l],
                    sgu_w_t=jnp.transpose(sgu_w[l], (0, 2, 1)),
                    sgu_bias_tile=jnp.broadcast_to(sgu_b[l][:, :, None], (SGU_HEADS, CHUNK, GROUP)),
                    conv_w_full=conv_full[l], conv_w_rev=conv_rev[l], conv_b=conv_b[l], conv_ln_g=conv_ln_g[l],
                    conv_ln_b=conv_ln_b[l])

    saved = []
    x_cur, xb_cur = xs, xb0
    for l in range(n_layers):
        lw = layer_weights(l)
        w_out_full = gathered[("w_out", l)].reshape(d, d)
        p = _proj(f"proj{l}", xb_cur, gathered[("w_in", l)], b_in[l], after=ff0_flight["token"] if l == 0 else None)
        mixed, cv = _mix_forward(f"mix_fwd{l}", p, lw)
        r1, x1b = _mix_out(f"mix_out{l}", mixed, w_out_full, b_out[l], x_cur, ln1_g[l], ln1_b[l])
        if l == 0:
            gathered.update(zip(ff0, _gather_finish("gather_ff0", ff0_flight, x1b)))
        nxt = [(n, l + 1) for n in SHARDED] if l + 1 < n_layers else None
        in_flight = _gather_start(f"gather_layer{l + 1}", [slot[k] for k in nxt]) if nxt else None
        w2_full = gathered[("w_ff2", l)].reshape(dff, d)
        hf, zr = _ff1(f"ff1_{l}", x1b, gathered[("w_ff1", l)], b_ff1[l], after=in_flight and in_flight["token"])
        fo = _ff2(f"ff2_{l}", hf, w2_full, b_ff2[l])
        sv = dict(lw=lw, xb_in=xb_cur, p=p, mixed=mixed, cv=cv, r1=r1, x1b=x1b, hf=hf, zr=zr, fo=fo,
                  w_out_full=w_out_full, w2_full=w2_full)
        if nxt:
            in_flight = _gather_pass_on(f"gather_layer{l + 1}", in_flight, fo)
            sv["r2"], x_cur, xb_cur = _resid_ln2(f"ln2_{l}", r1, ln1_g[l], ln1_b[l], fo, ln2_g[l], ln2_b[l],
                                                 after=in_flight["token"])
            gathered.update(zip(nxt, _exchange_wait(in_flight, xb_cur)))
        saved.append(sv)

    grads = {n: [None] * n_layers for n in REPLICATED + ("conv_w",)}
    g_final, delta, new_m, new_v = {}, {}, {}, {}
    results = {n: None for n in SHARDED}
    dx_mm, dx_resid = None, None
    loss_tile = None
    pending = None

    def finish_reduce(begun, after):
        names, layer, state = begun
        reduced = _reduce_end(state, after)
        for n, (mine, theirs) in zip(names, reduced):
            results[n] = _adamw_sharded(f"adamw_{n}{layer}", layer, place, w[n], m[n], v[n], mine, theirs, results[n])
        return reduced

    for l in reversed(range(n_layers)):
        sv = saved[l]
        if dx_mm is None:
            dr2, dr2b, dg2, db2, dbff2, loss_tile = _ln_backward(
                f"ln2_bwd{l}", None, ln2_g[l], last=(sv["r1"], ln1_g[l], ln1_b[l], sv["fo"], ln2_b[l], target))
        else:
            dr2, dr2b, dg2, db2, dbff2 = _ln_backward(f"ln2_bwd{l}", sv["r2"], ln2_g[l], dy=dx_mm, resid=dx_resid)
        dzb, dbff1 = _dff_hidden(f"ff2_bwd{l}", dr2b, sv["w2_full"], sv["zr"])
        dw = {"w_ff2": _matmul_tn(f"dw_ff2_{l}", sv["hf"], dr2b, 4, dff // 4, d, True),
              "w_ff1": _matmul_tn(f"dw_ff1_{l}", sv["x1b"], dzb, 4, d, dff // 4, False)}
        if pending is not None:
            names, layer, state = pending
            pending = (names, layer, _reduce_swap(state, dw["w_ff1"]))
        ff_red = _reduce_begin(f"grads_ff{l}", place, [(dw[n], True, BF16) for n in ("w_ff1", "w_ff2")])
        if pending is not None:
            finish_reduce(pending, ff_red["token"])
            pending = None
        dx1 = _dx_sharded(f"ff1_bwd{l}", dzb, gathered[("w_ff1", l)], after=ff_red["token"])
        ff_red = _reduce_between_chips(ff_red, dx1)
        dr1, dr1b, dg1, db1, dbout = _ln_backward(f"ln1_bwd{l}", sv["r1"], ln1_g[l], dy=dx1, resid=dr2, after=ff_red["token"])
        dmixed = _dmixed(f"mix_out_bwd{l}", dr1b, sv["w_out_full"])
        dw["w_out"] = _matmul_tn(f"dw_out{l}", sv["mixed"], dr1b, 4, d // 4, d, True)
        (dp, dbin, dwp, dps, dslg, dslb, dws, dsb_tile, dcw, dcb, dclg, dclb) = _mix_backward(
            f"mix_bwd{l}", sv["p"], sv["cv"], dmixed, sv["lw"])
        dw["w_in"] = _matmul_tn(f"dw_in{l}", sv["xb_in"], dp, 4, d, IN_WIDTH // 4, False)
        for name, g in (("b_in", dbin), ("w_pool", dwp), ("pool_scale", dps), ("sgu_ln_g", dslg), ("sgu_ln_b", dslb),
                        ("sgu_w", dws), ("sgu_b", dsb_tile[:, :, 0]), ("conv_w", dcw[:CONV_KERNEL]), ("conv_b", dcb),
                        ("conv_ln_g", dclg), ("conv_ln_b", dclb), ("b_out", dbout), ("ln1_g", dg1), ("ln1_b", db1),
                        ("b_ff1", dbff1), ("b_ff2", dbff2), ("ln2_g", dg2), ("ln2_b", db2)):
            grads[name][l] = g.reshape(w[name].shape[1:]) if name != "conv_w" else g

        items = [(dw[n], True, BF16) for n in ("w_in", "w_out")]
        if l == 0:
            small_like = [jnp.stack(grads[n]) for n in REPLICATED + ("conv_w",)]
            items.append((_pack(small_like)[None], False, F32))
        mix_red = _reduce_begin(f"grads_mix{l}", place, items)
        ff_red = _reduce_swap(ff_red, mix_red["token"])
        mix_red = _reduce_between_chips(mix_red, ff_red["token"])
        finish_reduce((("w_ff1", "w_ff2"), l, ff_red), mix_red["token"])
        if l > 0:
            dx_mm, dx_resid = _dx_sharded(f"proj_bwd{l}", dp, gathered[("w_in", l)], after=mix_red["token"]), dr1
            pending = (("w_in", "w_out"), l, mix_red)
        else:
            grad_x = _dx_sharded(f"proj_bwd{l}", dp, gathered[("w_in", l)], dr1, after=mix_red["token"])
            mix_red = _reduce_swap(mix_red, grad_x)
            mine, theirs = finish_reduce((("w_in", "w_out"), l, mix_red), mix_red["token"])[-1]
            grad_x = grad_x[None]
            reduced_small = jnp.where(cc == 0, jnp.concatenate([mine, theirs]), jnp.concatenate([theirs, mine]))

    loss = lax.psum(loss_tile[0, 0], ("x", "y", "c"))
    for n in SHARDED:
        g_final[n], delta[n], new_m[n], new_v[n] = results[n]
    unpacked = _unpack(reduced_small, small_like)
    g_final.update(zip(REPLICATED, unpacked[:-1]))
    g_final["conv_w"] = lax.dynamic_slice_in_dim(unpacked[-1], me * conv_shard, conv_shard, axis=2)

    delta["conv_w"], new_m["conv_w"], new_v["conv_w"] = _adamw("adamw_conv_w", conv_w, g_final["conv_w"], m["conv_w"], v["conv_w"])
    packed = [_pack([t[n] for n in REPLICATED]) for t in (w, g_final, m, v)]
    like = [w[n] for n in REPLICATED]
    for res, packed_out in zip((delta, new_m, new_v), _adamw("adamw_replicated", *packed)):
        res.update(zip(REPLICATED, _unpack(packed_out, like)))

    return (loss, grad_x, *[g_final[n] for n in WEIGHTS], *[delta[n] for n in WEIGHTS],
            *[new_m[n] for n in WEIGHTS], *[new_v[n] for n in WEIGHTS])
```

```python
import functools
import math

import jax
import jax.numpy as jnp
from jax import lax
from jax.experimental import pallas as pl
from jax.experimental.pallas import tpu as pltpu

F32 = jnp.float32
BF16 = jnp.bfloat16
MESH = pl.DeviceIdType.MESH

DEPTH = 2
POOL_WINDOWS = (2, 4, 8, 16)
POOL_WIDTH = 512
GROUP = 128
SGU_WIDTH = 768
SGU_HEADS = 6
CHUNK = 128
CONV_WIDTH = 768
CONV_KERNEL = 31
CONV_ROWS = 32
HALO = 32
COL_B = POOL_WIDTH
COL_C = POOL_WIDTH + 2 * SGU_WIDTH
IN_WIDTH = COL_C + 2 * CONV_WIDTH
ALPHA = (2 * DEPTH) ** 0.25
LN_EPS = 1e-5
ADAM_LR = 0.001
ADAM_B1 = 0.9
ADAM_B2 = 0.999
ADAM_EPS = 1e-08
ADAM_WD = 0.01
ADAM_STEP = 10
GELU_C = math.sqrt(2.0 / math.pi)
GELU_A = 0.044715

V7X_VMEM_BYTES = 64 * 2 ** 20
VMEM_LIMIT = 56 * 2 ** 20

BM_MM = 1024
BN_MM = 1024
BM_LN = 512
BK_MM = 1024
BK_WIDE = 2048
BS_TN = 2048
TN_OUT_BYTES = 8 * 2 ** 20
TS_MIX = 256
RB_CONV = 64
CONV_LIVE_BLOCKS = 2
BR_EW = 512
EW_BLOCK_ELEMS = 2 ** 18


def _tile(n, pref):
    t = min(n, pref)
    assert n % t == 0, (n, pref)
    return t


def _params(n_grid):
    return pltpu.CompilerParams(dimension_semantics=("arbitrary",) * n_grid, vmem_limit_bytes=VMEM_LIMIT)


def _sigmoid(x):
    return 1.0 / (1.0 + jnp.exp(-x))


def _gelu(x):
    return 0.5 * x * (1.0 + jnp.tanh(GELU_C * (x + GELU_A * x * x * x)))


def _gelu_with_grad(x):
    t = jnp.tanh(GELU_C * (x + GELU_A * x * x * x))
    half = 0.5 * (1.0 + t)
    return x * half, half + 0.5 * x * (1.0 - t * t) * GELU_C * (1.0 + 3.0 * GELU_A * x * x)


def _ln_stats(r):
    mu = jnp.mean(r, axis=-1, keepdims=True)
    xc = r - mu
    var = jnp.mean(xc * xc, axis=-1, keepdims=True)
    rstd = lax.rsqrt(var + LN_EPS)
    return xc * rstd, rstd


def _ln_bwd(dy, xhat, rstd, g):
    dxh = dy * g
    m1 = jnp.mean(dxh, axis=-1, keepdims=True)
    m2 = jnp.mean(dxh * xhat, axis=-1, keepdims=True)
    return rstd * (dxh - m1 - xhat * m2)


def _colsum(x):
    return jnp.sum(x, axis=0, keepdims=True)


def _accumulate(ref, val, first):
    @pl.when(first)
    def _():
        ref[...] = val

    @pl.when(jnp.logical_not(first))
    def _():
        ref[...] += val


def _matmul(name, grid, a, a_spec, b, b_spec, *, nt, extras, outs, epilogue, after=None, rider=None):
    ni, nj, nk = grid
    if after is not None:
        extras = list(extras) + [(after, pl.BlockSpec(memory_space=pl.ANY))]
    ne, no = len(extras), len(outs)
    dims = (((1,), (1,)), ((), ())) if nt else (((1,), (0,)), ((), ()))
    r_fn, r_ins, r_outs, r_aliases, r_blocks = rider or (None, [], [], {}, 0)
    assert r_blocks <= ni * nj * nk
    n_ri = len(r_ins)
    n_data = sum(1 for _, bs, _ in r_ins if bs is not None)
    step = lambda i, j, k: (i * nj + j) * nk + k

    def body(*refs):
        a_ref, b_ref = refs[0], refs[1]
        ex = refs[2:2 + ne]
        out_refs = refs[2 + ne + n_ri:2 + ne + n_ri + no]
        ids = (pl.program_id(0), pl.program_id(1), pl.program_id(2))
        part = lax.dot_general(a_ref[...], b_ref[...], dims, preferred_element_type=F32)
        if nk == 1:
            epilogue(part, ex, out_refs, ids)
        else:
            @pl.when(ids[2] == 0)
            def _():
                epilogue(part, ex, out_refs, ids)

            @pl.when(ids[2] > 0)
            def _():
                out_refs[0][...] += part
        if rider is not None:
            @pl.when(step(*ids) < r_blocks)
            def _():
                res = r_fn(*[r[...] for r in refs[2 + ne:2 + ne + n_data]])
                for o_ref, val in zip(refs[2 + ne + n_ri + no:], res):
                    o_ref[...] = val

    def rider_spec(bs, f):
        if bs is None:
            return pl.BlockSpec(memory_space=pl.ANY)
        return pl.BlockSpec(bs, lambda i, j, k: f(jnp.minimum(step(i, j, k), r_blocks - 1)))

    return pl.pallas_call(
        body,
        name=name,
        grid=grid,
        in_specs=[a_spec, b_spec] + [s for _, s in extras] + [rider_spec(bs, f) for _, bs, f in r_ins],
        out_specs=[s for _, s in outs] + [rider_spec(bs, f) for _, bs, f in r_outs],
        out_shape=[o for o, _ in outs] + [o for o, _, _ in r_outs],
        input_output_aliases={2 + ne + i: no + o for i, o in r_aliases.items()},
        compiler_params=_params(3),
    )(a, b, *[e for e, _ in extras], *[arr for arr, _, _ in r_ins])


def _matmul_tn(name, a, b, n_shards, shard_rows, shard_cols, row_sharded):
    s_len, ka = a.shape
    n = b.shape[1]
    assert (n_shards * shard_rows, shard_cols) == (ka, n) if row_sharded else (shard_rows, n_shards * shard_cols) == (ka, n)
    bs = _tile(s_len, BS_TN)
    bka = _tile(shard_rows, 2048)
    bn = next((t for t in (2 * BN_MM, BN_MM) if shard_cols % t == 0 and bka * t * 4 <= TN_OUT_BYTES), shard_cols)
    ni, nj, ns = ka // bka, n // bn, s_len // bs
    per_shard_i = shard_rows // bka
    per_shard_j = shard_cols // bn

    if row_sharded:
        out_map = lambda i, j, s: (i // per_shard_i, i % per_shard_i, j)
    else:
        out_map = lambda i, j, s: (j // per_shard_j, i, j % per_shard_j)

    def body(a_ref, b_ref, o_ref):
        s = pl.program_id(2)
        part = lax.dot_general(a_ref[...], b_ref[...], (((0,), (0,)), ((), ())), preferred_element_type=F32)
        _accumulate(o_ref, part, s == 0)

    return pl.pallas_call(
        body,
        name=name,
        grid=(ni, nj, ns),
        in_specs=[pl.BlockSpec((bs, bka), lambda i, j, s: (s, i)), pl.BlockSpec((bs, bn), lambda i, j, s: (s, j))],
        out_specs=pl.BlockSpec((None, bka, bn), out_map),
        out_shape=jax.ShapeDtypeStruct((n_shards, shard_rows, shard_cols), F32),
        compiler_params=_params(3),
    )(a, b)


def _row(v):
    return v.reshape(1, -1)


def _proj(name, xb, w_g, b_in, after=None):
    s_len, d = xb.shape
    ncs = w_g.shape[2]
    bm = _tile(s_len, BM_MM)

    def epilogue(acc, ex, outs, ids):
        outs[0][...] = acc + ex[0][...]

    return _matmul(
        name, (s_len // bm, 4, 1),
        xb, pl.BlockSpec((bm, d), lambda i, j, k: (i, 0)),
        w_g, pl.BlockSpec((None, d, ncs), lambda i, j, k: (j, 0, 0)),
        nt=False,
        extras=[(_row(b_in), pl.BlockSpec((1, ncs), lambda i, j, k: (0, j)))],
        outs=[(jax.ShapeDtypeStruct((s_len, 4 * ncs), F32), pl.BlockSpec((bm, ncs), lambda i, j, k: (i, j)))],
        epilogue=epilogue, after=after,
    )[0]


def _mix_out(name, mixed, w_out_full, b_out, x0, g1, b1):
    s_len, d = mixed.shape
    bm = _tile(s_len, BM_LN // 2)
    row = pl.BlockSpec((1, d), lambda i, j, k: (0, 0))
    blk = pl.BlockSpec((bm, d), lambda i, j, k: (i, 0))

    def epilogue(acc, ex, outs, ids):
        r1 = ALPHA * ex[1][...] + (acc + ex[0][...])
        outs[0][...] = r1
        xhat, _ = _ln_stats(r1)
        outs[1][...] = (xhat * ex[2][...] + ex[3][...]).astype(BF16)

    return _matmul(
        name, (s_len // bm, 1, 1),
        mixed, blk,
        w_out_full, pl.BlockSpec((d, d), lambda i, j, k: (0, 0)),
        nt=False,
        extras=[(_row(b_out), row), (x0, blk), (_row(g1), row), (_row(b1), row)],
        outs=[(jax.ShapeDtypeStruct((s_len, d), F32), blk), (jax.ShapeDtypeStruct((s_len, d), BF16), blk)],
        epilogue=epilogue,
    )


def _ff1(name, x1b, w_g, b_ff1, after=None):
    s_len, d = x1b.shape
    ncs = w_g.shape[2]
    bm = _tile(s_len, BM_MM)
    bn = _tile(ncs, BN_MM)
    per = ncs // bn
    blk = pl.BlockSpec((bm, bn), lambda i, j, k: (i, j))

    def epilogue(acc, ex, outs, ids):
        zr = jnp.maximum(acc + ex[0][...], 0.0)
        outs[0][...] = (zr * zr).astype(BF16)
        outs[1][...] = zr.astype(BF16)

    shape = jax.ShapeDtypeStruct((s_len, 4 * ncs), BF16)
    return _matmul(
        name, (s_len // bm, 4 * per, 1),
        x1b, pl.BlockSpec((bm, d), lambda i, j, k: (i, 0)),
        w_g, pl.BlockSpec((None, d, bn), lambda i, j, k: (j // per, 0, j % per)),
        nt=False,
        extras=[(_row(b_ff1), pl.BlockSpec((1, bn), lambda i, j, k: (0, j)))],
        outs=[(shape, blk), (shape, blk)],
        epilogue=epilogue, after=after,
    )


def _ff2(name, hf, w2_full, b_ff2):
    s_len, dff = hf.shape
    d = w2_full.shape[1]
    bm = _tile(s_len, BM_MM)
    bk = _tile(dff, BK_WIDE)
    blk = pl.BlockSpec((bm, d), lambda i, j, k: (i, 0))

    def epilogue(acc, ex, outs, ids):
        outs[0][...] = acc + ex[0][...]

    return _matmul(
        name, (s_len // bm, 1, dff // bk),
        hf, pl.BlockSpec((bm, bk), lambda i, j, k: (i, k)),
        w2_full, pl.BlockSpec((bk, d), lambda i, j, k: (k, 0)),
        nt=False,
        extras=[(_row(b_ff2), pl.BlockSpec((1, d), lambda i, j, k: (0, 0)))],
        outs=[(jax.ShapeDtypeStruct((s_len, d), F32), blk)],
        epilogue=epilogue,
    )[0]


def _resid_ln2(name, r1, g1, b1, fo, g2, b2, after=None):
    s_len, d = r1.shape
    bm = _tile(s_len, BM_LN // 2)
    blk = pl.BlockSpec((bm, d), lambda i: (i, 0))
    row = pl.BlockSpec((1, d), lambda i: (0, 0))
    order = [] if after is None else [after]

    def body(*refs):
        r1_ref, g1_ref, b1_ref, fo_ref, g2_ref, b2_ref, r2_ref, x2_ref, x2b_ref = refs[len(order):]
        xhat1, _ = _ln_stats(r1_ref[...])
        r2 = ALPHA * (xhat1 * g1_ref[...] + b1_ref[...]) + fo_ref[...]
        r2_ref[...] = r2
        xhat2, _ = _ln_stats(r2)
        x2 = xhat2 * g2_ref[...] + b2_ref[...]
        x2_ref[...] = x2
        x2b_ref[...] = x2.astype(BF16)

    return pl.pallas_call(
        body, name=name, grid=(s_len // bm,),
        in_specs=[pl.BlockSpec(memory_space=pl.ANY)] * len(order) + [blk, row, row, blk, row, row], out_specs=[blk, blk, blk],
        out_shape=[jax.ShapeDtypeStruct((s_len, d), F32), jax.ShapeDtypeStruct((s_len, d), F32),
                   jax.ShapeDtypeStruct((s_len, d), BF16)],
        compiler_params=_params(1))(*order, r1, _row(g1), _row(b1), fo, _row(g2), _row(b2))


def _dff_hidden(name, dr2b, w2_full, zr, rider=None):
    s_len, d = dr2b.shape
    dff = w2_full.shape[0]
    bm = _tile(s_len, BM_MM)
    bn = _tile(dff, BN_MM)

    def epilogue(acc, ex, outs, ids):
        dz = acc * (2.0 * ex[0][...].astype(F32))
        outs[0][...] = dz.astype(BF16)
        _accumulate(outs[1], _colsum(dz), ids[1] == 0)

    grid = (dff // bn, s_len // bm, 1)
    rider = rider and rider(math.prod(grid))
    res = _matmul(
        name, grid,
        dr2b, pl.BlockSpec((bm, d), lambda j, i, k: (i, 0)),
        w2_full, pl.BlockSpec((bn, d), lambda j, i, k: (j, 0)),
        nt=True,
        extras=[(zr, pl.BlockSpec((bm, bn), lambda j, i, k: (i, j)))],
        outs=[(jax.ShapeDtypeStruct((s_len, dff), BF16), pl.BlockSpec((bm, bn), lambda j, i, k: (i, j))),
              (jax.ShapeDtypeStruct((1, dff), F32), pl.BlockSpec((1, bn), lambda j, i, k: (0, j)))],
        epilogue=epilogue, rider=rider,
    )
    return tuple(res) if rider is None else (tuple(res[:2]), res[2:])


def _dx_sharded(name, dyb, w_g, resid=None, after=None, rider=None):
    s_len = dyb.shape[0]
    d, ncs = w_g.shape[1], w_g.shape[2]
    bm = _tile(s_len, BM_MM if resid is None else BM_LN)
    bk = _tile(ncs, BK_WIDE) if ncs % BK_MM == 0 else ncs
    per = ncs // bk
    blk = pl.BlockSpec((bm, d), lambda i, j, k: (i, 0))

    def epilogue(acc, ex, outs, ids):
        outs[0][...] = acc if resid is None else acc + ALPHA * ex[0][...]

    rider = rider and rider(s_len // bm * 4 * per)
    res = _matmul(
        name, (s_len // bm, 1, 4 * per),
        dyb, pl.BlockSpec((bm, bk), lambda i, j, k: (i, k)),
        w_g, pl.BlockSpec((None, d, bk), lambda i, j, k: (k // per, 0, k % per)),
        nt=True,
        extras=[] if resid is None else [(resid, blk)],
        outs=[(jax.ShapeDtypeStruct((s_len, d), F32), blk)],
        epilogue=epilogue, after=after, rider=rider,
    )
    return res[0] if rider is None else (res[0], res[1:])


def _dmixed(name, dr1b, w_out_full, rider=None):
    s_len, d = dr1b.shape
    bm = _tile(s_len, BM_LN)
    blk = pl.BlockSpec((bm, d), lambda i, j, k: (i, 0))

    def epilogue(acc, ex, outs, ids):
        outs[0][...] = acc

    rider = rider and rider(s_len // bm)
    res = _matmul(
        name, (s_len // bm, 1, 1),
        dr1b, blk,
        w_out_full, pl.BlockSpec((d, d), lambda i, j, k: (0, 0)),
        nt=True, extras=[],
        outs=[(jax.ShapeDtypeStruct((s_len, d), F32), blk)],
        epilogue=epilogue, rider=rider,
    )
    return res[0] if rider is None else (res[0], res[1:])


def _ln_backward(name, r, g, *, dy=None, resid=None, last=None, after=None):
    from_loss = last is not None
    s_len, d = last[0].shape if from_loss else r.shape
    bm = _tile(s_len, BM_LN // 2)
    blk = pl.BlockSpec((bm, d), lambda i: (i, 0))
    row = pl.BlockSpec((1, d), lambda i: (0, 0))

    def body(*refs):
        i = pl.program_id(0)
        if after is not None:
            refs = refs[1:]
        if from_loss:
            r1_ref, fo_ref, t_ref, g1_ref, b1_ref, b_ref, g_ref, dr_ref, drb_ref, dg_ref, db_ref, dbias_ref, loss_ref = refs
            xhat1, _ = _ln_stats(r1_ref[...])
            rv = ALPHA * (xhat1 * g1_ref[...] + b1_ref[...]) + fo_ref[...]
            xhat, rstd = _ln_stats(rv)
            diff = (xhat * g_ref[...] + b_ref[...]) - t_ref[...]
            dyv = diff * (1.0 / d)
            part = 0.5 * jnp.sum(jnp.sum(diff * diff, axis=1, keepdims=True) * (1.0 / d), axis=0, keepdims=True)
            _accumulate(loss_ref, jnp.broadcast_to(part, loss_ref.shape), i == 0)
        else:
            if resid is not None:
                dy_ref, res_ref, r_ref, g_ref, dr_ref, drb_ref, dg_ref, db_ref, dbias_ref = refs
                dyv = dy_ref[...] + ALPHA * res_ref[...]
            else:
                dy_ref, r_ref, g_ref, dr_ref, drb_ref, dg_ref, db_ref, dbias_ref = refs
                dyv = dy_ref[...]
            xhat, rstd = _ln_stats(r_ref[...])
        dr = _ln_bwd(dyv, xhat, rstd, g_ref[...])
        dr_ref[...] = dr
        drb_ref[...] = dr.astype(BF16)
        _accumulate(dg_ref, _colsum(dyv * xhat), i == 0)
        _accumulate(db_ref, _colsum(dyv), i == 0)
        _accumulate(dbias_ref, _colsum(dr), i == 0)

    if from_loss:
        r1, g1, b1, fo, b, target = last
        ins = [r1, fo, target, _row(g1), _row(b1), _row(b), _row(g)]
        in_specs = [blk] * 3 + [row] * 4
    else:
        lead = [dy] if resid is None else [dy, resid]
        ins = lead + [r, _row(g)]
        in_specs = [blk] * len(lead) + [blk, row]
    if after is not None:
        ins, in_specs = [after] + ins, [pl.BlockSpec(memory_space=pl.ANY)] + in_specs
    out_shape = [jax.ShapeDtypeStruct((s_len, d), F32), jax.ShapeDtypeStruct((s_len, d), BF16)] + \
                [jax.ShapeDtypeStruct((1, d), F32)] * 3
    out_specs = [blk, blk, row, row, row]
    if from_loss:
        out_shape.append(jax.ShapeDtypeStruct((8, 128), F32))
        out_specs.append(pl.BlockSpec((8, 128), lambda i: (0, 0)))
    return pl.pallas_call(body, name=name, grid=(s_len // bm,), in_specs=in_specs, out_specs=out_specs,
                          out_shape=out_shape, compiler_params=_params(1))(*ins)


def _rowwise(name, fn, ins, out_dtypes, rows_pref=BR_EW):
    r, c = ins[0].shape
    br = _tile(r, rows_pref)
    blk = pl.BlockSpec((br, c), lambda i: (i, 0))

    def body(*refs):
        res = fn(*[ref[...] for ref in refs[:len(ins)]])
        for o_ref, v in zip(refs[len(ins):], res):
            o_ref[...] = v.astype(o_ref.dtype)

    return pl.pallas_call(body, name=name, grid=(r // br,), in_specs=[blk] * len(ins),
                          out_specs=[blk] * len(out_dtypes),
                          out_shape=[jax.ShapeDtypeStruct((r, c), dt) for dt in out_dtypes],
                          compiler_params=_params(1))(*ins)


def _adamw_math(w, g, m, v):
    m = ADAM_B1 * m + (1.0 - ADAM_B1) * g
    v = ADAM_B2 * v + (1.0 - ADAM_B2) * (g * g)
    m_hat = m / (1.0 - ADAM_B1 ** ADAM_STEP)
    v_hat = v / (1.0 - ADAM_B2 ** ADAM_STEP)
    delta = -ADAM_LR * (m_hat / (jnp.sqrt(v_hat) + ADAM_EPS) + ADAM_WD * w)
    return delta, m, v


def _adamw(name, w, g, m, v):
    shape = w.shape
    c = shape[-1]
    flat = [a.reshape(-1, c) for a in (w, g, m, v)]
    rows = flat[0].shape[0]
    pref = max(8, 2 ** int(math.log2(EW_BLOCK_ELEMS // c)))
    res = _rowwise(name, _adamw_math, flat, [F32, F32, F32], rows_pref=pref if rows % pref == 0 else rows)
    return tuple(a.reshape(shape) for a in res)


def _pool_means(ext_ref, ts, tile_index):
    t_glob = tile_index * ts + lax.broadcasted_iota(jnp.int32, (ts, GROUP), 0)
    qs = []
    for g, win in enumerate(POOL_WINDOWS):
        cols = pl.ds(g * GROUP, GROUP)
        cur = ext_ref[pl.ds(16, ts), cols]
        acc = cur
        for j in range(1, win):
            acc = acc + ext_ref[pl.ds(16 - j, ts), cols]
        cnt = jnp.minimum(t_glob + 1, win).astype(F32)
        qs.append(acc / cnt - cur)
    return qs


def _masked_sgu_w(w_ref, h):
    r = lax.broadcasted_iota(jnp.int32, (CHUNK, CHUNK), 0)
    c = lax.broadcasted_iota(jnp.int32, (CHUNK, CHUNK), 1)
    return jnp.where(r >= c, w_ref[h], 0.0)


SUBLANES = 8


def _fill_shifted(src_ref, shifted_ref, ts):
    rows = ts + HALO - SUBLANES
    for b in range(1, SUBLANES):
        shifted_ref[b - 1] = src_ref[pl.ds(b, rows), :]


def _shifted_rows(src_ref, shifted_ref, offset, rows, cols):
    a, b = divmod(offset, SUBLANES)
    if b == 0:
        return src_ref[pl.ds(offset, rows), cols]
    return shifted_ref[b - 1, pl.ds(a * SUBLANES, rows), cols]


def _conv_taps(acc_init, w_ref, src_ref, shifted_ref, ts, base, emit):
    nrb = ts // RB_CONV
    for cg in range(CONV_WIDTH // GROUP):
        cols = pl.ds(cg * GROUP, GROUP)
        for rb0 in range(0, nrb, CONV_LIVE_BLOCKS):
            group = range(rb0, min(rb0 + CONV_LIVE_BLOCKS, nrb))
            accs = {rb: acc_init(cg) for rb in group}
            for k in range(CONV_KERNEL):
                wk = jnp.broadcast_to(w_ref[pl.ds(k, 1), cols], (RB_CONV, GROUP))
                for rb in group:
                    accs[rb] = accs[rb] + wk * _shifted_rows(src_ref, shifted_ref, rb * RB_CONV + base(k), RB_CONV, cols)
            for rb in group:
                emit(rb * RB_CONV, cg * GROUP, accs[rb])


def _mix_forward(name, p, lw):
    s_len = p.shape[0]
    ts = _tile(s_len, TS_MIX)
    per_halo = ts // HALO
    d = POOL_WIDTH + SGU_WIDTH + CONV_WIDTH

    def body(p_ref, ph_ref, wp_ref, ps_ref, slg_ref, slb_ref, ws_ref, sb_ref, cw_ref, cb_ref, clg_ref, clb_ref,
             mixed_ref, cv_ref, pool_ext, hh_ext, shifted):
        i = pl.program_id(0)
        keep = (i > 0).astype(F32)
        pool_ext[pl.ds(0, 16), :] = ph_ref[pl.ds(16, 16), pl.ds(0, POOL_WIDTH)] * keep
        pool_ext[pl.ds(16, ts), :] = p_ref[:, pl.ds(0, POOL_WIDTH)]
        qs = _pool_means(pool_ext, ts, i)
        for g in range(len(POOL_WINDOWS)):
            cols = pl.ds(g * GROUP, GROUP)
            e = jnp.dot(qs[g].astype(BF16), wp_ref[g].astype(BF16), preferred_element_type=F32)
            mixed_ref[:, cols] = (e * ps_ref[:, cols]).astype(BF16)
        uv = _gelu(p_ref[:, pl.ds(COL_B, 2 * SGU_WIDTH)])
        u = uv[:, :SGU_WIDTH]
        vhat, _ = _ln_stats(uv[:, SGU_WIDTH:])
        vn = (vhat * slg_ref[...] + slb_ref[...]).astype(BF16)
        for h in range(SGU_HEADS):
            wm = _masked_sgu_w(ws_ref, h).astype(BF16)
            for n in range(ts // CHUNK):
                rows = slice(n * CHUNK, (n + 1) * CHUNK)
                cols = slice(h * GROUP, (h + 1) * GROUP)
                mx = jnp.dot(wm, vn[rows, cols], preferred_element_type=F32) + sb_ref[h]
                mixed_ref[pl.ds(n * CHUNK, CHUNK), pl.ds(POOL_WIDTH + h * GROUP, GROUP)] = (u[rows, cols] * mx).astype(BF16)
        hh_ext[pl.ds(0, HALO), :] = (ph_ref[:, pl.ds(COL_C, CONV_WIDTH)]
                                     * _sigmoid(ph_ref[:, pl.ds(COL_C + CONV_WIDTH, CONV_WIDTH)])) * keep
        hh_ext[pl.ds(HALO, ts), :] = p_ref[:, pl.ds(COL_C, CONV_WIDTH)] * _sigmoid(p_ref[:, pl.ds(COL_C + CONV_WIDTH, CONV_WIDTH)])
        init = lambda cg: jnp.broadcast_to(cb_ref[:, pl.ds(cg * GROUP, GROUP)], (RB_CONV, GROUP))
        _fill_shifted(hh_ext, shifted, ts)
        def store_cv(r0, c0, blk):
            cv_ref[pl.ds(r0, RB_CONV), pl.ds(c0, GROUP)] = blk

        _conv_taps(init, cw_ref, hh_ext, shifted, ts, lambda k: k + HALO - (CONV_KERNEL - 1), store_cv)
        cvhat, _ = _ln_stats(cv_ref[...])
        cn = cvhat * clg_ref[...] + clb_ref[...]
        mixed_ref[:, pl.ds(POOL_WIDTH + SGU_WIDTH, CONV_WIDTH)] = (cn * _sigmoid(cn)).astype(BF16)

    full = lambda a: pl.BlockSpec(a.shape, lambda i: (0,) * a.ndim)
    weights = [lw["w_pool"], _row(lw["pool_scale"]), _row(lw["sgu_ln_g"]), _row(lw["sgu_ln_b"]), lw["sgu_w"],
               lw["sgu_bias_tile"], lw["conv_w_full"], _row(lw["conv_b"]), _row(lw["conv_ln_g"]), _row(lw["conv_ln_b"])]
    return pl.pallas_call(
        body, name=name, grid=(s_len // ts,),
        in_specs=[pl.BlockSpec((ts, IN_WIDTH), lambda i: (i, 0)),
                  pl.BlockSpec((HALO, IN_WIDTH), lambda i: (jnp.maximum(i * per_halo - 1, 0), 0))] + [full(a) for a in weights],
        out_specs=[pl.BlockSpec((ts, d), lambda i: (i, 0)), pl.BlockSpec((ts, CONV_WIDTH), lambda i: (i, 0))],
        out_shape=[jax.ShapeDtypeStruct((s_len, d), BF16), jax.ShapeDtypeStruct((s_len, CONV_WIDTH), F32)],
        scratch_shapes=[pltpu.VMEM((16 + ts, POOL_WIDTH), F32), pltpu.VMEM((HALO + ts, CONV_WIDTH), F32),
                        pltpu.VMEM((SUBLANES - 1, ts + HALO - SUBLANES, CONV_WIDTH), F32)],
        compiler_params=_params(1),
    )(p, p, *weights)


def _mix_backward(name, p, cv, dmixed, lw):
    s_len = p.shape[0]
    ts = _tile(s_len, TS_MIX)
    nt = s_len // ts
    per_halo = ts // HALO
    d = POOL_WIDTH + SGU_WIDTH + CONV_WIDTH
    nch = ts // CHUNK
    col_yc = POOL_WIDTH + SGU_WIDTH

    def body(p_ref, ph_ref, cv_ref, cvn_ref, dm_ref, dmn_ref, wp_ref, ps_ref, slg_ref, slb_ref, ws_ref, wst_ref, sb_ref,
             cw_ref, cwr_ref, clg_ref, clb_ref,
             dp_ref, dbin_ref, dwp_ref, dps_ref, dslg_ref, dslb_ref, dws_ref, dsb_ref, dcw_ref, dcb_ref, dclg_ref, dclb_ref,
             pool_ext, dq_ext, hh_ext, dcv_ext, dcw_acc, shifted):
        i = pl.program_id(0)
        first = i == 0
        keep_prev = (i > 0).astype(F32)
        keep_next = (i < nt - 1).astype(F32)

        pool_ext[pl.ds(0, 16), :] = ph_ref[pl.ds(16, 16), pl.ds(0, POOL_WIDTH)] * keep_prev
        pool_ext[pl.ds(16, ts), :] = p_ref[:, pl.ds(0, POOL_WIDTH)]
        qs = _pool_means(pool_ext, ts, i)
        t_ext = i * ts + lax.broadcasted_iota(jnp.int32, (ts + HALO, GROUP), 0)
        for g, win in enumerate(POOL_WINDOWS):
            cols = pl.ds(g * GROUP, GROUP)
            wpb = wp_ref[g].astype(BF16)
            qb = qs[g].astype(BF16)
            de = dm_ref[:, cols] * ps_ref[:, cols]
            e = jnp.dot(qb, wpb, preferred_element_type=F32)
            _accumulate(dps_ref.at[:, cols], _colsum(dm_ref[:, cols] * e), first)
            deb = de.astype(BF16)
            _accumulate(dwp_ref.at[g], lax.dot_general(qb, deb, (((0,), (0,)), ((), ())), preferred_element_type=F32), first)
            de_next = (dmn_ref[:, cols] * ps_ref[:, cols] * keep_next).astype(BF16)
            de_all = jnp.concatenate([deb, de_next], axis=0)
            dq = lax.dot_general(de_all, wpb, (((1,), (1,)), ((), ())), preferred_element_type=F32)
            inv = 1.0 / jnp.minimum(t_ext + 1, win).astype(F32)
            dq_ext[:, cols] = dq * inv
            acc = dq_ext[pl.ds(0, ts), cols]
            for j in range(1, win):
                acc = acc + dq_ext[pl.ds(j, ts), cols]
            dpa = acc - dq[:ts]
            dp_ref[:, cols] = dpa.astype(BF16)
            _accumulate(dbin_ref.at[:, cols], _colsum(dpa), first)

        pb = p_ref[:, pl.ds(COL_B, 2 * SGU_WIDTH)]
        uv, duv = _gelu_with_grad(pb)
        u = uv[:, :SGU_WIDTH]
        vhat, vrstd = _ln_stats(uv[:, SGU_WIDTH:])
        vn = (vhat * slg_ref[...] + slb_ref[...]).astype(BF16)
        dyb = dm_ref[:, pl.ds(POOL_WIDTH, SGU_WIDTH)]
        dmix = dyb * u
        dmixb = dmix.astype(BF16)
        du_parts, dvn_parts = [], []
        for h in range(SGU_HEADS):
            cols = slice(h * GROUP, (h + 1) * GROUP)
            wm = _masked_sgu_w(ws_ref, h).astype(BF16)
            wmt = _masked_sgu_w_t(wst_ref, h).astype(BF16)
            dws_h = jnp.zeros((CHUNK, CHUNK), F32)
            dsb_h = jnp.zeros((CHUNK, GROUP), F32)
            du_rows, dvn_rows = [], []
            for n in range(nch):
                rows = slice(n * CHUNK, (n + 1) * CHUNK)
                mx = jnp.dot(wm, vn[rows, cols], preferred_element_type=F32) + sb_ref[h]
                du_rows.append(dyb[rows, cols] * mx)
                dws_h = dws_h + lax.dot_general(dmixb[rows, cols], vn[rows, cols], (((1,), (1,)), ((), ())),
                                                preferred_element_type=F32)
                dsb_h = dsb_h + dmix[rows, cols]
                dvn_rows.append(jnp.dot(wmt, dmixb[rows, cols], preferred_element_type=F32))
            r = lax.broadcasted_iota(jnp.int32, (CHUNK, CHUNK), 0)
            c = lax.broadcasted_iota(jnp.int32, (CHUNK, CHUNK), 1)
            _accumulate(dws_ref.at[h], jnp.where(r >= c, dws_h, 0.0), first)
            _accumulate(dsb_ref.at[h], jnp.broadcast_to(jnp.sum(dsb_h, axis=1, keepdims=True), (CHUNK, GROUP)), first)
            du_parts.append(jnp.concatenate(du_rows, axis=0))
            dvn_parts.append(jnp.concatenate(dvn_rows, axis=0))
        du = jnp.concatenate(du_parts, axis=1)
        dvn = jnp.concatenate(dvn_parts, axis=1)
        _accumulate(dslg_ref, _colsum(dvn * vhat), first)
        _accumulate(dslb_ref, _colsum(dvn), first)
        dv = _ln_bwd(dvn, vhat, vrstd, slg_ref[...])
        dpb = jnp.concatenate([du, dv], axis=1) * duv
        dp_ref[:, pl.ds(COL_B, 2 * SGU_WIDTH)] = dpb.astype(BF16)
        _accumulate(dbin_ref.at[:, pl.ds(COL_B, 2 * SGU_WIDTH)], _colsum(dpb), first)

        a_main = p_ref[:, pl.ds(COL_C, CONV_WIDTH)]
        sg_main = _sigmoid(p_ref[:, pl.ds(COL_C + CONV_WIDTH, CONV_WIDTH)])
        hh_ext[pl.ds(HALO, ts), :] = a_main * sg_main

        def conv_ln_backward(cv_v, dyc_v):
            cvhat, crstd = _ln_stats(cv_v)
            cn = cvhat * clg_ref[...] + clb_ref[...]
            s = _sigmoid(cn)
            dcn = dyc_v * (s * (1.0 + cn * (1.0 - s)))
            return _ln_bwd(dcn, cvhat, crstd, clg_ref[...]), dcn, cvhat

        dcv, dcn, cvhat = conv_ln_backward(cv_ref[...], dm_ref[:, pl.ds(col_yc, CONV_WIDTH)])
        _accumulate(dclg_ref, _colsum(dcn * cvhat), first)
        _accumulate(dclb_ref, _colsum(dcn), first)
        _accumulate(dcb_ref, _colsum(dcv), first)
        dcv_next, _, _ = conv_ln_backward(cvn_ref[...], dmn_ref[:, pl.ds(col_yc, CONV_WIDTH)])
        dcv_ext[pl.ds(0, ts), :] = dcv
        dcv_ext[pl.ds(ts, HALO), :] = dcv_next * keep_next

        @pl.when(first)
        def _():
            dcw_acc[...] = jnp.zeros_like(dcw_acc)

        nrb = ts // RB_CONV
        _fill_shifted(dcv_ext, shifted, ts)
        for cg in range(CONV_WIDTH // GROUP):
            cols = pl.ds(cg * GROUP, GROUP)
            for rb0 in range(0, nrb, CONV_LIVE_BLOCKS):
                group = range(rb0, min(rb0 + CONV_LIVE_BLOCKS, nrb))
                hblk = {rb: hh_ext[pl.ds(HALO + rb * RB_CONV, RB_CONV), cols] for rb in group}
                for k in range(CONV_KERNEL):
                    part = jnp.zeros((8, GROUP), F32)
                    for rb in group:
                        offset = rb * RB_CONV + CONV_KERNEL - 1 - k
                        prod = hblk[rb] * _shifted_rows(dcv_ext, shifted, offset, RB_CONV, cols)
                        part = part + jnp.sum(prod.reshape(RB_CONV // 8, 8, GROUP), axis=0)
                    dcw_acc[k, :, cols] += part

        @pl.when(i == nt - 1)
        def _():
            dcw_ref[...] = jnp.sum(dcw_acc[...], axis=1)

        zero = lambda cg: jnp.zeros((RB_CONV, GROUP), F32)
        def glu_backward(r0, c0, blk):
            rows, cols = pl.ds(r0, RB_CONV), pl.ds(c0, GROUP)
            a_blk = p_ref[rows, pl.ds(COL_C + c0, GROUP)]
            s_blk = _sigmoid(p_ref[rows, pl.ds(COL_C + CONV_WIDTH + c0, GROUP)])
            da = blk * s_blk
            dg = blk * a_blk * s_blk * (1.0 - s_blk)
            dp_ref[rows, pl.ds(COL_C + c0, GROUP)] = da.astype(BF16)
            dp_ref[rows, pl.ds(COL_C + CONV_WIDTH + c0, GROUP)] = dg.astype(BF16)
            hh_ext[pl.ds(HALO + r0, RB_CONV), cols] = da
            dcv_ext[rows, cols] = dg

        _conv_taps(zero, cwr_ref, dcv_ext, shifted, ts, lambda k: k, glu_backward)
        _accumulate(dbin_ref.at[:, pl.ds(COL_C, CONV_WIDTH)], _colsum(hh_ext[pl.ds(HALO, ts), :]), first)
        _accumulate(dbin_ref.at[:, pl.ds(COL_C + CONV_WIDTH, CONV_WIDTH)], _colsum(dcv_ext[pl.ds(0, ts), :]), first)

    full = lambda a: pl.BlockSpec(a.shape, lambda i: (0,) * a.ndim)
    weights = [lw["w_pool"], _row(lw["pool_scale"]), _row(lw["sgu_ln_g"]), _row(lw["sgu_ln_b"]), lw["sgu_w"], lw["sgu_w_t"],
               lw["sgu_bias_tile"], lw["conv_w_full"], lw["conv_w_rev"], _row(lw["conv_ln_g"]), _row(lw["conv_ln_b"])]
    prev_halo = lambda i: (jnp.maximum(i * per_halo - 1, 0), 0)
    next_halo = lambda i: (jnp.minimum((i + 1) * per_halo, s_len // HALO - 1), 0)
    small = lambda shape: (jax.ShapeDtypeStruct(shape, F32), pl.BlockSpec(shape, lambda i: (0,) * len(shape)))
    outs = [(jax.ShapeDtypeStruct((s_len, IN_WIDTH), BF16), pl.BlockSpec((ts, IN_WIDTH), lambda i: (i, 0))),
            small((1, IN_WIDTH)), small((len(POOL_WINDOWS), GROUP, GROUP)), small((1, POOL_WIDTH)),
            small((1, SGU_WIDTH)), small((1, SGU_WIDTH)), small((SGU_HEADS, CHUNK, CHUNK)), small((SGU_HEADS, CHUNK, GROUP)),
            small((CONV_ROWS, CONV_WIDTH)), small((1, CONV_WIDTH)), small((1, CONV_WIDTH)), small((1, CONV_WIDTH))]
    return pl.pallas_call(
        body, name=name, grid=(nt,),
        in_specs=[pl.BlockSpec((ts, IN_WIDTH), lambda i: (i, 0)), pl.BlockSpec((HALO, IN_WIDTH), prev_halo),
                  pl.BlockSpec((ts, CONV_WIDTH), lambda i: (i, 0)), pl.BlockSpec((HALO, CONV_WIDTH), next_halo),
                  pl.BlockSpec((ts, d), lambda i: (i, 0)), pl.BlockSpec((HALO, d), next_halo)] + [full(a) for a in weights],
        out_specs=[s for _, s in outs],
        out_shape=[o for o, _ in outs],
        scratch_shapes=[pltpu.VMEM((16 + ts, POOL_WIDTH), F32), pltpu.VMEM((ts + HALO, POOL_WIDTH), F32),
                        pltpu.VMEM((HALO + ts, CONV_WIDTH), F32), pltpu.VMEM((ts + HALO, CONV_WIDTH), F32),
                        pltpu.VMEM((CONV_ROWS, 8, CONV_WIDTH), F32),
                        pltpu.VMEM((SUBLANES - 1, ts + HALO - SUBLANES, CONV_WIDTH), F32)],
        compiler_params=_params(1),
    )(p, p, cv, cv, dmixed, dmixed, *weights)


def _masked_sgu_w_t(wt_ref, h):
    r = lax.broadcasted_iota(jnp.int32, (CHUNK, CHUNK), 0)
    c = lax.broadcasted_iota(jnp.int32, (CHUNK, CHUNK), 1)
    return jnp.where(c >= r, wt_ref[h], 0.0)


HBM = pl.BlockSpec(memory_space=pltpu.HBM)
CHIP_FLIPS = ((1, 0), (0, 1), (1, 1))


def _place():
    return lax.axis_index("x"), lax.axis_index("y"), lax.axis_index("c")


def _half(ref, axis, which, size):
    idx = [slice(None)] * len(ref.shape)
    idx[axis] = pl.ds(which * size, size)
    return ref.at[tuple(idx)]


def _exchange(name, sources, inplace, fresh, copies):
    n_src, n_in, n = len(sources), len(inplace), len(copies)
    n_out = n_in + len(fresh)

    def body(*refs):
        ins = refs[:n_src + n_in]
        outs = refs[n_src + n_in:n_src + n_in + n_out]
        send_sems, recv_sems = refs[n_src + n_in + n_out:]
        started = _descriptors(copies, list(ins[:n_src]) + list(outs), send_sems, recv_sems)
        for cp in started:
            cp.start()
        for cp in started:
            cp.wait()

    out_shape = [jax.ShapeDtypeStruct(a.shape, a.dtype) for a in inplace] + list(fresh)
    return pl.pallas_call(
        body, name=name, in_specs=[HBM] * (n_src + n_in), out_specs=[HBM] * n_out, out_shape=out_shape,
        input_output_aliases={n_src + i: i for i in range(n_in)},
        scratch_shapes=[pltpu.SemaphoreType.DMA((n,)), pltpu.SemaphoreType.DMA((n,))],
    )(*sources, *inplace)


def _descriptors(copies, bufs, send_sems, recv_sems):
    x, y, c = _place()
    return [pltpu.make_async_remote_copy(
        src_ref=src_view(bufs[src], x, y, c), dst_ref=dst_view(bufs[dst], x, y, c),
        send_sem=send_sems.at[k], recv_sem=recv_sems.at[k], device_id=peer(x, y, c), device_id_type=MESH)
        for k, (src, src_view, dst, dst_view, peer) in enumerate(copies)]


SEM = pl.BlockSpec(memory_space=pltpu.SEMAPHORE)
IN_FLIGHT = pltpu.CompilerParams(has_side_effects=pltpu.SideEffectType.DATAFLOW_SIDE_EFFECTING)


def _exchange_start(name, sources, inplace, fresh, copies):
    n_src, n = len(sources), len(copies)
    landing = [lax.empty(f.shape, f.dtype) for f in fresh]
    bufs = [pltpu.with_memory_space_constraint(a, pltpu.HBM) for a in (*inplace, *landing)]
    srcs = [pltpu.with_memory_space_constraint(a, pltpu.HBM) for a in sources]
    n_buf = len(bufs)

    def body(*refs):
        ins = refs[:n_src]
        send_sems, recv_sems = refs[n_src + n_buf], refs[n_src + n_buf + 1]
        outs = refs[n_src + n_buf + 2:n_src + 2 * n_buf + 2]
        token = refs[n_src + 2 * n_buf + 2]
        for cp in _descriptors(copies, list(ins) + list(outs), send_sems, recv_sems):
            cp.start()
        token[...] = jnp.zeros_like(token)

    res = pl.pallas_call(
        body, name=name + "_start",
        out_shape=(pltpu.SemaphoreType.DMA((n,)), pltpu.SemaphoreType.DMA((n,)),
                   *[pltpu.HBM(b.shape, b.dtype) for b in bufs], jax.ShapeDtypeStruct((8, 128), F32)),
        in_specs=[HBM] * (n_src + n_buf),
        out_specs=(SEM, SEM, *[HBM] * n_buf, pl.BlockSpec(memory_space=pltpu.VMEM)),
        input_output_aliases={n_src + i: 2 + i for i in range(n_buf)},
        compiler_params=IN_FLIGHT,
    )(*srcs, *bufs)
    return dict(name=name, sources=srcs, sems=res[:2], bufs=list(res[2:2 + n_buf]), token=res[2 + n_buf], copies=copies)


def _exchange_wait(handle, after):
    srcs, bufs, copies = handle["sources"], handle["bufs"], handle["copies"]
    n_src, n_buf = len(srcs), len(bufs)

    def body(*refs):
        ins = refs[:n_src]
        send_sems, recv_sems = refs[n_src + n_buf], refs[n_src + n_buf + 1]
        outs = refs[n_src + n_buf + 3:]
        for cp in _descriptors(copies, list(ins) + list(outs), send_sems, recv_sems):
            cp.wait()

    return list(pl.pallas_call(
        body, name=handle["name"] + "_wait",
        out_shape=tuple(pltpu.HBM(b.shape, b.dtype) for b in bufs),
        in_specs=[HBM] * (n_src + n_buf) + [SEM, SEM, pl.BlockSpec(memory_space=pl.ANY)],
        out_specs=tuple([HBM] * n_buf),
        input_output_aliases={n_src + i: i for i in range(n_buf)},
        compiler_params=IN_FLIGHT,
    )(*srcs, *bufs, *handle["sems"], after))


def _into_slot(name, src, layer, place, dtype, after=None):
    _, r, c = src.shape
    br = _tile(r, BR_EW) if r % BR_EW == 0 else r
    order = [] if after is None else [after]

    def body(pr_ref, s_ref, *rest):
        rest[-1][...] = s_ref[...].astype(dtype)

    return pl.pallas_call(
        body, name=name,
        grid_spec=pltpu.PrefetchScalarGridSpec(
            num_scalar_prefetch=1, grid=(r // br,),
            in_specs=[pl.BlockSpec((None, br, c), lambda i, pr: (layer, i, 0))] + [pl.BlockSpec(memory_space=pl.ANY)] * len(order),
            out_specs=pl.BlockSpec((None, br, c), lambda i, pr: (pr[1], i, 0))),
        out_shape=jax.ShapeDtypeStruct((4, r, c), dtype), compiler_params=_params(1),
    )(place, src, *order)


def _gather_group(name, bufs):
    ici, d2d = _gather_copies(len(bufs))
    bufs = _exchange(name + "_chips", [], bufs, [], ici)
    return _exchange(name + "_sibling", [], bufs, [], d2d)


def _gather_copies(n_bufs):
    def own_half(ref, x, y, c):
        return _half(ref.at[2 * x + y], 0, c, ref.shape[1] // 2)

    ici, d2d = [], []
    for b in range(n_bufs):
        for fx, fy in CHIP_FLIPS:
            ici.append((b, own_half, b, own_half, lambda x, y, c, fx=fx, fy=fy: (x ^ fx, y ^ fy, c)))
            landed = lambda ref, x, y, c, fx=fx, fy=fy: _half(ref.at[2 * (x ^ fx) + (y ^ fy)], 0, c, ref.shape[1] // 2)
            d2d.append((b, landed, b, landed, _sibling))
    return ici, d2d


def _gather_start(name, bufs):
    return _exchange_start(name + "_chips", [], bufs, [], _gather_copies(len(bufs))[0])


def _gather_finish(name, handle, after):
    bufs = _exchange_wait(handle, after)
    return _exchange(name + "_sibling", [], bufs, [], _gather_copies(len(bufs))[1])


def _gather_pass_on(name, handle, after):
    bufs = _exchange_wait(handle, after)
    return _exchange_start(name + "_sibling", [], bufs, [], _gather_copies(len(bufs))[1])


def _sibling(x, y, c):
    return (x, y, 1 - c)


def _scalar_spec_call(name, fn, scalars, ins, in_blocks, out_shapes, out_blocks, grid):
    def body(s_ref, *refs):
        res = fn(*[r[...] for r in refs[:len(ins)]])
        for o_ref, v in zip(refs[len(ins):], res):
            o_ref[...] = v.astype(o_ref.dtype)

    return pl.pallas_call(
        body, name=name,
        grid_spec=pltpu.PrefetchScalarGridSpec(num_scalar_prefetch=1, grid=grid, in_specs=in_blocks, out_specs=out_blocks),
        out_shape=out_shapes, compiler_params=_params(len(grid)),
    )(scalars, *ins)


def _reduce_begin(tag, place, items):
    n = len(items)
    whole = lambda ref, x, y, c: ref
    halves = [g.shape[1] // 2 for g, _, _ in items]
    fresh = [jax.ShapeDtypeStruct((g.shape[0], halves[a], g.shape[2]), F32) for a, (g, _, _) in enumerate(items)]
    copies = [(a, lambda ref, x, y, c, hr=halves[a]: _half(ref, 1, 1 - c, hr), n + a, whole, _sibling) for a in range(n)]
    handle = _exchange_start(tag + "_sibling_in", [g for g, _, _ in items], [], fresh, copies)
    return dict(tag=tag, place=place, items=items, handle=handle, token=handle["token"])


def _reduce_between_chips(state, after):
    tag, place, items = state["tag"], state["place"], state["items"]
    n = len(items)
    slot_of = lambda scatter: (lambda x, y: 2 * x + y) if scatter else (lambda x, y: 0)
    halves = [g.shape[1] // 2 for g, _, _ in items]
    landed = _exchange_wait(state["handle"], after)

    chip_sums = []
    for a, (g, _, wire) in enumerate(items):
        ns, _, cols = g.shape
        hr = halves[a]
        br = _tile(hr, BR_EW)
        chip_sums.append(_scalar_spec_call(
            f"{tag}_chip_sum{a}", lambda u, v: (u + v,), place, [g, landed[a]],
            [pl.BlockSpec((None, br, cols), lambda s, i, pr, hb=hr // br: (s, pr[0] * hb + i, 0)),
             pl.BlockSpec((None, br, cols), lambda s, i, pr: (s, i, 0))],
            [jax.ShapeDtypeStruct((ns, hr, cols), wire)], [pl.BlockSpec((None, br, cols), lambda s, i, pr: (s, i, 0))],
            (ns, hr // br))[0])

    copies, fresh = [], []
    for a, (g, scatter, wire) in enumerate(items):
        slot = slot_of(scatter)
        fresh.append(jax.ShapeDtypeStruct((3, halves[a], g.shape[2]), wire))
        for j, (fx, fy) in enumerate(CHIP_FLIPS):
            copies.append((a, lambda ref, x, y, c, fx=fx, fy=fy, slot=slot: ref.at[slot(x ^ fx, y ^ fy)],
                           n + a, lambda ref, x, y, c, j=j: ref.at[j], lambda x, y, c, fx=fx, fy=fy: (x ^ fx, y ^ fy, c)))
    handle = _exchange_start(tag + "_chips", chip_sums, [], fresh, copies)
    return dict(tag=tag, place=place, items=items, chip_sums=chip_sums, handle=handle, token=handle["token"])


def _reduce_swap(state, after):
    tag, place, items, chip_sums = state["tag"], state["place"], state["items"], state["chip_sums"]
    n = len(items)
    whole = lambda ref, x, y, c: ref
    halves = [g.shape[1] // 2 for g, _, _ in items]
    arrived = _exchange_wait(state["handle"], after)

    tree = lambda own, fx, fy, fxy: ((own.astype(F32) + fx.astype(F32)) + (fy.astype(F32) + fxy.astype(F32)),)
    mine = []
    for a, (g, scatter, _) in enumerate(items):
        hr, cols = halves[a], g.shape[2]
        br = _tile(hr, BR_EW)
        got = lambda j: pl.BlockSpec((None, br, cols), lambda i, pr, j=j: (j, i, 0))
        own = pl.BlockSpec((None, br, cols), (lambda i, pr: (pr[1], i, 0)) if scatter else (lambda i, pr: (0, i, 0)))
        mine.append(_scalar_spec_call(
            f"{tag}_tree_sum{a}", tree, place, [chip_sums[a], arrived[a], arrived[a], arrived[a]],
            [own, got(0), got(1), got(2)],
            [jax.ShapeDtypeStruct((2, hr, cols), F32)], [pl.BlockSpec((None, br, cols), lambda i, pr: (pr[0], i, 0))],
            (hr // br,))[0])

    my_half = lambda ref, x, y, c: ref.at[c]
    handle = _exchange_start(tag + "_sibling_out", [], mine, [], [(a, my_half, a, my_half, _sibling) for a in range(n)])
    return dict(handle=handle, token=handle["token"])


def _reduce_end(state, after):
    return [b.reshape(2 * b.shape[1], b.shape[2]) for b in _exchange_wait(state["handle"], after)]


def _adamw_blocks(layer, w, m, v, g, earlier, n_blocks):
    n_layers, r, c = w.shape
    assert r % n_blocks == 0 and (r // n_blocks) % 8 == 0, (r, n_blocks)
    br = r // n_blocks
    in_layer = ((None, br, c), lambda b: (layer, b, 0))
    ins = [(a, *in_layer) for a in (w, m, v)] + [(g, (br, c), lambda b: (b, 0))] + [(a, None, None) for a in earlier or ()]
    outs = [(jax.ShapeDtypeStruct((n_layers, r, c), F32), *in_layer)] * 4
    aliases = {4 + i: i for i in range(len(earlier or ()))}

    def fn(wv, mv, vv, gv):
        return (gv,) + _adamw_math(wv, gv, mv, vv)

    return fn, ins, outs, aliases, n_blocks


def _adamw_layer(name, layer, w, m, v, g, earlier):
    c = w.shape[2]
    br = max(8, 2 ** int(math.log2(EW_BLOCK_ELEMS // c)))
    fn, ins, outs, aliases, n_blocks = _adamw_blocks(layer, w, m, v, g, earlier, w.shape[1] // min(br, w.shape[1]))
    n_data = sum(1 for _, bs, _ in ins if bs is not None)

    def body(*refs):
        res = fn(*[r[...] for r in refs[:n_data]])
        for o_ref, val in zip(refs[len(ins):], res):
            o_ref[...] = val

    spec = lambda bs, f: pl.BlockSpec(memory_space=pl.ANY) if bs is None else pl.BlockSpec(bs, lambda i, f=f: f(i))
    return pl.pallas_call(
        body, name=name, grid=(n_blocks,),
        in_specs=[spec(bs, f) for _, bs, f in ins], out_specs=[spec(bs, f) for _, bs, f in outs],
        out_shape=[o for o, _, _ in outs], input_output_aliases=aliases, compiler_params=_params(1),
    )(*[a for a, _, _ in ins])


SHARDED = ("w_in", "w_out", "w_ff1", "w_ff2")
REPLICATED = ("b_in", "w_pool", "pool_scale", "sgu_ln_g", "sgu_ln_b", "sgu_w", "sgu_b", "conv_b", "conv_ln_g", "conv_ln_b",
              "b_out", "ln1_g", "ln1_b", "b_ff1", "b_ff2", "ln2_g", "ln2_b")
WEIGHTS = ("w_in", "b_in", "w_pool", "pool_scale", "sgu_ln_g", "sgu_ln_b", "sgu_w", "sgu_b", "conv_w", "conv_b", "conv_ln_g",
           "conv_ln_b", "w_out", "b_out", "ln1_g", "ln1_b", "w_ff1", "b_ff1", "w_ff2", "b_ff2", "ln2_g", "ln2_b")
PACK_ROWS = 1024


def _pack(arrays):
    flat = jnp.concatenate([a.reshape(-1) for a in arrays])
    rows = -(-flat.shape[0] // (128 * PACK_ROWS)) * PACK_ROWS
    return jnp.pad(flat, (0, rows * 128 - flat.shape[0])).reshape(rows, 128)


def _unpack(packed, like):
    flat = packed.reshape(-1)
    out, at = [], 0
    for a in like:
        out.append(flat[at:at + a.size].reshape(a.shape))
        at += a.size
    return out


def kernel(x, w_in, b_in, w_pool, pool_scale, sgu_ln_g, sgu_ln_b, sgu_w, sgu_b, conv_w, conv_b, conv_ln_g, conv_ln_b, w_out, b_out, ln1_g, ln1_b, w_ff1, b_ff1, w_ff2, b_ff2, ln2_g, ln2_b, loss_target, m_w_in, m_b_in, m_w_pool, m_pool_scale, m_sgu_ln_g, m_sgu_ln_b, m_sgu_w, m_sgu_b, m_conv_w, m_conv_b, m_conv_ln_g, m_conv_ln_b, m_w_out, m_b_out, m_ln1_g, m_ln1_b, m_w_ff1, m_b_ff1, m_w_ff2, m_b_ff2, m_ln2_g, m_ln2_b, v_w_in, v_b_in, v_w_pool, v_pool_scale, v_sgu_ln_g, v_sgu_ln_b, v_sgu_w, v_sgu_b, v_conv_w, v_conv_b, v_conv_ln_g, v_conv_ln_b, v_w_out, v_b_out, v_ln1_g, v_ln1_b, v_w_ff1, v_b_ff1, v_w_ff2, v_b_ff2, v_ln2_g, v_ln2_b):
    w = dict(w_in=w_in, b_in=b_in, w_pool=w_pool, pool_scale=pool_scale, sgu_ln_g=sgu_ln_g, sgu_ln_b=sgu_ln_b, sgu_w=sgu_w,
             sgu_b=sgu_b, conv_w=conv_w, conv_b=conv_b, conv_ln_g=conv_ln_g, conv_ln_b=conv_ln_b, w_out=w_out, b_out=b_out,
             ln1_g=ln1_g, ln1_b=ln1_b, w_ff1=w_ff1, b_ff1=b_ff1, w_ff2=w_ff2, b_ff2=b_ff2, ln2_g=ln2_g, ln2_b=ln2_b)
    m = dict(w_in=m_w_in, b_in=m_b_in, w_pool=m_w_pool, pool_scale=m_pool_scale, sgu_ln_g=m_sgu_ln_g, sgu_ln_b=m_sgu_ln_b,
             sgu_w=m_sgu_w, sgu_b=m_sgu_b, conv_w=m_conv_w, conv_b=m_conv_b, conv_ln_g=m_conv_ln_g, conv_ln_b=m_conv_ln_b,
             w_out=m_w_out, b_out=m_b_out, ln1_g=m_ln1_g, ln1_b=m_ln1_b, w_ff1=m_w_ff1, b_ff1=m_b_ff1, w_ff2=m_w_ff2,
             b_ff2=m_b_ff2, ln2_g=m_ln2_g, ln2_b=m_ln2_b)
    v = dict(w_in=v_w_in, b_in=v_b_in, w_pool=v_w_pool, pool_scale=v_pool_scale, sgu_ln_g=v_sgu_ln_g, sgu_ln_b=v_sgu_ln_b,
             sgu_w=v_sgu_w, sgu_b=v_sgu_b, conv_w=v_conv_w, conv_b=v_conv_b, conv_ln_g=v_conv_ln_g, conv_ln_b=v_conv_ln_b,
             w_out=v_w_out, b_out=v_b_out, ln1_g=v_ln1_g, ln1_b=v_ln1_b, w_ff1=v_w_ff1, b_ff1=v_b_ff1, w_ff2=v_w_ff2,
             b_ff2=v_b_ff2, ln2_g=v_ln2_g, ln2_b=v_ln2_b)
    assert x.shape[0] == 1 and x.shape[2] == POOL_WIDTH + SGU_WIDTH + CONV_WIDTH, x.shape
    xs, target = x[0], loss_target[0]
    s_len, d = xs.shape
    n_layers = w_in.shape[0]
    dff = 4 * w_ff1.shape[2]
    conv_shard = conv_w.shape[2]

    cx, cy, cc = _place()
    me = 2 * cx + cy
    place = jnp.stack([cc, me]).astype(jnp.int32)

    def cast(key, after=None):
        return _into_slot(f"cast_{key[0]}{key[1]}", w[key[0]], key[1], place, BF16, after=after)

    mix0 = [("w_in", 0), ("w_out", 0), "conv_w"]
    ff0 = [("w_ff1", 0), ("w_ff2", 0)]
    conv_padded = jnp.pad(conv_w, ((0, 0), (0, CONV_ROWS - CONV_KERNEL), (0, 0))).reshape(1, n_layers * CONV_ROWS, conv_shard)
    slot = {"conv_w": _into_slot("slot_conv_w", conv_padded, 0, place, F32), ("w_in", 0): cast(("w_in", 0)),
            ("w_out", 0): cast(("w_out", 0))}
    mix0_flight = _gather_start("gather_mix0", [slot[k] for k in mix0])
    slot.update({k: cast(k, mix0_flight["token"]) for k in ff0})
    ff0_flight = _gather_start("gather_ff0", [slot[k] for k in ff0])
    slot.update({(n, l): cast((n, l), ff0_flight["token"]) for n in SHARDED for l in range(1, n_layers)})
    xb0 = _rowwise("cast_x", lambda t: (t,), [xs], [BF16])[0]
    gathered = dict(zip(mix0, _gather_finish("gather_mix0", mix0_flight, slot[(SHARDED[-1], n_layers - 1)])))
    conv_full = jnp.transpose(gathered["conv_w"].reshape(4, n_layers, CONV_ROWS, conv_shard), (1, 2, 0, 3))
    conv_full = conv_full.reshape(n_layers, CONV_ROWS, CONV_WIDTH)
    conv_rev = jnp.pad(conv_full[:, CONV_KERNEL - 1::-1], ((0, 0), (0, CONV_ROWS - CONV_KERNEL), (0, 0)))

    def layer_weights(l):
        return dict(w_pool=w_pool[l], pool_scale=pool_scale[l], sgu_ln_g=sgu_ln_g[l], sgu_ln_b=sgu_ln_b[l], sgu_w=sgu_w[l],
                    sgu_w_t=jnp.transpose(sgu_w[l], (0, 2, 1)),
                    sgu_bias_tile=jnp.broadcast_to(sgu_b[l][:, :, None], (SGU_HEADS, CHUNK, GROUP)),
                    conv_w_full=conv_full[l], conv_w_rev=conv_rev[l], conv_b=conv_b[l], conv_ln_g=conv_ln_g[l],
                    conv_ln_b=conv_ln_b[l])

    saved = []
    x_cur, xb_cur = xs, xb0
    for l in range(n_layers):
        lw = layer_weights(l)
        w_out_full = gathered[("w_out", l)].reshape(d, d)
        p = _proj(f"proj{l}", xb_cur, gathered[("w_in", l)], b_in[l], after=ff0_flight["token"] if l == 0 else None)
        mixed, cv = _mix_forward(f"mix_fwd{l}", p, lw)
        r1, x1b = _mix_out(f"mix_out{l}", mixed, w_out_full, b_out[l], x_cur, ln1_g[l], ln1_b[l])
        if l == 0:
            gathered.update(zip(ff0, _gather_finish("gather_ff0", ff0_flight, x1b)))
        nxt = [(n, l + 1) for n in SHARDED] if l + 1 < n_layers else None
        in_flight = _gather_start(f"gather_layer{l + 1}", [slot[k] for k in nxt]) if nxt else None
        w2_full = gathered[("w_ff2", l)].reshape(dff, d)
        hf, zr = _ff1(f"ff1_{l}", x1b, gathered[("w_ff1", l)], b_ff1[l], after=in_flight and in_flight["token"])
        fo = _ff2(f"ff2_{l}", hf, w2_full, b_ff2[l])
        sv = dict(lw=lw, xb_in=xb_cur, p=p, mixed=mixed, cv=cv, r1=r1, x1b=x1b, hf=hf, zr=zr, fo=fo,
                  w_out_full=w_out_full, w2_full=w2_full)
        if nxt:
            in_flight = _gather_pass_on(f"gather_layer{l + 1}", in_flight, fo)
            sv["r2"], x_cur, xb_cur = _resid_ln2(f"ln2_{l}", r1, ln1_g[l], ln1_b[l], fo, ln2_g[l], ln2_b[l],
                                                 after=in_flight["token"])
            gathered.update(zip(nxt, _exchange_wait(in_flight, xb_cur)))
        saved.append(sv)

    grads = {n: [None] * n_layers for n in REPLICATED + ("conv_w",)}
    g_final, delta, new_m, new_v = {}, {}, {}, {}
    results = {n: None for n in SHARDED}
    dx_mm, dx_resid = None, None
    loss_tile = None
    pending = None

    jobs = []

    def finish_reduce(begun, after):
        names, layer, state = begun
        reduced = _reduce_end(state, after)
        jobs.extend((n, layer, g) for n, g in zip(names, reduced))
        return reduced

    def hosted(call):
        taken = []

        def rider(steps):
            fits = [j for j in jobs if w[j[0]].shape[1] % steps == 0 and (w[j[0]].shape[1] // steps) % 8 == 0]
            if not fits:
                return None
            job = max(fits, key=lambda j: w[j[0]].shape[1] * w[j[0]].shape[2])
            jobs.remove(job)
            taken.append(job)
            n, layer, g = job
            return _adamw_blocks(layer, w[n], m[n], v[n], g, results[n], steps)

        out = call(rider)
        if not taken:
            return out
        results[taken[0][0]] = tuple(out[1])
        return out[0]

    for l in reversed(range(n_layers)):
        sv = saved[l]
        if dx_mm is None:
            dr2, dr2b, dg2, db2, dbff2, loss_tile = _ln_backward(
                f"ln2_bwd{l}", None, ln2_g[l], last=(sv["r1"], ln1_g[l], ln1_b[l], sv["fo"], ln2_b[l], target))
        else:
            dr2, dr2b, dg2, db2, dbff2 = _ln_backward(f"ln2_bwd{l}", sv["r2"], ln2_g[l], dy=dx_mm, resid=dx_resid)
        dzb, dbff1 = hosted(lambda rider: _dff_hidden(f"ff2_bwd{l}", dr2b, sv["w2_full"], sv["zr"], rider=rider))
        dw = {"w_ff2": _matmul_tn(f"dw_ff2_{l}", sv["hf"], dr2b, 4, dff // 4, d, True),
              "w_ff1": _matmul_tn(f"dw_ff1_{l}", sv["x1b"], dzb, 4, d, dff // 4, False)}
        if pending is not None:
            names, layer, state = pending
            pending = (names, layer, _reduce_swap(state, dw["w_ff1"]))
        ff_red = _reduce_begin(f"grads_ff{l}", place, [(dw[n], True, BF16) for n in ("w_ff1", "w_ff2")])
        if pending is not None:
            finish_reduce(pending, ff_red["token"])
            pending = None
        dx1 = _dx_sharded(f"ff1_bwd{l}", dzb, gathered[("w_ff1", l)], after=ff_red["token"])
        ff_red = _reduce_between_chips(ff_red, dx1)
        dr1, dr1b, dg1, db1, dbout = _ln_backward(f"ln1_bwd{l}", sv["r1"], ln1_g[l], dy=dx1, resid=dr2, after=ff_red["token"])
        dmixed = hosted(lambda rider: _dmixed(f"mix_out_bwd{l}", dr1b, sv["w_out_full"], rider=rider))
        dw["w_out"] = _matmul_tn(f"dw_out{l}", sv["mixed"], dr1b, 4, d // 4, d, True)
        (dp, dbin, dwp, dps, dslg, dslb, dws, dsb_tile, dcw, dcb, dclg, dclb) = _mix_backward(
            f"mix_bwd{l}", sv["p"], sv["cv"], dmixed, sv["lw"])
        dw["w_in"] = _matmul_tn(f"dw_in{l}", sv["xb_in"], dp, 4, d, IN_WIDTH // 4, False)
        for name, g in (("b_in", dbin), ("w_pool", dwp), ("pool_scale", dps), ("sgu_ln_g", dslg), ("sgu_ln_b", dslb),
                        ("sgu_w", dws), ("sgu_b", dsb_tile[:, :, 0]), ("conv_w", dcw[:CONV_KERNEL]), ("conv_b", dcb),
                        ("conv_ln_g", dclg), ("conv_ln_b", dclb), ("b_out", dbout), ("ln1_g", dg1), ("ln1_b", db1),
                        ("b_ff1", dbff1), ("b_ff2", dbff2), ("ln2_g", dg2), ("ln2_b", db2)):
            grads[name][l] = g.reshape(w[name].shape[1:]) if name != "conv_w" else g

        items = [(dw[n], True, BF16) for n in ("w_in", "w_out")]
        if l == 0:
            small_like = [jnp.stack(grads[n]) for n in REPLICATED + ("conv_w",)]
            items.append((_pack(small_like)[None], False, F32))
        mix_red = _reduce_begin(f"grads_mix{l}", place, items)
        ff_red = _reduce_swap(ff_red, mix_red["token"])
        mix_red = _reduce_between_chips(mix_red, ff_red["token"])
        finish_reduce((("w_ff1", "w_ff2"), l, ff_red), mix_red["token"])
        if l > 0:
            dx_mm = hosted(lambda rider: _dx_sharded(f"proj_bwd{l}", dp, gathered[("w_in", l)], after=mix_red["token"],
                                                     rider=rider))
            dx_resid = dr1
            pending = (("w_in", "w_out"), l, mix_red)
        else:
            grad_x = hosted(lambda rider: _dx_sharded(f"proj_bwd{l}", dp, gathered[("w_in", l)], dr1,
                                                      after=mix_red["token"], rider=rider))
            mix_red = _reduce_swap(mix_red, grad_x)
            reduced_small = finish_reduce((("w_in", "w_out"), l, mix_red), mix_red["token"])[-1]
            grad_x = grad_x[None]

    for n, layer, g in jobs:
        results[n] = _adamw_layer(f"adamw_{n}{layer}", layer, w[n], m[n], v[n], g, results[n])
    loss = lax.psum(loss_tile[0, 0], ("x", "y", "c"))
    for n in SHARDED:
        g_final[n], delta[n], new_m[n], new_v[n] = results[n]
    unpacked = _unpack(reduced_small, small_like)
    g_final.update(zip(REPLICATED, unpacked[:-1]))
    g_final["conv_w"] = lax.dynamic_slice_in_dim(unpacked[-1], me * conv_shard, conv_shard, axis=2)

    delta["conv_w"], new_m["conv_w"], new_v["conv_w"] = _adamw("adamw_conv_w", conv_w, g_final["conv_w"], m["conv_w"], v["conv_w"])
    packed = [_pack([t[n] for n in REPLICATED]) for t in (w, g_final, m, v)]
    like = [w[n] for n in REPLICATED]
    for res, packed_out in zip((delta, new_m, new_v), _adamw("adamw_replicated", *packed)):
        res.update(zip(REPLICATED, _unpack(packed_out, like)))

    return (loss, grad_x, *[g_final[n] for n in WEIGHTS], *[delta[n] for n in WEIGHTS],
            *[new_m[n] for n in WEIGHTS], *[new_v[n] for n in WEIGHTS])
```

```python
import functools
import math

import jax
import jax.numpy as jnp
from jax import lax
from jax.experimental import pallas as pl
from jax.experimental.pallas import tpu as pltpu

F32 = jnp.float32
BF16 = jnp.bfloat16
MESH = pl.DeviceIdType.MESH

DEPTH = 2
POOL_WINDOWS = (2, 4, 8, 16)
POOL_WIDTH = 512
GROUP = 128
SGU_WIDTH = 768
SGU_HEADS = 6
CHUNK = 128
CONV_WIDTH = 768
CONV_KERNEL = 31
CONV_ROWS = 32
HALO = 32
COL_B = POOL_WIDTH
COL_C = POOL_WIDTH + 2 * SGU_WIDTH
IN_WIDTH = COL_C + 2 * CONV_WIDTH
ALPHA = (2 * DEPTH) ** 0.25
LN_EPS = 1e-5
ADAM_LR = 0.001
ADAM_B1 = 0.9
ADAM_B2 = 0.999
ADAM_EPS = 1e-08
ADAM_WD = 0.01
ADAM_STEP = 10
GELU_C = math.sqrt(2.0 / math.pi)
GELU_A = 0.044715

V7X_VMEM_BYTES = 64 * 2 ** 20
VMEM_LIMIT = 56 * 2 ** 20

BM_MM = 1024
BN_MM = 1024
BM_LN = 512
BK_MM = 1024
BK_WIDE = 2048
BS_TN = 2048
TN_OUT_BYTES = 8 * 2 ** 20
TS_MIX = 256
RB_CONV = 64
CONV_LIVE_BLOCKS = 2
BR_EW = 512
EW_BLOCK_ELEMS = 2 ** 18


def _tile(n, pref):
    t = min(n, pref)
    assert n % t == 0, (n, pref)
    return t


def _params(n_grid):
    return pltpu.CompilerParams(dimension_semantics=("arbitrary",) * n_grid, vmem_limit_bytes=VMEM_LIMIT)


def _sigmoid(x):
    return 1.0 / (1.0 + jnp.exp(-x))


def _gelu(x):
    return 0.5 * x * (1.0 + jnp.tanh(GELU_C * (x + GELU_A * x * x * x)))


def _gelu_with_grad(x):
    t = jnp.tanh(GELU_C * (x + GELU_A * x * x * x))
    half = 0.5 * (1.0 + t)
    return x * half, half + 0.5 * x * (1.0 - t * t) * GELU_C * (1.0 + 3.0 * GELU_A * x * x)


def _ln_stats(r):
    mu = jnp.mean(r, axis=-1, keepdims=True)
    xc = r - mu
    var = jnp.mean(xc * xc, axis=-1, keepdims=True)
    rstd = lax.rsqrt(var + LN_EPS)
    return xc * rstd, rstd


def _ln_bwd(dy, xhat, rstd, g):
    dxh = dy * g
    m1 = jnp.mean(dxh, axis=-1, keepdims=True)
    m2 = jnp.mean(dxh * xhat, axis=-1, keepdims=True)
    return rstd * (dxh - m1 - xhat * m2)


def _colsum(x):
    return jnp.sum(x, axis=0, keepdims=True)


def _accumulate(ref, val, first):
    @pl.when(first)
    def _():
        ref[...] = val

    @pl.when(jnp.logical_not(first))
    def _():
        ref[...] += val


def _matmul(name, grid, a, a_spec, b, b_spec, *, nt, extras, outs, epilogue, after=None, rider=None):
    ni, nj, nk = grid
    if after is not None:
        extras = list(extras) + [(after, pl.BlockSpec(memory_space=pl.ANY))]
    ne, no = len(extras), len(outs)
    dims = (((1,), (1,)), ((), ())) if nt else (((1,), (0,)), ((), ()))
    r_fn, r_ins, r_outs, r_aliases, r_blocks = rider or (None, [], [], {}, 0)
    assert r_blocks <= ni * nj * nk
    n_ri = len(r_ins)
    n_data = sum(1 for _, bs, _ in r_ins if bs is not None)
    step = lambda i, j, k: (i * nj + j) * nk + k

    def body(*refs):
        a_ref, b_ref = refs[0], refs[1]
        ex = refs[2:2 + ne]
        out_refs = refs[2 + ne + n_ri:2 + ne + n_ri + no]
        ids = (pl.program_id(0), pl.program_id(1), pl.program_id(2))
        part = lax.dot_general(a_ref[...], b_ref[...], dims, preferred_element_type=F32)
        if nk == 1:
            epilogue(part, ex, out_refs, ids)
        else:
            @pl.when(ids[2] == 0)
            def _():
                epilogue(part, ex, out_refs, ids)

            @pl.when(ids[2] > 0)
            def _():
                out_refs[0][...] += part
        if rider is not None:
            @pl.when(step(*ids) < r_blocks)
            def _():
                res = r_fn(*[r[...] for r in refs[2 + ne:2 + ne + n_data]])
                for o_ref, val in zip(refs[2 + ne + n_ri + no:], res):
                    o_ref[...] = val

    def rider_spec(bs, f):
        if bs is None:
            return pl.BlockSpec(memory_space=pl.ANY)
        return pl.BlockSpec(bs, lambda i, j, k: f(jnp.minimum(step(i, j, k), r_blocks - 1)))

    return pl.pallas_call(
        body,
        name=name,
        grid=grid,
        in_specs=[a_spec, b_spec] + [s for _, s in extras] + [rider_spec(bs, f) for _, bs, f in r_ins],
        out_specs=[s for _, s in outs] + [rider_spec(bs, f) for _, bs, f in r_outs],
        out_shape=[o for o, _ in outs] + [o for o, _, _ in r_outs],
        input_output_aliases={2 + ne + i: no + o for i, o in r_aliases.items()},
        compiler_params=_params(3),
    )(a, b, *[e for e, _ in extras], *[arr for arr, _, _ in r_ins])


def _matmul_tn(name, a, b, n_shards, shard_rows, shard_cols, row_sharded):
    s_len, ka = a.shape
    n = b.shape[1]
    assert (n_shards * shard_rows, shard_cols) == (ka, n) if row_sharded else (shard_rows, n_shards * shard_cols) == (ka, n)
    bs = _tile(s_len, BS_TN)
    bka = _tile(shard_rows, 2048)
    bn = next((t for t in (2 * BN_MM, BN_MM) if shard_cols % t == 0 and bka * t * 4 <= TN_OUT_BYTES), shard_cols)
    ni, nj, ns = ka // bka, n // bn, s_len // bs
    per_shard_i = shard_rows // bka
    per_shard_j = shard_cols // bn

    if row_sharded:
        out_map = lambda i, j, s: (i // per_shard_i, i % per_shard_i, j)
    else:
        out_map = lambda i, j, s: (j // per_shard_j, i, j % per_shard_j)

    def body(a_ref, b_ref, o_ref):
        s = pl.program_id(2)
        part = lax.dot_general(a_ref[...], b_ref[...], (((0,), (0,)), ((), ())), preferred_element_type=F32)
        _accumulate(o_ref, part, s == 0)

    return pl.pallas_call(
        body,
        name=name,
        grid=(ni, nj, ns),
        in_specs=[pl.BlockSpec((bs, bka), lambda i, j, s: (s, i)), pl.BlockSpec((bs, bn), lambda i, j, s: (s, j))],
        out_specs=pl.BlockSpec((None, bka, bn), out_map),
        out_shape=jax.ShapeDtypeStruct((n_shards, shard_rows, shard_cols), F32),
        compiler_params=_params(3),
    )(a, b)


def _row(v):
    return v.reshape(1, -1)


def _proj(name, xb, w_g, b_in, after=None):
    s_len, d = xb.shape
    ncs = w_g.shape[2]
    bm = _tile(s_len, BM_MM)

    def epilogue(acc, ex, outs, ids):
        outs[0][...] = acc + ex[0][...]

    return _matmul(
        name, (s_len // bm, 4, 1),
        xb, pl.BlockSpec((bm, d), lambda i, j, k: (i, 0)),
        w_g, pl.BlockSpec((None, d, ncs), lambda i, j, k: (j, 0, 0)),
        nt=False,
        extras=[(_row(b_in), pl.BlockSpec((1, ncs), lambda i, j, k: (0, j)))],
        outs=[(jax.ShapeDtypeStruct((s_len, 4 * ncs), F32), pl.BlockSpec((bm, ncs), lambda i, j, k: (i, j)))],
        epilogue=epilogue, after=after,
    )[0]


def _mix_out(name, mixed, w_out_full, b_out, x0, g1, b1, prev_ln=None, after=None):
    s_len, d = mixed.shape
    bm = _tile(s_len, BM_LN // 2)
    row = pl.BlockSpec((1, d), lambda i, j, k: (0, 0))
    blk = pl.BlockSpec((bm, d), lambda i, j, k: (i, 0))

    def epilogue(acc, ex, outs, ids):
        x_in = ex[1][...]
        if prev_ln is not None:
            xhat_in, _ = _ln_stats(x_in)
            x_in = xhat_in * ex[4][...] + ex[5][...]
        r1 = ALPHA * x_in + (acc + ex[0][...])
        outs[0][...] = r1
        xhat, _ = _ln_stats(r1)
        outs[1][...] = (xhat * ex[2][...] + ex[3][...]).astype(BF16)

    return _matmul(
        name, (s_len // bm, 1, 1),
        mixed, blk,
        w_out_full, pl.BlockSpec((d, d), lambda i, j, k: (0, 0)),
        nt=False,
        extras=[(_row(b_out), row), (x0, blk), (_row(g1), row), (_row(b1), row)] + [(_row(t), row) for t in prev_ln or ()],
        outs=[(jax.ShapeDtypeStruct((s_len, d), F32), blk), (jax.ShapeDtypeStruct((s_len, d), BF16), blk)],
        epilogue=epilogue, after=after,
    )


def _ff1(name, x1b, w_g, b_ff1, after=None):
    s_len, d = x1b.shape
    ncs = w_g.shape[2]
    bm = _tile(s_len, BM_MM)
    bn = _tile(ncs, BN_MM)
    per = ncs // bn
    blk = pl.BlockSpec((bm, bn), lambda i, j, k: (i, j))

    def epilogue(acc, ex, outs, ids):
        zr = jnp.maximum(acc + ex[0][...], 0.0)
        outs[0][...] = (zr * zr).astype(BF16)
        outs[1][...] = zr.astype(BF16)

    shape = jax.ShapeDtypeStruct((s_len, 4 * ncs), BF16)
    return _matmul(
        name, (s_len // bm, 4 * per, 1),
        x1b, pl.BlockSpec((bm, d), lambda i, j, k: (i, 0)),
        w_g, pl.BlockSpec((None, d, bn), lambda i, j, k: (j // per, 0, j % per)),
        nt=False,
        extras=[(_row(b_ff1), pl.BlockSpec((1, bn), lambda i, j, k: (0, j)))],
        outs=[(shape, blk), (shape, blk)],
        epilogue=epilogue, after=after,
    )


def _ff2(name, hf, w2_full, b_ff2):
    s_len, dff = hf.shape
    d = w2_full.shape[1]
    bm = _tile(s_len, BM_MM)
    bk = _tile(dff, BK_WIDE)
    blk = pl.BlockSpec((bm, d), lambda i, j, k: (i, 0))

    def epilogue(acc, ex, outs, ids):
        outs[0][...] = acc + ex[0][...]

    return _matmul(
        name, (s_len // bm, 1, dff // bk),
        hf, pl.BlockSpec((bm, bk), lambda i, j, k: (i, k)),
        w2_full, pl.BlockSpec((bk, d), lambda i, j, k: (k, 0)),
        nt=False,
        extras=[(_row(b_ff2), pl.BlockSpec((1, d), lambda i, j, k: (0, 0)))],
        outs=[(jax.ShapeDtypeStruct((s_len, d), F32), blk)],
        epilogue=epilogue,
    )[0]


def _resid_ln2(name, r1, g1, b1, fo, g2, b2, after=None):
    s_len, d = r1.shape
    bm = _tile(s_len, BM_LN // 2)
    blk = pl.BlockSpec((bm, d), lambda i: (i, 0))
    row = pl.BlockSpec((1, d), lambda i: (0, 0))
    order = [] if after is None else [after]

    def body(*refs):
        r1_ref, g1_ref, b1_ref, fo_ref, g2_ref, b2_ref, r2_ref, x2b_ref = refs[len(order):]
        xhat1, _ = _ln_stats(r1_ref[...])
        r2 = ALPHA * (xhat1 * g1_ref[...] + b1_ref[...]) + fo_ref[...]
        r2_ref[...] = r2
        xhat2, _ = _ln_stats(r2)
        x2b_ref[...] = (xhat2 * g2_ref[...] + b2_ref[...]).astype(BF16)

    return pl.pallas_call(
        body, name=name, grid=(s_len // bm,),
        in_specs=[pl.BlockSpec(memory_space=pl.ANY)] * len(order) + [blk, row, row, blk, row, row], out_specs=[blk, blk],
        out_shape=[jax.ShapeDtypeStruct((s_len, d), F32), jax.ShapeDtypeStruct((s_len, d), BF16)],
        compiler_params=_params(1))(*order, r1, _row(g1), _row(b1), fo, _row(g2), _row(b2))


def _dff_hidden(name, dr2b, w2_full, zr, rider=None):
    s_len, d = dr2b.shape
    dff = w2_full.shape[0]
    bm = _tile(s_len, BM_MM)
    bn = _tile(dff, BN_MM)

    def epilogue(acc, ex, outs, ids):
        dz = acc * (2.0 * ex[0][...].astype(F32))
        outs[0][...] = dz.astype(BF16)
        _accumulate(outs[1], _colsum(dz), ids[1] == 0)

    grid = (dff // bn, s_len // bm, 1)
    rider = rider and rider(math.prod(grid))
    res = _matmul(
        name, grid,
        dr2b, pl.BlockSpec((bm, d), lambda j, i, k: (i, 0)),
        w2_full, pl.BlockSpec((bn, d), lambda j, i, k: (j, 0)),
        nt=True,
        extras=[(zr, pl.BlockSpec((bm, bn), lambda j, i, k: (i, j)))],
        outs=[(jax.ShapeDtypeStruct((s_len, dff), BF16), pl.BlockSpec((bm, bn), lambda j, i, k: (i, j))),
              (jax.ShapeDtypeStruct((1, dff), F32), pl.BlockSpec((1, bn), lambda j, i, k: (0, j)))],
        epilogue=epilogue, rider=rider,
    )
    return tuple(res) if rider is None else (tuple(res[:2]), res[2:])


def _dx_sharded(name, dyb, w_g, resid=None, after=None, rider=None):
    s_len = dyb.shape[0]
    d, ncs = w_g.shape[1], w_g.shape[2]
    bm = _tile(s_len, BM_MM if resid is None else BM_LN)
    bk = _tile(ncs, BK_WIDE) if ncs % BK_MM == 0 else ncs
    per = ncs // bk
    blk = pl.BlockSpec((bm, d), lambda i, j, k: (i, 0))

    def epilogue(acc, ex, outs, ids):
        outs[0][...] = acc if resid is None else acc + ALPHA * ex[0][...]

    rider = rider and rider(s_len // bm * 4 * per)
    res = _matmul(
        name, (s_len // bm, 1, 4 * per),
        dyb, pl.BlockSpec((bm, bk), lambda i, j, k: (i, k)),
        w_g, pl.BlockSpec((None, d, bk), lambda i, j, k: (k // per, 0, k % per)),
        nt=True,
        extras=[] if resid is None else [(resid, blk)],
        outs=[(jax.ShapeDtypeStruct((s_len, d), F32), blk)],
        epilogue=epilogue, after=after, rider=rider,
    )
    return res[0] if rider is None else (res[0], res[1:])


def _dmixed(name, dr1b, w_out_full, rider=None):
    s_len, d = dr1b.shape
    bm = _tile(s_len, BM_LN)
    blk = pl.BlockSpec((bm, d), lambda i, j, k: (i, 0))

    def epilogue(acc, ex, outs, ids):
        outs[0][...] = acc

    rider = rider and rider(s_len // bm)
    res = _matmul(
        name, (s_len // bm, 1, 1),
        dr1b, blk,
        w_out_full, pl.BlockSpec((d, d), lambda i, j, k: (0, 0)),
        nt=True, extras=[],
        outs=[(jax.ShapeDtypeStruct((s_len, d), F32), blk)],
        epilogue=epilogue, rider=rider,
    )
    return res[0] if rider is None else (res[0], res[1:])


def _ln_backward(name, r, g, *, dy=None, resid=None, last=None, after=None):
    from_loss = last is not None
    s_len, d = last[0].shape if from_loss else r.shape
    bm = _tile(s_len, BM_LN // 2)
    blk = pl.BlockSpec((bm, d), lambda i: (i, 0))
    row = pl.BlockSpec((1, d), lambda i: (0, 0))

    def body(*refs):
        i = pl.program_id(0)
        if after is not None:
            refs = refs[1:]
        if from_loss:
            r1_ref, fo_ref, t_ref, g1_ref, b1_ref, b_ref, g_ref, dr_ref, drb_ref, dg_ref, db_ref, dbias_ref, loss_ref = refs
            xhat1, _ = _ln_stats(r1_ref[...])
            rv = ALPHA * (xhat1 * g1_ref[...] + b1_ref[...]) + fo_ref[...]
            xhat, rstd = _ln_stats(rv)
            diff = (xhat * g_ref[...] + b_ref[...]) - t_ref[...]
            dyv = diff * (1.0 / d)
            part = 0.5 * jnp.sum(jnp.sum(diff * diff, axis=1, keepdims=True) * (1.0 / d), axis=0, keepdims=True)
            _accumulate(loss_ref, jnp.broadcast_to(part, loss_ref.shape), i == 0)
        else:
            if resid is not None:
                dy_ref, res_ref, r_ref, g_ref, dr_ref, drb_ref, dg_ref, db_ref, dbias_ref = refs
                dyv = dy_ref[...] + ALPHA * res_ref[...]
            else:
                dy_ref, r_ref, g_ref, dr_ref, drb_ref, dg_ref, db_ref, dbias_ref = refs
                dyv = dy_ref[...]
            xhat, rstd = _ln_stats(r_ref[...])
        dr = _ln_bwd(dyv, xhat, rstd, g_ref[...])
        dr_ref[...] = dr
        drb_ref[...] = dr.astype(BF16)
        _accumulate(dg_ref, _colsum(dyv * xhat), i == 0)
        _accumulate(db_ref, _colsum(dyv), i == 0)
        _accumulate(dbias_ref, _colsum(dr), i == 0)

    if from_loss:
        r1, g1, b1, fo, b, target = last
        ins = [r1, fo, target, _row(g1), _row(b1), _row(b), _row(g)]
        in_specs = [blk] * 3 + [row] * 4
    else:
        lead = [dy] if resid is None else [dy, resid]
        ins = lead + [r, _row(g)]
        in_specs = [blk] * len(lead) + [blk, row]
    if after is not None:
        ins, in_specs = [after] + ins, [pl.BlockSpec(memory_space=pl.ANY)] + in_specs
    out_shape = [jax.ShapeDtypeStruct((s_len, d), F32), jax.ShapeDtypeStruct((s_len, d), BF16)] + \
                [jax.ShapeDtypeStruct((1, d), F32)] * 3
    out_specs = [blk, blk, row, row, row]
    if from_loss:
        out_shape.append(jax.ShapeDtypeStruct((8, 128), F32))
        out_specs.append(pl.BlockSpec((8, 128), lambda i: (0, 0)))
    return pl.pallas_call(body, name=name, grid=(s_len // bm,), in_specs=in_specs, out_specs=out_specs,
                          out_shape=out_shape, compiler_params=_params(1))(*ins)


def _rowwise(name, fn, ins, out_dtypes, rows_pref=BR_EW):
    r, c = ins[0].shape
    br = _tile(r, rows_pref)
    blk = pl.BlockSpec((br, c), lambda i: (i, 0))

    def body(*refs):
        res = fn(*[ref[...] for ref in refs[:len(ins)]])
        for o_ref, v in zip(refs[len(ins):], res):
            o_ref[...] = v.astype(o_ref.dtype)

    return pl.pallas_call(body, name=name, grid=(r // br,), in_specs=[blk] * len(ins),
                          out_specs=[blk] * len(out_dtypes),
                          out_shape=[jax.ShapeDtypeStruct((r, c), dt) for dt in out_dtypes],
                          compiler_params=_params(1))(*ins)


def _adamw_math(w, g, m, v):
    m = ADAM_B1 * m + (1.0 - ADAM_B1) * g
    v = ADAM_B2 * v + (1.0 - ADAM_B2) * (g * g)
    m_hat = m / (1.0 - ADAM_B1 ** ADAM_STEP)
    v_hat = v / (1.0 - ADAM_B2 ** ADAM_STEP)
    delta = -ADAM_LR * (m_hat / (jnp.sqrt(v_hat) + ADAM_EPS) + ADAM_WD * w)
    return delta, m, v


def _adamw(name, w, g, m, v):
    shape = w.shape
    c = shape[-1]
    flat = [a.reshape(-1, c) for a in (w, g, m, v)]
    rows = flat[0].shape[0]
    pref = max(8, 2 ** int(math.log2(EW_BLOCK_ELEMS // c)))
    res = _rowwise(name, _adamw_math, flat, [F32, F32, F32], rows_pref=pref if rows % pref == 0 else rows)
    return tuple(a.reshape(shape) for a in res)


def _pool_means(ext_ref, ts, tile_index):
    t_glob = tile_index * ts + lax.broadcasted_iota(jnp.int32, (ts, GROUP), 0)
    qs = []
    for g, win in enumerate(POOL_WINDOWS):
        cols = pl.ds(g * GROUP, GROUP)
        cur = ext_ref[pl.ds(16, ts), cols]
        acc = cur
        for j in range(1, win):
            acc = acc + ext_ref[pl.ds(16 - j, ts), cols]
        cnt = jnp.minimum(t_glob + 1, win).astype(F32)
        qs.append(acc / cnt - cur)
    return qs


def _masked_sgu_w(w_ref, h):
    r = lax.broadcasted_iota(jnp.int32, (CHUNK, CHUNK), 0)
    c = lax.broadcasted_iota(jnp.int32, (CHUNK, CHUNK), 1)
    return jnp.where(r >= c, w_ref[h], 0.0)


SUBLANES = 8


def _fill_shifted(src_ref, shifted_ref, ts):
    rows = ts + HALO - SUBLANES
    for b in range(1, SUBLANES):
        shifted_ref[b - 1] = src_ref[pl.ds(b, rows), :]


def _shifted_rows(src_ref, shifted_ref, offset, rows, cols):
    a, b = divmod(offset, SUBLANES)
    if b == 0:
        return src_ref[pl.ds(offset, rows), cols]
    return shifted_ref[b - 1, pl.ds(a * SUBLANES, rows), cols]


def _conv_taps(acc_init, w_ref, src_ref, shifted_ref, ts, base, emit):
    nrb = ts // RB_CONV
    for cg in range(CONV_WIDTH // GROUP):
        cols = pl.ds(cg * GROUP, GROUP)
        for rb0 in range(0, nrb, CONV_LIVE_BLOCKS):
            group = range(rb0, min(rb0 + CONV_LIVE_BLOCKS, nrb))
            accs = {rb: acc_init(cg) for rb in group}
            for k in range(CONV_KERNEL):
                wk = jnp.broadcast_to(w_ref[pl.ds(k, 1), cols], (RB_CONV, GROUP))
                for rb in group:
                    accs[rb] = accs[rb] + wk * _shifted_rows(src_ref, shifted_ref, rb * RB_CONV + base(k), RB_CONV, cols)
            for rb in group:
                emit(rb * RB_CONV, cg * GROUP, accs[rb])


def _mix_forward(name, p, lw):
    s_len = p.shape[0]
    ts = _tile(s_len, TS_MIX)
    per_halo = ts // HALO
    d = POOL_WIDTH + SGU_WIDTH + CONV_WIDTH

    def body(p_ref, ph_ref, wp_ref, ps_ref, slg_ref, slb_ref, ws_ref, sb_ref, cw_ref, cb_ref, clg_ref, clb_ref,
             mixed_ref, cv_ref, pool_ext, hh_ext, shifted):
        i = pl.program_id(0)
        keep = (i > 0).astype(F32)
        pool_ext[pl.ds(0, 16), :] = ph_ref[pl.ds(16, 16), pl.ds(0, POOL_WIDTH)] * keep
        pool_ext[pl.ds(16, ts), :] = p_ref[:, pl.ds(0, POOL_WIDTH)]
        qs = _pool_means(pool_ext, ts, i)
        for g in range(len(POOL_WINDOWS)):
            cols = pl.ds(g * GROUP, GROUP)
            e = jnp.dot(qs[g].astype(BF16), wp_ref[g].astype(BF16), preferred_element_type=F32)
            mixed_ref[:, cols] = (e * ps_ref[:, cols]).astype(BF16)
        uv = _gelu(p_ref[:, pl.ds(COL_B, 2 * SGU_WIDTH)])
        u = uv[:, :SGU_WIDTH]
        vhat, _ = _ln_stats(uv[:, SGU_WIDTH:])
        vn = (vhat * slg_ref[...] + slb_ref[...]).astype(BF16)
        for h in range(SGU_HEADS):
            wm = _masked_sgu_w(ws_ref, h).astype(BF16)
            for n in range(ts // CHUNK):
                rows = slice(n * CHUNK, (n + 1) * CHUNK)
                cols = slice(h * GROUP, (h + 1) * GROUP)
                mx = jnp.dot(wm, vn[rows, cols], preferred_element_type=F32) + sb_ref[h]
                mixed_ref[pl.ds(n * CHUNK, CHUNK), pl.ds(POOL_WIDTH + h * GROUP, GROUP)] = (u[rows, cols] * mx).astype(BF16)
        hh_ext[pl.ds(0, HALO), :] = (ph_ref[:, pl.ds(COL_C, CONV_WIDTH)]
                                     * _sigmoid(ph_ref[:, pl.ds(COL_C + CONV_WIDTH, CONV_WIDTH)])) * keep
        hh_ext[pl.ds(HALO, ts), :] = p_ref[:, pl.ds(COL_C, CONV_WIDTH)] * _sigmoid(p_ref[:, pl.ds(COL_C + CONV_WIDTH, CONV_WIDTH)])
        init = lambda cg: jnp.broadcast_to(cb_ref[:, pl.ds(cg * GROUP, GROUP)], (RB_CONV, GROUP))
        _fill_shifted(hh_ext, shifted, ts)
        def store_cv(r0, c0, blk):
            cv_ref[pl.ds(r0, RB_CONV), pl.ds(c0, GROUP)] = blk

        _conv_taps(init, cw_ref, hh_ext, shifted, ts, lambda k: k + HALO - (CONV_KERNEL - 1), store_cv)
        cvhat, _ = _ln_stats(cv_ref[...])
        cn = cvhat * clg_ref[...] + clb_ref[...]
        mixed_ref[:, pl.ds(POOL_WIDTH + SGU_WIDTH, CONV_WIDTH)] = (cn * _sigmoid(cn)).astype(BF16)

    full = lambda a: pl.BlockSpec(a.shape, lambda i: (0,) * a.ndim)
    weights = [lw["w_pool"], _row(lw["pool_scale"]), _row(lw["sgu_ln_g"]), _row(lw["sgu_ln_b"]), lw["sgu_w"],
               lw["sgu_bias_tile"], lw["conv_w_full"], _row(lw["conv_b"]), _row(lw["conv_ln_g"]), _row(lw["conv_ln_b"])]
    return pl.pallas_call(
        body, name=name, grid=(s_len // ts,),
        in_specs=[pl.BlockSpec((ts, IN_WIDTH), lambda i: (i, 0)),
                  pl.BlockSpec((HALO, IN_WIDTH), lambda i: (jnp.maximum(i * per_halo - 1, 0), 0))] + [full(a) for a in weights],
        out_specs=[pl.BlockSpec((ts, d), lambda i: (i, 0)), pl.BlockSpec((ts, CONV_WIDTH), lambda i: (i, 0))],
        out_shape=[jax.ShapeDtypeStruct((s_len, d), BF16), jax.ShapeDtypeStruct((s_len, CONV_WIDTH), F32)],
        scratch_shapes=[pltpu.VMEM((16 + ts, POOL_WIDTH), F32), pltpu.VMEM((HALO + ts, CONV_WIDTH), F32),
                        pltpu.VMEM((SUBLANES - 1, ts + HALO - SUBLANES, CONV_WIDTH), F32)],
        compiler_params=_params(1),
    )(p, p, *weights)


def _mix_backward(name, p, cv, dmixed, lw):
    s_len = p.shape[0]
    ts = _tile(s_len, TS_MIX)
    nt = s_len // ts
    per_halo = ts // HALO
    d = POOL_WIDTH + SGU_WIDTH + CONV_WIDTH
    nch = ts // CHUNK
    col_yc = POOL_WIDTH + SGU_WIDTH

    def body(p_ref, ph_ref, cv_ref, cvn_ref, dm_ref, dmn_ref, wp_ref, ps_ref, slg_ref, slb_ref, ws_ref, wst_ref, sb_ref,
             cw_ref, cwr_ref, clg_ref, clb_ref,
             dp_ref, dbin_ref, dwp_ref, dps_ref, dslg_ref, dslb_ref, dws_ref, dsb_ref, dcw_ref, dcb_ref, dclg_ref, dclb_ref,
             pool_ext, dq_ext, hh_ext, dcv_ext, dcw_acc, shifted):
        i = pl.program_id(0)
        first = i == 0
        keep_prev = (i > 0).astype(F32)
        keep_next = (i < nt - 1).astype(F32)

        pool_ext[pl.ds(0, 16), :] = ph_ref[pl.ds(16, 16), pl.ds(0, POOL_WIDTH)] * keep_prev
        pool_ext[pl.ds(16, ts), :] = p_ref[:, pl.ds(0, POOL_WIDTH)]
        qs = _pool_means(pool_ext, ts, i)
        t_ext = i * ts + lax.broadcasted_iota(jnp.int32, (ts + HALO, GROUP), 0)
        for g, win in enumerate(POOL_WINDOWS):
            cols = pl.ds(g * GROUP, GROUP)
            wpb = wp_ref[g].astype(BF16)
            qb = qs[g].astype(BF16)
            de = dm_ref[:, cols] * ps_ref[:, cols]
            e = jnp.dot(qb, wpb, preferred_element_type=F32)
            _accumulate(dps_ref.at[:, cols], _colsum(dm_ref[:, cols] * e), first)
            deb = de.astype(BF16)
            _accumulate(dwp_ref.at[g], lax.dot_general(qb, deb, (((0,), (0,)), ((), ())), preferred_element_type=F32), first)
            de_next = (dmn_ref[:, cols] * ps_ref[:, cols] * keep_next).astype(BF16)
            de_all = jnp.concatenate([deb, de_next], axis=0)
            dq = lax.dot_general(de_all, wpb, (((1,), (1,)), ((), ())), preferred_element_type=F32)
            inv = 1.0 / jnp.minimum(t_ext + 1, win).astype(F32)
            dq_ext[:, cols] = dq * inv
            acc = dq_ext[pl.ds(0, ts), cols]
            for j in range(1, win):
                acc = acc + dq_ext[pl.ds(j, ts), cols]
            dpa = acc - dq[:ts]
            dp_ref[:, cols] = dpa.astype(BF16)
            _accumulate(dbin_ref.at[:, cols], _colsum(dpa), first)

        pb = p_ref[:, pl.ds(COL_B, 2 * SGU_WIDTH)]
        uv, duv = _gelu_with_grad(pb)
        u = uv[:, :SGU_WIDTH]
        vhat, vrstd = _ln_stats(uv[:, SGU_WIDTH:])
        vn = (vhat * slg_ref[...] + slb_ref[...]).astype(BF16)
        dyb = dm_ref[:, pl.ds(POOL_WIDTH, SGU_WIDTH)]
        dmix = dyb * u
        dmixb = dmix.astype(BF16)
        du_parts, dvn_parts = [], []
        for h in range(SGU_HEADS):
            cols = slice(h * GROUP, (h + 1) * GROUP)
            wm = _masked_sgu_w(ws_ref, h).astype(BF16)
            wmt = _masked_sgu_w_t(wst_ref, h).astype(BF16)
            dws_h = jnp.zeros((CHUNK, CHUNK), F32)
            dsb_h = jnp.zeros((CHUNK, GROUP), F32)
            du_rows, dvn_rows = [], []
            for n in range(nch):
                rows = slice(n * CHUNK, (n + 1) * CHUNK)
                mx = jnp.dot(wm, vn[rows, cols], preferred_element_type=F32) + sb_ref[h]
                du_rows.append(dyb[rows, cols] * mx)
                dws_h = dws_h + lax.dot_general(dmixb[rows, cols], vn[rows, cols], (((1,), (1,)), ((), ())),
                                                preferred_element_type=F32)
                dsb_h = dsb_h + dmix[rows, cols]
                dvn_rows.append(jnp.dot(wmt, dmixb[rows, cols], preferred_element_type=F32))
            r = lax.broadcasted_iota(jnp.int32, (CHUNK, CHUNK), 0)
            c = lax.broadcasted_iota(jnp.int32, (CHUNK, CHUNK), 1)
            _accumulate(dws_ref.at[h], jnp.where(r >= c, dws_h, 0.0), first)
            _accumulate(dsb_ref.at[h], jnp.broadcast_to(jnp.sum(dsb_h, axis=1, keepdims=True), (CHUNK, GROUP)), first)
            du_parts.append(jnp.concatenate(du_rows, axis=0))
            dvn_parts.append(jnp.concatenate(dvn_rows, axis=0))
        du = jnp.concatenate(du_parts, axis=1)
        dvn = jnp.concatenate(dvn_parts, axis=1)
        _accumulate(dslg_ref, _colsum(dvn * vhat), first)
        _accumulate(dslb_ref, _colsum(dvn), first)
        dv = _ln_bwd(dvn, vhat, vrstd, slg_ref[...])
        dpb = jnp.concatenate([du, dv], axis=1) * duv
        dp_ref[:, pl.ds(COL_B, 2 * SGU_WIDTH)] = dpb.astype(BF16)
        _accumulate(dbin_ref.at[:, pl.ds(COL_B, 2 * SGU_WIDTH)], _colsum(dpb), first)

        a_main = p_ref[:, pl.ds(COL_C, CONV_WIDTH)]
        sg_main = _sigmoid(p_ref[:, pl.ds(COL_C + CONV_WIDTH, CONV_WIDTH)])
        hh_ext[pl.ds(HALO, ts), :] = a_main * sg_main

        def conv_ln_backward(cv_v, dyc_v):
            cvhat, crstd = _ln_stats(cv_v)
            cn = cvhat * clg_ref[...] + clb_ref[...]
            s = _sigmoid(cn)
            dcn = dyc_v * (s * (1.0 + cn * (1.0 - s)))
            return _ln_bwd(dcn, cvhat, crstd, clg_ref[...]), dcn, cvhat

        dcv, dcn, cvhat = conv_ln_backward(cv_ref[...], dm_ref[:, pl.ds(col_yc, CONV_WIDTH)])
        _accumulate(dclg_ref, _colsum(dcn * cvhat), first)
        _accumulate(dclb_ref, _colsum(dcn), first)
        _accumulate(dcb_ref, _colsum(dcv), first)
        dcv_next, _, _ = conv_ln_backward(cvn_ref[...], dmn_ref[:, pl.ds(col_yc, CONV_WIDTH)])
        dcv_ext[pl.ds(0, ts), :] = dcv
        dcv_ext[pl.ds(ts, HALO), :] = dcv_next * keep_next

        @pl.when(first)
        def _():
            dcw_acc[...] = jnp.zeros_like(dcw_acc)

        nrb = ts // RB_CONV
        _fill_shifted(dcv_ext, shifted, ts)
        for cg in range(CONV_WIDTH // GROUP):
            cols = pl.ds(cg * GROUP, GROUP)
            for rb0 in range(0, nrb, CONV_LIVE_BLOCKS):
                group = range(rb0, min(rb0 + CONV_LIVE_BLOCKS, nrb))
                hblk = {rb: hh_ext[pl.ds(HALO + rb * RB_CONV, RB_CONV), cols] for rb in group}
                for k in range(CONV_KERNEL):
                    part = jnp.zeros((8, GROUP), F32)
                    for rb in group:
                        offset = rb * RB_CONV + CONV_KERNEL - 1 - k
                        prod = hblk[rb] * _shifted_rows(dcv_ext, shifted, offset, RB_CONV, cols)
                        part = part + jnp.sum(prod.reshape(RB_CONV // 8, 8, GROUP), axis=0)
                    dcw_acc[k, :, cols] += part

        @pl.when(i == nt - 1)
        def _():
            dcw_ref[...] = jnp.sum(dcw_acc[...], axis=1)

        zero = lambda cg: jnp.zeros((RB_CONV, GROUP), F32)
        def glu_backward(r0, c0, blk):
            rows, cols = pl.ds(r0, RB_CONV), pl.ds(c0, GROUP)
            a_blk = p_ref[rows, pl.ds(COL_C + c0, GROUP)]
            s_blk = _sigmoid(p_ref[rows, pl.ds(COL_C + CONV_WIDTH + c0, GROUP)])
            da = blk * s_blk
            dg = blk * a_blk * s_blk * (1.0 - s_blk)
            dp_ref[rows, pl.ds(COL_C + c0, GROUP)] = da.astype(BF16)
            dp_ref[rows, pl.ds(COL_C + CONV_WIDTH + c0, GROUP)] = dg.astype(BF16)
            hh_ext[pl.ds(HALO + r0, RB_CONV), cols] = da
            dcv_ext[rows, cols] = dg

        _conv_taps(zero, cwr_ref, dcv_ext, shifted, ts, lambda k: k, glu_backward)
        _accumulate(dbin_ref.at[:, pl.ds(COL_C, CONV_WIDTH)], _colsum(hh_ext[pl.ds(HALO, ts), :]), first)
        _accumulate(dbin_ref.at[:, pl.ds(COL_C + CONV_WIDTH, CONV_WIDTH)], _colsum(dcv_ext[pl.ds(0, ts), :]), first)

    full = lambda a: pl.BlockSpec(a.shape, lambda i: (0,) * a.ndim)
    weights = [lw["w_pool"], _row(lw["pool_scale"]), _row(lw["sgu_ln_g"]), _row(lw["sgu_ln_b"]), lw["sgu_w"], lw["sgu_w_t"],
               lw["sgu_bias_tile"], lw["conv_w_full"], lw["conv_w_rev"], _row(lw["conv_ln_g"]), _row(lw["conv_ln_b"])]
    prev_halo = lambda i: (jnp.maximum(i * per_halo - 1, 0), 0)
    next_halo = lambda i: (jnp.minimum((i + 1) * per_halo, s_len // HALO - 1), 0)
    small = lambda shape: (jax.ShapeDtypeStruct(shape, F32), pl.BlockSpec(shape, lambda i: (0,) * len(shape)))
    outs = [(jax.ShapeDtypeStruct((s_len, IN_WIDTH), BF16), pl.BlockSpec((ts, IN_WIDTH), lambda i: (i, 0))),
            small((1, IN_WIDTH)), small((len(POOL_WINDOWS), GROUP, GROUP)), small((1, POOL_WIDTH)),
            small((1, SGU_WIDTH)), small((1, SGU_WIDTH)), small((SGU_HEADS, CHUNK, CHUNK)), small((SGU_HEADS, CHUNK, GROUP)),
            small((CONV_ROWS, CONV_WIDTH)), small((1, CONV_WIDTH)), small((1, CONV_WIDTH)), small((1, CONV_WIDTH))]
    return pl.pallas_call(
        body, name=name, grid=(nt,),
        in_specs=[pl.BlockSpec((ts, IN_WIDTH), lambda i: (i, 0)), pl.BlockSpec((HALO, IN_WIDTH), prev_halo),
                  pl.BlockSpec((ts, CONV_WIDTH), lambda i: (i, 0)), pl.BlockSpec((HALO, CONV_WIDTH), next_halo),
                  pl.BlockSpec((ts, d), lambda i: (i, 0)), pl.BlockSpec((HALO, d), next_halo)] + [full(a) for a in weights],
        out_specs=[s for _, s in outs],
        out_shape=[o for o, _ in outs],
        scratch_shapes=[pltpu.VMEM((16 + ts, POOL_WIDTH), F32), pltpu.VMEM((ts + HALO, POOL_WIDTH), F32),
                        pltpu.VMEM((HALO + ts, CONV_WIDTH), F32), pltpu.VMEM((ts + HALO, CONV_WIDTH), F32),
                        pltpu.VMEM((CONV_ROWS, 8, CONV_WIDTH), F32),
                        pltpu.VMEM((SUBLANES - 1, ts + HALO - SUBLANES, CONV_WIDTH), F32)],
        compiler_params=_params(1),
    )(p, p, cv, cv, dmixed, dmixed, *weights)


def _masked_sgu_w_t(wt_ref, h):
    r = lax.broadcasted_iota(jnp.int32, (CHUNK, CHUNK), 0)
    c = lax.broadcasted_iota(jnp.int32, (CHUNK, CHUNK), 1)
    return jnp.where(c >= r, wt_ref[h], 0.0)


HBM = pl.BlockSpec(memory_space=pltpu.HBM)
CHIP_FLIPS = ((1, 0), (0, 1), (1, 1))


def _place():
    return lax.axis_index("x"), lax.axis_index("y"), lax.axis_index("c")


def _half(ref, axis, which, size):
    idx = [slice(None)] * len(ref.shape)
    idx[axis] = pl.ds(which * size, size)
    return ref.at[tuple(idx)]


def _exchange(name, sources, inplace, fresh, copies):
    n_src, n_in, n = len(sources), len(inplace), len(copies)
    n_out = n_in + len(fresh)

    def body(*refs):
        ins = refs[:n_src + n_in]
        outs = refs[n_src + n_in:n_src + n_in + n_out]
        send_sems, recv_sems = refs[n_src + n_in + n_out:]
        started = _descriptors(copies, list(ins[:n_src]) + list(outs), send_sems, recv_sems)
        for cp in started:
            cp.start()
        for cp in started:
            cp.wait()

    out_shape = [jax.ShapeDtypeStruct(a.shape, a.dtype) for a in inplace] + list(fresh)
    return pl.pallas_call(
        body, name=name, in_specs=[HBM] * (n_src + n_in), out_specs=[HBM] * n_out, out_shape=out_shape,
        input_output_aliases={n_src + i: i for i in range(n_in)},
        scratch_shapes=[pltpu.SemaphoreType.DMA((n,)), pltpu.SemaphoreType.DMA((n,))],
    )(*sources, *inplace)


def _descriptors(copies, bufs, send_sems, recv_sems):
    x, y, c = _place()
    return [pltpu.make_async_remote_copy(
        src_ref=src_view(bufs[src], x, y, c), dst_ref=dst_view(bufs[dst], x, y, c),
        send_sem=send_sems.at[k], recv_sem=recv_sems.at[k], device_id=peer(x, y, c), device_id_type=MESH)
        for k, (src, src_view, dst, dst_view, peer) in enumerate(copies)]


SEM = pl.BlockSpec(memory_space=pltpu.SEMAPHORE)
IN_FLIGHT = pltpu.CompilerParams(has_side_effects=pltpu.SideEffectType.DATAFLOW_SIDE_EFFECTING)


def _exchange_start(name, sources, inplace, fresh, copies, after=None):
    n_src, n = len(sources), len(copies)
    landing = [lax.empty(f.shape, f.dtype) for f in fresh]
    bufs = [pltpu.with_memory_space_constraint(a, pltpu.HBM) for a in (*inplace, *landing)]
    srcs = [pltpu.with_memory_space_constraint(a, pltpu.HBM) for a in sources]
    n_buf = len(bufs)
    order = [] if after is None else [after]
    n_in = n_src + n_buf + len(order)

    def body(*refs):
        ins = refs[:n_src]
        send_sems, recv_sems = refs[n_in], refs[n_in + 1]
        outs = refs[n_in + 2:n_in + 2 + n_buf]
        token = refs[n_in + 2 + n_buf]
        for cp in _descriptors(copies, list(ins) + list(outs), send_sems, recv_sems):
            cp.start()
        token[...] = jnp.zeros_like(token)

    res = pl.pallas_call(
        body, name=name + "_start",
        out_shape=(pltpu.SemaphoreType.DMA((n,)), pltpu.SemaphoreType.DMA((n,)),
                   *[pltpu.HBM(b.shape, b.dtype) for b in bufs], jax.ShapeDtypeStruct((8, 128), F32)),
        in_specs=[HBM] * (n_src + n_buf) + [pl.BlockSpec(memory_space=pl.ANY)] * len(order),
        out_specs=(SEM, SEM, *[HBM] * n_buf, pl.BlockSpec(memory_space=pltpu.VMEM)),
        input_output_aliases={n_src + i: 2 + i for i in range(n_buf)},
        compiler_params=IN_FLIGHT,
    )(*srcs, *bufs, *order)
    return dict(name=name, sources=srcs, sems=res[:2], bufs=list(res[2:2 + n_buf]), token=res[2 + n_buf], copies=copies)


def _exchange_wait(handle, after):
    srcs, bufs, copies = handle["sources"], handle["bufs"], handle["copies"]
    n_src, n_buf = len(srcs), len(bufs)

    def body(*refs):
        ins = refs[:n_src]
        send_sems, recv_sems = refs[n_src + n_buf], refs[n_src + n_buf + 1]
        outs = refs[n_src + n_buf + 3:]
        for cp in _descriptors(copies, list(ins) + list(outs), send_sems, recv_sems):
            cp.wait()

    return list(pl.pallas_call(
        body, name=handle["name"] + "_wait",
        out_shape=tuple(pltpu.HBM(b.shape, b.dtype) for b in bufs),
        in_specs=[HBM] * (n_src + n_buf) + [SEM, SEM, pl.BlockSpec(memory_space=pl.ANY)],
        out_specs=tuple([HBM] * n_buf),
        input_output_aliases={n_src + i: i for i in range(n_buf)},
        compiler_params=IN_FLIGHT,
    )(*srcs, *bufs, *handle["sems"], after))


def _into_slot(name, src, layer, place, dtype, after=None):
    _, r, c = src.shape
    br = _tile(r, BR_EW) if r % BR_EW == 0 else r
    order = [] if after is None else [after]

    def body(pr_ref, s_ref, *rest):
        rest[-1][...] = s_ref[...].astype(dtype)

    return pl.pallas_call(
        body, name=name,
        grid_spec=pltpu.PrefetchScalarGridSpec(
            num_scalar_prefetch=1, grid=(r // br,),
            in_specs=[pl.BlockSpec((None, br, c), lambda i, pr: (layer, i, 0))] + [pl.BlockSpec(memory_space=pl.ANY)] * len(order),
            out_specs=pl.BlockSpec((None, br, c), lambda i, pr: (pr[1], i, 0))),
        out_shape=jax.ShapeDtypeStruct((4, r, c), dtype), compiler_params=_params(1),
    )(place, src, *order)


def _gather_copies(n_bufs):
    def own_half(ref, x, y, c):
        return _half(ref.at[2 * x + y], 0, c, ref.shape[1] // 2)

    ici, d2d = [], []
    for b in range(n_bufs):
        for fx, fy in CHIP_FLIPS:
            ici.append((b, own_half, b, own_half, lambda x, y, c, fx=fx, fy=fy: (x ^ fx, y ^ fy, c)))
            landed = lambda ref, x, y, c, fx=fx, fy=fy: _half(ref.at[2 * (x ^ fx) + (y ^ fy)], 0, c, ref.shape[1] // 2)
            d2d.append((b, landed, b, landed, _sibling))
    return ici, d2d


def _gather_start(name, bufs, after=None):
    return _exchange_start(name + "_chips", [], bufs, [], _gather_copies(len(bufs))[0], after=after)


def _gather_finish(name, handle, after):
    bufs = _exchange_wait(handle, after)
    return _exchange(name + "_sibling", [], bufs, [], _gather_copies(len(bufs))[1])


def _gather_pass_on(name, handle, after):
    bufs = _exchange_wait(handle, after)
    return _exchange_start(name + "_sibling", [], bufs, [], _gather_copies(len(bufs))[1])


def _sibling(x, y, c):
    return (x, y, 1 - c)


def _scalar_spec_call(name, fn, scalars, ins, in_blocks, out_shapes, out_blocks, grid):
    def body(s_ref, *refs):
        res = fn(*[r[...] for r in refs[:len(ins)]])
        for o_ref, v in zip(refs[len(ins):], res):
            o_ref[...] = v.astype(o_ref.dtype)

    return pl.pallas_call(
        body, name=name,
        grid_spec=pltpu.PrefetchScalarGridSpec(num_scalar_prefetch=1, grid=grid, in_specs=in_blocks, out_specs=out_blocks),
        out_shape=out_shapes, compiler_params=_params(len(grid)),
    )(scalars, *ins)


def _reduce_begin(tag, place, items):
    n = len(items)
    whole = lambda ref, x, y, c: ref
    halves = [g.shape[1] // 2 for g, _, _ in items]
    fresh = [jax.ShapeDtypeStruct((g.shape[0], halves[a], g.shape[2]), F32) for a, (g, _, _) in enumerate(items)]
    copies = [(a, lambda ref, x, y, c, hr=halves[a]: _half(ref, 1, 1 - c, hr), n + a, whole, _sibling) for a in range(n)]
    handle = _exchange_start(tag + "_sibling_in", [g for g, _, _ in items], [], fresh, copies)
    return dict(tag=tag, place=place, items=items, handle=handle, token=handle["token"])


def _reduce_between_chips(state, after):
    tag, place, items = state["tag"], state["place"], state["items"]
    n = len(items)
    slot_of = lambda scatter: (lambda x, y: 2 * x + y) if scatter else (lambda x, y: 0)
    halves = [g.shape[1] // 2 for g, _, _ in items]
    landed = _exchange_wait(state["handle"], after)

    chip_sums = []
    for a, (g, _, wire) in enumerate(items):
        ns, _, cols = g.shape
        hr = halves[a]
        br = _tile(hr, BR_EW)
        chip_sums.append(_scalar_spec_call(
            f"{tag}_chip_sum{a}", lambda u, v: (u + v,), place, [g, landed[a]],
            [pl.BlockSpec((None, br, cols), lambda s, i, pr, hb=hr // br: (s, pr[0] * hb + i, 0)),
             pl.BlockSpec((None, br, cols), lambda s, i, pr: (s, i, 0))],
            [jax.ShapeDtypeStruct((ns, hr, cols), wire)], [pl.BlockSpec((None, br, cols), lambda s, i, pr: (s, i, 0))],
            (ns, hr // br))[0])

    copies, fresh = [], []
    for a, (g, scatter, wire) in enumerate(items):
        slot = slot_of(scatter)
        fresh.append(jax.ShapeDtypeStruct((3, halves[a], g.shape[2]), wire))
        for j, (fx, fy) in enumerate(CHIP_FLIPS):
            copies.append((a, lambda ref, x, y, c, fx=fx, fy=fy, slot=slot: ref.at[slot(x ^ fx, y ^ fy)],
                           n + a, lambda ref, x, y, c, j=j: ref.at[j], lambda x, y, c, fx=fx, fy=fy: (x ^ fx, y ^ fy, c)))
    handle = _exchange_start(tag + "_chips", chip_sums, [], fresh, copies)
    return dict(tag=tag, place=place, items=items, chip_sums=chip_sums, handle=handle, token=handle["token"])


def _reduce_swap(state, after):
    tag, place, items, chip_sums = state["tag"], state["place"], state["items"], state["chip_sums"]
    n = len(items)
    whole = lambda ref, x, y, c: ref
    halves = [g.shape[1] // 2 for g, _, _ in items]
    arrived = _exchange_wait(state["handle"], after)

    tree = lambda own, fx, fy, fxy: ((own.astype(F32) + fx.astype(F32)) + (fy.astype(F32) + fxy.astype(F32)),)
    mine = []
    for a, (g, scatter, _) in enumerate(items):
        hr, cols = halves[a], g.shape[2]
        br = _tile(hr, BR_EW)
        got = lambda j: pl.BlockSpec((None, br, cols), lambda i, pr, j=j: (j, i, 0))
        own = pl.BlockSpec((None, br, cols), (lambda i, pr: (pr[1], i, 0)) if scatter else (lambda i, pr: (0, i, 0)))
        mine.append(_scalar_spec_call(
            f"{tag}_tree_sum{a}", tree, place, [chip_sums[a], arrived[a], arrived[a], arrived[a]],
            [own, got(0), got(1), got(2)],
            [jax.ShapeDtypeStruct((2, hr, cols), F32)], [pl.BlockSpec((None, br, cols), lambda i, pr: (pr[0], i, 0))],
            (hr // br,))[0])

    my_half = lambda ref, x, y, c: ref.at[c]
    handle = _exchange_start(tag + "_sibling_out", [], mine, [], [(a, my_half, a, my_half, _sibling) for a in range(n)])
    return dict(handle=handle, token=handle["token"])


def _reduce_end(state, after):
    return [b.reshape(2 * b.shape[1], b.shape[2]) for b in _exchange_wait(state["handle"], after)]


def _adamw_blocks(layer, w, m, v, g, earlier, n_blocks):
    n_layers, r, c = w.shape
    assert r % n_blocks == 0 and (r // n_blocks) % 8 == 0, (r, n_blocks)
    br = r // n_blocks
    in_layer = ((None, br, c), lambda b: (layer, b, 0))
    ins = [(a, *in_layer) for a in (w, m, v)] + [(g, (br, c), lambda b: (b, 0))] + [(a, None, None) for a in earlier or ()]
    outs = [(jax.ShapeDtypeStruct((n_layers, r, c), F32), *in_layer)] * 4
    aliases = {4 + i: i for i in range(len(earlier or ()))}

    def fn(wv, mv, vv, gv):
        return (gv,) + _adamw_math(wv, gv, mv, vv)

    return fn, ins, outs, aliases, n_blocks


def _adamw_layer(name, layer, w, m, v, g, earlier):
    c = w.shape[2]
    br = max(8, 2 ** int(math.log2(EW_BLOCK_ELEMS // c)))
    fn, ins, outs, aliases, n_blocks = _adamw_blocks(layer, w, m, v, g, earlier, w.shape[1] // min(br, w.shape[1]))
    n_data = sum(1 for _, bs, _ in ins if bs is not None)

    def body(*refs):
        res = fn(*[r[...] for r in refs[:n_data]])
        for o_ref, val in zip(refs[len(ins):], res):
            o_ref[...] = val

    spec = lambda bs, f: pl.BlockSpec(memory_space=pl.ANY) if bs is None else pl.BlockSpec(bs, lambda i, f=f: f(i))
    return pl.pallas_call(
        body, name=name, grid=(n_blocks,),
        in_specs=[spec(bs, f) for _, bs, f in ins], out_specs=[spec(bs, f) for _, bs, f in outs],
        out_shape=[o for o, _, _ in outs], input_output_aliases=aliases, compiler_params=_params(1),
    )(*[a for a, _, _ in ins])


SHARDED = ("w_in", "w_out", "w_ff1", "w_ff2")
REPLICATED = ("b_in", "w_pool", "pool_scale", "sgu_ln_g", "sgu_ln_b", "sgu_w", "sgu_b", "conv_b", "conv_ln_g", "conv_ln_b",
              "b_out", "ln1_g", "ln1_b", "b_ff1", "b_ff2", "ln2_g", "ln2_b")
WEIGHTS = ("w_in", "b_in", "w_pool", "pool_scale", "sgu_ln_g", "sgu_ln_b", "sgu_w", "sgu_b", "conv_w", "conv_b", "conv_ln_g",
           "conv_ln_b", "w_out", "b_out", "ln1_g", "ln1_b", "w_ff1", "b_ff1", "w_ff2", "b_ff2", "ln2_g", "ln2_b")
PACK_ROWS = 1024


def _pack(arrays):
    flat = jnp.concatenate([a.reshape(-1) for a in arrays])
    rows = -(-flat.shape[0] // (128 * PACK_ROWS)) * PACK_ROWS
    return jnp.pad(flat, (0, rows * 128 - flat.shape[0])).reshape(rows, 128)


def _unpack(packed, like):
    flat = packed.reshape(-1)
    out, at = [], 0
    for a in like:
        out.append(flat[at:at + a.size].reshape(a.shape))
        at += a.size
    return out


def kernel(x, w_in, b_in, w_pool, pool_scale, sgu_ln_g, sgu_ln_b, sgu_w, sgu_b, conv_w, conv_b, conv_ln_g, conv_ln_b, w_out, b_out, ln1_g, ln1_b, w_ff1, b_ff1, w_ff2, b_ff2, ln2_g, ln2_b, loss_target, m_w_in, m_b_in, m_w_pool, m_pool_scale, m_sgu_ln_g, m_sgu_ln_b, m_sgu_w, m_sgu_b, m_conv_w, m_conv_b, m_conv_ln_g, m_conv_ln_b, m_w_out, m_b_out, m_ln1_g, m_ln1_b, m_w_ff1, m_b_ff1, m_w_ff2, m_b_ff2, m_ln2_g, m_ln2_b, v_w_in, v_b_in, v_w_pool, v_pool_scale, v_sgu_ln_g, v_sgu_ln_b, v_sgu_w, v_sgu_b, v_conv_w, v_conv_b, v_conv_ln_g, v_conv_ln_b, v_w_out, v_b_out, v_ln1_g, v_ln1_b, v_w_ff1, v_b_ff1, v_w_ff2, v_b_ff2, v_ln2_g, v_ln2_b):
    w = dict(w_in=w_in, b_in=b_in, w_pool=w_pool, pool_scale=pool_scale, sgu_ln_g=sgu_ln_g, sgu_ln_b=sgu_ln_b, sgu_w=sgu_w,
             sgu_b=sgu_b, conv_w=conv_w, conv_b=conv_b, conv_ln_g=conv_ln_g, conv_ln_b=conv_ln_b, w_out=w_out, b_out=b_out,
             ln1_g=ln1_g, ln1_b=ln1_b, w_ff1=w_ff1, b_ff1=b_ff1, w_ff2=w_ff2, b_ff2=b_ff2, ln2_g=ln2_g, ln2_b=ln2_b)
    m = dict(w_in=m_w_in, b_in=m_b_in, w_pool=m_w_pool, pool_scale=m_pool_scale, sgu_ln_g=m_sgu_ln_g, sgu_ln_b=m_sgu_ln_b,
             sgu_w=m_sgu_w, sgu_b=m_sgu_b, conv_w=m_conv_w, conv_b=m_conv_b, conv_ln_g=m_conv_ln_g, conv_ln_b=m_conv_ln_b,
             w_out=m_w_out, b_out=m_b_out, ln1_g=m_ln1_g, ln1_b=m_ln1_b, w_ff1=m_w_ff1, b_ff1=m_b_ff1, w_ff2=m_w_ff2,
             b_ff2=m_b_ff2, ln2_g=m_ln2_g, ln2_b=m_ln2_b)
    v = dict(w_in=v_w_in, b_in=v_b_in, w_pool=v_w_pool, pool_scale=v_pool_scale, sgu_ln_g=v_sgu_ln_g, sgu_ln_b=v_sgu_ln_b,
             sgu_w=v_sgu_w, sgu_b=v_sgu_b, conv_w=v_conv_w, conv_b=v_conv_b, conv_ln_g=v_conv_ln_g, conv_ln_b=v_conv_ln_b,
             w_out=v_w_out, b_out=v_b_out, ln1_g=v_ln1_g, ln1_b=v_ln1_b, w_ff1=v_w_ff1, b_ff1=v_b_ff1, w_ff2=v_w_ff2,
             b_ff2=v_b_ff2, ln2_g=v_ln2_g, ln2_b=v_ln2_b)
    assert x.shape[0] == 1 and x.shape[2] == POOL_WIDTH + SGU_WIDTH + CONV_WIDTH, x.shape
    xs, target = x[0], loss_target[0]
    s_len, d = xs.shape
    n_layers = w_in.shape[0]
    dff = 4 * w_ff1.shape[2]
    conv_shard = conv_w.shape[2]

    cx, cy, cc = _place()
    me = 2 * cx + cy
    place = jnp.stack([cc, me]).astype(jnp.int32)

    def cast(key, after=None):
        return _into_slot(f"cast_{key[0]}{key[1]}", w[key[0]], key[1], place, BF16, after=after)

    conv_padded = jnp.pad(conv_w, ((0, 0), (0, CONV_ROWS - CONV_KERNEL), (0, 0))).reshape(1, n_layers * CONV_ROWS, conv_shard)
    slot, flights, token = {}, {}, None
    for name, keys in (("in0", [("w_in", 0)]), ("rest0", [("w_out", 0), "conv_w"]), ("ff1_0", [("w_ff1", 0)]),
                       ("ff2_0", [("w_ff2", 0)])):
        for k in keys:
            slot[k] = _into_slot("slot_conv_w", conv_padded, 0, place, F32, after=token) if k == "conv_w" else cast(k, token)
        flights[name] = _gather_start("gather_" + name, [slot[k] for k in keys])
        token = flights[name]["token"]
    slot.update({(n, l): cast((n, l), token) for n in SHARDED for l in range(1, n_layers)})
    xb0 = _rowwise("cast_x", lambda t: (t,), [xs], [BF16])[0]
    gathered = {("w_in", 0): _gather_finish("gather_in0", flights["in0"], slot[(SHARDED[-1], n_layers - 1)])[0]}
    conv = {}

    def layer_weights(l):
        conv_full, conv_rev = conv["full"], conv["rev"]
        return dict(w_pool=w_pool[l], pool_scale=pool_scale[l], sgu_ln_g=sgu_ln_g[l], sgu_ln_b=sgu_ln_b[l], sgu_w=sgu_w[l],
                    sgu_w_t=jnp.transpose(sgu_w[l], (0, 2, 1)),
                    sgu_bias_tile=jnp.broadcast_to(sgu_b[l][:, :, None], (SGU_HEADS, CHUNK, GROUP)),
                    conv_w_full=conv_full[l], conv_w_rev=conv_rev[l], conv_b=conv_b[l], conv_ln_g=conv_ln_g[l],
                    conv_ln_b=conv_ln_b[l])

    saved = []
    x_cur, xb_cur, prev_ln = xs, xb0, None
    for l in range(n_layers):
        p = _proj(f"proj{l}", xb_cur, gathered[("w_in", l)], b_in[l], after=token if l == 0 else None)
        if l == 0:
            gathered[("w_out", 0)], conv_buf = _gather_finish("gather_rest0", flights["rest0"], p)
            full = jnp.transpose(conv_buf.reshape(4, n_layers, CONV_ROWS, conv_shard), (1, 2, 0, 3))
            conv["full"] = full.reshape(n_layers, CONV_ROWS, CONV_WIDTH)
            conv["rev"] = jnp.pad(conv["full"][:, CONV_KERNEL - 1::-1], ((0, 0), (0, CONV_ROWS - CONV_KERNEL), (0, 0)))
        lw = layer_weights(l)
        w_out_full = gathered[("w_out", l)].reshape(d, d)
        mixed, cv = _mix_forward(f"mix_fwd{l}", p, lw)
        passing = _gather_pass_on("gather_ff1_0", flights["ff1_0"], mixed) if l == 0 else None
        r1, x1b = _mix_out(f"mix_out{l}", mixed, w_out_full, b_out[l], x_cur, ln1_g[l], ln1_b[l], prev_ln,
                           after=passing and passing["token"])
        token = None
        if l == 0:
            gathered[("w_ff1", 0)] = _exchange_wait(passing, x1b)[0]
            passing = _gather_pass_on("gather_ff2_0", flights["ff2_0"], x1b)
            token = passing["token"]
        nxt = [(n, l + 1) for n in SHARDED] if l + 1 < n_layers else None
        if nxt:
            in_flight = _gather_start(f"gather_layer{l + 1}", [slot[k] for k in nxt], after=token)
            token = in_flight["token"]
        hf, zr = _ff1(f"ff1_{l}", x1b, gathered[("w_ff1", l)], b_ff1[l], after=token)
        if l == 0:
            gathered[("w_ff2", 0)] = _exchange_wait(passing, hf)[0]
        w2_full = gathered[("w_ff2", l)].reshape(dff, d)
        fo = _ff2(f"ff2_{l}", hf, w2_full, b_ff2[l])
        sv = dict(lw=lw, xb_in=xb_cur, p=p, mixed=mixed, cv=cv, r1=r1, x1b=x1b, hf=hf, zr=zr, fo=fo,
                  w_out_full=w_out_full, w2_full=w2_full)
        if nxt:
            in_flight = _gather_pass_on(f"gather_layer{l + 1}", in_flight, fo)
            sv["r2"], xb_cur = _resid_ln2(f"ln2_{l}", r1, ln1_g[l], ln1_b[l], fo, ln2_g[l], ln2_b[l], after=in_flight["token"])
            x_cur, prev_ln = sv["r2"], (ln2_g[l], ln2_b[l])
            gathered.update(zip(nxt, _exchange_wait(in_flight, xb_cur)))
        saved.append(sv)

    grads = {n: [None] * n_layers for n in REPLICATED + ("conv_w",)}
    g_final, delta, new_m, new_v = {}, {}, {}, {}
    results = {n: None for n in SHARDED}
    dx_mm, dx_resid = None, None
    loss_tile = None
    pending = None

    jobs = []

    def finish_reduce(begun, after):
        names, layer, state = begun
        reduced = _reduce_end(state, after)
        jobs.extend((n, layer, g) for n, g in zip(names, reduced))
        return reduced

    def hosted(call):
        taken = []

        def rider(steps):
            fits = [j for j in jobs if w[j[0]].shape[1] % steps == 0 and (w[j[0]].shape[1] // steps) % 8 == 0]
            if not fits:
                return None
            job = max(fits, key=lambda j: w[j[0]].shape[1] * w[j[0]].shape[2])
            jobs.remove(job)
            taken.append(job)
            n, layer, g = job
            return _adamw_blocks(layer, w[n], m[n], v[n], g, results[n], steps)

        out = call(rider)
        if not taken:
            return out
        results[taken[0][0]] = tuple(out[1])
        return out[0]

    for l in reversed(range(n_layers)):
        sv = saved[l]
        if dx_mm is None:
            dr2, dr2b, dg2, db2, dbff2, loss_tile = _ln_backward(
                f"ln2_bwd{l}", None, ln2_g[l], last=(sv["r1"], ln1_g[l], ln1_b[l], sv["fo"], ln2_b[l], target))
        else:
            dr2, dr2b, dg2, db2, dbff2 = _ln_backward(f"ln2_bwd{l}", sv["r2"], ln2_g[l], dy=dx_mm, resid=dx_resid)
        dzb, dbff1 = hosted(lambda rider: _dff_hidden(f"ff2_bwd{l}", dr2b, sv["w2_full"], sv["zr"], rider=rider))
        dw = {"w_ff2": _matmul_tn(f"dw_ff2_{l}", sv["hf"], dr2b, 4, dff // 4, d, True),
              "w_ff1": _matmul_tn(f"dw_ff1_{l}", sv["x1b"], dzb, 4, d, dff // 4, False)}
        if pending is not None:
            names, layer, state = pending
            pending = (names, layer, _reduce_swap(state, dw["w_ff1"]))
        ff_red = _reduce_begin(f"grads_ff{l}", place, [(dw[n], True, BF16) for n in ("w_ff1", "w_ff2")])
        if pending is not None:
            finish_reduce(pending, ff_red["token"])
            pending = None
        dx1 = _dx_sharded(f"ff1_bwd{l}", dzb, gathered[("w_ff1", l)], after=ff_red["token"])
        ff_red = _reduce_between_chips(ff_red, dx1)
        dr1, dr1b, dg1, db1, dbout = _ln_backward(f"ln1_bwd{l}", sv["r1"], ln1_g[l], dy=dx1, resid=dr2, after=ff_red["token"])
        dmixed = hosted(lambda rider: _dmixed(f"mix_out_bwd{l}", dr1b, sv["w_out_full"], rider=rider))
        dw["w_out"] = _matmul_tn(f"dw_out{l}", sv["mixed"], dr1b, 4, d // 4, d, True)
        (dp, dbin, dwp, dps, dslg, dslb, dws, dsb_tile, dcw, dcb, dclg, dclb) = _mix_backward(
            f"mix_bwd{l}", sv["p"], sv["cv"], dmixed, sv["lw"])
        dw["w_in"] = _matmul_tn(f"dw_in{l}", sv["xb_in"], dp, 4, d, IN_WIDTH // 4, False)
        for name, g in (("b_in", dbin), ("w_pool", dwp), ("pool_scale", dps), ("sgu_ln_g", dslg), ("sgu_ln_b", dslb),
                        ("sgu_w", dws), ("sgu_b", dsb_tile[:, :, 0]), ("conv_w", dcw[:CONV_KERNEL]), ("conv_b", dcb),
                        ("conv_ln_g", dclg), ("conv_ln_b", dclb), ("b_out", dbout), ("ln1_g", dg1), ("ln1_b", db1),
                        ("b_ff1", dbff1), ("b_ff2", dbff2), ("ln2_g", dg2), ("ln2_b", db2)):
            grads[name][l] = g.reshape(w[name].shape[1:]) if name != "conv_w" else g

        items = [(dw[n], True, BF16) for n in ("w_in", "w_out")]
        if l == 0:
            small_like = [jnp.stack(grads[n]) for n in REPLICATED + ("conv_w",)]
            items.append((_pack(small_like)[None], False, F32))
        mix_red = _reduce_begin(f"grads_mix{l}", place, items)
        ff_red = _reduce_swap(ff_red, mix_red["token"])
        mix_red = _reduce_between_chips(mix_red, ff_red["token"])
        finish_reduce((("w_ff1", "w_ff2"), l, ff_red), mix_red["token"])
        if l > 0:
            dx_mm = hosted(lambda rider: _dx_sharded(f"proj_bwd{l}", dp, gathered[("w_in", l)], after=mix_red["token"],
                                                     rider=rider))
            dx_resid = dr1
            pending = (("w_in", "w_out"), l, mix_red)
        else:
            grad_x = hosted(lambda rider: _dx_sharded(f"proj_bwd{l}", dp, gathered[("w_in", l)], dr1,
                                                      after=mix_red["token"], rider=rider))
            mix_red = _reduce_swap(mix_red, grad_x)
            reduced_small = finish_reduce((("w_in", "w_out"), l, mix_red), mix_red["token"])[-1]
            grad_x = grad_x[None]

    for n, layer, g in jobs:
        results[n] = _adamw_layer(f"adamw_{n}{layer}", layer, w[n], m[n], v[n], g, results[n])
    loss = lax.psum(loss_tile[0, 0], ("x", "y", "c"))
    for n in SHARDED:
        g_final[n], delta[n], new_m[n], new_v[n] = results[n]
    unpacked = _unpack(reduced_small, small_like)
    g_final.update(zip(REPLICATED, unpacked[:-1]))
    g_final["conv_w"] = lax.dynamic_slice_in_dim(unpacked[-1], me * conv_shard, conv_shard, axis=2)

    delta["conv_w"], new_m["conv_w"], new_v["conv_w"] = _adamw("adamw_conv_w", conv_w, g_final["conv_w"], m["conv_w"], v["conv_w"])
    packed = [_pack([t[n] for n in REPLICATED]) for t in (w, g_final, m, v)]
    like = [w[n] for n in REPLICATED]
    for res, packed_out in zip((delta, new_m, new_v), _adamw("adamw_replicated", *packed)):
        res.update(zip(REPLICATED, _unpack(packed_out, like)))

    return (loss, grad_x, *[g_final[n] for n in WEIGHTS], *[delta[n] for n in WEIGHTS],
            *[new_m[n] for n in WEIGHTS], *[new_v[n] for n in WEIGHTS])
```

```python
import functools
import math

import jax
import jax.numpy as jnp
from jax import lax
from jax.experimental import pallas as pl
from jax.experimental.pallas import tpu as pltpu

F32 = jnp.float32
BF16 = jnp.bfloat16
MESH = pl.DeviceIdType.MESH

DEPTH = 2
POOL_WINDOWS = (2, 4, 8, 16)
POOL_WIDTH = 512
GROUP = 128
SGU_WIDTH = 768
SGU_HEADS = 6
CHUNK = 128
CONV_WIDTH = 768
CONV_KERNEL = 31
CONV_ROWS = 32
HALO = 32
COL_B = POOL_WIDTH
COL_C = POOL_WIDTH + 2 * SGU_WIDTH
IN_WIDTH = COL_C + 2 * CONV_WIDTH
ALPHA = (2 * DEPTH) ** 0.25
LN_EPS = 1e-5
ADAM_LR = 0.001
ADAM_B1 = 0.9
ADAM_B2 = 0.999
ADAM_EPS = 1e-08
ADAM_WD = 0.01
ADAM_STEP = 10
GELU_C = math.sqrt(2.0 / math.pi)
GELU_A = 0.044715

V7X_VMEM_BYTES = 64 * 2 ** 20
VMEM_LIMIT = 56 * 2 ** 20

BM_MM = 1024
BN_MM = 1024
BM_LN = 512
BK_MM = 1024
BK_WIDE = 2048
BS_TN = 2048
TN_OUT_BYTES = 8 * 2 ** 20
RIDER_VMEM_BYTES = 8 * 2 ** 20
RIDER_VMEM_TIGHT = 2 * 2 ** 20
TS_MIX = 256
RB_CONV = 64
CONV_LIVE_BLOCKS = 2
BR_EW = 512
EW_BLOCK_ELEMS = 2 ** 18


def _tile(n, pref):
    t = min(n, pref)
    assert n % t == 0, (n, pref)
    return t


def _params(n_grid):
    return pltpu.CompilerParams(dimension_semantics=("arbitrary",) * n_grid, vmem_limit_bytes=VMEM_LIMIT)


def _sigmoid(x):
    return 1.0 / (1.0 + jnp.exp(-x))


def _gelu(x):
    return 0.5 * x * (1.0 + jnp.tanh(GELU_C * (x + GELU_A * x * x * x)))


def _gelu_with_grad(x):
    t = jnp.tanh(GELU_C * (x + GELU_A * x * x * x))
    half = 0.5 * (1.0 + t)
    return x * half, half + 0.5 * x * (1.0 - t * t) * GELU_C * (1.0 + 3.0 * GELU_A * x * x)


def _ln_stats(r):
    mu = jnp.mean(r, axis=-1, keepdims=True)
    xc = r - mu
    var = jnp.mean(xc * xc, axis=-1, keepdims=True)
    rstd = lax.rsqrt(var + LN_EPS)
    return xc * rstd, rstd


def _ln_bwd(dy, xhat, rstd, g):
    dxh = dy * g
    m1 = jnp.mean(dxh, axis=-1, keepdims=True)
    m2 = jnp.mean(dxh * xhat, axis=-1, keepdims=True)
    return rstd * (dxh - m1 - xhat * m2)


def _colsum(x):
    return jnp.sum(x, axis=0, keepdims=True)


def _accumulate(ref, val, first):
    @pl.when(first)
    def _():
        ref[...] = val

    @pl.when(jnp.logical_not(first))
    def _():
        ref[...] += val


def _matmul(name, grid, a, a_spec, b, b_spec, *, nt, extras, outs, epilogue, after=None, rider=None):
    ni, nj, nk = grid
    if after is not None:
        extras = list(extras) + [(after, pl.BlockSpec(memory_space=pl.ANY))]
    ne, no = len(extras), len(outs)
    dims = (((1,), (1,)), ((), ())) if nt else (((1,), (0,)), ((), ()))
    r_fn, r_ins, r_outs, r_aliases, r_blocks = rider or (None, [], [], {}, 0)
    assert r_blocks <= ni * nj * nk
    n_ri = len(r_ins)
    n_data = sum(1 for _, bs, _ in r_ins if bs is not None)
    step = lambda i, j, k: (i * nj + j) * nk + k

    def body(*refs):
        a_ref, b_ref = refs[0], refs[1]
        ex = refs[2:2 + ne]
        out_refs = refs[2 + ne + n_ri:2 + ne + n_ri + no]
        ids = (pl.program_id(0), pl.program_id(1), pl.program_id(2))
        part = lax.dot_general(a_ref[...], b_ref[...], dims, preferred_element_type=F32)
        if nk == 1:
            epilogue(part, ex, out_refs, ids)
        else:
            @pl.when(ids[2] == 0)
            def _():
                epilogue(part, ex, out_refs, ids)

            @pl.when(ids[2] > 0)
            def _():
                out_refs[0][...] += part
        if rider is not None:
            @pl.when(step(*ids) < r_blocks)
            def _():
                res = r_fn(*[r[...] for r in refs[2 + ne:2 + ne + n_data]])
                for o_ref, val in zip(refs[2 + ne + n_ri + no:], res):
                    o_ref[...] = val

    def rider_spec(bs, f):
        if bs is None:
            return pl.BlockSpec(memory_space=pl.ANY)
        return pl.BlockSpec(bs, lambda i, j, k: f(jnp.minimum(step(i, j, k), r_blocks - 1)))

    return pl.pallas_call(
        body,
        name=name,
        grid=grid,
        in_specs=[a_spec, b_spec] + [s for _, s in extras] + [rider_spec(bs, f) for _, bs, f in r_ins],
        out_specs=[s for _, s in outs] + [rider_spec(bs, f) for _, bs, f in r_outs],
        out_shape=[o for o, _ in outs] + [o for o, _, _ in r_outs],
        input_output_aliases={2 + ne + i: no + o for i, o in r_aliases.items()},
        compiler_params=_params(3),
    )(a, b, *[e for e, _ in extras], *[arr for arr, _, _ in r_ins])


def _matmul_tn(name, a, b, n_shards, shard_rows, shard_cols, row_sharded):
    s_len, ka = a.shape
    n = b.shape[1]
    assert (n_shards * shard_rows, shard_cols) == (ka, n) if row_sharded else (shard_rows, n_shards * shard_cols) == (ka, n)
    bs = _tile(s_len, BS_TN)
    bka = _tile(shard_rows, 2048)
    bn = next((t for t in (2 * BN_MM, BN_MM) if shard_cols % t == 0 and bka * t * 4 <= TN_OUT_BYTES), shard_cols)
    ni, nj, ns = ka // bka, n // bn, s_len // bs
    per_shard_i = shard_rows // bka
    per_shard_j = shard_cols // bn

    if row_sharded:
        out_map = lambda i, j, s: (i // per_shard_i, i % per_shard_i, j)
    else:
        out_map = lambda i, j, s: (j // per_shard_j, i, j % per_shard_j)

    def body(a_ref, b_ref, o_ref):
        s = pl.program_id(2)
        part = lax.dot_general(a_ref[...], b_ref[...], (((0,), (0,)), ((), ())), preferred_element_type=F32)
        _accumulate(o_ref, part, s == 0)

    return pl.pallas_call(
        body,
        name=name,
        grid=(ni, nj, ns),
        in_specs=[pl.BlockSpec((bs, bka), lambda i, j, s: (s, i)), pl.BlockSpec((bs, bn), lambda i, j, s: (s, j))],
        out_specs=pl.BlockSpec((None, bka, bn), out_map),
        out_shape=jax.ShapeDtypeStruct((n_shards, shard_rows, shard_cols), F32),
        compiler_params=_params(3),
    )(a, b)


def _row(v):
    return v.reshape(1, -1)


def _proj(name, xb, w_g, b_in, after=None):
    s_len, d = xb.shape
    ncs = w_g.shape[2]
    bm = _tile(s_len, BM_MM)

    def epilogue(acc, ex, outs, ids):
        outs[0][...] = acc + ex[0][...]

    return _matmul(
        name, (s_len // bm, 4, 1),
        xb, pl.BlockSpec((bm, d), lambda i, j, k: (i, 0)),
        w_g, pl.BlockSpec((None, d, ncs), lambda i, j, k: (j, 0, 0)),
        nt=False,
        extras=[(_row(b_in), pl.BlockSpec((1, ncs), lambda i, j, k: (0, j)))],
        outs=[(jax.ShapeDtypeStruct((s_len, 4 * ncs), F32), pl.BlockSpec((bm, ncs), lambda i, j, k: (i, j)))],
        epilogue=epilogue, after=after,
    )[0]


def _mix_out(name, mixed, w_out_full, b_out, x0, g1, b1, prev_ln=None, after=None):
    s_len, d = mixed.shape
    bm = _tile(s_len, BM_LN // 2)
    row = pl.BlockSpec((1, d), lambda i, j, k: (0, 0))
    blk = pl.BlockSpec((bm, d), lambda i, j, k: (i, 0))

    def epilogue(acc, ex, outs, ids):
        x_in = ex[1][...]
        if prev_ln is not None:
            xhat_in, _ = _ln_stats(x_in)
            x_in = xhat_in * ex[4][...] + ex[5][...]
        r1 = ALPHA * x_in + (acc + ex[0][...])
        outs[0][...] = r1
        xhat, _ = _ln_stats(r1)
        outs[1][...] = (xhat * ex[2][...] + ex[3][...]).astype(BF16)

    return _matmul(
        name, (s_len // bm, 1, 1),
        mixed, blk,
        w_out_full, pl.BlockSpec((d, d), lambda i, j, k: (0, 0)),
        nt=False,
        extras=[(_row(b_out), row), (x0, blk), (_row(g1), row), (_row(b1), row)] + [(_row(t), row) for t in prev_ln or ()],
        outs=[(jax.ShapeDtypeStruct((s_len, d), F32), blk), (jax.ShapeDtypeStruct((s_len, d), BF16), blk)],
        epilogue=epilogue, after=after,
    )


def _ff1(name, x1b, w_g, b_ff1, after=None):
    s_len, d = x1b.shape
    ncs = w_g.shape[2]
    bm = _tile(s_len, BM_MM)
    bn = _tile(ncs, BN_MM)
    per = ncs // bn
    blk = pl.BlockSpec((bm, bn), lambda i, j, k: (i, j))

    def epilogue(acc, ex, outs, ids):
        zr = jnp.maximum(acc + ex[0][...], 0.0)
        outs[0][...] = (zr * zr).astype(BF16)
        outs[1][...] = zr.astype(BF16)

    shape = jax.ShapeDtypeStruct((s_len, 4 * ncs), BF16)
    return _matmul(
        name, (s_len // bm, 4 * per, 1),
        x1b, pl.BlockSpec((bm, d), lambda i, j, k: (i, 0)),
        w_g, pl.BlockSpec((None, d, bn), lambda i, j, k: (j // per, 0, j % per)),
        nt=False,
        extras=[(_row(b_ff1), pl.BlockSpec((1, bn), lambda i, j, k: (0, j)))],
        outs=[(shape, blk), (shape, blk)],
        epilogue=epilogue, after=after,
    )


def _ff2(name, hf, w2_full, b_ff2):
    s_len, dff = hf.shape
    d = w2_full.shape[1]
    bm = _tile(s_len, BM_MM)
    bk = _tile(dff, BK_WIDE)
    blk = pl.BlockSpec((bm, d), lambda i, j, k: (i, 0))

    def epilogue(acc, ex, outs, ids):
        outs[0][...] = acc + ex[0][...]

    return _matmul(
        name, (s_len // bm, 1, dff // bk),
        hf, pl.BlockSpec((bm, bk), lambda i, j, k: (i, k)),
        w2_full, pl.BlockSpec((bk, d), lambda i, j, k: (k, 0)),
        nt=False,
        extras=[(_row(b_ff2), pl.BlockSpec((1, d), lambda i, j, k: (0, 0)))],
        outs=[(jax.ShapeDtypeStruct((s_len, d), F32), blk)],
        epilogue=epilogue,
    )[0]


def _resid_ln2(name, r1, g1, b1, fo, g2, b2, after=None):
    s_len, d = r1.shape
    bm = _tile(s_len, BM_LN // 2)
    blk = pl.BlockSpec((bm, d), lambda i: (i, 0))
    row = pl.BlockSpec((1, d), lambda i: (0, 0))
    order = [] if after is None else [after]

    def body(*refs):
        r1_ref, g1_ref, b1_ref, fo_ref, g2_ref, b2_ref, r2_ref, x2b_ref = refs[len(order):]
        xhat1, _ = _ln_stats(r1_ref[...])
        r2 = ALPHA * (xhat1 * g1_ref[...] + b1_ref[...]) + fo_ref[...]
        r2_ref[...] = r2
        xhat2, _ = _ln_stats(r2)
        x2b_ref[...] = (xhat2 * g2_ref[...] + b2_ref[...]).astype(BF16)

    return pl.pallas_call(
        body, name=name, grid=(s_len // bm,),
        in_specs=[pl.BlockSpec(memory_space=pl.ANY)] * len(order) + [blk, row, row, blk, row, row], out_specs=[blk, blk],
        out_shape=[jax.ShapeDtypeStruct((s_len, d), F32), jax.ShapeDtypeStruct((s_len, d), BF16)],
        compiler_params=_params(1))(*order, r1, _row(g1), _row(b1), fo, _row(g2), _row(b2))


def _dff_hidden(name, dr2b, w2_full, zr, rider=None):
    s_len, d = dr2b.shape
    dff = w2_full.shape[0]
    bm = _tile(s_len, BM_MM)
    bn = _tile(dff, BN_MM)

    def epilogue(acc, ex, outs, ids):
        dz = acc * (2.0 * ex[0][...].astype(F32))
        outs[0][...] = dz.astype(BF16)
        _accumulate(outs[1], _colsum(dz), ids[1] == 0)

    grid = (dff // bn, s_len // bm, 1)
    rider = rider and rider(math.prod(grid))
    res = _matmul(
        name, grid,
        dr2b, pl.BlockSpec((bm, d), lambda j, i, k: (i, 0)),
        w2_full, pl.BlockSpec((bn, d), lambda j, i, k: (j, 0)),
        nt=True,
        extras=[(zr, pl.BlockSpec((bm, bn), lambda j, i, k: (i, j)))],
        outs=[(jax.ShapeDtypeStruct((s_len, dff), BF16), pl.BlockSpec((bm, bn), lambda j, i, k: (i, j))),
              (jax.ShapeDtypeStruct((1, dff), F32), pl.BlockSpec((1, bn), lambda j, i, k: (0, j)))],
        epilogue=epilogue, rider=rider,
    )
    return tuple(res) if rider is None else (tuple(res[:2]), res[2:])


def _dx_sharded(name, dyb, w_g, resid=None, after=None, rider=None):
    s_len = dyb.shape[0]
    d, ncs = w_g.shape[1], w_g.shape[2]
    bm = _tile(s_len, BM_MM if resid is None else BM_LN)
    bk = _tile(ncs, BK_WIDE) if ncs % BK_MM == 0 else ncs
    per = ncs // bk
    blk = pl.BlockSpec((bm, d), lambda i, j, k: (i, 0))

    def epilogue(acc, ex, outs, ids):
        outs[0][...] = acc if resid is None else acc + ALPHA * ex[0][...]

    rider = rider and rider(s_len // bm * 4 * per)
    res = _matmul(
        name, (s_len // bm, 1, 4 * per),
        dyb, pl.BlockSpec((bm, bk), lambda i, j, k: (i, k)),
        w_g, pl.BlockSpec((None, d, bk), lambda i, j, k: (k // per, 0, k % per)),
        nt=True,
        extras=[] if resid is None else [(resid, blk)],
        outs=[(jax.ShapeDtypeStruct((s_len, d), F32), blk)],
        epilogue=epilogue, after=after, rider=rider,
    )
    return res[0] if rider is None else (res[0], res[1:])


def _dmixed(name, dr1b, w_out_full, rider=None):
    s_len, d = dr1b.shape
    bm = _tile(s_len, BM_LN)
    blk = pl.BlockSpec((bm, d), lambda i, j, k: (i, 0))

    def epilogue(acc, ex, outs, ids):
        outs[0][...] = acc

    rider = rider and rider(s_len // bm)
    res = _matmul(
        name, (s_len // bm, 1, 1),
        dr1b, blk,
        w_out_full, pl.BlockSpec((d, d), lambda i, j, k: (0, 0)),
        nt=True, extras=[],
        outs=[(jax.ShapeDtypeStruct((s_len, d), F32), blk)],
        epilogue=epilogue, rider=rider,
    )
    return res[0] if rider is None else (res[0], res[1:])


def _ln_backward(name, r, g, *, dy=None, resid=None, last=None, after=None):
    from_loss = last is not None
    s_len, d = last[0].shape if from_loss else r.shape
    bm = _tile(s_len, BM_LN // 2)
    blk = pl.BlockSpec((bm, d), lambda i: (i, 0))
    row = pl.BlockSpec((1, d), lambda i: (0, 0))

    def body(*refs):
        i = pl.program_id(0)
        if after is not None:
            refs = refs[1:]
        if from_loss:
            r1_ref, fo_ref, t_ref, g1_ref, b1_ref, b_ref, g_ref, dr_ref, drb_ref, dg_ref, db_ref, dbias_ref, loss_ref = refs
            xhat1, _ = _ln_stats(r1_ref[...])
            rv = ALPHA * (xhat1 * g1_ref[...] + b1_ref[...]) + fo_ref[...]
            xhat, rstd = _ln_stats(rv)
            diff = (xhat * g_ref[...] + b_ref[...]) - t_ref[...]
            dyv = diff * (1.0 / d)
            part = 0.5 * jnp.sum(jnp.sum(diff * diff, axis=1, keepdims=True) * (1.0 / d), axis=0, keepdims=True)
            _accumulate(loss_ref, jnp.broadcast_to(part, loss_ref.shape), i == 0)
        else:
            if resid is not None:
                dy_ref, res_ref, r_ref, g_ref, dr_ref, drb_ref, dg_ref, db_ref, dbias_ref = refs
                dyv = dy_ref[...] + ALPHA * res_ref[...]
            else:
                dy_ref, r_ref, g_ref, dr_ref, drb_ref, dg_ref, db_ref, dbias_ref = refs
                dyv = dy_ref[...]
            xhat, rstd = _ln_stats(r_ref[...])
        dr = _ln_bwd(dyv, xhat, rstd, g_ref[...])
        dr_ref[...] = dr
        drb_ref[...] = dr.astype(BF16)
        _accumulate(dg_ref, _colsum(dyv * xhat), i == 0)
        _accumulate(db_ref, _colsum(dyv), i == 0)
        _accumulate(dbias_ref, _colsum(dr), i == 0)

    if from_loss:
        r1, g1, b1, fo, b, target = last
        ins = [r1, fo, target, _row(g1), _row(b1), _row(b), _row(g)]
        in_specs = [blk] * 3 + [row] * 4
    else:
        lead = [dy] if resid is None else [dy, resid]
        ins = lead + [r, _row(g)]
        in_specs = [blk] * len(lead) + [blk, row]
    if after is not None:
        ins, in_specs = [after] + ins, [pl.BlockSpec(memory_space=pl.ANY)] + in_specs
    out_shape = [jax.ShapeDtypeStruct((s_len, d), F32), jax.ShapeDtypeStruct((s_len, d), BF16)] + \
                [jax.ShapeDtypeStruct((1, d), F32)] * 3
    out_specs = [blk, blk, row, row, row]
    if from_loss:
        out_shape.append(jax.ShapeDtypeStruct((8, 128), F32))
        out_specs.append(pl.BlockSpec((8, 128), lambda i: (0, 0)))
    return pl.pallas_call(body, name=name, grid=(s_len // bm,), in_specs=in_specs, out_specs=out_specs,
                          out_shape=out_shape, compiler_params=_params(1))(*ins)


def _rowwise(name, fn, ins, out_dtypes, rows_pref=BR_EW):
    r, c = ins[0].shape
    br = _tile(r, rows_pref)
    blk = pl.BlockSpec((br, c), lambda i: (i, 0))

    def body(*refs):
        res = fn(*[ref[...] for ref in refs[:len(ins)]])
        for o_ref, v in zip(refs[len(ins):], res):
            o_ref[...] = v.astype(o_ref.dtype)

    return pl.pallas_call(body, name=name, grid=(r // br,), in_specs=[blk] * len(ins),
                          out_specs=[blk] * len(out_dtypes),
                          out_shape=[jax.ShapeDtypeStruct((r, c), dt) for dt in out_dtypes],
                          compiler_params=_params(1))(*ins)


def _adamw_math(w, g, m, v):
    m = ADAM_B1 * m + (1.0 - ADAM_B1) * g
    v = ADAM_B2 * v + (1.0 - ADAM_B2) * (g * g)
    m_hat = m / (1.0 - ADAM_B1 ** ADAM_STEP)
    v_hat = v / (1.0 - ADAM_B2 ** ADAM_STEP)
    delta = -ADAM_LR * (m_hat / (jnp.sqrt(v_hat) + ADAM_EPS) + ADAM_WD * w)
    return delta, m, v


def _adamw(name, w, g, m, v):
    shape = w.shape
    c = shape[-1]
    flat = [a.reshape(-1, c) for a in (w, g, m, v)]
    rows = flat[0].shape[0]
    pref = max(8, 2 ** int(math.log2(EW_BLOCK_ELEMS // c)))
    res = _rowwise(name, _adamw_math, flat, [F32, F32, F32], rows_pref=pref if rows % pref == 0 else rows)
    return tuple(a.reshape(shape) for a in res)


def _pool_means(ext_ref, ts, tile_index):
    t_glob = tile_index * ts + lax.broadcasted_iota(jnp.int32, (ts, GROUP), 0)
    qs = []
    for g, win in enumerate(POOL_WINDOWS):
        cols = pl.ds(g * GROUP, GROUP)
        cur = ext_ref[pl.ds(16, ts), cols]
        acc = cur
        for j in range(1, win):
            acc = acc + ext_ref[pl.ds(16 - j, ts), cols]
        cnt = jnp.minimum(t_glob + 1, win).astype(F32)
        qs.append(acc / cnt - cur)
    return qs


def _masked_sgu_w(w_ref, h):
    r = lax.broadcasted_iota(jnp.int32, (CHUNK, CHUNK), 0)
    c = lax.broadcasted_iota(jnp.int32, (CHUNK, CHUNK), 1)
    return jnp.where(r >= c, w_ref[h], 0.0)


SUBLANES = 8


def _fill_shifted(src_ref, shifted_ref, ts):
    rows = ts + HALO - SUBLANES
    for b in range(1, SUBLANES):
        shifted_ref[b - 1] = src_ref[pl.ds(b, rows), :]


def _shifted_rows(src_ref, shifted_ref, offset, rows, cols):
    a, b = divmod(offset, SUBLANES)
    if b == 0:
        return src_ref[pl.ds(offset, rows), cols]
    return shifted_ref[b - 1, pl.ds(a * SUBLANES, rows), cols]


def _conv_taps(acc_init, w_ref, src_ref, shifted_ref, ts, base, emit):
    nrb = ts // RB_CONV
    for cg in range(CONV_WIDTH // GROUP):
        cols = pl.ds(cg * GROUP, GROUP)
        for rb0 in range(0, nrb, CONV_LIVE_BLOCKS):
            group = range(rb0, min(rb0 + CONV_LIVE_BLOCKS, nrb))
            accs = {rb: acc_init(cg) for rb in group}
            for k in range(CONV_KERNEL):
                wk = jnp.broadcast_to(w_ref[pl.ds(k, 1), cols], (RB_CONV, GROUP))
                for rb in group:
                    accs[rb] = accs[rb] + wk * _shifted_rows(src_ref, shifted_ref, rb * RB_CONV + base(k), RB_CONV, cols)
            for rb in group:
                emit(rb * RB_CONV, cg * GROUP, accs[rb])


def _mix_forward(name, p, lw):
    s_len = p.shape[0]
    ts = _tile(s_len, TS_MIX)
    per_halo = ts // HALO
    d = POOL_WIDTH + SGU_WIDTH + CONV_WIDTH

    def body(p_ref, ph_ref, wp_ref, ps_ref, slg_ref, slb_ref, ws_ref, sb_ref, cw_ref, cb_ref, clg_ref, clb_ref,
             mixed_ref, cv_ref, pool_ext, hh_ext, shifted):
        i = pl.program_id(0)
        keep = (i > 0).astype(F32)
        pool_ext[pl.ds(0, 16), :] = ph_ref[pl.ds(16, 16), pl.ds(0, POOL_WIDTH)] * keep
        pool_ext[pl.ds(16, ts), :] = p_ref[:, pl.ds(0, POOL_WIDTH)]
        qs = _pool_means(pool_ext, ts, i)
        for g in range(len(POOL_WINDOWS)):
            cols = pl.ds(g * GROUP, GROUP)
            e = jnp.dot(qs[g].astype(BF16), wp_ref[g].astype(BF16), preferred_element_type=F32)
            mixed_ref[:, cols] = (e * ps_ref[:, cols]).astype(BF16)
        uv = _gelu(p_ref[:, pl.ds(COL_B, 2 * SGU_WIDTH)])
        u = uv[:, :SGU_WIDTH]
        vhat, _ = _ln_stats(uv[:, SGU_WIDTH:])
        vn = (vhat * slg_ref[...] + slb_ref[...]).astype(BF16)
        for h in range(SGU_HEADS):
            wm = _masked_sgu_w(ws_ref, h).astype(BF16)
            for n in range(ts // CHUNK):
                rows = slice(n * CHUNK, (n + 1) * CHUNK)
                cols = slice(h * GROUP, (h + 1) * GROUP)
                mx = jnp.dot(wm, vn[rows, cols], preferred_element_type=F32) + sb_ref[h]
                mixed_ref[pl.ds(n * CHUNK, CHUNK), pl.ds(POOL_WIDTH + h * GROUP, GROUP)] = (u[rows, cols] * mx).astype(BF16)
        hh_ext[pl.ds(0, HALO), :] = (ph_ref[:, pl.ds(COL_C, CONV_WIDTH)]
                                     * _sigmoid(ph_ref[:, pl.ds(COL_C + CONV_WIDTH, CONV_WIDTH)])) * keep
        hh_ext[pl.ds(HALO, ts), :] = p_ref[:, pl.ds(COL_C, CONV_WIDTH)] * _sigmoid(p_ref[:, pl.ds(COL_C + CONV_WIDTH, CONV_WIDTH)])
        init = lambda cg: jnp.broadcast_to(cb_ref[:, pl.ds(cg * GROUP, GROUP)], (RB_CONV, GROUP))
        _fill_shifted(hh_ext, shifted, ts)
        def store_cv(r0, c0, blk):
            cv_ref[pl.ds(r0, RB_CONV), pl.ds(c0, GROUP)] = blk

        _conv_taps(init, cw_ref, hh_ext, shifted, ts, lambda k: k + HALO - (CONV_KERNEL - 1), store_cv)
        cvhat, _ = _ln_stats(cv_ref[...])
        cn = cvhat * clg_ref[...] + clb_ref[...]
        mixed_ref[:, pl.ds(POOL_WIDTH + SGU_WIDTH, CONV_WIDTH)] = (cn * _sigmoid(cn)).astype(BF16)

    full = lambda a: pl.BlockSpec(a.shape, lambda i: (0,) * a.ndim)
    weights = [lw["w_pool"], _row(lw["pool_scale"]), _row(lw["sgu_ln_g"]), _row(lw["sgu_ln_b"]), lw["sgu_w"],
               lw["sgu_bias_tile"], lw["conv_w_full"], _row(lw["conv_b"]), _row(lw["conv_ln_g"]), _row(lw["conv_ln_b"])]
    return pl.pallas_call(
        body, name=name, grid=(s_len // ts,),
        in_specs=[pl.BlockSpec((ts, IN_WIDTH), lambda i: (i, 0)),
                  pl.BlockSpec((HALO, IN_WIDTH), lambda i: (jnp.maximum(i * per_halo - 1, 0), 0))] + [full(a) for a in weights],
        out_specs=[pl.BlockSpec((ts, d), lambda i: (i, 0)), pl.BlockSpec((ts, CONV_WIDTH), lambda i: (i, 0))],
        out_shape=[jax.ShapeDtypeStruct((s_len, d), BF16), jax.ShapeDtypeStruct((s_len, CONV_WIDTH), F32)],
        scratch_shapes=[pltpu.VMEM((16 + ts, POOL_WIDTH), F32), pltpu.VMEM((HALO + ts, CONV_WIDTH), F32),
                        pltpu.VMEM((SUBLANES - 1, ts + HALO - SUBLANES, CONV_WIDTH), F32)],
        compiler_params=_params(1),
    )(p, p, *weights)


def _mix_backward(name, p, cv, dmixed, lw):
    s_len = p.shape[0]
    ts = _tile(s_len, TS_MIX)
    nt = s_len // ts
    per_halo = ts // HALO
    d = POOL_WIDTH + SGU_WIDTH + CONV_WIDTH
    nch = ts // CHUNK
    col_yc = POOL_WIDTH + SGU_WIDTH

    def body(p_ref, ph_ref, cv_ref, cvn_ref, dm_ref, dmn_ref, wp_ref, ps_ref, slg_ref, slb_ref, ws_ref, wst_ref, sb_ref,
             cw_ref, cwr_ref, clg_ref, clb_ref,
             dp_ref, dbin_ref, dwp_ref, dps_ref, dslg_ref, dslb_ref, dws_ref, dsb_ref, dcw_ref, dcb_ref, dclg_ref, dclb_ref,
             pool_ext, dq_ext, hh_ext, dcv_ext, dcw_acc, shifted):
        i = pl.program_id(0)
        first = i == 0
        keep_prev = (i > 0).astype(F32)
        keep_next = (i < nt - 1).astype(F32)

        pool_ext[pl.ds(0, 16), :] = ph_ref[pl.ds(16, 16), pl.ds(0, POOL_WIDTH)] * keep_prev
        pool_ext[pl.ds(16, ts), :] = p_ref[:, pl.ds(0, POOL_WIDTH)]
        qs = _pool_means(pool_ext, ts, i)
        t_ext = i * ts + lax.broadcasted_iota(jnp.int32, (ts + HALO, GROUP), 0)
        for g, win in enumerate(POOL_WINDOWS):
            cols = pl.ds(g * GROUP, GROUP)
            wpb = wp_ref[g].astype(BF16)
            qb = qs[g].astype(BF16)
            de = dm_ref[:, cols] * ps_ref[:, cols]
            e = jnp.dot(qb, wpb, preferred_element_type=F32)
            _accumulate(dps_ref.at[:, cols], _colsum(dm_ref[:, cols] * e), first)
            deb = de.astype(BF16)
            _accumulate(dwp_ref.at[g], lax.dot_general(qb, deb, (((0,), (0,)), ((), ())), preferred_element_type=F32), first)
            de_next = (dmn_ref[:, cols] * ps_ref[:, cols] * keep_next).astype(BF16)
            de_all = jnp.concatenate([deb, de_next], axis=0)
            dq = lax.dot_general(de_all, wpb, (((1,), (1,)), ((), ())), preferred_element_type=F32)
            inv = 1.0 / jnp.minimum(t_ext + 1, win).astype(F32)
            dq_ext[:, cols] = dq * inv
            acc = dq_ext[pl.ds(0, ts), cols]
            for j in range(1, win):
                acc = acc + dq_ext[pl.ds(j, ts), cols]
            dpa = acc - dq[:ts]
            dp_ref[:, cols] = dpa.astype(BF16)
            _accumulate(dbin_ref.at[:, cols], _colsum(dpa), first)

        pb = p_ref[:, pl.ds(COL_B, 2 * SGU_WIDTH)]
        uv, duv = _gelu_with_grad(pb)
        u = uv[:, :SGU_WIDTH]
        vhat, vrstd = _ln_stats(uv[:, SGU_WIDTH:])
        vn = (vhat * slg_ref[...] + slb_ref[...]).astype(BF16)
        dyb = dm_ref[:, pl.ds(POOL_WIDTH, SGU_WIDTH)]
        dmix = dyb * u
        dmixb = dmix.astype(BF16)
        du_parts, dvn_parts = [], []
        for h in range(SGU_HEADS):
            cols = slice(h * GROUP, (h + 1) * GROUP)
            wm = _masked_sgu_w(ws_ref, h).astype(BF16)
            wmt = _masked_sgu_w_t(wst_ref, h).astype(BF16)
            dws_h = jnp.zeros((CHUNK, CHUNK), F32)
            dsb_h = jnp.zeros((CHUNK, GROUP), F32)
            du_rows, dvn_rows = [], []
            for n in range(nch):
                rows = slice(n * CHUNK, (n + 1) * CHUNK)
                mx = jnp.dot(wm, vn[rows, cols], preferred_element_type=F32) + sb_ref[h]
                du_rows.append(dyb[rows, cols] * mx)
                dws_h = dws_h + lax.dot_general(dmixb[rows, cols], vn[rows, cols], (((1,), (1,)), ((), ())),
                                                preferred_element_type=F32)
                dsb_h = dsb_h + dmix[rows, cols]
                dvn_rows.append(jnp.dot(wmt, dmixb[rows, cols], preferred_element_type=F32))
            r = lax.broadcasted_iota(jnp.int32, (CHUNK, CHUNK), 0)
            c = lax.broadcasted_iota(jnp.int32, (CHUNK, CHUNK), 1)
            _accumulate(dws_ref.at[h], jnp.where(r >= c, dws_h, 0.0), first)
            _accumulate(dsb_ref.at[h], jnp.broadcast_to(jnp.sum(dsb_h, axis=1, keepdims=True), (CHUNK, GROUP)), first)
            du_parts.append(jnp.concatenate(du_rows, axis=0))
            dvn_parts.append(jnp.concatenate(dvn_rows, axis=0))
        du = jnp.concatenate(du_parts, axis=1)
        dvn = jnp.concatenate(dvn_parts, axis=1)
        _accumulate(dslg_ref, _colsum(dvn * vhat), first)
        _accumulate(dslb_ref, _colsum(dvn), first)
        dv = _ln_bwd(dvn, vhat, vrstd, slg_ref[...])
        dpb = jnp.concatenate([du, dv], axis=1) * duv
        dp_ref[:, pl.ds(COL_B, 2 * SGU_WIDTH)] = dpb.astype(BF16)
        _accumulate(dbin_ref.at[:, pl.ds(COL_B, 2 * SGU_WIDTH)], _colsum(dpb), first)

        a_main = p_ref[:, pl.ds(COL_C, CONV_WIDTH)]
        sg_main = _sigmoid(p_ref[:, pl.ds(COL_C + CONV_WIDTH, CONV_WIDTH)])
        hh_ext[pl.ds(HALO, ts), :] = a_main * sg_main

        def conv_ln_backward(cv_v, dyc_v):
            cvhat, crstd = _ln_stats(cv_v)
            cn = cvhat * clg_ref[...] + clb_ref[...]
            s = _sigmoid(cn)
            dcn = dyc_v * (s * (1.0 + cn * (1.0 - s)))
            return _ln_bwd(dcn, cvhat, crstd, clg_ref[...]), dcn, cvhat

        dcv, dcn, cvhat = conv_ln_backward(cv_ref[...], dm_ref[:, pl.ds(col_yc, CONV_WIDTH)])
        _accumulate(dclg_ref, _colsum(dcn * cvhat), first)
        _accumulate(dclb_ref, _colsum(dcn), first)
        _accumulate(dcb_ref, _colsum(dcv), first)
        dcv_next, _, _ = conv_ln_backward(cvn_ref[...], dmn_ref[:, pl.ds(col_yc, CONV_WIDTH)])
        dcv_ext[pl.ds(0, ts), :] = dcv
        dcv_ext[pl.ds(ts, HALO), :] = dcv_next * keep_next

        @pl.when(first)
        def _():
            dcw_acc[...] = jnp.zeros_like(dcw_acc)

        nrb = ts // RB_CONV
        _fill_shifted(dcv_ext, shifted, ts)
        for cg in range(CONV_WIDTH // GROUP):
            cols = pl.ds(cg * GROUP, GROUP)
            for rb0 in range(0, nrb, CONV_LIVE_BLOCKS):
                group = range(rb0, min(rb0 + CONV_LIVE_BLOCKS, nrb))
                hblk = {rb: hh_ext[pl.ds(HALO + rb * RB_CONV, RB_CONV), cols] for rb in group}
                for k in range(CONV_KERNEL):
                    part = jnp.zeros((8, GROUP), F32)
                    for rb in group:
                        offset = rb * RB_CONV + CONV_KERNEL - 1 - k
                        prod = hblk[rb] * _shifted_rows(dcv_ext, shifted, offset, RB_CONV, cols)
                        part = part + jnp.sum(prod.reshape(RB_CONV // 8, 8, GROUP), axis=0)
                    dcw_acc[k, :, cols] += part

        @pl.when(i == nt - 1)
        def _():
            dcw_ref[...] = jnp.sum(dcw_acc[...], axis=1)

        zero = lambda cg: jnp.zeros((RB_CONV, GROUP), F32)
        def glu_backward(r0, c0, blk):
            rows, cols = pl.ds(r0, RB_CONV), pl.ds(c0, GROUP)
            a_blk = p_ref[rows, pl.ds(COL_C + c0, GROUP)]
            s_blk = _sigmoid(p_ref[rows, pl.ds(COL_C + CONV_WIDTH + c0, GROUP)])
            da = blk * s_blk
            dg = blk * a_blk * s_blk * (1.0 - s_blk)
            dp_ref[rows, pl.ds(COL_C + c0, GROUP)] = da.astype(BF16)
            dp_ref[rows, pl.ds(COL_C + CONV_WIDTH + c0, GROUP)] = dg.astype(BF16)
            hh_ext[pl.ds(HALO + r0, RB_CONV), cols] = da
            dcv_ext[rows, cols] = dg

        _conv_taps(zero, cwr_ref, dcv_ext, shifted, ts, lambda k: k, glu_backward)
        _accumulate(dbin_ref.at[:, pl.ds(COL_C, CONV_WIDTH)], _colsum(hh_ext[pl.ds(HALO, ts), :]), first)
        _accumulate(dbin_ref.at[:, pl.ds(COL_C + CONV_WIDTH, CONV_WIDTH)], _colsum(dcv_ext[pl.ds(0, ts), :]), first)

    full = lambda a: pl.BlockSpec(a.shape, lambda i: (0,) * a.ndim)
    weights = [lw["w_pool"], _row(lw["pool_scale"]), _row(lw["sgu_ln_g"]), _row(lw["sgu_ln_b"]), lw["sgu_w"], lw["sgu_w_t"],
               lw["sgu_bias_tile"], lw["conv_w_full"], lw["conv_w_rev"], _row(lw["conv_ln_g"]), _row(lw["conv_ln_b"])]
    prev_halo = lambda i: (jnp.maximum(i * per_halo - 1, 0), 0)
    next_halo = lambda i: (jnp.minimum((i + 1) * per_halo, s_len // HALO - 1), 0)
    small = lambda shape: (jax.ShapeDtypeStruct(shape, F32), pl.BlockSpec(shape, lambda i: (0,) * len(shape)))
    outs = [(jax.ShapeDtypeStruct((s_len, IN_WIDTH), BF16), pl.BlockSpec((ts, IN_WIDTH), lambda i: (i, 0))),
            small((1, IN_WIDTH)), small((len(POOL_WINDOWS), GROUP, GROUP)), small((1, POOL_WIDTH)),
            small((1, SGU_WIDTH)), small((1, SGU_WIDTH)), small((SGU_HEADS, CHUNK, CHUNK)), small((SGU_HEADS, CHUNK, GROUP)),
            small((CONV_ROWS, CONV_WIDTH)), small((1, CONV_WIDTH)), small((1, CONV_WIDTH)), small((1, CONV_WIDTH))]
    return pl.pallas_call(
        body, name=name, grid=(nt,),
        in_specs=[pl.BlockSpec((ts, IN_WIDTH), lambda i: (i, 0)), pl.BlockSpec((HALO, IN_WIDTH), prev_halo),
                  pl.BlockSpec((ts, CONV_WIDTH), lambda i: (i, 0)), pl.BlockSpec((HALO, CONV_WIDTH), next_halo),
                  pl.BlockSpec((ts, d), lambda i: (i, 0)), pl.BlockSpec((HALO, d), next_halo)] + [full(a) for a in weights],
        out_specs=[s for _, s in outs],
        out_shape=[o for o, _ in outs],
        scratch_shapes=[pltpu.VMEM((16 + ts, POOL_WIDTH), F32), pltpu.VMEM((ts + HALO, POOL_WIDTH), F32),
                        pltpu.VMEM((HALO + ts, CONV_WIDTH), F32), pltpu.VMEM((ts + HALO, CONV_WIDTH), F32),
                        pltpu.VMEM((CONV_ROWS, 8, CONV_WIDTH), F32),
                        pltpu.VMEM((SUBLANES - 1, ts + HALO - SUBLANES, CONV_WIDTH), F32)],
        compiler_params=_params(1),
    )(p, p, cv, cv, dmixed, dmixed, *weights)


def _masked_sgu_w_t(wt_ref, h):
    r = lax.broadcasted_iota(jnp.int32, (CHUNK, CHUNK), 0)
    c = lax.broadcasted_iota(jnp.int32, (CHUNK, CHUNK), 1)
    return jnp.where(c >= r, wt_ref[h], 0.0)


HBM = pl.BlockSpec(memory_space=pltpu.HBM)
CHIP_FLIPS = ((1, 0), (0, 1), (1, 1))


def _place():
    return lax.axis_index("x"), lax.axis_index("y"), lax.axis_index("c")


def _half(ref, axis, which, size):
    idx = [slice(None)] * len(ref.shape)
    idx[axis] = pl.ds(which * size, size)
    return ref.at[tuple(idx)]


def _exchange(name, sources, inplace, fresh, copies):
    n_src, n_in, n = len(sources), len(inplace), len(copies)
    n_out = n_in + len(fresh)

    def body(*refs):
        ins = refs[:n_src + n_in]
        outs = refs[n_src + n_in:n_src + n_in + n_out]
        send_sems, recv_sems = refs[n_src + n_in + n_out:]
        started = _descriptors(copies, list(ins[:n_src]) + list(outs), send_sems, recv_sems)
        for cp in started:
            cp.start()
        for cp in started:
            cp.wait()

    out_shape = [jax.ShapeDtypeStruct(a.shape, a.dtype) for a in inplace] + list(fresh)
    return pl.pallas_call(
        body, name=name, in_specs=[HBM] * (n_src + n_in), out_specs=[HBM] * n_out, out_shape=out_shape,
        input_output_aliases={n_src + i: i for i in range(n_in)},
        scratch_shapes=[pltpu.SemaphoreType.DMA((n,)), pltpu.SemaphoreType.DMA((n,))],
    )(*sources, *inplace)


def _descriptors(copies, bufs, send_sems, recv_sems):
    x, y, c = _place()
    return [pltpu.make_async_remote_copy(
        src_ref=src_view(bufs[src], x, y, c), dst_ref=dst_view(bufs[dst], x, y, c),
        send_sem=send_sems.at[k], recv_sem=recv_sems.at[k], device_id=peer(x, y, c), device_id_type=MESH)
        for k, (src, src_view, dst, dst_view, peer) in enumerate(copies)]


SEM = pl.BlockSpec(memory_space=pltpu.SEMAPHORE)
IN_FLIGHT = pltpu.CompilerParams(has_side_effects=pltpu.SideEffectType.DATAFLOW_SIDE_EFFECTING)


def _exchange_start(name, sources, inplace, fresh, copies, after=None):
    n_src, n = len(sources), len(copies)
    landing = [lax.empty(f.shape, f.dtype) for f in fresh]
    bufs = [pltpu.with_memory_space_constraint(a, pltpu.HBM) for a in (*inplace, *landing)]
    srcs = [pltpu.with_memory_space_constraint(a, pltpu.HBM) for a in sources]
    n_buf = len(bufs)
    order = [] if after is None else [after]
    n_in = n_src + n_buf + len(order)

    def body(*refs):
        ins = refs[:n_src]
        send_sems, recv_sems = refs[n_in], refs[n_in + 1]
        outs = refs[n_in + 2:n_in + 2 + n_buf]
        token = refs[n_in + 2 + n_buf]
        for cp in _descriptors(copies, list(ins) + list(outs), send_sems, recv_sems):
            cp.start()
        token[...] = jnp.zeros_like(token)

    res = pl.pallas_call(
        body, name=name + "_start",
        out_shape=(pltpu.SemaphoreType.DMA((n,)), pltpu.SemaphoreType.DMA((n,)),
                   *[pltpu.HBM(b.shape, b.dtype) for b in bufs], jax.ShapeDtypeStruct((8, 128), F32)),
        in_specs=[HBM] * (n_src + n_buf) + [pl.BlockSpec(memory_space=pl.ANY)] * len(order),
        out_specs=(SEM, SEM, *[HBM] * n_buf, pl.BlockSpec(memory_space=pltpu.VMEM)),
        input_output_aliases={n_src + i: 2 + i for i in range(n_buf)},
        compiler_params=IN_FLIGHT,
    )(*srcs, *bufs, *order)
    return dict(name=name, sources=srcs, sems=res[:2], bufs=list(res[2:2 + n_buf]), token=res[2 + n_buf], copies=copies)


def _exchange_wait(handle, after):
    srcs, bufs, copies = handle["sources"], handle["bufs"], handle["copies"]
    n_src, n_buf = len(srcs), len(bufs)

    def body(*refs):
        ins = refs[:n_src]
        send_sems, recv_sems = refs[n_src + n_buf], refs[n_src + n_buf + 1]
        outs = refs[n_src + n_buf + 3:]
        for cp in _descriptors(copies, list(ins) + list(outs), send_sems, recv_sems):
            cp.wait()

    return list(pl.pallas_call(
        body, name=handle["name"] + "_wait",
        out_shape=tuple(pltpu.HBM(b.shape, b.dtype) for b in bufs),
        in_specs=[HBM] * (n_src + n_buf) + [SEM, SEM, pl.BlockSpec(memory_space=pl.ANY)],
        out_specs=tuple([HBM] * n_buf),
        input_output_aliases={n_src + i: i for i in range(n_buf)},
        compiler_params=IN_FLIGHT,
    )(*srcs, *bufs, *handle["sems"], after))


def _into_slot(name, src, layer, place, dtype, after=None):
    _, r, c = src.shape
    br = _tile(r, BR_EW) if r % BR_EW == 0 else r
    order = [] if after is None else [after]

    def body(pr_ref, s_ref, *rest):
        rest[-1][...] = s_ref[...].astype(dtype)

    return pl.pallas_call(
        body, name=name,
        grid_spec=pltpu.PrefetchScalarGridSpec(
            num_scalar_prefetch=1, grid=(r // br,),
            in_specs=[pl.BlockSpec((None, br, c), lambda i, pr: (layer, i, 0))] + [pl.BlockSpec(memory_space=pl.ANY)] * len(order),
            out_specs=pl.BlockSpec((None, br, c), lambda i, pr: (pr[1], i, 0))),
        out_shape=jax.ShapeDtypeStruct((4, r, c), dtype), compiler_params=_params(1),
    )(place, src, *order)


def _gather_copies(n_bufs):
    def own_half(ref, x, y, c):
        return _half(ref.at[2 * x + y], 0, c, ref.shape[1] // 2)

    ici, d2d = [], []
    for b in range(n_bufs):
        for fx, fy in CHIP_FLIPS:
            ici.append((b, own_half, b, own_half, lambda x, y, c, fx=fx, fy=fy: (x ^ fx, y ^ fy, c)))
            landed = lambda ref, x, y, c, fx=fx, fy=fy: _half(ref.at[2 * (x ^ fx) + (y ^ fy)], 0, c, ref.shape[1] // 2)
            d2d.append((b, landed, b, landed, _sibling))
    return ici, d2d


def _gather_start(name, bufs, after=None):
    return _exchange_start(name + "_chips", [], bufs, [], _gather_copies(len(bufs))[0], after=after)


def _gather_finish(name, handle, after):
    bufs = _exchange_wait(handle, after)
    return _exchange(name + "_sibling", [], bufs, [], _gather_copies(len(bufs))[1])


def _gather_pass_on(name, handle, after):
    bufs = _exchange_wait(handle, after)
    return _exchange_start(name + "_sibling", [], bufs, [], _gather_copies(len(bufs))[1])


def _sibling(x, y, c):
    return (x, y, 1 - c)


def _scalar_spec_call(name, fn, scalars, ins, in_blocks, out_shapes, out_blocks, grid):
    def body(s_ref, *refs):
        res = fn(*[r[...] for r in refs[:len(ins)]])
        for o_ref, v in zip(refs[len(ins):], res):
            o_ref[...] = v.astype(o_ref.dtype)

    return pl.pallas_call(
        body, name=name,
        grid_spec=pltpu.PrefetchScalarGridSpec(num_scalar_prefetch=1, grid=grid, in_specs=in_blocks, out_specs=out_blocks),
        out_shape=out_shapes, compiler_params=_params(len(grid)),
    )(scalars, *ins)


def _reduce_begin(tag, place, items):
    n = len(items)
    whole = lambda ref, x, y, c: ref
    halves = [g.shape[1] // 2 for g, _, _ in items]
    fresh = [jax.ShapeDtypeStruct((g.shape[0], halves[a], g.shape[2]), F32) for a, (g, _, _) in enumerate(items)]
    copies = [(a, lambda ref, x, y, c, hr=halves[a]: _half(ref, 1, 1 - c, hr), n + a, whole, _sibling) for a in range(n)]
    handle = _exchange_start(tag + "_sibling_in", [g for g, _, _ in items], [], fresh, copies)
    return dict(tag=tag, place=place, items=items, handle=handle, token=handle["token"])


def _reduce_between_chips(state, after):
    tag, place, items = state["tag"], state["place"], state["items"]
    n = len(items)
    slot_of = lambda scatter: (lambda x, y: 2 * x + y) if scatter else (lambda x, y: 0)
    halves = [g.shape[1] // 2 for g, _, _ in items]
    landed = _exchange_wait(state["handle"], after)

    chip_sums = []
    for a, (g, _, wire) in enumerate(items):
        ns, _, cols = g.shape
        hr = halves[a]
        br = _tile(hr, BR_EW)
        chip_sums.append(_scalar_spec_call(
            f"{tag}_chip_sum{a}", lambda u, v: (u + v,), place, [g, landed[a]],
            [pl.BlockSpec((None, br, cols), lambda s, i, pr, hb=hr // br: (s, pr[0] * hb + i, 0)),
             pl.BlockSpec((None, br, cols), lambda s, i, pr: (s, i, 0))],
            [jax.ShapeDtypeStruct((ns, hr, cols), wire)], [pl.BlockSpec((None, br, cols), lambda s, i, pr: (s, i, 0))],
            (ns, hr // br))[0])

    copies, fresh = [], []
    for a, (g, scatter, wire) in enumerate(items):
        slot = slot_of(scatter)
        fresh.append(jax.ShapeDtypeStruct((3, halves[a], g.shape[2]), wire))
        for j, (fx, fy) in enumerate(CHIP_FLIPS):
            copies.append((a, lambda ref, x, y, c, fx=fx, fy=fy, slot=slot: ref.at[slot(x ^ fx, y ^ fy)],
                           n + a, lambda ref, x, y, c, j=j: ref.at[j], lambda x, y, c, fx=fx, fy=fy: (x ^ fx, y ^ fy, c)))
    handle = _exchange_start(tag + "_chips", chip_sums, [], fresh, copies)
    return dict(tag=tag, place=place, items=items, chip_sums=chip_sums, handle=handle, token=handle["token"])


def _reduce_swap(state, after):
    tag, place, items, chip_sums = state["tag"], state["place"], state["items"], state["chip_sums"]
    n = len(items)
    whole = lambda ref, x, y, c: ref
    halves = [g.shape[1] // 2 for g, _, _ in items]
    arrived = _exchange_wait(state["handle"], after)

    tree = lambda own, fx, fy, fxy: ((own.astype(F32) + fx.astype(F32)) + (fy.astype(F32) + fxy.astype(F32)),)
    mine = []
    for a, (g, scatter, _) in enumerate(items):
        hr, cols = halves[a], g.shape[2]
        br = _tile(hr, BR_EW)
        got = lambda j: pl.BlockSpec((None, br, cols), lambda i, pr, j=j: (j, i, 0))
        own = pl.BlockSpec((None, br, cols), (lambda i, pr: (pr[1], i, 0)) if scatter else (lambda i, pr: (0, i, 0)))
        mine.append(_scalar_spec_call(
            f"{tag}_tree_sum{a}", tree, place, [chip_sums[a], arrived[a], arrived[a], arrived[a]],
            [own, got(0), got(1), got(2)],
            [jax.ShapeDtypeStruct((2, hr, cols), F32)], [pl.BlockSpec((None, br, cols), lambda i, pr: (pr[0], i, 0))],
            (hr // br,))[0])

    my_half = lambda ref, x, y, c: ref.at[c]
    handle = _exchange_start(tag + "_sibling_out", [], mine, [], [(a, my_half, a, my_half, _sibling) for a in range(n)])
    return dict(handle=handle, token=handle["token"])


def _reduce_end(state, after):
    return [b.reshape(2 * b.shape[1], b.shape[2]) for b in _exchange_wait(state["handle"], after)]


def _adamw_blocks(layer, w, m, v, g, earlier, n_blocks):
    n_layers, r, c = w.shape
    assert r % n_blocks == 0 and (r // n_blocks) % 8 == 0, (r, n_blocks)
    br = r // n_blocks
    in_layer = ((None, br, c), lambda b: (layer, b, 0))
    ins = [(a, *in_layer) for a in (w, m, v)] + [(g, (br, c), lambda b: (b, 0))] + [(a, None, None) for a in earlier or ()]
    outs = [(jax.ShapeDtypeStruct((n_layers, r, c), F32), *in_layer)] * 4
    aliases = {4 + i: i for i in range(len(earlier or ()))}

    def fn(wv, mv, vv, gv):
        return (gv,) + _adamw_math(wv, gv, mv, vv)

    return fn, ins, outs, aliases, n_blocks


def _join_riders(riders):
    n_blocks = riders[0][4]
    data = [[i for i in r[1] if i[1] is not None] for r in riders]
    thru = [[i for i in r[1] if i[1] is None] for r in riders]
    ins = sum(data, []) + sum(thru, [])
    outs = sum((r[2] for r in riders), [])
    aliases, at_in, at_out = {}, len(sum(data, [])), 0
    for r, t in zip(riders, thru):
        assert r[4] == n_blocks and sorted(r[3].values()) == list(range(len(t)))
        aliases.update({at_in + i: at_out + i for i in range(len(t))})
        at_in, at_out = at_in + len(t), at_out + len(r[2])

    def fn(*vals):
        res, at = (), 0
        for r, d in zip(riders, data):
            res, at = res + tuple(r[0](*vals[at:at + len(d)])), at + len(d)
        return res

    return fn, ins, outs, aliases, n_blocks


def _adamw_layer(name, layer, w, m, v, g, earlier):
    c = w.shape[2]
    br = max(8, 2 ** int(math.log2(EW_BLOCK_ELEMS // c)))
    fn, ins, outs, aliases, n_blocks = _adamw_blocks(layer, w, m, v, g, earlier, w.shape[1] // min(br, w.shape[1]))
    n_data = sum(1 for _, bs, _ in ins if bs is not None)

    def body(*refs):
        res = fn(*[r[...] for r in refs[:n_data]])
        for o_ref, val in zip(refs[len(ins):], res):
            o_ref[...] = val

    spec = lambda bs, f: pl.BlockSpec(memory_space=pl.ANY) if bs is None else pl.BlockSpec(bs, lambda i, f=f: f(i))
    return pl.pallas_call(
        body, name=name, grid=(n_blocks,),
        in_specs=[spec(bs, f) for _, bs, f in ins], out_specs=[spec(bs, f) for _, bs, f in outs],
        out_shape=[o for o, _, _ in outs], input_output_aliases=aliases, compiler_params=_params(1),
    )(*[a for a, _, _ in ins])


SHARDED = ("w_in", "w_out", "w_ff1", "w_ff2")
REPLICATED = ("b_in", "w_pool", "pool_scale", "sgu_ln_g", "sgu_ln_b", "sgu_w", "sgu_b", "conv_b", "conv_ln_g", "conv_ln_b",
              "b_out", "ln1_g", "ln1_b", "b_ff1", "b_ff2", "ln2_g", "ln2_b")
WEIGHTS = ("w_in", "b_in", "w_pool", "pool_scale", "sgu_ln_g", "sgu_ln_b", "sgu_w", "sgu_b", "conv_w", "conv_b", "conv_ln_g",
           "conv_ln_b", "w_out", "b_out", "ln1_g", "ln1_b", "w_ff1", "b_ff1", "w_ff2", "b_ff2", "ln2_g", "ln2_b")
PACK_ROWS = 1024


def _pack(arrays):
    flat = jnp.concatenate([a.reshape(-1) for a in arrays])
    rows = -(-flat.shape[0] // (128 * PACK_ROWS)) * PACK_ROWS
    return jnp.pad(flat, (0, rows * 128 - flat.shape[0])).reshape(rows, 128)


def _unpack(packed, like):
    flat = packed.reshape(-1)
    out, at = [], 0
    for a in like:
        out.append(flat[at:at + a.size].reshape(a.shape))
        at += a.size
    return out


def kernel(x, w_in, b_in, w_pool, pool_scale, sgu_ln_g, sgu_ln_b, sgu_w, sgu_b, conv_w, conv_b, conv_ln_g, conv_ln_b, w_out, b_out, ln1_g, ln1_b, w_ff1, b_ff1, w_ff2, b_ff2, ln2_g, ln2_b, loss_target, m_w_in, m_b_in, m_w_pool, m_pool_scale, m_sgu_ln_g, m_sgu_ln_b, m_sgu_w, m_sgu_b, m_conv_w, m_conv_b, m_conv_ln_g, m_conv_ln_b, m_w_out, m_b_out, m_ln1_g, m_ln1_b, m_w_ff1, m_b_ff1, m_w_ff2, m_b_ff2, m_ln2_g, m_ln2_b, v_w_in, v_b_in, v_w_pool, v_pool_scale, v_sgu_ln_g, v_sgu_ln_b, v_sgu_w, v_sgu_b, v_conv_w, v_conv_b, v_conv_ln_g, v_conv_ln_b, v_w_out, v_b_out, v_ln1_g, v_ln1_b, v_w_ff1, v_b_ff1, v_w_ff2, v_b_ff2, v_ln2_g, v_ln2_b):
    w = dict(w_in=w_in, b_in=b_in, w_pool=w_pool, pool_scale=pool_scale, sgu_ln_g=sgu_ln_g, sgu_ln_b=sgu_ln_b, sgu_w=sgu_w,
             sgu_b=sgu_b, conv_w=conv_w, conv_b=conv_b, conv_ln_g=conv_ln_g, conv_ln_b=conv_ln_b, w_out=w_out, b_out=b_out,
             ln1_g=ln1_g, ln1_b=ln1_b, w_ff1=w_ff1, b_ff1=b_ff1, w_ff2=w_ff2, b_ff2=b_ff2, ln2_g=ln2_g, ln2_b=ln2_b)
    m = dict(w_in=m_w_in, b_in=m_b_in, w_pool=m_w_pool, pool_scale=m_pool_scale, sgu_ln_g=m_sgu_ln_g, sgu_ln_b=m_sgu_ln_b,
             sgu_w=m_sgu_w, sgu_b=m_sgu_b, conv_w=m_conv_w, conv_b=m_conv_b, conv_ln_g=m_conv_ln_g, conv_ln_b=m_conv_ln_b,
             w_out=m_w_out, b_out=m_b_out, ln1_g=m_ln1_g, ln1_b=m_ln1_b, w_ff1=m_w_ff1, b_ff1=m_b_ff1, w_ff2=m_w_ff2,
             b_ff2=m_b_ff2, ln2_g=m_ln2_g, ln2_b=m_ln2_b)
    v = dict(w_in=v_w_in, b_in=v_b_in, w_pool=v_w_pool, pool_scale=v_pool_scale, sgu_ln_g=v_sgu_ln_g, sgu_ln_b=v_sgu_ln_b,
             sgu_w=v_sgu_w, sgu_b=v_sgu_b, conv_w=v_conv_w, conv_b=v_conv_b, conv_ln_g=v_conv_ln_g, conv_ln_b=v_conv_ln_b,
             w_out=v_w_out, b_out=v_b_out, ln1_g=v_ln1_g, ln1_b=v_ln1_b, w_ff1=v_w_ff1, b_ff1=v_b_ff1, w_ff2=v_w_ff2,
             b_ff2=v_b_ff2, ln2_g=v_ln2_g, ln2_b=v_ln2_b)
    assert x.shape[0] == 1 and x.shape[2] == POOL_WIDTH + SGU_WIDTH + CONV_WIDTH, x.shape
    xs, target = x[0], loss_target[0]
    s_len, d = xs.shape
    n_layers = w_in.shape[0]
    dff = 4 * w_ff1.shape[2]
    conv_shard = conv_w.shape[2]

    cx, cy, cc = _place()
    me = 2 * cx + cy
    place = jnp.stack([cc, me]).astype(jnp.int32)

    def cast(key, after=None):
        return _into_slot(f"cast_{key[0]}{key[1]}", w[key[0]], key[1], place, BF16, after=after)

    conv_padded = jnp.pad(conv_w, ((0, 0), (0, CONV_ROWS - CONV_KERNEL), (0, 0))).reshape(1, n_layers * CONV_ROWS, conv_shard)
    slot, flights, token = {}, {}, None
    for name, keys in (("in0", [("w_in", 0)]), ("rest0", [("w_out", 0), "conv_w"]), ("ff1_0", [("w_ff1", 0)]),
                       ("ff2_0", [("w_ff2", 0)])):
        for k in keys:
            slot[k] = _into_slot("slot_conv_w", conv_padded, 0, place, F32, after=token) if k == "conv_w" else cast(k, token)
        flights[name] = _gather_start("gather_" + name, [slot[k] for k in keys])
        token = flights[name]["token"]
    slot.update({(n, l): cast((n, l), token) for n in SHARDED for l in range(1, n_layers)})
    xb0 = _rowwise("cast_x", lambda t: (t,), [xs], [BF16])[0]
    gathered = {("w_in", 0): _gather_finish("gather_in0", flights["in0"], slot[(SHARDED[-1], n_layers - 1)])[0]}
    conv = {}

    def layer_weights(l):
        conv_full, conv_rev = conv["full"], conv["rev"]
        return dict(w_pool=w_pool[l], pool_scale=pool_scale[l], sgu_ln_g=sgu_ln_g[l], sgu_ln_b=sgu_ln_b[l], sgu_w=sgu_w[l],
                    sgu_w_t=jnp.transpose(sgu_w[l], (0, 2, 1)),
                    sgu_bias_tile=jnp.broadcast_to(sgu_b[l][:, :, None], (SGU_HEADS, CHUNK, GROUP)),
                    conv_w_full=conv_full[l], conv_w_rev=conv_rev[l], conv_b=conv_b[l], conv_ln_g=conv_ln_g[l],
                    conv_ln_b=conv_ln_b[l])

    saved = []
    x_cur, xb_cur, prev_ln = xs, xb0, None
    for l in range(n_layers):
        p = _proj(f"proj{l}", xb_cur, gathered[("w_in", l)], b_in[l], after=token if l == 0 else None)
        if l == 0:
            gathered[("w_out", 0)], conv_buf = _gather_finish("gather_rest0", flights["rest0"], p)
            full = jnp.transpose(conv_buf.reshape(4, n_layers, CONV_ROWS, conv_shard), (1, 2, 0, 3))
            conv["full"] = full.reshape(n_layers, CONV_ROWS, CONV_WIDTH)
            conv["rev"] = jnp.pad(conv["full"][:, CONV_KERNEL - 1::-1], ((0, 0), (0, CONV_ROWS - CONV_KERNEL), (0, 0)))
        lw = layer_weights(l)
        w_out_full = gathered[("w_out", l)].reshape(d, d)
        mixed, cv = _mix_forward(f"mix_fwd{l}", p, lw)
        passing = _gather_pass_on("gather_ff1_0", flights["ff1_0"], mixed) if l == 0 else None
        r1, x1b = _mix_out(f"mix_out{l}", mixed, w_out_full, b_out[l], x_cur, ln1_g[l], ln1_b[l], prev_ln,
                           after=passing and passing["token"])
        token = None
        if l == 0:
            gathered[("w_ff1", 0)] = _exchange_wait(passing, x1b)[0]
            passing = _gather_pass_on("gather_ff2_0", flights["ff2_0"], x1b)
            token = passing["token"]
        nxt = [(n, l + 1) for n in SHARDED] if l + 1 < n_layers else None
        if nxt:
            in_flight = _gather_start(f"gather_layer{l + 1}", [slot[k] for k in nxt], after=token)
            token = in_flight["token"]
        hf, zr = _ff1(f"ff1_{l}", x1b, gathered[("w_ff1", l)], b_ff1[l], after=token)
        if l == 0:
            gathered[("w_ff2", 0)] = _exchange_wait(passing, hf)[0]
        w2_full = gathered[("w_ff2", l)].reshape(dff, d)
        fo = _ff2(f"ff2_{l}", hf, w2_full, b_ff2[l])
        sv = dict(lw=lw, xb_in=xb_cur, p=p, mixed=mixed, cv=cv, r1=r1, x1b=x1b, hf=hf, zr=zr, fo=fo,
                  w_out_full=w_out_full, w2_full=w2_full)
        if nxt:
            in_flight = _gather_pass_on(f"gather_layer{l + 1}", in_flight, fo)
            sv["r2"], xb_cur = _resid_ln2(f"ln2_{l}", r1, ln1_g[l], ln1_b[l], fo, ln2_g[l], ln2_b[l], after=in_flight["token"])
            x_cur, prev_ln = sv["r2"], (ln2_g[l], ln2_b[l])
            gathered.update(zip(nxt, _exchange_wait(in_flight, xb_cur)))
        saved.append(sv)

    grads = {n: [None] * n_layers for n in REPLICATED + ("conv_w",)}
    g_final, delta, new_m, new_v = {}, {}, {}, {}
    results = {n: None for n in SHARDED}
    dx_mm, dx_resid = None, None
    loss_tile = None
    pending = None

    jobs = []

    def finish_reduce(begun, after):
        names, layer, state = begun
        reduced = _reduce_end(state, after)
        jobs.extend((n, layer, g) for n, g in zip(names, reduced))
        return reduced

    def hosted(call, vmem_bytes=RIDER_VMEM_BYTES):
        taken = []

        def rider(steps):
            room = vmem_bytes
            for job in sorted(jobs, key=lambda j: -w[j[0]].shape[1] * w[j[0]].shape[2]):
                rows, cols = w[job[0]].shape[1:]
                cost = 2 * 8 * (rows // steps) * cols * 4
                if rows % steps == 0 and (rows // steps) % 8 == 0 and cost <= room:
                    room -= cost
                    jobs.remove(job)
                    taken.append(job)
            if not taken:
                return None
            return _join_riders([_adamw_blocks(layer, w[n], m[n], v[n], g, results[n], steps) for n, layer, g in taken])

        out = call(rider)
        if not taken:
            return out
        for i, (n, _, _) in enumerate(taken):
            results[n] = tuple(out[1][4 * i:4 * i + 4])
        return out[0]

    for l in reversed(range(n_layers)):
        sv = saved[l]
        if dx_mm is None:
            dr2, dr2b, dg2, db2, dbff2, loss_tile = _ln_backward(
                f"ln2_bwd{l}", None, ln2_g[l], last=(sv["r1"], ln1_g[l], ln1_b[l], sv["fo"], ln2_b[l], target))
        else:
            dr2, dr2b, dg2, db2, dbff2 = _ln_backward(f"ln2_bwd{l}", sv["r2"], ln2_g[l], dy=dx_mm, resid=dx_resid)
        dzb, dbff1 = hosted(lambda rider: _dff_hidden(f"ff2_bwd{l}", dr2b, sv["w2_full"], sv["zr"], rider=rider))
        dw = {"w_ff2": _matmul_tn(f"dw_ff2_{l}", sv["hf"], dr2b, 4, dff // 4, d, True),
              "w_ff1": _matmul_tn(f"dw_ff1_{l}", sv["x1b"], dzb, 4, d, dff // 4, False)}
        if pending is not None:
            names, layer, state = pending
            pending = (names, layer, _reduce_swap(state, dw["w_ff1"]))
        ff_red = _reduce_begin(f"grads_ff{l}", place, [(dw[n], True, BF16) for n in ("w_ff1", "w_ff2")])
        if pending is not None:
            finish_reduce(pending, ff_red["token"])
            pending = None
        dx1 = hosted(lambda rider: _dx_sharded(f"ff1_bwd{l}", dzb, gathered[("w_ff1", l)], after=ff_red["token"], rider=rider),
                     RIDER_VMEM_TIGHT)
        ff_red = _reduce_between_chips(ff_red, dx1)
        dr1, dr1b, dg1, db1, dbout = _ln_backward(f"ln1_bwd{l}", sv["r1"], ln1_g[l], dy=dx1, resid=dr2, after=ff_red["token"])
        dmixed = hosted(lambda rider: _dmixed(f"mix_out_bwd{l}", dr1b, sv["w_out_full"], rider=rider))
        dw["w_out"] = _matmul_tn(f"dw_out{l}", sv["mixed"], dr1b, 4, d // 4, d, True)
        (dp, dbin, dwp, dps, dslg, dslb, dws, dsb_tile, dcw, dcb, dclg, dclb) = _mix_backward(
            f"mix_bwd{l}", sv["p"], sv["cv"], dmixed, sv["lw"])
        dw["w_in"] = _matmul_tn(f"dw_in{l}", sv["xb_in"], dp, 4, d, IN_WIDTH // 4, False)
        for name, g in (("b_in", dbin), ("w_pool", dwp), ("pool_scale", dps), ("sgu_ln_g", dslg), ("sgu_ln_b", dslb),
                        ("sgu_w", dws), ("sgu_b", dsb_tile[:, :, 0]), ("conv_w", dcw[:CONV_KERNEL]), ("conv_b", dcb),
                        ("conv_ln_g", dclg), ("conv_ln_b", dclb), ("b_out", dbout), ("ln1_g", dg1), ("ln1_b", db1),
                        ("b_ff1", dbff1), ("b_ff2", dbff2), ("ln2_g", dg2), ("ln2_b", db2)):
            grads[name][l] = g.reshape(w[name].shape[1:]) if name != "conv_w" else g

        items = [(dw[n], True, BF16) for n in ("w_in", "w_out")]
        if l == 0:
            small_like = [jnp.stack(grads[n]) for n in REPLICATED + ("conv_w",)]
            items.append((_pack(small_like)[None], False, F32))
        mix_red = _reduce_begin(f"grads_mix{l}", place, items)
        ff_red = _reduce_swap(ff_red, mix_red["token"])
        mix_red = _reduce_between_chips(mix_red, ff_red["token"])
        finish_reduce((("w_ff1", "w_ff2"), l, ff_red), mix_red["token"])
        if l > 0:
            dx_mm = hosted(lambda rider: _dx_sharded(f"proj_bwd{l}", dp, gathered[("w_in", l)], after=mix_red["token"],
                                                     rider=rider))
            dx_resid = dr1
            pending = (("w_in", "w_out"), l, mix_red)
        else:
            grad_x = hosted(lambda rider: _dx_sharded(f"proj_bwd{l}", dp, gathered[("w_in", l)], dr1,
                                                      after=mix_red["token"], rider=rider))
            mix_red = _reduce_swap(mix_red, grad_x)
            reduced_small = finish_reduce((("w_in", "w_out"), l, mix_red), mix_red["token"])[-1]
            grad_x = grad_x[None]

    for n, layer, g in jobs:
        results[n] = _adamw_layer(f"adamw_{n}{layer}", layer, w[n], m[n], v[n], g, results[n])
    loss = lax.psum(loss_tile[0, 0], ("x", "y", "c"))
    for n in SHARDED:
        g_final[n], delta[n], new_m[n], new_v[n] = results[n]
    unpacked = _unpack(reduced_small, small_like)
    g_final.update(zip(REPLICATED, unpacked[:-1]))
    g_final["conv_w"] = lax.dynamic_slice_in_dim(unpacked[-1], me * conv_shard, conv_shard, axis=2)

    delta["conv_w"], new_m["conv_w"], new_v["conv_w"] = _adamw("adamw_conv_w", conv_w, g_final["conv_w"], m["conv_w"], v["conv_w"])
    packed = [_pack([t[n] for n in REPLICATED]) for t in (w, g_final, m, v)]
    like = [w[n] for n in REPLICATED]
    for res, packed_out in zip((delta, new_m, new_v), _adamw("adamw_replicated", *packed)):
        res.update(zip(REPLICATED, _unpack(packed_out, like)))

    return (loss, grad_x, *[g_final[n] for n in WEIGHTS], *[delta[n] for n in WEIGHTS],
            *[new_m[n] for n in WEIGHTS], *[new_v[n] for n in WEIGHTS])
```

```python
import functools
import math

import jax
import jax.numpy as jnp
from jax import lax
from jax.experimental import pallas as pl
from jax.experimental.pallas import tpu as pltpu

F32 = jnp.float32
BF16 = jnp.bfloat16
MESH = pl.DeviceIdType.MESH

DEPTH = 2
POOL_WINDOWS = (2, 4, 8, 16)
POOL_WIDTH = 512
GROUP = 128
SGU_WIDTH = 768
SGU_HEADS = 6
CHUNK = 128
CONV_WIDTH = 768
CONV_KERNEL = 31
CONV_ROWS = 32
HALO = 32
COL_B = POOL_WIDTH
COL_C = POOL_WIDTH + 2 * SGU_WIDTH
IN_WIDTH = COL_C + 2 * CONV_WIDTH
ALPHA = (2 * DEPTH) ** 0.25
LN_EPS = 1e-5
ADAM_LR = 0.001
ADAM_B1 = 0.9
ADAM_B2 = 0.999
ADAM_EPS = 1e-08
ADAM_WD = 0.01
ADAM_STEP = 10
GELU_C = math.sqrt(2.0 / math.pi)
GELU_A = 0.044715

V7X_VMEM_BYTES = 64 * 2 ** 20
VMEM_LIMIT = 56 * 2 ** 20

BM_MM = 1024
BN_MM = 1024
BM_LN = 512
BK_MM = 1024
BK_WIDE = 2048
BS_TN = 2048
TN_OUT_BYTES = 8 * 2 ** 20
TS_MIX = 256
RB_CONV = 64
CONV_LIVE_BLOCKS = 2
BR_EW = 512
EW_BLOCK_ELEMS = 2 ** 18


def _tile(n, pref):
    t = min(n, pref)
    assert n % t == 0, (n, pref)
    return t


def _params(n_grid):
    return pltpu.CompilerParams(dimension_semantics=("arbitrary",) * n_grid, vmem_limit_bytes=VMEM_LIMIT)


def _sigmoid(x):
    return 1.0 / (1.0 + jnp.exp(-x))


def _gelu(x):
    return 0.5 * x * (1.0 + jnp.tanh(GELU_C * (x + GELU_A * x * x * x)))


def _gelu_with_grad(x):
    t = jnp.tanh(GELU_C * (x + GELU_A * x * x * x))
    half = 0.5 * (1.0 + t)
    return x * half, half + 0.5 * x * (1.0 - t * t) * GELU_C * (1.0 + 3.0 * GELU_A * x * x)


def _ln_stats(r):
    mu = jnp.mean(r, axis=-1, keepdims=True)
    xc = r - mu
    var = jnp.mean(xc * xc, axis=-1, keepdims=True)
    rstd = lax.rsqrt(var + LN_EPS)
    return xc * rstd, rstd


def _ln_bwd(dy, xhat, rstd, g):
    dxh = dy * g
    m1 = jnp.mean(dxh, axis=-1, keepdims=True)
    m2 = jnp.mean(dxh * xhat, axis=-1, keepdims=True)
    return rstd * (dxh - m1 - xhat * m2)


def _colsum(x):
    return jnp.sum(x, axis=0, keepdims=True)


def _accumulate(ref, val, first):
    @pl.when(first)
    def _():
        ref[...] = val

    @pl.when(jnp.logical_not(first))
    def _():
        ref[...] += val


def _matmul(name, grid, a, a_spec, b, b_spec, *, nt, extras, outs, epilogue, after=None, rider=None):
    ni, nj, nk = grid
    if after is not None:
        extras = list(extras) + [(after, pl.BlockSpec(memory_space=pl.ANY))]
    ne, no = len(extras), len(outs)
    dims = (((1,), (1,)), ((), ())) if nt else (((1,), (0,)), ((), ()))
    r_fn, r_ins, r_outs, r_aliases, r_blocks = rider or (None, [], [], {}, 0)
    assert r_blocks <= ni * nj * nk
    n_ri = len(r_ins)
    n_data = sum(1 for _, bs, _ in r_ins if bs is not None)
    step = lambda i, j, k: (i * nj + j) * nk + k

    def body(*refs):
        a_ref, b_ref = refs[0], refs[1]
        ex = refs[2:2 + ne]
        out_refs = refs[2 + ne + n_ri:2 + ne + n_ri + no]
        ids = (pl.program_id(0), pl.program_id(1), pl.program_id(2))
        part = lax.dot_general(a_ref[...], b_ref[...], dims, preferred_element_type=F32)
        if nk == 1:
            epilogue(part, ex, out_refs, ids)
        else:
            @pl.when(ids[2] == 0)
            def _():
                epilogue(part, ex, out_refs, ids)

            @pl.when(ids[2] > 0)
            def _():
                out_refs[0][...] += part
        if rider is not None:
            @pl.when(step(*ids) < r_blocks)
            def _():
                res = r_fn(*[r[...] for r in refs[2 + ne:2 + ne + n_data]])
                for o_ref, val in zip(refs[2 + ne + n_ri + no:], res):
                    o_ref[...] = val

    def rider_spec(bs, f):
        if bs is None:
            return pl.BlockSpec(memory_space=pl.ANY)
        return pl.BlockSpec(bs, lambda i, j, k: f(jnp.minimum(step(i, j, k), r_blocks - 1)))

    return pl.pallas_call(
        body,
        name=name,
        grid=grid,
        in_specs=[a_spec, b_spec] + [s for _, s in extras] + [rider_spec(bs, f) for _, bs, f in r_ins],
        out_specs=[s for _, s in outs] + [rider_spec(bs, f) for _, bs, f in r_outs],
        out_shape=[o for o, _ in outs] + [o for o, _, _ in r_outs],
        input_output_aliases={2 + ne + i: no + o for i, o in r_aliases.items()},
        compiler_params=_params(3),
    )(a, b, *[e for e, _ in extras], *[arr for arr, _, _ in r_ins])


def _matmul_tn(name, a, b, n_shards, shard_rows, shard_cols, row_sharded):
    s_len, ka = a.shape
    n = b.shape[1]
    assert (n_shards * shard_rows, shard_cols) == (ka, n) if row_sharded else (shard_rows, n_shards * shard_cols) == (ka, n)
    bs = _tile(s_len, BS_TN)
    bka = _tile(shard_rows, 2048)
    bn = next((t for t in (2 * BN_MM, BN_MM) if shard_cols % t == 0 and bka * t * 4 <= TN_OUT_BYTES), shard_cols)
    ni, nj, ns = ka // bka, n // bn, s_len // bs
    per_shard_i = shard_rows // bka
    per_shard_j = shard_cols // bn

    if row_sharded:
        out_map = lambda i, j, s: (i // per_shard_i, i % per_shard_i, j)
    else:
        out_map = lambda i, j, s: (j // per_shard_j, i, j % per_shard_j)

    def body(a_ref, b_ref, o_ref):
        s = pl.program_id(2)
        part = lax.dot_general(a_ref[...], b_ref[...], (((0,), (0,)), ((), ())), preferred_element_type=F32)
        _accumulate(o_ref, part, s == 0)

    return pl.pallas_call(
        body,
        name=name,
        grid=(ni, nj, ns),
        in_specs=[pl.BlockSpec((bs, bka), lambda i, j, s: (s, i)), pl.BlockSpec((bs, bn), lambda i, j, s: (s, j))],
        out_specs=pl.BlockSpec((None, bka, bn), out_map),
        out_shape=jax.ShapeDtypeStruct((n_shards, shard_rows, shard_cols), F32),
        compiler_params=_params(3),
    )(a, b)


def _row(v):
    return v.reshape(1, -1)


def _proj(name, xb, w_g, b_in, after=None):
    s_len, d = xb.shape
    ncs = w_g.shape[2]
    bm = _tile(s_len, BM_MM)

    def epilogue(acc, ex, outs, ids):
        outs[0][...] = acc + ex[0][...]

    return _matmul(
        name, (s_len // bm, 4, 1),
        xb, pl.BlockSpec((bm, d), lambda i, j, k: (i, 0)),
        w_g, pl.BlockSpec((None, d, ncs), lambda i, j, k: (j, 0, 0)),
        nt=False,
        extras=[(_row(b_in), pl.BlockSpec((1, ncs), lambda i, j, k: (0, j)))],
        outs=[(jax.ShapeDtypeStruct((s_len, 4 * ncs), F32), pl.BlockSpec((bm, ncs), lambda i, j, k: (i, j)))],
        epilogue=epilogue, after=after,
    )[0]


def _mix_out(name, mixed, w_out_full, b_out, x0, g1, b1, prev_ln=None, after=None):
    s_len, d = mixed.shape
    bm = _tile(s_len, BM_LN // 2)
    row = pl.BlockSpec((1, d), lambda i, j, k: (0, 0))
    blk = pl.BlockSpec((bm, d), lambda i, j, k: (i, 0))

    def epilogue(acc, ex, outs, ids):
        x_in = ex[1][...]
        if prev_ln is not None:
            xhat_in, _ = _ln_stats(x_in)
            x_in = xhat_in * ex[4][...] + ex[5][...]
        r1 = ALPHA * x_in + (acc + ex[0][...])
        outs[0][...] = r1
        xhat, _ = _ln_stats(r1)
        outs[1][...] = (xhat * ex[2][...] + ex[3][...]).astype(BF16)

    return _matmul(
        name, (s_len // bm, 1, 1),
        mixed, blk,
        w_out_full, pl.BlockSpec((d, d), lambda i, j, k: (0, 0)),
        nt=False,
        extras=[(_row(b_out), row), (x0, blk), (_row(g1), row), (_row(b1), row)] + [(_row(t), row) for t in prev_ln or ()],
        outs=[(jax.ShapeDtypeStruct((s_len, d), F32), blk), (jax.ShapeDtypeStruct((s_len, d), BF16), blk)],
        epilogue=epilogue, after=after,
    )


def _ff1(name, x1b, w_g, b_ff1, after=None):
    s_len, d = x1b.shape
    ncs = w_g.shape[2]
    bm = _tile(s_len, BM_MM)
    bn = _tile(ncs, BN_MM)
    per = ncs // bn
    blk = pl.BlockSpec((bm, bn), lambda i, j, k: (i, j))

    def epilogue(acc, ex, outs, ids):
        zr = jnp.maximum(acc + ex[0][...], 0.0)
        outs[0][...] = (zr * zr).astype(BF16)
        outs[1][...] = zr.astype(BF16)

    shape = jax.ShapeDtypeStruct((s_len, 4 * ncs), BF16)
    return _matmul(
        name, (s_len // bm, 4 * per, 1),
        x1b, pl.BlockSpec((bm, d), lambda i, j, k: (i, 0)),
        w_g, pl.BlockSpec((None, d, bn), lambda i, j, k: (j // per, 0, j % per)),
        nt=False,
        extras=[(_row(b_ff1), pl.BlockSpec((1, bn), lambda i, j, k: (0, j)))],
        outs=[(shape, blk), (shape, blk)],
        epilogue=epilogue, after=after,
    )


def _ff2(name, hf, w2_full, b_ff2):
    s_len, dff = hf.shape
    d = w2_full.shape[1]
    bm = _tile(s_len, BM_MM)
    bk = _tile(dff, BK_WIDE)
    blk = pl.BlockSpec((bm, d), lambda i, j, k: (i, 0))

    def epilogue(acc, ex, outs, ids):
        outs[0][...] = acc + ex[0][...]

    return _matmul(
        name, (s_len // bm, 1, dff // bk),
        hf, pl.BlockSpec((bm, bk), lambda i, j, k: (i, k)),
        w2_full, pl.BlockSpec((bk, d), lambda i, j, k: (k, 0)),
        nt=False,
        extras=[(_row(b_ff2), pl.BlockSpec((1, d), lambda i, j, k: (0, 0)))],
        outs=[(jax.ShapeDtypeStruct((s_len, d), F32), blk)],
        epilogue=epilogue,
    )[0]


def _resid_ln2(name, r1, g1, b1, fo, g2, b2, after=None):
    s_len, d = r1.shape
    bm = _tile(s_len, BM_LN // 2)
    blk = pl.BlockSpec((bm, d), lambda i: (i, 0))
    row = pl.BlockSpec((1, d), lambda i: (0, 0))
    order = [] if after is None else [after]

    def body(*refs):
        r1_ref, g1_ref, b1_ref, fo_ref, g2_ref, b2_ref, r2_ref, x2b_ref = refs[len(order):]
        xhat1, _ = _ln_stats(r1_ref[...])
        r2 = ALPHA * (xhat1 * g1_ref[...] + b1_ref[...]) + fo_ref[...]
        r2_ref[...] = r2
        xhat2, _ = _ln_stats(r2)
        x2b_ref[...] = (xhat2 * g2_ref[...] + b2_ref[...]).astype(BF16)

    return pl.pallas_call(
        body, name=name, grid=(s_len // bm,),
        in_specs=[pl.BlockSpec(memory_space=pl.ANY)] * len(order) + [blk, row, row, blk, row, row], out_specs=[blk, blk],
        out_shape=[jax.ShapeDtypeStruct((s_len, d), F32), jax.ShapeDtypeStruct((s_len, d), BF16)],
        compiler_params=_params(1))(*order, r1, _row(g1), _row(b1), fo, _row(g2), _row(b2))


def _dff_hidden(name, dr2b, w2_full, zr, rider=None):
    s_len, d = dr2b.shape
    dff = w2_full.shape[0]
    bm = _tile(s_len, BM_MM)
    bn = _tile(dff, BN_MM)

    def epilogue(acc, ex, outs, ids):
        dz = acc * (2.0 * ex[0][...].astype(F32))
        outs[0][...] = dz.astype(BF16)
        _accumulate(outs[1], _colsum(dz), ids[1] == 0)

    grid = (dff // bn, s_len // bm, 1)
    rider = rider and rider(math.prod(grid))
    res = _matmul(
        name, grid,
        dr2b, pl.BlockSpec((bm, d), lambda j, i, k: (i, 0)),
        w2_full, pl.BlockSpec((bn, d), lambda j, i, k: (j, 0)),
        nt=True,
        extras=[(zr, pl.BlockSpec((bm, bn), lambda j, i, k: (i, j)))],
        outs=[(jax.ShapeDtypeStruct((s_len, dff), BF16), pl.BlockSpec((bm, bn), lambda j, i, k: (i, j))),
              (jax.ShapeDtypeStruct((1, dff), F32), pl.BlockSpec((1, bn), lambda j, i, k: (0, j)))],
        epilogue=epilogue, rider=rider,
    )
    return tuple(res) if rider is None else (tuple(res[:2]), res[2:])


def _dx_sharded(name, dyb, w_g, resid=None, after=None, rider=None):
    s_len = dyb.shape[0]
    d, ncs = w_g.shape[1], w_g.shape[2]
    bm = _tile(s_len, BM_MM if resid is None else BM_LN)
    bk = _tile(ncs, BK_WIDE) if ncs % BK_MM == 0 else ncs
    per = ncs // bk
    blk = pl.BlockSpec((bm, d), lambda i, j, k: (i, 0))

    def epilogue(acc, ex, outs, ids):
        outs[0][...] = acc if resid is None else acc + ALPHA * ex[0][...]

    rider = rider and rider(s_len // bm * 4 * per)
    res = _matmul(
        name, (s_len // bm, 1, 4 * per),
        dyb, pl.BlockSpec((bm, bk), lambda i, j, k: (i, k)),
        w_g, pl.BlockSpec((None, d, bk), lambda i, j, k: (k // per, 0, k % per)),
        nt=True,
        extras=[] if resid is None else [(resid, blk)],
        outs=[(jax.ShapeDtypeStruct((s_len, d), F32), blk)],
        epilogue=epilogue, after=after, rider=rider,
    )
    return res[0] if rider is None else (res[0], res[1:])


def _dmixed(name, dr1b, w_out_full, rider=None):
    s_len, d = dr1b.shape
    bm = _tile(s_len, BM_LN)
    blk = pl.BlockSpec((bm, d), lambda i, j, k: (i, 0))

    def epilogue(acc, ex, outs, ids):
        outs[0][...] = acc

    rider = rider and rider(s_len // bm)
    res = _matmul(
        name, (s_len // bm, 1, 1),
        dr1b, blk,
        w_out_full, pl.BlockSpec((d, d), lambda i, j, k: (0, 0)),
        nt=True, extras=[],
        outs=[(jax.ShapeDtypeStruct((s_len, d), F32), blk)],
        epilogue=epilogue, rider=rider,
    )
    return res[0] if rider is None else (res[0], res[1:])


def _ln_backward(name, r, g, *, dy=None, resid=None, last=None, after=None):
    from_loss = last is not None
    s_len, d = last[0].shape if from_loss else r.shape
    bm = _tile(s_len, BM_LN // 2)
    blk = pl.BlockSpec((bm, d), lambda i: (i, 0))
    row = pl.BlockSpec((1, d), lambda i: (0, 0))

    def body(*refs):
        i = pl.program_id(0)
        if after is not None:
            refs = refs[1:]
        if from_loss:
            r1_ref, fo_ref, t_ref, g1_ref, b1_ref, b_ref, g_ref, dr_ref, drb_ref, dg_ref, db_ref, dbias_ref, loss_ref = refs
            xhat1, _ = _ln_stats(r1_ref[...])
            rv = ALPHA * (xhat1 * g1_ref[...] + b1_ref[...]) + fo_ref[...]
            xhat, rstd = _ln_stats(rv)
            diff = (xhat * g_ref[...] + b_ref[...]) - t_ref[...]
            dyv = diff * (1.0 / d)
            part = 0.5 * jnp.sum(jnp.sum(diff * diff, axis=1, keepdims=True) * (1.0 / d), axis=0, keepdims=True)
            _accumulate(loss_ref, jnp.broadcast_to(part, loss_ref.shape), i == 0)
        else:
            if resid is not None:
                dy_ref, res_ref, r_ref, g_ref, dr_ref, drb_ref, dg_ref, db_ref, dbias_ref = refs
                dyv = dy_ref[...] + ALPHA * res_ref[...]
            else:
                dy_ref, r_ref, g_ref, dr_ref, drb_ref, dg_ref, db_ref, dbias_ref = refs
                dyv = dy_ref[...]
            xhat, rstd = _ln_stats(r_ref[...])
        dr = _ln_bwd(dyv, xhat, rstd, g_ref[...])
        dr_ref[...] = dr
        drb_ref[...] = dr.astype(BF16)
        _accumulate(dg_ref, _colsum(dyv * xhat), i == 0)
        _accumulate(db_ref, _colsum(dyv), i == 0)
        _accumulate(dbias_ref, _colsum(dr), i == 0)

    if from_loss:
        r1, g1, b1, fo, b, target = last
        ins = [r1, fo, target, _row(g1), _row(b1), _row(b), _row(g)]
        in_specs = [blk] * 3 + [row] * 4
    else:
        lead = [dy] if resid is None else [dy, resid]
        ins = lead + [r, _row(g)]
        in_specs = [blk] * len(lead) + [blk, row]
    if after is not None:
        ins, in_specs = [after] + ins, [pl.BlockSpec(memory_space=pl.ANY)] + in_specs
    out_shape = [jax.ShapeDtypeStruct((s_len, d), F32), jax.ShapeDtypeStruct((s_len, d), BF16)] + \
                [jax.ShapeDtypeStruct((1, d), F32)] * 3
    out_specs = [blk, blk, row, row, row]
    if from_loss:
        out_shape.append(jax.ShapeDtypeStruct((8, 128), F32))
        out_specs.append(pl.BlockSpec((8, 128), lambda i: (0, 0)))
    return pl.pallas_call(body, name=name, grid=(s_len // bm,), in_specs=in_specs, out_specs=out_specs,
                          out_shape=out_shape, compiler_params=_params(1))(*ins)


def _rowwise(name, fn, ins, out_dtypes, rows_pref=BR_EW):
    r, c = ins[0].shape
    br = _tile(r, rows_pref)
    blk = pl.BlockSpec((br, c), lambda i: (i, 0))

    def body(*refs):
        res = fn(*[ref[...] for ref in refs[:len(ins)]])
        for o_ref, v in zip(refs[len(ins):], res):
            o_ref[...] = v.astype(o_ref.dtype)

    return pl.pallas_call(body, name=name, grid=(r // br,), in_specs=[blk] * len(ins),
                          out_specs=[blk] * len(out_dtypes),
                          out_shape=[jax.ShapeDtypeStruct((r, c), dt) for dt in out_dtypes],
                          compiler_params=_params(1))(*ins)


def _adamw_math(w, g, m, v):
    m = ADAM_B1 * m + (1.0 - ADAM_B1) * g
    v = ADAM_B2 * v + (1.0 - ADAM_B2) * (g * g)
    m_hat = m / (1.0 - ADAM_B1 ** ADAM_STEP)
    v_hat = v / (1.0 - ADAM_B2 ** ADAM_STEP)
    delta = -ADAM_LR * (m_hat / (jnp.sqrt(v_hat) + ADAM_EPS) + ADAM_WD * w)
    return delta, m, v


def _adamw(name, w, g, m, v):
    shape = w.shape
    c = shape[-1]
    flat = [a.reshape(-1, c) for a in (w, g, m, v)]
    rows = flat[0].shape[0]
    pref = max(8, 2 ** int(math.log2(EW_BLOCK_ELEMS // c)))
    res = _rowwise(name, _adamw_math, flat, [F32, F32, F32], rows_pref=pref if rows % pref == 0 else rows)
    return tuple(a.reshape(shape) for a in res)


def _pool_means(ext_ref, ts, tile_index):
    t_glob = tile_index * ts + lax.broadcasted_iota(jnp.int32, (ts, GROUP), 0)
    qs = []
    for g, win in enumerate(POOL_WINDOWS):
        cols = pl.ds(g * GROUP, GROUP)
        cur = ext_ref[pl.ds(16, ts), cols]
        acc = cur
        for j in range(1, win):
            acc = acc + ext_ref[pl.ds(16 - j, ts), cols]
        cnt = jnp.minimum(t_glob + 1, win).astype(F32)
        qs.append(acc / cnt - cur)
    return qs


def _masked_sgu_w(w_ref, h):
    r = lax.broadcasted_iota(jnp.int32, (CHUNK, CHUNK), 0)
    c = lax.broadcasted_iota(jnp.int32, (CHUNK, CHUNK), 1)
    return jnp.where(r >= c, w_ref[h], 0.0)


SUBLANES = 8


def _fill_shifted(src_ref, shifted_ref, ts):
    rows = ts + HALO - SUBLANES
    for b in range(1, SUBLANES):
        shifted_ref[b - 1] = src_ref[pl.ds(b, rows), :]


def _shifted_rows(src_ref, shifted_ref, offset, rows, cols):
    a, b = divmod(offset, SUBLANES)
    if b == 0:
        return src_ref[pl.ds(offset, rows), cols]
    return shifted_ref[b - 1, pl.ds(a * SUBLANES, rows), cols]


def _conv_taps(acc_init, w_ref, src_ref, shifted_ref, ts, base, emit):
    nrb = ts // RB_CONV
    for cg in range(CONV_WIDTH // GROUP):
        cols = pl.ds(cg * GROUP, GROUP)
        for rb0 in range(0, nrb, CONV_LIVE_BLOCKS):
            group = range(rb0, min(rb0 + CONV_LIVE_BLOCKS, nrb))
            accs = {rb: acc_init(cg) for rb in group}
            for k in range(CONV_KERNEL):
                wk = jnp.broadcast_to(w_ref[pl.ds(k, 1), cols], (RB_CONV, GROUP))
                for rb in group:
                    accs[rb] = accs[rb] + wk * _shifted_rows(src_ref, shifted_ref, rb * RB_CONV + base(k), RB_CONV, cols)
            for rb in group:
                emit(rb * RB_CONV, cg * GROUP, accs[rb])


def _mix_forward(name, p, lw):
    s_len = p.shape[0]
    ts = _tile(s_len, TS_MIX)
    per_halo = ts // HALO
    d = POOL_WIDTH + SGU_WIDTH + CONV_WIDTH

    def body(p_ref, ph_ref, wp_ref, ps_ref, slg_ref, slb_ref, ws_ref, sb_ref, cw_ref, cb_ref, clg_ref, clb_ref,
             mixed_ref, cv_ref, pool_ext, hh_ext, shifted):
        i = pl.program_id(0)
        keep = (i > 0).astype(F32)
        pool_ext[pl.ds(0, 16), :] = ph_ref[pl.ds(16, 16), pl.ds(0, POOL_WIDTH)] * keep
        pool_ext[pl.ds(16, ts), :] = p_ref[:, pl.ds(0, POOL_WIDTH)]
        qs = _pool_means(pool_ext, ts, i)
        for g in range(len(POOL_WINDOWS)):
            cols = pl.ds(g * GROUP, GROUP)
            e = jnp.dot(qs[g].astype(BF16), wp_ref[g].astype(BF16), preferred_element_type=F32)
            mixed_ref[:, cols] = (e * ps_ref[:, cols]).astype(BF16)
        uv = _gelu(p_ref[:, pl.ds(COL_B, 2 * SGU_WIDTH)])
        u = uv[:, :SGU_WIDTH]
        vhat, _ = _ln_stats(uv[:, SGU_WIDTH:])
        vn = (vhat * slg_ref[...] + slb_ref[...]).astype(BF16)
        for h in range(SGU_HEADS):
            wm = _masked_sgu_w(ws_ref, h).astype(BF16)
            for n in range(ts // CHUNK):
                rows = slice(n * CHUNK, (n + 1) * CHUNK)
                cols = slice(h * GROUP, (h + 1) * GROUP)
                mx = jnp.dot(wm, vn[rows, cols], preferred_element_type=F32) + sb_ref[h]
                mixed_ref[pl.ds(n * CHUNK, CHUNK), pl.ds(POOL_WIDTH + h * GROUP, GROUP)] = (u[rows, cols] * mx).astype(BF16)
        hh_ext[pl.ds(0, HALO), :] = (ph_ref[:, pl.ds(COL_C, CONV_WIDTH)]
                                     * _sigmoid(ph_ref[:, pl.ds(COL_C + CONV_WIDTH, CONV_WIDTH)])) * keep
        hh_ext[pl.ds(HALO, ts), :] = p_ref[:, pl.ds(COL_C, CONV_WIDTH)] * _sigmoid(p_ref[:, pl.ds(COL_C + CONV_WIDTH, CONV_WIDTH)])
        init = lambda cg: jnp.broadcast_to(cb_ref[:, pl.ds(cg * GROUP, GROUP)], (RB_CONV, GROUP))
        _fill_shifted(hh_ext, shifted, ts)
        def store_cv(r0, c0, blk):
            cv_ref[pl.ds(r0, RB_CONV), pl.ds(c0, GROUP)] = blk

        _conv_taps(init, cw_ref, hh_ext, shifted, ts, lambda k: k + HALO - (CONV_KERNEL - 1), store_cv)
        cvhat, _ = _ln_stats(cv_ref[...])
        cn = cvhat * clg_ref[...] + clb_ref[...]
        mixed_ref[:, pl.ds(POOL_WIDTH + SGU_WIDTH, CONV_WIDTH)] = (cn * _sigmoid(cn)).astype(BF16)

    full = lambda a: pl.BlockSpec(a.shape, lambda i: (0,) * a.ndim)
    weights = [lw["w_pool"], _row(lw["pool_scale"]), _row(lw["sgu_ln_g"]), _row(lw["sgu_ln_b"]), lw["sgu_w"],
               lw["sgu_bias_tile"], lw["conv_w_full"], _row(lw["conv_b"]), _row(lw["conv_ln_g"]), _row(lw["conv_ln_b"])]
    return pl.pallas_call(
        body, name=name, grid=(s_len // ts,),
        in_specs=[pl.BlockSpec((ts, IN_WIDTH), lambda i: (i, 0)),
                  pl.BlockSpec((HALO, IN_WIDTH), lambda i: (jnp.maximum(i * per_halo - 1, 0), 0))] + [full(a) for a in weights],
        out_specs=[pl.BlockSpec((ts, d), lambda i: (i, 0)), pl.BlockSpec((ts, CONV_WIDTH), lambda i: (i, 0))],
        out_shape=[jax.ShapeDtypeStruct((s_len, d), BF16), jax.ShapeDtypeStruct((s_len, CONV_WIDTH), F32)],
        scratch_shapes=[pltpu.VMEM((16 + ts, POOL_WIDTH), F32), pltpu.VMEM((HALO + ts, CONV_WIDTH), F32),
                        pltpu.VMEM((SUBLANES - 1, ts + HALO - SUBLANES, CONV_WIDTH), F32)],
        compiler_params=_params(1),
    )(p, p, *weights)


def _mix_backward(name, p, cv, dmixed, lw):
    s_len = p.shape[0]
    ts = _tile(s_len, TS_MIX)
    nt = s_len // ts
    per_halo = ts // HALO
    d = POOL_WIDTH + SGU_WIDTH + CONV_WIDTH
    nch = ts // CHUNK
    col_yc = POOL_WIDTH + SGU_WIDTH

    def body(p_ref, ph_ref, cv_ref, cvn_ref, dm_ref, dmn_ref, wp_ref, ps_ref, slg_ref, slb_ref, ws_ref, wst_ref, sb_ref,
             cw_ref, cwr_ref, clg_ref, clb_ref,
             dp_ref, dbin_ref, dwp_ref, dps_ref, dslg_ref, dslb_ref, dws_ref, dsb_ref, dcw_ref, dcb_ref, dclg_ref, dclb_ref,
             pool_ext, dq_ext, hh_ext, dcv_ext, dcw_acc, shifted):
        i = pl.program_id(0)
        first = i == 0
        keep_prev = (i > 0).astype(F32)
        keep_next = (i < nt - 1).astype(F32)

        pool_ext[pl.ds(0, 16), :] = ph_ref[pl.ds(16, 16), pl.ds(0, POOL_WIDTH)] * keep_prev
        pool_ext[pl.ds(16, ts), :] = p_ref[:, pl.ds(0, POOL_WIDTH)]
        qs = _pool_means(pool_ext, ts, i)
        t_ext = i * ts + lax.broadcasted_iota(jnp.int32, (ts + HALO, GROUP), 0)
        for g, win in enumerate(POOL_WINDOWS):
            cols = pl.ds(g * GROUP, GROUP)
            wpb = wp_ref[g].astype(BF16)
            qb = qs[g].astype(BF16)
            de = dm_ref[:, cols] * ps_ref[:, cols]
            e = jnp.dot(qb, wpb, preferred_element_type=F32)
            _accumulate(dps_ref.at[:, cols], _colsum(dm_ref[:, cols] * e), first)
            deb = de.astype(BF16)
            _accumulate(dwp_ref.at[g], lax.dot_general(qb, deb, (((0,), (0,)), ((), ())), preferred_element_type=F32), first)
            de_next = (dmn_ref[:, cols] * ps_ref[:, cols] * keep_next).astype(BF16)
            de_all = jnp.concatenate([deb, de_next], axis=0)
            dq = lax.dot_general(de_all, wpb, (((1,), (1,)), ((), ())), preferred_element_type=F32)
            inv = 1.0 / jnp.minimum(t_ext + 1, win).astype(F32)
            dq_ext[:, cols] = dq * inv
            acc = dq_ext[pl.ds(0, ts), cols]
            for j in range(1, win):
                acc = acc + dq_ext[pl.ds(j, ts), cols]
            dpa = acc - dq[:ts]
            dp_ref[:, cols] = dpa.astype(BF16)
            _accumulate(dbin_ref.at[:, cols], _colsum(dpa), first)

        wms = [_masked_sgu_w(ws_ref, h).astype(BF16) for h in range(SGU_HEADS)]
        wmts = [_masked_sgu_w_t(wst_ref, h).astype(BF16) for h in range(SGU_HEADS)]
        dws = [jnp.zeros((CHUNK, CHUNK), F32) for _ in range(SGU_HEADS)]
        dsb = [jnp.zeros((CHUNK, GROUP), F32) for _ in range(SGU_HEADS)]
        dslg_sum = jnp.zeros((1, SGU_WIDTH), F32)
        dslb_sum = jnp.zeros((1, SGU_WIDTH), F32)
        dbin_sum = jnp.zeros((1, 2 * SGU_WIDTH), F32)
        for n in range(nch):
            rows = pl.ds(n * CHUNK, CHUNK)
            pb = p_ref[rows, pl.ds(COL_B, 2 * SGU_WIDTH)]
            uv, duv = _gelu_with_grad(pb)
            u = uv[:, :SGU_WIDTH]
            vhat, vrstd = _ln_stats(uv[:, SGU_WIDTH:])
            vn = (vhat * slg_ref[...] + slb_ref[...]).astype(BF16)
            dyb = dm_ref[rows, pl.ds(POOL_WIDTH, SGU_WIDTH)]
            dmix = dyb * u
            dmixb = dmix.astype(BF16)
            du_parts, dvn_parts = [], []
            for h in range(SGU_HEADS):
                cols = slice(h * GROUP, (h + 1) * GROUP)
                mx = jnp.dot(wms[h], vn[:, cols], preferred_element_type=F32) + sb_ref[h]
                du_parts.append(dyb[:, cols] * mx)
                dws[h] = dws[h] + lax.dot_general(dmixb[:, cols], vn[:, cols], (((1,), (1,)), ((), ())),
                                                  preferred_element_type=F32)
                dsb[h] = dsb[h] + dmix[:, cols]
                dvn_parts.append(jnp.dot(wmts[h], dmixb[:, cols], preferred_element_type=F32))
            dvn = jnp.concatenate(dvn_parts, axis=1)
            dslg_sum = dslg_sum + _colsum(dvn * vhat)
            dslb_sum = dslb_sum + _colsum(dvn)
            dv = _ln_bwd(dvn, vhat, vrstd, slg_ref[...])
            dpb = jnp.concatenate(du_parts + [dv], axis=1) * duv
            dp_ref[rows, pl.ds(COL_B, 2 * SGU_WIDTH)] = dpb.astype(BF16)
            dbin_sum = dbin_sum + _colsum(dpb)
        r = lax.broadcasted_iota(jnp.int32, (CHUNK, CHUNK), 0)
        c = lax.broadcasted_iota(jnp.int32, (CHUNK, CHUNK), 1)
        for h in range(SGU_HEADS):
            _accumulate(dws_ref.at[h], jnp.where(r >= c, dws[h], 0.0), first)
            _accumulate(dsb_ref.at[h], jnp.broadcast_to(jnp.sum(dsb[h], axis=1, keepdims=True), (CHUNK, GROUP)), first)
        _accumulate(dslg_ref, dslg_sum, first)
        _accumulate(dslb_ref, dslb_sum, first)
        _accumulate(dbin_ref.at[:, pl.ds(COL_B, 2 * SGU_WIDTH)], dbin_sum, first)

        a_main = p_ref[:, pl.ds(COL_C, CONV_WIDTH)]
        sg_main = _sigmoid(p_ref[:, pl.ds(COL_C + CONV_WIDTH, CONV_WIDTH)])
        hh_ext[pl.ds(HALO, ts), :] = a_main * sg_main

        def conv_ln_backward(cv_v, dyc_v):
            cvhat, crstd = _ln_stats(cv_v)
            cn = cvhat * clg_ref[...] + clb_ref[...]
            s = _sigmoid(cn)
            dcn = dyc_v * (s * (1.0 + cn * (1.0 - s)))
            return _ln_bwd(dcn, cvhat, crstd, clg_ref[...]), dcn, cvhat

        dcv, dcn, cvhat = conv_ln_backward(cv_ref[...], dm_ref[:, pl.ds(col_yc, CONV_WIDTH)])
        _accumulate(dclg_ref, _colsum(dcn * cvhat), first)
        _accumulate(dclb_ref, _colsum(dcn), first)
        _accumulate(dcb_ref, _colsum(dcv), first)
        dcv_next, _, _ = conv_ln_backward(cvn_ref[...], dmn_ref[:, pl.ds(col_yc, CONV_WIDTH)])
        dcv_ext[pl.ds(0, ts), :] = dcv
        dcv_ext[pl.ds(ts, HALO), :] = dcv_next * keep_next

        @pl.when(first)
        def _():
            dcw_acc[...] = jnp.zeros_like(dcw_acc)

        nrb = ts // RB_CONV
        _fill_shifted(dcv_ext, shifted, ts)
        for cg in range(CONV_WIDTH // GROUP):
            cols = pl.ds(cg * GROUP, GROUP)
            for rb0 in range(0, nrb, CONV_LIVE_BLOCKS):
                group = range(rb0, min(rb0 + CONV_LIVE_BLOCKS, nrb))
                hblk = {rb: hh_ext[pl.ds(HALO + rb * RB_CONV, RB_CONV), cols] for rb in group}
                for k in range(CONV_KERNEL):
                    part = jnp.zeros((8, GROUP), F32)
                    for rb in group:
                        offset = rb * RB_CONV + CONV_KERNEL - 1 - k
                        prod = hblk[rb] * _shifted_rows(dcv_ext, shifted, offset, RB_CONV, cols)
                        part = part + jnp.sum(prod.reshape(RB_CONV // 8, 8, GROUP), axis=0)
                    dcw_acc[k, :, cols] += part

        @pl.when(i == nt - 1)
        def _():
            dcw_ref[...] = jnp.sum(dcw_acc[...], axis=1)

        zero = lambda cg: jnp.zeros((RB_CONV, GROUP), F32)
        def glu_backward(r0, c0, blk):
            rows, cols = pl.ds(r0, RB_CONV), pl.ds(c0, GROUP)
            a_blk = p_ref[rows, pl.ds(COL_C + c0, GROUP)]
            s_blk = _sigmoid(p_ref[rows, pl.ds(COL_C + CONV_WIDTH + c0, GROUP)])
            da = blk * s_blk
            dg = blk * a_blk * s_blk * (1.0 - s_blk)
            dp_ref[rows, pl.ds(COL_C + c0, GROUP)] = da.astype(BF16)
            dp_ref[rows, pl.ds(COL_C + CONV_WIDTH + c0, GROUP)] = dg.astype(BF16)
            hh_ext[pl.ds(HALO + r0, RB_CONV), cols] = da
            dcv_ext[rows, cols] = dg

        _conv_taps(zero, cwr_ref, dcv_ext, shifted, ts, lambda k: k, glu_backward)
        _accumulate(dbin_ref.at[:, pl.ds(COL_C, CONV_WIDTH)], _colsum(hh_ext[pl.ds(HALO, ts), :]), first)
        _accumulate(dbin_ref.at[:, pl.ds(COL_C + CONV_WIDTH, CONV_WIDTH)], _colsum(dcv_ext[pl.ds(0, ts), :]), first)

    full = lambda a: pl.BlockSpec(a.shape, lambda i: (0,) * a.ndim)
    weights = [lw["w_pool"], _row(lw["pool_scale"]), _row(lw["sgu_ln_g"]), _row(lw["sgu_ln_b"]), lw["sgu_w"], lw["sgu_w_t"],
               lw["sgu_bias_tile"], lw["conv_w_full"], lw["conv_w_rev"], _row(lw["conv_ln_g"]), _row(lw["conv_ln_b"])]
    prev_halo = lambda i: (jnp.maximum(i * per_halo - 1, 0), 0)
    next_halo = lambda i: (jnp.minimum((i + 1) * per_halo, s_len // HALO - 1), 0)
    small = lambda shape: (jax.ShapeDtypeStruct(shape, F32), pl.BlockSpec(shape, lambda i: (0,) * len(shape)))
    outs = [(jax.ShapeDtypeStruct((s_len, IN_WIDTH), BF16), pl.BlockSpec((ts, IN_WIDTH), lambda i: (i, 0))),
            small((1, IN_WIDTH)), small((len(POOL_WINDOWS), GROUP, GROUP)), small((1, POOL_WIDTH)),
            small((1, SGU_WIDTH)), small((1, SGU_WIDTH)), small((SGU_HEADS, CHUNK, CHUNK)), small((SGU_HEADS, CHUNK, GROUP)),
            small((CONV_ROWS, CONV_WIDTH)), small((1, CONV_WIDTH)), small((1, CONV_WIDTH)), small((1, CONV_WIDTH))]
    return pl.pallas_call(
        body, name=name, grid=(nt,),
        in_specs=[pl.BlockSpec((ts, IN_WIDTH), lambda i: (i, 0)), pl.BlockSpec((HALO, IN_WIDTH), prev_halo),
                  pl.BlockSpec((ts, CONV_WIDTH), lambda i: (i, 0)), pl.BlockSpec((HALO, CONV_WIDTH), next_halo),
                  pl.BlockSpec((ts, d), lambda i: (i, 0)), pl.BlockSpec((HALO, d), next_halo)] + [full(a) for a in weights],
        out_specs=[s for _, s in outs],
        out_shape=[o for o, _ in outs],
        scratch_shapes=[pltpu.VMEM((16 + ts, POOL_WIDTH), F32), pltpu.VMEM((ts + HALO, POOL_WIDTH), F32),
                        pltpu.VMEM((HALO + ts, CONV_WIDTH), F32), pltpu.VMEM((ts + HALO, CONV_WIDTH), F32),
                        pltpu.VMEM((CONV_ROWS, 8, CONV_WIDTH), F32),
                        pltpu.VMEM((SUBLANES - 1, ts + HALO - SUBLANES, CONV_WIDTH), F32)],
        compiler_params=_params(1),
    )(p, p, cv, cv, dmixed, dmixed, *weights)


def _masked_sgu_w_t(wt_ref, h):
    r = lax.broadcasted_iota(jnp.int32, (CHUNK, CHUNK), 0)
    c = lax.broadcasted_iota(jnp.int32, (CHUNK, CHUNK), 1)
    return jnp.where(c >= r, wt_ref[h], 0.0)


HBM = pl.BlockSpec(memory_space=pltpu.HBM)
CHIP_FLIPS = ((1, 0), (0, 1), (1, 1))


def _place():
    return lax.axis_index("x"), lax.axis_index("y"), lax.axis_index("c")


def _half(ref, axis, which, size):
    idx = [slice(None)] * len(ref.shape)
    idx[axis] = pl.ds(which * size, size)
    return ref.at[tuple(idx)]


def _exchange(name, sources, inplace, fresh, copies):
    n_src, n_in, n = len(sources), len(inplace), len(copies)
    n_out = n_in + len(fresh)

    def body(*refs):
        ins = refs[:n_src + n_in]
        outs = refs[n_src + n_in:n_src + n_in + n_out]
        send_sems, recv_sems = refs[n_src + n_in + n_out:]
        started = _descriptors(copies, list(ins[:n_src]) + list(outs), send_sems, recv_sems)
        for cp in started:
            cp.start()
        for cp in started:
            cp.wait()

    out_shape = [jax.ShapeDtypeStruct(a.shape, a.dtype) for a in inplace] + list(fresh)
    return pl.pallas_call(
        body, name=name, in_specs=[HBM] * (n_src + n_in), out_specs=[HBM] * n_out, out_shape=out_shape,
        input_output_aliases={n_src + i: i for i in range(n_in)},
        scratch_shapes=[pltpu.SemaphoreType.DMA((n,)), pltpu.SemaphoreType.DMA((n,))],
    )(*sources, *inplace)


def _descriptors(copies, bufs, send_sems, recv_sems):
    x, y, c = _place()
    return [pltpu.make_async_remote_copy(
        src_ref=src_view(bufs[src], x, y, c), dst_ref=dst_view(bufs[dst], x, y, c),
        send_sem=send_sems.at[k], recv_sem=recv_sems.at[k], device_id=peer(x, y, c), device_id_type=MESH)
        for k, (src, src_view, dst, dst_view, peer) in enumerate(copies)]


SEM = pl.BlockSpec(memory_space=pltpu.SEMAPHORE)
IN_FLIGHT = pltpu.CompilerParams(has_side_effects=pltpu.SideEffectType.DATAFLOW_SIDE_EFFECTING)


def _exchange_start(name, sources, inplace, fresh, copies, after=None):
    n_src, n = len(sources), len(copies)
    landing = [lax.empty(f.shape, f.dtype) for f in fresh]
    bufs = [pltpu.with_memory_space_constraint(a, pltpu.HBM) for a in (*inplace, *landing)]
    srcs = [pltpu.with_memory_space_constraint(a, pltpu.HBM) for a in sources]
    n_buf = len(bufs)
    order = [] if after is None else [after]
    n_in = n_src + n_buf + len(order)

    def body(*refs):
        ins = refs[:n_src]
        send_sems, recv_sems = refs[n_in], refs[n_in + 1]
        outs = refs[n_in + 2:n_in + 2 + n_buf]
        token = refs[n_in + 2 + n_buf]
        for cp in _descriptors(copies, list(ins) + list(outs), send_sems, recv_sems):
            cp.start()
        token[...] = jnp.zeros_like(token)

    res = pl.pallas_call(
        body, name=name + "_start",
        out_shape=(pltpu.SemaphoreType.DMA((n,)), pltpu.SemaphoreType.DMA((n,)),
                   *[pltpu.HBM(b.shape, b.dtype) for b in bufs], jax.ShapeDtypeStruct((8, 128), F32)),
        in_specs=[HBM] * (n_src + n_buf) + [pl.BlockSpec(memory_space=pl.ANY)] * len(order),
        out_specs=(SEM, SEM, *[HBM] * n_buf, pl.BlockSpec(memory_space=pltpu.VMEM)),
        input_output_aliases={n_src + i: 2 + i for i in range(n_buf)},
        compiler_params=IN_FLIGHT,
    )(*srcs, *bufs, *order)
    return dict(name=name, sources=srcs, sems=res[:2], bufs=list(res[2:2 + n_buf]), token=res[2 + n_buf], copies=copies)


def _exchange_wait(handle, after):
    srcs, bufs, copies = handle["sources"], handle["bufs"], handle["copies"]
    n_src, n_buf = len(srcs), len(bufs)

    def body(*refs):
        ins = refs[:n_src]
        send_sems, recv_sems = refs[n_src + n_buf], refs[n_src + n_buf + 1]
        outs = refs[n_src + n_buf + 3:]
        for cp in _descriptors(copies, list(ins) + list(outs), send_sems, recv_sems):
            cp.wait()

    return list(pl.pallas_call(
        body, name=handle["name"] + "_wait",
        out_shape=tuple(pltpu.HBM(b.shape, b.dtype) for b in bufs),
        in_specs=[HBM] * (n_src + n_buf) + [SEM, SEM, pl.BlockSpec(memory_space=pl.ANY)],
        out_specs=tuple([HBM] * n_buf),
        input_output_aliases={n_src + i: i for i in range(n_buf)},
        compiler_params=IN_FLIGHT,
    )(*srcs, *bufs, *handle["sems"], after))


def _into_slot(name, src, layer, place, dtype, after=None):
    _, r, c = src.shape
    br = _tile(r, BR_EW) if r % BR_EW == 0 else r
    order = [] if after is None else [after]

    def body(pr_ref, s_ref, *rest):
        rest[-1][...] = s_ref[...].astype(dtype)

    return pl.pallas_call(
        body, name=name,
        grid_spec=pltpu.PrefetchScalarGridSpec(
            num_scalar_prefetch=1, grid=(r // br,),
            in_specs=[pl.BlockSpec((None, br, c), lambda i, pr: (layer, i, 0))] + [pl.BlockSpec(memory_space=pl.ANY)] * len(order),
            out_specs=pl.BlockSpec((None, br, c), lambda i, pr: (pr[1], i, 0))),
        out_shape=jax.ShapeDtypeStruct((4, r, c), dtype), compiler_params=_params(1),
    )(place, src, *order)


def _gather_copies(n_bufs):
    def own_half(ref, x, y, c):
        return _half(ref.at[2 * x + y], 0, c, ref.shape[1] // 2)

    ici, d2d = [], []
    for b in range(n_bufs):
        for fx, fy in CHIP_FLIPS:
            ici.append((b, own_half, b, own_half, lambda x, y, c, fx=fx, fy=fy: (x ^ fx, y ^ fy, c)))
            landed = lambda ref, x, y, c, fx=fx, fy=fy: _half(ref.at[2 * (x ^ fx) + (y ^ fy)], 0, c, ref.shape[1] // 2)
            d2d.append((b, landed, b, landed, _sibling))
    return ici, d2d


def _gather_start(name, bufs, after=None):
    return _exchange_start(name + "_chips", [], bufs, [], _gather_copies(len(bufs))[0], after=after)


def _gather_finish(name, handle, after):
    bufs = _exchange_wait(handle, after)
    return _exchange(name + "_sibling", [], bufs, [], _gather_copies(len(bufs))[1])


def _gather_pass_on(name, handle, after):
    bufs = _exchange_wait(handle, after)
    return _exchange_start(name + "_sibling", [], bufs, [], _gather_copies(len(bufs))[1])


def _sibling(x, y, c):
    return (x, y, 1 - c)


def _scalar_spec_call(name, fn, scalars, ins, in_blocks, out_shapes, out_blocks, grid):
    def body(s_ref, *refs):
        res = fn(*[r[...] for r in refs[:len(ins)]])
        for o_ref, v in zip(refs[len(ins):], res):
            o_ref[...] = v.astype(o_ref.dtype)

    return pl.pallas_call(
        body, name=name,
        grid_spec=pltpu.PrefetchScalarGridSpec(num_scalar_prefetch=1, grid=grid, in_specs=in_blocks, out_specs=out_blocks),
        out_shape=out_shapes, compiler_params=_params(len(grid)),
    )(scalars, *ins)


def _reduce_begin(tag, place, items):
    n = len(items)
    whole = lambda ref, x, y, c: ref
    halves = [g.shape[1] // 2 for g, _, _ in items]
    fresh = [jax.ShapeDtypeStruct((g.shape[0], halves[a], g.shape[2]), F32) for a, (g, _, _) in enumerate(items)]
    copies = [(a, lambda ref, x, y, c, hr=halves[a]: _half(ref, 1, 1 - c, hr), n + a, whole, _sibling) for a in range(n)]
    handle = _exchange_start(tag + "_sibling_in", [g for g, _, _ in items], [], fresh, copies)
    return dict(tag=tag, place=place, items=items, handle=handle, token=handle["token"])


def _reduce_between_chips(state, after):
    tag, place, items = state["tag"], state["place"], state["items"]
    n = len(items)
    slot_of = lambda scatter: (lambda x, y: 2 * x + y) if scatter else (lambda x, y: 0)
    halves = [g.shape[1] // 2 for g, _, _ in items]
    landed = _exchange_wait(state["handle"], after)

    chip_sums = []
    for a, (g, _, wire) in enumerate(items):
        ns, _, cols = g.shape
        hr = halves[a]
        br = _tile(hr, BR_EW)
        chip_sums.append(_scalar_spec_call(
            f"{tag}_chip_sum{a}", lambda u, v: (u + v,), place, [g, landed[a]],
            [pl.BlockSpec((None, br, cols), lambda s, i, pr, hb=hr // br: (s, pr[0] * hb + i, 0)),
             pl.BlockSpec((None, br, cols), lambda s, i, pr: (s, i, 0))],
            [jax.ShapeDtypeStruct((ns, hr, cols), wire)], [pl.BlockSpec((None, br, cols), lambda s, i, pr: (s, i, 0))],
            (ns, hr // br))[0])

    copies, fresh = [], []
    for a, (g, scatter, wire) in enumerate(items):
        slot = slot_of(scatter)
        fresh.append(jax.ShapeDtypeStruct((3, halves[a], g.shape[2]), wire))
        for j, (fx, fy) in enumerate(CHIP_FLIPS):
            copies.append((a, lambda ref, x, y, c, fx=fx, fy=fy, slot=slot: ref.at[slot(x ^ fx, y ^ fy)],
                           n + a, lambda ref, x, y, c, j=j: ref.at[j], lambda x, y, c, fx=fx, fy=fy: (x ^ fx, y ^ fy, c)))
    handle = _exchange_start(tag + "_chips", chip_sums, [], fresh, copies)
    return dict(tag=tag, place=place, items=items, chip_sums=chip_sums, handle=handle, token=handle["token"])


def _reduce_swap(state, after):
    tag, place, items, chip_sums = state["tag"], state["place"], state["items"], state["chip_sums"]
    n = len(items)
    whole = lambda ref, x, y, c: ref
    halves = [g.shape[1] // 2 for g, _, _ in items]
    arrived = _exchange_wait(state["handle"], after)

    tree = lambda own, fx, fy, fxy: ((own.astype(F32) + fx.astype(F32)) + (fy.astype(F32) + fxy.astype(F32)),)
    mine = []
    for a, (g, scatter, _) in enumerate(items):
        hr, cols = halves[a], g.shape[2]
        br = _tile(hr, BR_EW)
        got = lambda j: pl.BlockSpec((None, br, cols), lambda i, pr, j=j: (j, i, 0))
        own = pl.BlockSpec((None, br, cols), (lambda i, pr: (pr[1], i, 0)) if scatter else (lambda i, pr: (0, i, 0)))
        mine.append(_scalar_spec_call(
            f"{tag}_tree_sum{a}", tree, place, [chip_sums[a], arrived[a], arrived[a], arrived[a]],
            [own, got(0), got(1), got(2)],
            [jax.ShapeDtypeStruct((2, hr, cols), F32)], [pl.BlockSpec((None, br, cols), lambda i, pr: (pr[0], i, 0))],
            (hr // br,))[0])

    my_half = lambda ref, x, y, c: ref.at[c]
    handle = _exchange_start(tag + "_sibling_out", [], mine, [], [(a, my_half, a, my_half, _sibling) for a in range(n)])
    return dict(handle=handle, token=handle["token"])


def _reduce_end(state, after):
    return [b.reshape(2 * b.shape[1], b.shape[2]) for b in _exchange_wait(state["handle"], after)]


def _adamw_blocks(layer, w, m, v, g, earlier, n_blocks):
    n_layers, r, c = w.shape
    assert r % n_blocks == 0 and (r // n_blocks) % 8 == 0, (r, n_blocks)
    br = r // n_blocks
    in_layer = ((None, br, c), lambda b: (layer, b, 0))
    ins = [(a, *in_layer) for a in (w, m, v)] + [(g, (br, c), lambda b: (b, 0))] + [(a, None, None) for a in earlier or ()]
    outs = [(jax.ShapeDtypeStruct((n_layers, r, c), F32), *in_layer)] * 4
    aliases = {4 + i: i for i in range(len(earlier or ()))}

    def fn(wv, mv, vv, gv):
        return (gv,) + _adamw_math(wv, gv, mv, vv)

    return fn, ins, outs, aliases, n_blocks


def _adamw_layer(name, layer, w, m, v, g, earlier):
    c = w.shape[2]
    br = max(8, 2 ** int(math.log2(EW_BLOCK_ELEMS // c)))
    fn, ins, outs, aliases, n_blocks = _adamw_blocks(layer, w, m, v, g, earlier, w.shape[1] // min(br, w.shape[1]))
    n_data = sum(1 for _, bs, _ in ins if bs is not None)

    def body(*refs):
        res = fn(*[r[...] for r in refs[:n_data]])
        for o_ref, val in zip(refs[len(ins):], res):
            o_ref[...] = val

    spec = lambda bs, f: pl.BlockSpec(memory_space=pl.ANY) if bs is None else pl.BlockSpec(bs, lambda i, f=f: f(i))
    return pl.pallas_call(
        body, name=name, grid=(n_blocks,),
        in_specs=[spec(bs, f) for _, bs, f in ins], out_specs=[spec(bs, f) for _, bs, f in outs],
        out_shape=[o for o, _, _ in outs], input_output_aliases=aliases, compiler_params=_params(1),
    )(*[a for a, _, _ in ins])


SHARDED = ("w_in", "w_out", "w_ff1", "w_ff2")
REPLICATED = ("b_in", "w_pool", "pool_scale", "sgu_ln_g", "sgu_ln_b", "sgu_w", "sgu_b", "conv_b", "conv_ln_g", "conv_ln_b",
              "b_out", "ln1_g", "ln1_b", "b_ff1", "b_ff2", "ln2_g", "ln2_b")
WEIGHTS = ("w_in", "b_in", "w_pool", "pool_scale", "sgu_ln_g", "sgu_ln_b", "sgu_w", "sgu_b", "conv_w", "conv_b", "conv_ln_g",
           "conv_ln_b", "w_out", "b_out", "ln1_g", "ln1_b", "w_ff1", "b_ff1", "w_ff2", "b_ff2", "ln2_g", "ln2_b")
PACK_ROWS = 1024


def _pack(arrays):
    flat = jnp.concatenate([a.reshape(-1) for a in arrays])
    rows = -(-flat.shape[0] // (128 * PACK_ROWS)) * PACK_ROWS
    return jnp.pad(flat, (0, rows * 128 - flat.shape[0])).reshape(rows, 128)


def _unpack(packed, like):
    flat = packed.reshape(-1)
    out, at = [], 0
    for a in like:
        out.append(flat[at:at + a.size].reshape(a.shape))
        at += a.size
    return out


def kernel(x, w_in, b_in, w_pool, pool_scale, sgu_ln_g, sgu_ln_b, sgu_w, sgu_b, conv_w, conv_b, conv_ln_g, conv_ln_b, w_out, b_out, ln1_g, ln1_b, w_ff1, b_ff1, w_ff2, b_ff2, ln2_g, ln2_b, loss_target, m_w_in, m_b_in, m_w_pool, m_pool_scale, m_sgu_ln_g, m_sgu_ln_b, m_sgu_w, m_sgu_b, m_conv_w, m_conv_b, m_conv_ln_g, m_conv_ln_b, m_w_out, m_b_out, m_ln1_g, m_ln1_b, m_w_ff1, m_b_ff1, m_w_ff2, m_b_ff2, m_ln2_g, m_ln2_b, v_w_in, v_b_in, v_w_pool, v_pool_scale, v_sgu_ln_g, v_sgu_ln_b, v_sgu_w, v_sgu_b, v_conv_w, v_conv_b, v_conv_ln_g, v_conv_ln_b, v_w_out, v_b_out, v_ln1_g, v_ln1_b, v_w_ff1, v_b_ff1, v_w_ff2, v_b_ff2, v_ln2_g, v_ln2_b):
    w = dict(w_in=w_in, b_in=b_in, w_pool=w_pool, pool_scale=pool_scale, sgu_ln_g=sgu_ln_g, sgu_ln_b=sgu_ln_b, sgu_w=sgu_w,
             sgu_b=sgu_b, conv_w=conv_w, conv_b=conv_b, conv_ln_g=conv_ln_g, conv_ln_b=conv_ln_b, w_out=w_out, b_out=b_out,
             ln1_g=ln1_g, ln1_b=ln1_b, w_ff1=w_ff1, b_ff1=b_ff1, w_ff2=w_ff2, b_ff2=b_ff2, ln2_g=ln2_g, ln2_b=ln2_b)
    m = dict(w_in=m_w_in, b_in=m_b_in, w_pool=m_w_pool, pool_scale=m_pool_scale, sgu_ln_g=m_sgu_ln_g, sgu_ln_b=m_sgu_ln_b,
             sgu_w=m_sgu_w, sgu_b=m_sgu_b, conv_w=m_conv_w, conv_b=m_conv_b, conv_ln_g=m_conv_ln_g, conv_ln_b=m_conv_ln_b,
             w_out=m_w_out, b_out=m_b_out, ln1_g=m_ln1_g, ln1_b=m_ln1_b, w_ff1=m_w_ff1, b_ff1=m_b_ff1, w_ff2=m_w_ff2,
             b_ff2=m_b_ff2, ln2_g=m_ln2_g, ln2_b=m_ln2_b)
    v = dict(w_in=v_w_in, b_in=v_b_in, w_pool=v_w_pool, pool_scale=v_pool_scale, sgu_ln_g=v_sgu_ln_g, sgu_ln_b=v_sgu_ln_b,
             sgu_w=v_sgu_w, sgu_b=v_sgu_b, conv_w=v_conv_w, conv_b=v_conv_b, conv_ln_g=v_conv_ln_g, conv_ln_b=v_conv_ln_b,
             w_out=v_w_out, b_out=v_b_out, ln1_g=v_ln1_g, ln1_b=v_ln1_b, w_ff1=v_w_ff1, b_ff1=v_b_ff1, w_ff2=v_w_ff2,
             b_ff2=v_b_ff2, ln2_g=v_ln2_g, ln2_b=v_ln2_b)
    assert x.shape[0] == 1 and x.shape[2] == POOL_WIDTH + SGU_WIDTH + CONV_WIDTH, x.shape
    xs, target = x[0], loss_target[0]
    s_len, d = xs.shape
    n_layers = w_in.shape[0]
    dff = 4 * w_ff1.shape[2]
    conv_shard = conv_w.shape[2]

    cx, cy, cc = _place()
    me = 2 * cx + cy
    place = jnp.stack([cc, me]).astype(jnp.int32)

    def cast(key, after=None):
        return _into_slot(f"cast_{key[0]}{key[1]}", w[key[0]], key[1], place, BF16, after=after)

    conv_padded = jnp.pad(conv_w, ((0, 0), (0, CONV_ROWS - CONV_KERNEL), (0, 0))).reshape(1, n_layers * CONV_ROWS, conv_shard)
    slot, flights, token = {}, {}, None
    for name, keys in (("in0", [("w_in", 0)]), ("rest0", [("w_out", 0), "conv_w"]), ("ff1_0", [("w_ff1", 0)]),
                       ("ff2_0", [("w_ff2", 0)])):
        for k in keys:
            slot[k] = _into_slot("slot_conv_w", conv_padded, 0, place, F32, after=token) if k == "conv_w" else cast(k, token)
        flights[name] = _gather_start("gather_" + name, [slot[k] for k in keys])
        token = flights[name]["token"]
    slot.update({(n, l): cast((n, l), token) for n in SHARDED for l in range(1, n_layers)})
    xb0 = _rowwise("cast_x", lambda t: (t,), [xs], [BF16])[0]
    gathered = {("w_in", 0): _gather_finish("gather_in0", flights["in0"], slot[(SHARDED[-1], n_layers - 1)])[0]}
    conv = {}

    def layer_weights(l):
        conv_full, conv_rev = conv["full"], conv["rev"]
        return dict(w_pool=w_pool[l], pool_scale=pool_scale[l], sgu_ln_g=sgu_ln_g[l], sgu_ln_b=sgu_ln_b[l], sgu_w=sgu_w[l],
                    sgu_w_t=jnp.transpose(sgu_w[l], (0, 2, 1)),
                    sgu_bias_tile=jnp.broadcast_to(sgu_b[l][:, :, None], (SGU_HEADS, CHUNK, GROUP)),
                    conv_w_full=conv_full[l], conv_w_rev=conv_rev[l], conv_b=conv_b[l], conv_ln_g=conv_ln_g[l],
                    conv_ln_b=conv_ln_b[l])

    saved = []
    x_cur, xb_cur, prev_ln = xs, xb0, None
    for l in range(n_layers):
        p = _proj(f"proj{l}", xb_cur, gathered[("w_in", l)], b_in[l], after=token if l == 0 else None)
        if l == 0:
            gathered[("w_out", 0)], conv_buf = _gather_finish("gather_rest0", flights["rest0"], p)
            full = jnp.transpose(conv_buf.reshape(4, n_layers, CONV_ROWS, conv_shard), (1, 2, 0, 3))
            conv["full"] = full.reshape(n_layers, CONV_ROWS, CONV_WIDTH)
            conv["rev"] = jnp.pad(conv["full"][:, CONV_KERNEL - 1::-1], ((0, 0), (0, CONV_ROWS - CONV_KERNEL), (0, 0)))
        lw = layer_weights(l)
        w_out_full = gathered[("w_out", l)].reshape(d, d)
        mixed, cv = _mix_forward(f"mix_fwd{l}", p, lw)
        passing = _gather_pass_on("gather_ff1_0", flights["ff1_0"], mixed) if l == 0 else None
        r1, x1b = _mix_out(f"mix_out{l}", mixed, w_out_full, b_out[l], x_cur, ln1_g[l], ln1_b[l], prev_ln,
                           after=passing and passing["token"])
        token = None
        if l == 0:
            gathered[("w_ff1", 0)] = _exchange_wait(passing, x1b)[0]
            passing = _gather_pass_on("gather_ff2_0", flights["ff2_0"], x1b)
            token = passing["token"]
        nxt = [(n, l + 1) for n in SHARDED] if l + 1 < n_layers else None
        if nxt:
            in_flight = _gather_start(f"gather_layer{l + 1}", [slot[k] for k in nxt], after=token)
            token = in_flight["token"]
        hf, zr = _ff1(f"ff1_{l}", x1b, gathered[("w_ff1", l)], b_ff1[l], after=token)
        if l == 0:
            gathered[("w_ff2", 0)] = _exchange_wait(passing, hf)[0]
        w2_full = gathered[("w_ff2", l)].reshape(dff, d)
        fo = _ff2(f"ff2_{l}", hf, w2_full, b_ff2[l])
        sv = dict(lw=lw, xb_in=xb_cur, p=p, mixed=mixed, cv=cv, r1=r1, x1b=x1b, hf=hf, zr=zr, fo=fo,
                  w_out_full=w_out_full, w2_full=w2_full)
        if nxt:
            in_flight = _gather_pass_on(f"gather_layer{l + 1}", in_flight, fo)
            sv["r2"], xb_cur = _resid_ln2(f"ln2_{l}", r1, ln1_g[l], ln1_b[l], fo, ln2_g[l], ln2_b[l], after=in_flight["token"])
            x_cur, prev_ln = sv["r2"], (ln2_g[l], ln2_b[l])
            gathered.update(zip(nxt, _exchange_wait(in_flight, xb_cur)))
        saved.append(sv)

    grads = {n: [None] * n_layers for n in REPLICATED + ("conv_w",)}
    g_final, delta, new_m, new_v = {}, {}, {}, {}
    results = {n: None for n in SHARDED}
    dx_mm, dx_resid = None, None
    loss_tile = None
    pending = None

    jobs = []

    def finish_reduce(begun, after):
        names, layer, state = begun
        reduced = _reduce_end(state, after)
        jobs.extend((n, layer, g) for n, g in zip(names, reduced))
        return reduced

    def hosted(call):
        taken = []

        def rider(steps):
            fits = [j for j in jobs if w[j[0]].shape[1] % steps == 0 and (w[j[0]].shape[1] // steps) % 8 == 0]
            if not fits:
                return None
            job = max(fits, key=lambda j: w[j[0]].shape[1] * w[j[0]].shape[2])
            jobs.remove(job)
            taken.append(job)
            n, layer, g = job
            return _adamw_blocks(layer, w[n], m[n], v[n], g, results[n], steps)

        out = call(rider)
        if not taken:
            return out
        results[taken[0][0]] = tuple(out[1])
        return out[0]

    for l in reversed(range(n_layers)):
        sv = saved[l]
        if dx_mm is None:
            dr2, dr2b, dg2, db2, dbff2, loss_tile = _ln_backward(
                f"ln2_bwd{l}", None, ln2_g[l], last=(sv["r1"], ln1_g[l], ln1_b[l], sv["fo"], ln2_b[l], target))
        else:
            dr2, dr2b, dg2, db2, dbff2 = _ln_backward(f"ln2_bwd{l}", sv["r2"], ln2_g[l], dy=dx_mm, resid=dx_resid)
        dzb, dbff1 = hosted(lambda rider: _dff_hidden(f"ff2_bwd{l}", dr2b, sv["w2_full"], sv["zr"], rider=rider))
        dw = {"w_ff2": _matmul_tn(f"dw_ff2_{l}", sv["hf"], dr2b, 4, dff // 4, d, True),
              "w_ff1": _matmul_tn(f"dw_ff1_{l}", sv["x1b"], dzb, 4, d, dff // 4, False)}
        if pending is not None:
            names, layer, state = pending
            pending = (names, layer, _reduce_swap(state, dw["w_ff1"]))
        ff_red = _reduce_begin(f"grads_ff{l}", place, [(dw[n], True, BF16) for n in ("w_ff1", "w_ff2")])
        if pending is not None:
            finish_reduce(pending, ff_red["token"])
            pending = None
        dx1 = _dx_sharded(f"ff1_bwd{l}", dzb, gathered[("w_ff1", l)], after=ff_red["token"])
        ff_red = _reduce_between_chips(ff_red, dx1)
        dr1, dr1b, dg1, db1, dbout = _ln_backward(f"ln1_bwd{l}", sv["r1"], ln1_g[l], dy=dx1, resid=dr2, after=ff_red["token"])
        dmixed = hosted(lambda rider: _dmixed(f"mix_out_bwd{l}", dr1b, sv["w_out_full"], rider=rider))
        dw["w_out"] = _matmul_tn(f"dw_out{l}", sv["mixed"], dr1b, 4, d // 4, d, True)
        (dp, dbin, dwp, dps, dslg, dslb, dws, dsb_tile, dcw, dcb, dclg, dclb) = _mix_backward(
            f"mix_bwd{l}", sv["p"], sv["cv"], dmixed, sv["lw"])
        dw["w_in"] = _matmul_tn(f"dw_in{l}", sv["xb_in"], dp, 4, d, IN_WIDTH // 4, False)
        for name, g in (("b_in", dbin), ("w_pool", dwp), ("pool_scale", dps), ("sgu_ln_g", dslg), ("sgu_ln_b", dslb),
                        ("sgu_w", dws), ("sgu_b", dsb_tile[:, :, 0]), ("conv_w", dcw[:CONV_KERNEL]), ("conv_b", dcb),
                        ("conv_ln_g", dclg), ("conv_ln_b", dclb), ("b_out", dbout), ("ln1_g", dg1), ("ln1_b", db1),
                        ("b_ff1", dbff1), ("b_ff2", dbff2), ("ln2_g", dg2), ("ln2_b", db2)):
            grads[name][l] = g.reshape(w[name].shape[1:]) if name != "conv_w" else g

        items = [(dw[n], True, BF16) for n in ("w_in", "w_out")]
        if l == 0:
            small_like = [jnp.stack(grads[n]) for n in REPLICATED + ("conv_w",)]
            items.append((_pack(small_like)[None], False, F32))
        mix_red = _reduce_begin(f"grads_mix{l}", place, items)
        ff_red = _reduce_swap(ff_red, mix_red["token"])
        mix_red = _reduce_between_chips(mix_red, ff_red["token"])
        finish_reduce((("w_ff1", "w_ff2"), l, ff_red), mix_red["token"])
        if l > 0:
            dx_mm = hosted(lambda rider: _dx_sharded(f"proj_bwd{l}", dp, gathered[("w_in", l)], after=mix_red["token"],
                                                     rider=rider))
            dx_resid = dr1
            pending = (("w_in", "w_out"), l, mix_red)
        else:
            grad_x = hosted(lambda rider: _dx_sharded(f"proj_bwd{l}", dp, gathered[("w_in", l)], dr1,
                                                      after=mix_red["token"], rider=rider))
            mix_red = _reduce_swap(mix_red, grad_x)
            reduced_small = finish_reduce((("w_in", "w_out"), l, mix_red), mix_red["token"])[-1]
            grad_x = grad_x[None]

    for n, layer, g in jobs:
        results[n] = _adamw_layer(f"adamw_{n}{layer}", layer, w[n], m[n], v[n], g, results[n])
    loss = lax.psum(loss_tile[0, 0], ("x", "y", "c"))
    for n in SHARDED:
        g_final[n], delta[n], new_m[n], new_v[n] = results[n]
    unpacked = _unpack(reduced_small, small_like)
    g_final.update(zip(REPLICATED, unpacked[:-1]))
    g_final["conv_w"] = lax.dynamic_slice_in_dim(unpacked[-1], me * conv_shard, conv_shard, axis=2)

    delta["conv_w"], new_m["conv_w"], new_v["conv_w"] = _adamw("adamw_conv_w", conv_w, g_final["conv_w"], m["conv_w"], v["conv_w"])
    packed = [_pack([t[n] for n in REPLICATED]) for t in (w, g_final, m, v)]
    like = [w[n] for n in REPLICATED]
    for res, packed_out in zip((delta, new_m, new_v), _adamw("adamw_replicated", *packed)):
        res.update(zip(REPLICATED, _unpack(packed_out, like)))

    return (loss, grad_x, *[g_final[n] for n in WEIGHTS], *[delta[n] for n in WEIGHTS],
            *[new_m[n] for n in WEIGHTS], *[new_v[n] for n in WEIGHTS])
```
